```python
import math
import jax, jax.numpy as jnp
from jax import lax
import numpy as np

D_MODEL = 1024
BATCH = 8
SEQ = 2048
DEPTH = 2
DEC_BATCH = 16
DEC_SEQ = 32
PAST_LEN = 4096

CHUNK = 64
CONV_W = 4
EPS = 1e-6
LRU_HEADS = 4
LRU_WIDTH = 256
LRU_BLOCK = LRU_WIDTH // LRU_HEADS
LRU_C = 8.0
FOX_HEADS = 8
FOX_HEAD_DIM = 64
FOX_WIDTH = FOX_HEADS * FOX_HEAD_DIM
Q_BLOCK = 128
SSD_HEADS = 4
SSD_HEAD_DIM = 64
SSD_WIDTH = SSD_HEADS * SSD_HEAD_DIM
SSD_GROUPS = 2
SSD_HPG = SSD_HEADS // SSD_GROUPS
D_STATE = 128
SSD_CONV_DIM = SSD_WIDTH + 2 * SSD_GROUPS * D_STATE
SSD_BLOCK = CHUNK
D_MIX = LRU_WIDTH + FOX_WIDTH + SSD_WIDTH
IN_SIZES = (LRU_WIDTH, LRU_WIDTH, FOX_WIDTH, FOX_WIDTH, FOX_WIDTH, FOX_HEADS, SSD_WIDTH, SSD_CONV_DIM, SSD_HEADS)
IN_OFFSETS = tuple(int(v) for v in np.cumsum(IN_SIZES)[:-1])
D_IN = int(sum(IN_SIZES))
D_FF = 2816
N_SUB = 3
STATE_KEYS = ("fox_k", "fox_v", "fox_logf", "lru_conv", "lru_h", "ssd_conv", "ssd_h")

kernel_name = "hybrid_streaming_encoder_step"

F32 = jnp.float32


def rmsnorm(x, g):
    xf = x.astype(F32)
    r = lax.rsqrt(jnp.mean(xf * xf, axis=-1, keepdims=True) + EPS)
    return (xf * r).astype(x.dtype) * g


def causal_conv(x, prev, w, b):
    L = x.shape[1]
    xp = jnp.concatenate([prev.astype(x.dtype), x], axis=1)
    y = b + w[0] * xp[:, 0:L]
    for k in range(1, CONV_W):
        y = y + w[k] * xp[:, k:k + L]
    return y, xp[:, -(CONV_W - 1):]


def swiglu(h, wg, wu, wd):
    return (jax.nn.silu(h @ wg) * (h @ wu)) @ wd


def rg_lru(x, h0, wa, ba, wx, bx, lam):
    b_, L, _ = x.shape
    xf = x.astype(F32)
    xb = xf.reshape(b_, L, LRU_HEADS, LRU_BLOCK)
    r = jax.nn.sigmoid(jnp.einsum('blhi,hij->blhj', xb, wa.astype(F32)).reshape(b_, L, LRU_WIDTH) + ba)
    i = jax.nn.sigmoid(jnp.einsum('blhi,hij->blhj', xb, wx.astype(F32)).reshape(b_, L, LRU_WIDTH) + bx)
    log_a = -LRU_C * r * jax.nn.softplus(-lam.astype(F32))
    a = jnp.exp(log_a)
    u = jnp.sqrt(-jnp.expm1(2.0 * log_a)) * (i * xf)

    def comb(left, right):
        al, bl = left
        ar, br = right
        return al * ar, ar * bl + br

    a_cum, b_cum = lax.associative_scan(comb, (a, u), axis=1)
    h = a_cum * h0.astype(F32)[:, None] + b_cum
    return h, h[:, -1]


def fox_block(q, k, v, fq, fk, pos_q, pos_k):
    s = jnp.einsum('bqhd,bkhd->bhqk', q, k, preferred_element_type=F32) * (FOX_HEAD_DIM ** -0.5)
    s = s + (jnp.transpose(fq, (0, 2, 1))[:, :, :, None] - jnp.transpose(fk, (0, 2, 1))[:, :, None, :])
    mask = pos_k[None, :] <= pos_q[:, None]
    s = jnp.where(mask, s, -1e30)
    p = jax.nn.softmax(s, axis=-1)
    return jnp.einsum('bhqk,bkhd->bqhd', p.astype(v.dtype), v)


def fox_prompt(q, k, v, logf):
    b_, S, H, dh = q.shape
    F = jnp.cumsum(logf, axis=1)
    nblk = S // Q_BLOCK
    qb = jnp.transpose(q.reshape(b_, nblk, Q_BLOCK, H, dh), (1, 0, 2, 3, 4))
    fb = jnp.transpose(F.reshape(b_, nblk, Q_BLOCK, H), (1, 0, 2, 3))
    pos_q = jnp.arange(S, dtype=jnp.int32).reshape(nblk, Q_BLOCK)
    pos_k = jnp.arange(S, dtype=jnp.int32)
    out = lax.map(lambda a: fox_block(a[0], k, v, a[1], F, a[2], pos_k), (qb, fb, pos_q))
    return jnp.transpose(out, (1, 0, 2, 3, 4)).reshape(b_, S, H, dh)


def fox_sample(q, k, v, logf, ck, cv, clogf):
    P = ck.shape[1]
    T = q.shape[1]
    k_all = jnp.concatenate([ck.astype(k.dtype), k], axis=1)
    v_all = jnp.concatenate([cv.astype(v.dtype), v], axis=1)
    F = jnp.cumsum(jnp.concatenate([clogf.astype(F32), logf], axis=1), axis=1)
    pos_k = jnp.arange(P + T, dtype=jnp.int32)
    pos_q = P + jnp.arange(T, dtype=jnp.int32)
    return fox_block(q, k_all, v_all, F[:, P:], F, pos_q, pos_k)


def ssd(x, dt, A, Bm, Cm, Dp, h0, block):
    b_, L = x.shape[:2]
    nc = L // block
    xg = x.reshape(b_, nc, block, SSD_GROUPS, SSD_HPG, SSD_HEAD_DIM)
    dtg = dt.reshape(b_, nc, block, SSD_GROUPS, SSD_HPG)
    Bc = Bm.reshape(b_, nc, block, SSD_GROUPS, D_STATE)
    Cc = Cm.reshape(b_, nc, block, SSD_GROUPS, D_STATE)
    cum = jnp.cumsum(dtg * A.reshape(SSD_GROUPS, SSD_HPG), axis=2)
    seg = cum[:, :, :, None] - cum[:, :, None, :]
    causal = jnp.tril(jnp.ones((block, block), dtype=bool))[:, :, None, None]
    Lmat = jnp.exp(jnp.where(causal, seg, -jnp.inf))
    CB = jnp.einsum('bcqgn,bcsgn->bcqsg', Cc, Bc)
    dx = dtg[..., None] * xg
    y_diag = jnp.einsum('bcqsgh,bcsghp->bcqghp', CB[..., None] * Lmat, dx)
    decay_end = jnp.exp(cum[:, :, -1:] - cum)
    states = jnp.einsum('bcsgn,bcsghp->bcghpn', Bc, decay_end[..., None] * dx)
    chunk_decay = jnp.exp(cum[:, :, -1])

    def step(h, inp):
        st, dec = inp
        return dec[..., None, None] * h + st, h

    h0g = h0.astype(F32).reshape(b_, SSD_GROUPS, SSD_HPG, SSD_HEAD_DIM, D_STATE)
    h_last, h_starts = lax.scan(step, h0g, (jnp.transpose(states, (1, 0, 2, 3, 4, 5)),
                                            jnp.transpose(chunk_decay, (1, 0, 2, 3))))
    h_starts = jnp.transpose(h_starts, (1, 0, 2, 3, 4, 5))
    y_off = jnp.einsum('bcqgn,bcghpn->bcqghp', Cc, h_starts) * jnp.exp(cum)[..., None]
    y = y_diag + y_off + Dp.astype(F32).reshape(SSD_GROUPS, SSD_HPG)[..., None] * xg
    return y.reshape(b_, L, SSD_HEADS, SSD_HEAD_DIM), h_last.reshape(b_, SSD_HEADS, SSD_HEAD_DIM, D_STATE)


def token_mix(h, lp, prev, ssd_block):
    b_, L, _ = h.shape
    proj = h @ lp["w_in"]
    lru_x, lru_g, q, k, v, f_raw, z, xbc, dt_raw = jnp.split(proj, IN_OFFSETS, axis=-1)
    u, lru_conv_new = causal_conv(lru_x, prev["lru_conv"], lp["lru_conv_w"], lp["lru_conv_b"])
    hA, lru_h_new = rg_lru(u, prev["lru_h"], lp["lru_wa"], lp["lru_ba"], lp["lru_wx"], lp["lru_bx"], lp["lru_lambda"])
    yA = hA.astype(h.dtype) * jax.nn.gelu(lru_g)
    q = q.reshape(b_, L, FOX_HEADS, FOX_HEAD_DIM)
    k = k.reshape(b_, L, FOX_HEADS, FOX_HEAD_DIM)
    v = v.reshape(b_, L, FOX_HEADS, FOX_HEAD_DIM)
    logf = jax.nn.log_sigmoid((f_raw + lp["fox_f_bias"]).astype(F32))
    if prev["fox_k"] is None:
        o = fox_prompt(q, k, v, logf)
    else:
        o = fox_sample(q, k, v, logf, prev["fox_k"], prev["fox_v"], prev["fox_logf"])
    yB = o.reshape(b_, L, FOX_WIDTH).astype(h.dtype)
    xbc, ssd_conv_new = causal_conv(xbc, prev["ssd_conv"], lp["ssd_conv_w"], lp["ssd_conv_b"])
    xbc = jax.nn.silu(xbc).astype(F32)
    xs, Bm, Cm = jnp.split(xbc, (SSD_WIDTH, SSD_WIDTH + SSD_GROUPS * D_STATE), axis=-1)
    dt = jax.nn.softplus((dt_raw + lp["ssd_dt_bias"]).astype(F32))
    A = -jnp.exp(lp["ssd_a_log"].astype(F32))
    yc, ssd_h_new = ssd(xs.reshape(b_, L, SSD_HEADS, SSD_HEAD_DIM), dt, A,
                        Bm.reshape(b_, L, SSD_GROUPS, D_STATE), Cm.reshape(b_, L, SSD_GROUPS, D_STATE),
                        lp["ssd_d"], prev["ssd_h"], ssd_block)
    yC = rmsnorm(yc.reshape(b_, L, SSD_WIDTH).astype(h.dtype) * jax.nn.silu(z), lp["ssd_norm_w"])
    y = jnp.concatenate([yA, yB, yC], axis=-1) @ lp["w_out"]
    new = {"fox_k": k, "fox_v": v, "fox_logf": logf, "lru_conv": lru_conv_new, "lru_h": lru_h_new,
           "ssd_conv": ssd_conv_new, "ssd_h": ssd_h_new}
    return y, new


def trunk(x, c, caches, params):
    b_, L, _ = x.shape
    ssd_block = SSD_BLOCK if caches is None else L
    new_states = {name: [] for name in STATE_KEYS}
    for l in range(DEPTH):
        lp = {name: arr[l] for name, arr in params.items()}
        if caches is None:
            prev = {"fox_k": None, "fox_v": None, "fox_logf": None,
                    "lru_conv": jnp.zeros((b_, CONV_W - 1, LRU_WIDTH), x.dtype),
                    "lru_h": jnp.zeros((b_, LRU_WIDTH), F32),
                    "ssd_conv": jnp.zeros((b_, CONV_W - 1, SSD_CONV_DIM), x.dtype),
                    "ssd_h": jnp.zeros((b_, SSD_HEADS, SSD_HEAD_DIM, D_STATE), F32)}
        else:
            prev = {name: arr[l] for name, arr in caches.items()}
        mod = (jax.nn.silu(c) @ lp["w_mod"] + lp["b_mod"]).reshape(b_, N_SUB, 3, D_MODEL)
        shift, scale, gate = mod[:, :, 0, None], mod[:, :, 1, None], mod[:, :, 2, None]

        def pre(x_, j):
            return rmsnorm(x_, lp["norm_pre"][j]) * (1.0 + scale[:, j]) + shift[:, j]

        def post(x_, y_, j, w):
            return x_ + w * gate[:, j] * rmsnorm(y_, lp["norm_post"][j])

        y = swiglu(pre(x, 0), lp["ffn_w_gate"][0], lp["ffn_w_up"][0], lp["ffn_w_down"][0])
        x = post(x, y, 0, 0.5)
        y, st = token_mix(pre(x, 1), lp, prev, ssd_block)
        x = post(x, y, 1, 1.0)
        y = swiglu(pre(x, 2), lp["ffn_w_gate"][1], lp["ffn_w_up"][1], lp["ffn_w_down"][1])
        x = post(x, y, 2, 0.5)
        for name in STATE_KEYS:
            new_states[name].append(st[name])
    return x, {name: jnp.stack(v, axis=0) for name, v in new_states.items()}


def setup_inputs(seed: int = 0) -> dict:
    key = jax.random.key(seed)
    ks = iter(jax.random.split(key, 48))

    def nrm(shape, s=1.0):
        return s * jax.random.normal(next(ks), shape, F32)

    def unif(shape, lo, hi):
        return jax.random.uniform(next(ks), shape, F32, lo, hi)

    a_root = unif((DEPTH, LRU_WIDTH), 0.9, 0.999) ** (1.0 / LRU_C)
    lru_lambda = jnp.log(a_root) - jnp.log1p(-a_root)
    dt0 = jnp.exp(unif((DEPTH, SSD_HEADS), math.log(1e-3), math.log(1e-1)))
    ssd_dt_bias = dt0 + jnp.log(-jnp.expm1(-dt0))
    return {
        "x_prompt": nrm((BATCH, SEQ, D_MODEL)),
        "x_sample": nrm((DEC_BATCH, DEC_SEQ, D_MODEL)),
        "c_prompt": nrm((BATCH, D_MODEL)),
        "c_sample": nrm((DEC_BATCH, D_MODEL)),
        "cache_fox_k": nrm((DEPTH, DEC_BATCH, PAST_LEN, FOX_HEADS, FOX_HEAD_DIM)),
        "cache_fox_v": nrm((DEPTH, DEC_BATCH, PAST_LEN, FOX_HEADS, FOX_HEAD_DIM)),
        "cache_fox_logf": jax.nn.log_sigmoid(nrm((DEPTH, DEC_BATCH, PAST_LEN, FOX_HEADS)) + 3.0),
        "state_lru_conv": nrm((DEPTH, DEC_BATCH, CONV_W - 1, LRU_WIDTH)),
        "state_lru_h": nrm((DEPTH, DEC_BATCH, LRU_WIDTH), 0.5),
        "state_ssd_conv": nrm((DEPTH, DEC_BATCH, CONV_W - 1, SSD_CONV_DIM)),
        "state_ssd_h": nrm((DEPTH, DEC_BATCH, SSD_HEADS, SSD_HEAD_DIM, D_STATE), 0.1),
        "w_mod": nrm((DEPTH, D_MODEL, N_SUB * 3 * D_MODEL), 0.5 * D_MODEL ** -0.5),
        "b_mod": nrm((DEPTH, N_SUB * 3 * D_MODEL), 0.1),
        "norm_pre": 1.0 + nrm((DEPTH, N_SUB, D_MODEL), 0.05),
        "norm_post": 1.0 + nrm((DEPTH, N_SUB, D_MODEL), 0.05),
        "ffn_w_gate": nrm((DEPTH, 2, D_MODEL, D_FF), D_MODEL ** -0.5),
        "ffn_w_up": nrm((DEPTH, 2, D_MODEL, D_FF), D_MODEL ** -0.5),
        "ffn_w_down": nrm((DEPTH, 2, D_FF, D_MODEL), D_FF ** -0.5),
        "w_in": nrm((DEPTH, D_MODEL, D_IN), D_MODEL ** -0.5),
        "w_out": nrm((DEPTH, D_MIX, D_MODEL), D_MIX ** -0.5),
        "lru_conv_w": nrm((DEPTH, CONV_W, LRU_WIDTH), CONV_W ** -0.5),
        "lru_conv_b": nrm((DEPTH, LRU_WIDTH), 0.02),
        "lru_wa": nrm((DEPTH, LRU_HEADS, LRU_BLOCK, LRU_BLOCK), LRU_BLOCK ** -0.5),
        "lru_ba": nrm((DEPTH, LRU_WIDTH), 0.02),
        "lru_wx": nrm((DEPTH, LRU_HEADS, LRU_BLOCK, LRU_BLOCK), LRU_BLOCK ** -0.5),
        "lru_bx": nrm((DEPTH, LRU_WIDTH), 0.02),
        "lru_lambda": lru_lambda,
        "fox_f_bias": unif((DEPTH, FOX_HEADS), 1.0, 4.0),
        "ssd_conv_w": nrm((DEPTH, CONV_W, SSD_CONV_DIM), CONV_W ** -0.5),
        "ssd_conv_b": nrm((DEPTH, SSD_CONV_DIM), 0.02),
        "ssd_dt_bias": ssd_dt_bias,
        "ssd_a_log": jnp.log(unif((DEPTH, SSD_HEADS), 1.0, 16.0)),
        "ssd_d": 1.0 + nrm((DEPTH, SSD_HEADS), 0.1),
        "ssd_norm_w": 1.0 + nrm((DEPTH, SSD_WIDTH), 0.05),
    }


def reference(x_prompt, x_sample, c_prompt, c_sample, cache_fox_k, cache_fox_v, cache_fox_logf,
              state_lru_conv, state_lru_h, state_ssd_conv, state_ssd_h,
              w_mod, b_mod, norm_pre, norm_post, ffn_w_gate, ffn_w_up, ffn_w_down, w_in, w_out,
              lru_conv_w, lru_conv_b, lru_wa, lru_ba, lru_wx, lru_bx, lru_lambda, fox_f_bias,
              ssd_conv_w, ssd_conv_b, ssd_dt_bias, ssd_a_log, ssd_d, ssd_norm_w):
    params = {"w_mod": w_mod, "b_mod": b_mod, "norm_pre": norm_pre, "norm_post": norm_post,
              "ffn_w_gate": ffn_w_gate, "ffn_w_up": ffn_w_up, "ffn_w_down": ffn_w_down,
              "w_in": w_in, "w_out": w_out, "lru_conv_w": lru_conv_w, "lru_conv_b": lru_conv_b,
              "lru_wa": lru_wa, "lru_ba": lru_ba, "lru_wx": lru_wx, "lru_bx": lru_bx, "lru_lambda": lru_lambda,
              "fox_f_bias": fox_f_bias, "ssd_conv_w": ssd_conv_w, "ssd_conv_b": ssd_conv_b,
              "ssd_dt_bias": ssd_dt_bias, "ssd_a_log": ssd_a_log, "ssd_d": ssd_d, "ssd_norm_w": ssd_norm_w}
    caches = {"fox_k": cache_fox_k, "fox_v": cache_fox_v, "fox_logf": cache_fox_logf,
              "lru_conv": state_lru_conv, "lru_h": state_lru_h, "ssd_conv": state_ssd_conv, "ssd_h": state_ssd_h}
    y_prompt, sp = trunk(x_prompt, c_prompt, None, params)
    y_sample, ss = trunk(x_sample, c_sample, caches, params)
    return (y_prompt, y_sample,
            sp["fox_k"], sp["fox_v"], sp["fox_logf"], sp["lru_conv"], sp["lru_h"], sp["ssd_conv"], sp["ssd_h"],
            ss["fox_k"], ss["fox_v"], ss["fox_logf"], ss["lru_conv"], ss["lru_h"], ss["ssd_conv"], ss["ssd_h"])
```

```python
import functools

import numpy as np
import jax
import jax.numpy as jnp
from jax import lax
from jax.experimental import pallas as pl
from jax.experimental.pallas import tpu as pltpu

F32 = jnp.float32
BF16 = jnp.bfloat16

D_MODEL = 1024
DEPTH = 2
CONV_W = 4
EPS = 1e-6
LRU_WIDTH = 256
LRU_HEADS = 4
LRU_BLOCK = LRU_WIDTH // LRU_HEADS
LRU_C = 8.0
FOX_HEADS = 8
FOX_HEAD_DIM = 64
FOX_WIDTH = FOX_HEADS * FOX_HEAD_DIM
SSD_HEADS = 4
SSD_HEAD_DIM = 64
SSD_WIDTH = SSD_HEADS * SSD_HEAD_DIM
SSD_GROUPS = 2
D_STATE = 128
SSD_CONV_DIM = SSD_WIDTH + 2 * SSD_GROUPS * D_STATE
IN_SIZES = (LRU_WIDTH, LRU_WIDTH, FOX_WIDTH, FOX_WIDTH, FOX_WIDTH, FOX_HEADS, SSD_WIDTH, SSD_CONV_DIM, SSD_HEADS)
D_FF = 2816
N_SUB = 3

LANES = 128
SUBLANES = 8
HEAD_PAIR = LANES // FOX_HEAD_DIM

F_LANE0 = 0
DT_LANE0 = FOX_HEADS
COL_LRU_X = 0
COL_LRU_G = COL_LRU_X + LRU_WIDTH
COL_Q = COL_LRU_G + LRU_WIDTH
COL_K = COL_Q + FOX_WIDTH
COL_V = COL_K + FOX_WIDTH
COL_Z = COL_V + FOX_WIDTH
COL_XBC = COL_Z + SSD_WIDTH
COL_SMALL = COL_XBC + SSD_CONV_DIM
D_IN_PAD = COL_SMALL + LANES

ROW_TILE = 512
FF_CHUNK = 512
ATTN_BLOCK = 512
SCAN_BLOCK = 256
NEG_BIG = -1e30

_HI = lax.Precision.HIGHEST


def _dot(a, b):
    return jnp.dot(a, b, preferred_element_type=F32)


def _dot_nt(a, b):
    return lax.dot_general(a, b, (((1,), (1,)), ((), ())), preferred_element_type=F32)


def _dot_tn(a, b):
    return lax.dot_general(a, b, (((0,), (0,)), ((), ())), preferred_element_type=F32)


def _silu(x):
    return x * jax.nn.sigmoid(x)


def _softplus(x):
    return jnp.maximum(x, 0.0) + jnp.log1p(jnp.exp(-jnp.abs(x)))


def _rms(x):
    return x * lax.rsqrt(jnp.mean(x * x, axis=-1, keepdims=True) + EPS)


def _per_seq(rows, per_seq, fn):
    g = per_seq[0].shape[0]
    if g == 1:
        return fn(rows, *per_seq)
    tm, d = rows.shape
    out = fn(rows.reshape(g, tm // g, d), *[p[:, None, :] for p in per_seq])
    return out.reshape(tm, d)


def _pre_norm(x, npre, mod_ref):
    h = _rms(x) * npre
    return _per_seq(h, (mod_ref[:, 1, :], mod_ref[:, 0, :]), lambda r, sc, sh: r * (1.0 + sc) + sh)


def _post_norm(x, y, npost, mod_ref, w):
    yn = _rms(y) * npost
    return x + _per_seq(yn, (mod_ref[:, 2, :],), lambda r, gt: (w * gt) * r)


def _seq_grouping(rows_per_seq, tm):
    if rows_per_seq % tm == 0:
        return 1, rows_per_seq // tm
    assert tm % rows_per_seq == 0
    return tm // rows_per_seq, 1


def _mod_spec(g, tiles_per_seq, sub):
    if g == 1:
        return pl.BlockSpec((1, None, 3, D_MODEL), lambda i: (i // tiles_per_seq, sub, 0, 0))
    return pl.BlockSpec((g, None, 3, D_MODEL), lambda i: (i, sub, 0, 0))


def _params(vmem_mb, sem):
    return pltpu.CompilerParams(dimension_semantics=sem, vmem_limit_bytes=vmem_mb << 20)


def _mod_kernel(c_ref, w_ref, b_ref, o_ref):
    a = _silu(c_ref[...]).astype(BF16)
    o_ref[...] = _dot(a, w_ref[...].astype(BF16)) + b_ref[...]


def _mod_call(c_all, w_mod, b_mod):
    nseq = c_all.shape[0]
    width = N_SUB * 3 * D_MODEL
    tn = 1024
    return pl.pallas_call(
        _mod_kernel,
        grid=(DEPTH, width // tn),
        in_specs=[
            pl.BlockSpec((nseq, D_MODEL), lambda l, n: (0, 0)),
            pl.BlockSpec((None, D_MODEL, tn), lambda l, n: (l, 0, n)),
            pl.BlockSpec((None, 1, tn), lambda l, n: (l, 0, n)),
        ],
        out_specs=pl.BlockSpec((None, nseq, tn), lambda l, n: (l, 0, n)),
        out_shape=jax.ShapeDtypeStruct((DEPTH, nseq, width), F32),
        compiler_params=_params(24, ("arbitrary", "arbitrary")),
        name="adaln_mod",
    )(c_all, w_mod, b_mod.reshape(DEPTH, 1, width))


def _ffn_kernel(x_ref, mod_ref, npre_ref, npost_ref, wg_ref, wu_ref, wd_ref, o_ref):
    x = x_ref[...]
    h = _pre_norm(x, npre_ref[...], mod_ref).astype(BF16)
    acc = None
    for off in range(0, D_FF, FF_CHUNK):
        fc = min(FF_CHUNK, D_FF - off)
        g = _dot(h, wg_ref[:, off:off + fc])
        u = _dot(h, wu_ref[:, off:off + fc])
        a = (_silu(g) * u).astype(BF16)
        d = _dot(a, wd_ref[off:off + fc, :])
        acc = d if acc is None else acc + d
    o_ref[...] = _post_norm(x, acc, npost_ref[...], mod_ref, 0.5)


def _ffn_call(x, mod4, npre, npost, wg, wu, wd, layer, sub, ffn_idx, rows_per_seq):
    m = x.shape[0]
    tm = min(ROW_TILE, m)
    g, tps = _seq_grouping(rows_per_seq, tm)
    wspec = lambda shape: pl.BlockSpec((None, None) + shape, lambda i: (layer, ffn_idx, 0, 0),
                                       pipeline_mode=pl.Buffered(1))
    nspec = pl.BlockSpec((None, None, 1, D_MODEL), lambda i: (layer, sub, 0, 0))
    return pl.pallas_call(
        _ffn_kernel,
        grid=(m // tm,),
        in_specs=[
            pl.BlockSpec((tm, D_MODEL), lambda i: (i, 0)),
            _mod_spec(g, tps, sub),
            nspec, nspec,
            wspec((D_MODEL, D_FF)), wspec((D_MODEL, D_FF)), wspec((D_FF, D_MODEL)),
        ],
        out_specs=pl.BlockSpec((tm, D_MODEL), lambda i: (i, 0)),
        out_shape=jax.ShapeDtypeStruct((m, D_MODEL), F32),
        compiler_params=_params(48, ("arbitrary",)),
        name="ffn",
    )(x, mod4, npre, npost, wg, wu, wd)


def _inproj_kernel(x_ref, mod_ref, npre_ref, w_ref, fbias_ref,
                   lrux_ref, lrug_ref, q_ref, k_ref, v_ref, z_ref, xbc_ref, small_ref):
    h = _pre_norm(x_ref[...], npre_ref[...], mod_ref).astype(BF16)
    col = lambda start, width: _dot(h, w_ref[:, start:start + width])
    lrux_ref[...] = col(COL_LRU_X, LRU_WIDTH)
    lrug_ref[...] = col(COL_LRU_G, LRU_WIDTH)
    q_ref[...] = (col(COL_Q, FOX_WIDTH) * (FOX_HEAD_DIM ** -0.5)).astype(BF16)
    k_ref[...] = col(COL_K, FOX_WIDTH)
    v_ref[...] = col(COL_V, FOX_WIDTH)
    z_ref[...] = col(COL_Z, SSD_WIDTH)
    xbc_ref[...] = col(COL_XBC, SSD_CONV_DIM)
    small = col(COL_SMALL, LANES)
    t = small + fbias_ref[...]
    logf = jnp.minimum(t, 0.0) - jnp.log1p(jnp.exp(-jnp.abs(t)))
    lane = lax.broadcasted_iota(jnp.int32, small.shape, 1)
    small_ref[...] = jnp.where(lane < DT_LANE0, logf, small)


def _inproj_call(x, mod4, npre, w_in_p, fbias, layer, rows_per_seq):
    m = x.shape[0]
    tm = min(ROW_TILE, m)
    g, tps = _seq_grouping(rows_per_seq, tm)
    widths = (LRU_WIDTH, LRU_WIDTH, FOX_WIDTH, FOX_WIDTH, FOX_WIDTH, SSD_WIDTH, SSD_CONV_DIM, LANES)
    dtypes = (F32, F32, BF16, F32, F32, F32, F32, F32)
    return pl.pallas_call(
        _inproj_kernel,
        grid=(m // tm,),
        in_specs=[
            pl.BlockSpec((tm, D_MODEL), lambda i: (i, 0)),
            _mod_spec(g, tps, 1),
            pl.BlockSpec((None, None, 1, D_MODEL), lambda i: (layer, 1, 0, 0)),
            pl.BlockSpec((None, D_MODEL, D_IN_PAD), lambda i: (layer, 0, 0)),
            pl.BlockSpec((None, 1, LANES), lambda i: (layer, 0, 0)),
        ],
        out_specs=[pl.BlockSpec((tm, w), lambda i: (i, 0)) for w in widths],
        out_shape=[jax.ShapeDtypeStruct((m, w), dt) for w, dt in zip(widths, dtypes)],
        compiler_params=_params(48, ("arbitrary",)),
        name="inproj",
    )(x, mod4, npre, w_in_p, fbias)


def _outproj_kernel(x_ref, ya_ref, yb_ref, yc_ref, mod_ref, npost_ref, w_ref, o_ref):
    y = (_dot(ya_ref[...], w_ref[0:LRU_WIDTH, :])
         + _dot(yb_ref[...], w_ref[LRU_WIDTH:LRU_WIDTH + FOX_WIDTH, :])
         + _dot(yc_ref[...], w_ref[LRU_WIDTH + FOX_WIDTH:, :]))
    o_ref[...] = _post_norm(x_ref[...], y, npost_ref[...], mod_ref, 1.0)


def _outproj_call(x, ya, yb, yc, mod4, npost, w_out, layer, rows_per_seq):
    m = x.shape[0]
    tm = min(ROW_TILE, m)
    g, tps = _seq_grouping(rows_per_seq, tm)
    row = lambda w: pl.BlockSpec((tm, w), lambda i: (i, 0))
    return pl.pallas_call(
        _outproj_kernel,
        grid=(m // tm,),
        in_specs=[
            row(D_MODEL), row(LRU_WIDTH), row(FOX_WIDTH), row(SSD_WIDTH),
            _mod_spec(g, tps, 1),
            pl.BlockSpec((None, None, 1, D_MODEL), lambda i: (layer, 1, 0, 0)),
            pl.BlockSpec((None, D_MODEL, D_MODEL), lambda i: (layer, 0, 0)),
        ],
        out_specs=row(D_MODEL),
        out_shape=jax.ShapeDtypeStruct((m, D_MODEL), F32),
        compiler_params=_params(32, ("arbitrary",)),
        name="outproj",
    )(x, ya, yb, yc, mod4, npost, w_out)


def _causal_conv(xx, cw, cb, n):
    u = cb + cw[0:1] * pltpu.roll(xx, 3, 0)[SUBLANES:SUBLANES + n]
    u = u + cw[1:2] * pltpu.roll(xx, 2, 0)[SUBLANES:SUBLANES + n]
    u = u + cw[2:3] * pltpu.roll(xx, 1, 0)[SUBLANES:SUBLANES + n]
    return u + cw[3:4] * xx[SUBLANES:SUBLANES + n]


def _last_rows(xx, k):
    return pltpu.roll(xx, k, 0)[0:SUBLANES][0:k]


def _tri(n, lower):
    r = lax.broadcasted_iota(jnp.int32, (n, n), 0)
    c = lax.broadcasted_iota(jnp.int32, (n, n), 1)
    return ((r >= c) if lower else (r <= c)).astype(F32)


def _lru_kernel(x_ref, g_ref, prev_ref, h0_ref, cw_ref, cb_ref, wgate_ref, bgate_ref, lam_ref,
                y_ref, convnew_ref, hnew_ref, a_scr, b_scr):
    n = x_ref.shape[0]
    xx = jnp.concatenate([prev_ref[...], x_ref[...]], axis=0)
    convnew_ref[...] = _last_rows(xx, CONV_W - 1)
    u = _causal_conv(xx, cw_ref[...], cb_ref[...], n)
    gates = _dot(u.astype(BF16), wgate_ref[...]) + bgate_ref[...]
    r = jax.nn.sigmoid(gates[:, :LRU_WIDTH])
    i = jax.nn.sigmoid(gates[:, LRU_WIDTH:])
    log_a = (-LRU_C * r) * _softplus(-lam_ref[...])
    a = jnp.exp(log_a)
    b = jnp.sqrt(-jnp.tanh(log_a) * (a * a + 1.0)) * (i * u)
    row = lax.broadcasted_iota(jnp.int32, (n, 1), 0) % SUBLANES
    for d in (1, 2, 4):
        keep = row >= d
        b = jnp.where(keep, a * pltpu.roll(b, d, 0) + b, b)
        a = jnp.where(keep, a * pltpu.roll(a, d, 0), a)
    a_scr[...] = a
    b_scr[...] = b

    def group(j, h):
        off = pl.multiple_of(j * SUBLANES, SUBLANES)
        hb = a_scr[pl.ds(off, SUBLANES), :] * h + b_scr[pl.ds(off, SUBLANES), :]
        b_scr[pl.ds(off, SUBLANES), :] = hb
        return jnp.broadcast_to(hb[SUBLANES - 1:SUBLANES, :], hb.shape)

    h_last = lax.fori_loop(0, n // SUBLANES, group,
                           jnp.broadcast_to(h0_ref[...], (SUBLANES, LRU_WIDTH)), unroll=4)
    hnew_ref[...] = h_last[0:1]
    y_ref[...] = (b_scr[...] * jax.nn.gelu(g_ref[...])).astype(BF16)


def _lru_call(lru_x, lru_g, prev8, h0, cw, cb, wgate, bgate, lam, layer, state_layer):
    bsz, n, _ = lru_x.shape
    seq = pl.BlockSpec((None, n, LRU_WIDTH), lambda b: (b, 0, 0))
    if state_layer is None:
        prev_spec = pl.BlockSpec((None, SUBLANES, LRU_WIDTH), lambda b: (0, 0, 0))
        h0_spec = pl.BlockSpec((None, 1, LRU_WIDTH), lambda b: (0, 0, 0))
    else:
        prev_spec = pl.BlockSpec((None, None, SUBLANES, LRU_WIDTH), lambda b: (state_layer, b, 0, 0))
        h0_spec = pl.BlockSpec((None, None, 1, LRU_WIDTH), lambda b: (state_layer, b, 0, 0))
    par = lambda r, w: pl.BlockSpec((None, r, w), lambda b: (layer, 0, 0))
    return pl.pallas_call(
        _lru_kernel,
        grid=(bsz,),
        in_specs=[seq, seq, prev_spec, h0_spec, par(CONV_W, LRU_WIDTH), par(1, LRU_WIDTH),
                  par(LRU_WIDTH, 2 * LRU_WIDTH), par(1, 2 * LRU_WIDTH), par(1, LRU_WIDTH)],
        out_specs=[seq,
                   pl.BlockSpec((None, CONV_W - 1, LRU_WIDTH), lambda b: (b, 0, 0)),
                   pl.BlockSpec((None, 1, LRU_WIDTH), lambda b: (b, 0, 0))],
        out_shape=[jax.ShapeDtypeStruct((bsz, n, LRU_WIDTH), BF16),
                   jax.ShapeDtypeStruct((bsz, CONV_W - 1, LRU_WIDTH), F32),
                   jax.ShapeDtypeStruct((bsz, 1, LRU_WIDTH), F32)],
        scratch_shapes=[pltpu.VMEM((n, LRU_WIDTH), F32), pltpu.VMEM((n, LRU_WIDTH), F32)],
        compiler_params=_params(40, ("arbitrary",)),
        name="rg_lru",
    )(lru_x, lru_g, prev8, h0, cw, cb, wgate, bgate, lam)


def _ssd_kernel(xbc_ref, z_ref, small_ref, prev_ref, h0_ref, cw_ref, cb_ref, dtb_ref, alog_ref, dvec_ref, nw_ref,
                y_ref, convnew_ref, h_ref, tail_scr):
    n = xbc_ref.shape[0]
    c = pl.program_id(1)

    @pl.when(c == 0)
    def _():
        tail_scr[...] = prev_ref[...]
        h_ref[...] = h0_ref[...]

    xx = jnp.concatenate([tail_scr[...], xbc_ref[...]], axis=0)
    tail_scr[...] = xx[n:n + SUBLANES]
    convnew_ref[...] = _last_rows(xx, CONV_W - 1)
    act = _silu(_causal_conv(xx, cw_ref[...], cb_ref[...], n))
    xs = act[:, :SSD_WIDTH]
    bm = act[:, SSD_WIDTH:SSD_WIDTH + SSD_GROUPS * D_STATE]
    cm = act[:, SSD_WIDTH + SSD_GROUPS * D_STATE:]

    dt = _softplus(small_ref[...] + dtb_ref[...])
    dta = dt * (-jnp.exp(alog_ref[...]))
    cum = jnp.dot(_tri(n, True), dta, preferred_element_type=F32, precision=_HI)
    cum_t = cum.T
    rr = lax.broadcasted_iota(jnp.int32, (n, n), 0)
    cc = lax.broadcasted_iota(jnp.int32, (n, n), 1)
    causal = rr >= cc
    lo_lane = lax.broadcasted_iota(jnp.int32, (1, LANES), 1) < SSD_HEAD_DIM
    lo_row = lax.broadcasted_iota(jnp.int32, (LANES, 1), 0) < SSD_HEAD_DIM
    dvec = dvec_ref[...]

    ys = []
    for g in range(SSD_GROUPS):
        sl = slice(g * LANES, (g + 1) * LANES)
        xg, bg, cg = xs[:, sl], bm[:, sl].astype(BF16), cm[:, sl].astype(BF16)
        heads = (2 * g, 2 * g + 1)
        col = lambda a, h: a[:, DT_LANE0 + h:DT_LANE0 + h + 1]
        pick = lambda f: jnp.where(lo_lane, f(heads[0]), f(heads[1]))
        dx = xg * pick(lambda h: col(dt, h))
        dxb = dx.astype(BF16)
        cb_mat = _dot_nt(cg, bg)
        yd = []
        for h in heads:
            seg = col(cum, h) - cum_t[DT_LANE0 + h:DT_LANE0 + h + 1, :]
            lmat = jnp.exp(jnp.where(causal, seg, NEG_BIG))
            yd.append(_dot((cb_mat * lmat).astype(BF16), dxb))
        y_diag = jnp.where(lo_lane, yd[0], yd[1])
        last = lambda h: col(cum, h)[n - 1:n, :]
        decay_end = pick(lambda h: jnp.exp(last(h) - col(cum, h)))
        states = _dot_tn((dx * decay_end).astype(BF16), bg)
        h_prev = h_ref[sl, :]
        y_off = _dot_nt(cg, h_prev.astype(BF16)) * pick(lambda h: jnp.exp(col(cum, h)))
        chunk_decay = jnp.where(lo_row, jnp.exp(last(heads[0])), jnp.exp(last(heads[1])))
        h_ref[sl, :] = chunk_decay * h_prev + states
        ys.append(y_diag + y_off + dvec[:, sl] * xg)
    y = jnp.concatenate(ys, axis=1)
    y_ref[...] = (_rms(y * _silu(z_ref[...])) * nw_ref[...]).astype(BF16)


def _ssd_call(xbc, z, small, prev8, h0, cw, cb, dtb, alog, dvec, nw, layer, state_layer, chunk):
    bsz, n, _ = xbc.shape
    nc = n // chunk
    seq = lambda w: pl.BlockSpec((None, chunk, w), lambda b, c: (b, c, 0))
    if state_layer is None:
        prev_spec = pl.BlockSpec((None, SUBLANES, SSD_CONV_DIM), lambda b, c: (0, 0, 0))
        h0_spec = pl.BlockSpec((None, SSD_WIDTH, D_STATE), lambda b, c: (0, 0, 0))
    else:
        prev_spec = pl.BlockSpec((None, None, SUBLANES, SSD_CONV_DIM), lambda b, c: (state_layer, b, 0, 0))
        h0_spec = pl.BlockSpec((None, None, SSD_WIDTH, D_STATE), lambda b, c: (state_layer, b, 0, 0))
    par = lambda r, w: pl.BlockSpec((None, r, w), lambda b, c: (layer, 0, 0))
    return pl.pallas_call(
        _ssd_kernel,
        grid=(bsz, nc),
        in_specs=[seq(SSD_CONV_DIM), seq(SSD_WIDTH), seq(LANES), prev_spec, h0_spec,
                  par(CONV_W, SSD_CONV_DIM), par(1, SSD_CONV_DIM), par(1, LANES), par(1, LANES),
                  par(1, SSD_WIDTH), par(1, SSD_WIDTH)],
        out_specs=[seq(SSD_WIDTH),
                   pl.BlockSpec((None, CONV_W - 1, SSD_CONV_DIM), lambda b, c: (b, 0, 0)),
                   pl.BlockSpec((None, SSD_WIDTH, D_STATE), lambda b, c: (b, 0, 0))],
        out_shape=[jax.ShapeDtypeStruct((bsz, n, SSD_WIDTH), BF16),
                   jax.ShapeDtypeStruct((bsz, CONV_W - 1, SSD_CONV_DIM), F32),
                   jax.ShapeDtypeStruct((bsz, SSD_WIDTH, D_STATE), F32)],
        scratch_shapes=[pltpu.VMEM((SUBLANES, SSD_CONV_DIM), F32)],
        compiler_params=_params(32, ("arbitrary", "arbitrary")),
        name="ssd",
    )(xbc, z, small, prev8, h0, cw, cb, dtb, alog, dvec, nw)


def _fox_prompt_kernel(q_ref, k_ref, v_ref, logf_ref, o_ref, kb_scr, vb_scr, f_scr, ft_scr):
    s = k_ref.shape[0]
    tq = q_ref.shape[0]
    p = pl.program_id(1)
    qi = pl.program_id(2)

    @pl.when(qi == 0)
    def _():
        kb_scr[...] = k_ref[...].astype(BF16)
        vb_scr[...] = v_ref[...].astype(BF16)

    @pl.when((qi == 0) & (p == 0))
    def _():
        tri = _tri(SCAN_BLOCK, True)
        carry = jnp.zeros((1, LANES), F32)
        for c in range(s // SCAN_BLOCK):
            rows = slice(c * SCAN_BLOCK, (c + 1) * SCAN_BLOCK)
            fc = jnp.dot(tri, logf_ref[rows, :], preferred_element_type=F32, precision=_HI) + carry
            f_scr[rows, :] = fc
            per = tq // SCAN_BLOCK
            ft_scr[c // per, :, (c % per) * SCAN_BLOCK:(c % per + 1) * SCAN_BLOCK] = fc.T
            carry = fc[SCAN_BLOCK - 1:SCAN_BLOCK, :]

    lane = lax.broadcasted_iota(jnp.int32, (1, LANES), 1)
    lo_lane = lane < FOX_HEAD_DIM
    q = q_ref[...]
    q_heads = (jnp.where(lo_lane, q, jnp.zeros_like(q)), jnp.where(lo_lane, jnp.zeros_like(q), q))
    heads = (HEAD_PAIR * p, HEAD_PAIR * p + 1)
    q0 = pl.multiple_of(qi * tq, tq)
    f_rows = f_scr[pl.ds(q0, tq), :]
    f_q = [jnp.sum(jnp.where(lane == F_LANE0 + h, f_rows, 0.0), axis=1, keepdims=True) for h in heads]

    def step(j, carry, masked):
        k0 = pl.multiple_of(j * tq, tq)
        kb = kb_scr[pl.ds(k0, tq), :]
        vb = vb_scr[pl.ds(k0, tq), :]
        out = []
        pv = []
        for hl in range(HEAD_PAIR):
            m_old, l_old = carry[2 * hl], carry[2 * hl + 1]
            f_k = ft_scr[j, pl.ds(F_LANE0 + heads[hl], 1), :]
            sc = _dot_nt(q_heads[hl], kb) + (f_q[hl] - f_k)
            if masked:
                rr = lax.broadcasted_iota(jnp.int32, sc.shape, 0)
                cc = lax.broadcasted_iota(jnp.int32, sc.shape, 1)
                sc = jnp.where(cc <= rr, sc, NEG_BIG)
            m_new = jnp.maximum(m_old, jnp.max(sc, axis=1, keepdims=True))
            alpha = jnp.exp(m_old - m_new)
            pr = jnp.exp(sc - m_new)
            out += [m_new, alpha * l_old + jnp.sum(pr, axis=1, keepdims=True)]
            pv.append((alpha, _dot(pr.astype(BF16), vb)))
        acc = jnp.where(lo_lane, pv[0][0], pv[1][0]) * carry[4] + jnp.where(lo_lane, pv[0][1], pv[1][1])
        return tuple(out) + (acc,)

    init = (jnp.full((tq, 1), NEG_BIG, F32), jnp.zeros((tq, 1), F32)) * HEAD_PAIR + (jnp.zeros((tq, LANES), F32),)
    carry = lax.fori_loop(0, qi, lambda j, cr: step(j, cr, False), init)
    carry = step(qi, carry, True)
    o_ref[...] = (carry[4] / jnp.where(lo_lane, carry[1], carry[3])).astype(BF16)


def _fox_prompt_call(q, k, v, small):
    bsz, s, _ = q.shape
    tq = min(ATTN_BLOCK, s)
    blk = pl.BlockSpec((None, tq, LANES), lambda b, p, i: (b, i, p))
    full = pl.BlockSpec((None, s, LANES), lambda b, p, i: (b, 0, p))
    return pl.pallas_call(
        _fox_prompt_kernel,
        grid=(bsz, FOX_WIDTH // LANES, s // tq),
        in_specs=[blk, full, full, pl.BlockSpec((None, s, LANES), lambda b, p, i: (b, 0, 0))],
        out_specs=blk,
        out_shape=jax.ShapeDtypeStruct((bsz, s, FOX_WIDTH), BF16),
        scratch_shapes=[pltpu.VMEM((s, LANES), BF16), pltpu.VMEM((s, LANES), BF16),
                        pltpu.VMEM((s, LANES), F32), pltpu.VMEM((s // tq, LANES, tq), F32)],
        compiler_params=_params(40, ("arbitrary", "arbitrary", "arbitrary")),
        name="fox_prompt",
    )(q, k, v, small)


def _fox_sample_kernel(q_ref, k_ref, v_ref, logf_ref, ck_ref, cv_ref, clogft_ref, o_ref, *, chunk):
    t = q_ref.shape[0]
    past = ck_ref.shape[0]
    nrow = FOX_HEADS * t
    lane_head = lax.broadcasted_iota(jnp.int32, (1, FOX_WIDTH), 1) // FOX_HEAD_DIM

    triu = _tri(chunk, False)
    carry = jnp.zeros((FOX_HEADS, 1), F32)
    f_cache = []
    for c in range(past // chunk):
        fc = jnp.dot(clogft_ref[:, c * chunk:(c + 1) * chunk], triu, preferred_element_type=F32, precision=_HI) + carry
        f_cache.append(fc)
        carry = fc[:, chunk - 1:chunk]
    sub = lax.broadcasted_iota(jnp.int32, (FOX_HEADS, LANES), 0)
    ln = lax.broadcasted_iota(jnp.int32, (FOX_HEADS, LANES), 1)
    total_row = jnp.sum(jnp.where(sub == ln - F_LANE0, carry, 0.0), axis=0, keepdims=True)
    f_new = jnp.dot(_tri(t, True), logf_ref[...], preferred_element_type=F32, precision=_HI) + total_row
    f_new_t = f_new.T

    q = q_ref[...]
    qs = jnp.concatenate([jnp.where(lane_head == h, q, jnp.zeros_like(q)) for h in range(FOX_HEADS)], axis=0)
    f_q = jnp.concatenate([f_new[:, F_LANE0 + h:F_LANE0 + h + 1] for h in range(FOX_HEADS)], axis=0)

    def rows_of(f_k):
        return jnp.concatenate([jnp.broadcast_to(f_k[h:h + 1, :], (t, f_k.shape[1])) for h in range(FOX_HEADS)], axis=0)

    scores = []
    for c in range(past // chunk):
        kc = ck_ref[c * chunk:(c + 1) * chunk, :].astype(BF16)
        scores.append(_dot_nt(qs, kc) + (f_q - rows_of(f_cache[c])))
    s_new = _dot_nt(qs, k_ref[...].astype(BF16)) + (f_q - rows_of(f_new_t[F_LANE0:F_LANE0 + FOX_HEADS, :]))
    rr = lax.broadcasted_iota(jnp.int32, (nrow, t), 0) % t
    cc = lax.broadcasted_iota(jnp.int32, (nrow, t), 1)
    scores.append(jnp.where(cc <= rr, s_new, NEG_BIG))

    m = functools.reduce(jnp.maximum, [jnp.max(sc, axis=1, keepdims=True) for sc in scores])
    probs = [jnp.exp(sc - m) for sc in scores]
    denom = functools.reduce(lambda a, b: a + b, [jnp.sum(pr, axis=1, keepdims=True) for pr in probs])
    acc = _dot(probs[-1].astype(BF16), v_ref[...].astype(BF16))
    for c in range(past // chunk):
        acc = acc + _dot(probs[c].astype(BF16), cv_ref[c * chunk:(c + 1) * chunk, :].astype(BF16))
    acc = acc / denom
    out = jnp.zeros((t, FOX_WIDTH), F32)
    for h in range(FOX_HEADS):
        out = out + jnp.where(lane_head == h, acc[h * t:(h + 1) * t, :], 0.0)
    o_ref[...] = out.astype(BF16)


def _fox_sample_call(q, k, v, small, cache_k, cache_v, cache_logf_t, layer):
    bsz, t, _ = q.shape
    past = cache_k.shape[2]
    new = lambda w: pl.BlockSpec((None, t, w), lambda b: (b, 0, 0))
    cache = pl.BlockSpec((None, None, past, FOX_WIDTH), lambda b: (layer, b, 0, 0))
    return pl.pallas_call(
        functools.partial(_fox_sample_kernel, chunk=min(1024, past)),
        grid=(bsz,),
        in_specs=[new(FOX_WIDTH), new(FOX_WIDTH), new(FOX_WIDTH), new(LANES), cache, cache,
                  pl.BlockSpec((None, None, FOX_HEADS, past), lambda b: (layer, b, 0, 0))],
        out_specs=new(FOX_WIDTH),
        out_shape=jax.ShapeDtypeStruct((bsz, t, FOX_WIDTH), BF16),
        compiler_params=_params(56, ("arbitrary",)),
        name="fox_sample",
    )(q, k, v, small, cache_k, cache_v, cache_logf_t)


def _permute_w_in(w_in):
    offs = np.concatenate([[0], np.cumsum(IN_SIZES)])
    seg = lambda i: w_in[..., int(offs[i]):int(offs[i + 1])]
    pad = jnp.zeros(w_in.shape[:-1] + (LANES - FOX_HEADS - SSD_HEADS,), w_in.dtype)
    small = jnp.concatenate([seg(5), seg(8), pad], axis=-1)
    return jnp.concatenate([seg(0), seg(1), seg(2), seg(3), seg(4), seg(6), seg(7), small], axis=-1).astype(BF16)


def _block_diag(w):
    d, h, b, _ = w.shape
    eye = jnp.eye(h, dtype=w.dtype)
    return jnp.einsum("dhij,hg->dhigj", w, eye).reshape(d, h * b, h * b)


def _lane_slab(v, lane0):
    d, k = v.shape
    return jnp.zeros((d, 1, LANES), v.dtype).at[:, 0, lane0:lane0 + k].set(v)


def _pad_history(state):
    return jnp.pad(state, ((0, 0), (0, 0), (SUBLANES - (CONV_W - 1), 0), (0, 0)))


def _trunk(x, mod_group, caches, prm, ssd_chunk):
    bsz, n, _ = x.shape
    x = x.reshape(bsz * n, D_MODEL)
    states = {name: [] for name in ("fox_k", "fox_v", "fox_logf", "lru_conv", "lru_h", "ssd_conv", "ssd_h")}
    for l in range(DEPTH):
        mod4 = mod_group[l].reshape(bsz, N_SUB, 3, D_MODEL)
        x = _ffn_call(x, mod4, prm["npre"], prm["npost"], prm["wg"], prm["wu"], prm["wd"], l, 0, 0, n)
        lrux, lrug, q, k, v, z, xbc, small = _inproj_call(x, mod4, prm["npre"], prm["w_in"], prm["fbias"], l, n)
        per_seq = lambda a: a.reshape(bsz, n, a.shape[-1])
        state_layer = None if caches is None else l
        src = prm["zero_state"] if caches is None else caches
        ya, lru_conv, lru_h = _lru_call(per_seq(lrux), per_seq(lrug), src["lru_conv"], src["lru_h"],
                                        prm["lru_cw"], prm["lru_cb"], prm["lru_wgate"], prm["lru_bgate"],
                                        prm["lru_lam"], l, state_layer)
        if caches is None:
            yb = _fox_prompt_call(per_seq(q), per_seq(k), per_seq(v), per_seq(small))
        else:
            yb = _fox_sample_call(per_seq(q), per_seq(k), per_seq(v), per_seq(small),
                                  caches["fox_k"], caches["fox_v"], caches["fox_logf_t"], l)
        yc, ssd_conv, ssd_h = _ssd_call(per_seq(xbc), per_seq(z), per_seq(small), src["ssd_conv"], src["ssd_h"],
                                        prm["ssd_cw"], prm["ssd_cb"], prm["ssd_dtb"], prm["ssd_alog"],
                                        prm["ssd_dvec"], prm["ssd_nw"], l, state_layer, ssd_chunk)
        x = _outproj_call(x, ya.reshape(bsz * n, -1), yb.reshape(bsz * n, -1), yc.reshape(bsz * n, -1),
                          mod4, prm["npost"], prm["w_out"], l, n)
        x = _ffn_call(x, mod4, prm["npre"], prm["npost"], prm["wg"], prm["wu"], prm["wd"], l, 2, 1, n)
        states["fox_k"].append(k.reshape(bsz, n, FOX_HEADS, FOX_HEAD_DIM))
        states["fox_v"].append(v.reshape(bsz, n, FOX_HEADS, FOX_HEAD_DIM))
        states["fox_logf"].append(small.reshape(bsz, n, LANES)[:, :, F_LANE0:F_LANE0 + FOX_HEADS])
        states["lru_conv"].append(lru_conv)
        states["lru_h"].append(lru_h.reshape(bsz, LRU_WIDTH))
        states["ssd_conv"].append(ssd_conv)
        states["ssd_h"].append(ssd_h.reshape(bsz, SSD_HEADS, SSD_HEAD_DIM, D_STATE))
    return x.reshape(bsz, n, D_MODEL), {name: jnp.stack(vals, axis=0) for name, vals in states.items()}


def kernel(x_prompt, x_sample, c_prompt, c_sample, cache_fox_k, cache_fox_v, cache_fox_logf, state_lru_conv, state_lru_h, state_ssd_conv, state_ssd_h, w_mod, b_mod, norm_pre, norm_post, ffn_w_gate, ffn_w_up, ffn_w_down, w_in, w_out, lru_conv_w, lru_conv_b, lru_wa, lru_ba, lru_wx, lru_bx, lru_lambda, fox_f_bias, ssd_conv_w, ssd_conv_b, ssd_dt_bias, ssd_a_log, ssd_d, ssd_norm_w):
    n_prompt, n_sample = x_prompt.shape[0], x_sample.shape[0]
    prm = {
        "npre": norm_pre.reshape(DEPTH, N_SUB, 1, D_MODEL),
        "npost": norm_post.reshape(DEPTH, N_SUB, 1, D_MODEL),
        "wg": ffn_w_gate.astype(BF16), "wu": ffn_w_up.astype(BF16), "wd": ffn_w_down.astype(BF16),
        "w_in": _permute_w_in(w_in), "w_out": w_out.astype(BF16),
        "fbias": _lane_slab(fox_f_bias, F_LANE0),
        "lru_cw": lru_conv_w, "lru_cb": lru_conv_b.reshape(DEPTH, 1, LRU_WIDTH),
        "lru_wgate": jnp.concatenate([_block_diag(lru_wa), _block_diag(lru_wx)], axis=-1).astype(BF16),
        "lru_bgate": jnp.concatenate([lru_ba, lru_bx], axis=-1).reshape(DEPTH, 1, 2 * LRU_WIDTH),
        "lru_lam": lru_lambda.reshape(DEPTH, 1, LRU_WIDTH),
        "ssd_cw": ssd_conv_w, "ssd_cb": ssd_conv_b.reshape(DEPTH, 1, SSD_CONV_DIM),
        "ssd_dtb": _lane_slab(ssd_dt_bias, DT_LANE0), "ssd_alog": _lane_slab(ssd_a_log, DT_LANE0),
        "ssd_dvec": jnp.repeat(ssd_d, SSD_HEAD_DIM, axis=-1).reshape(DEPTH, 1, SSD_WIDTH),
        "ssd_nw": ssd_norm_w.reshape(DEPTH, 1, SSD_WIDTH),
        "zero_state": {
            "lru_conv": jnp.zeros((1, SUBLANES, LRU_WIDTH), F32), "lru_h": jnp.zeros((1, 1, LRU_WIDTH), F32),
            "ssd_conv": jnp.zeros((1, SUBLANES, SSD_CONV_DIM), F32), "ssd_h": jnp.zeros((1, SSD_WIDTH, D_STATE), F32),
        },
    }
    caches = {
        "fox_k": cache_fox_k.reshape(cache_fox_k.shape[:3] + (FOX_WIDTH,)),
        "fox_v": cache_fox_v.reshape(cache_fox_v.shape[:3] + (FOX_WIDTH,)),
        "fox_logf_t": jnp.swapaxes(cache_fox_logf, 2, 3),
        "lru_conv": _pad_history(state_lru_conv),
        "lru_h": state_lru_h.reshape(DEPTH, n_sample, 1, LRU_WIDTH),
        "ssd_conv": _pad_history(state_ssd_conv),
        "ssd_h": state_ssd_h.reshape(DEPTH, n_sample, SSD_WIDTH, D_STATE),
    }
    mod = _mod_call(jnp.concatenate([c_prompt, c_sample], axis=0), w_mod, b_mod)
    y_prompt, sp = _trunk(x_prompt, mod[:, :n_prompt], None, prm, ssd_chunk=256)
    y_sample, ss = _trunk(x_sample, mod[:, n_prompt:], caches, prm, ssd_chunk=x_sample.shape[1])
    names = ("fox_k", "fox_v", "fox_logf", "lru_conv", "lru_h", "ssd_conv", "ssd_h")
    return (y_prompt, y_sample) + tuple(sp[n] for n in names) + tuple(ss[n] for n in names)
```

```python
import functools

import numpy as np
import jax
import jax.numpy as jnp
from jax import lax
from jax.experimental import pallas as pl
from jax.experimental.pallas import tpu as pltpu

F32 = jnp.float32
BF16 = jnp.bfloat16

D_MODEL = 1024
DEPTH = 2
CONV_W = 4
EPS = 1e-6
LRU_WIDTH = 256
LRU_HEADS = 4
LRU_BLOCK = LRU_WIDTH // LRU_HEADS
LRU_C = 8.0
FOX_HEADS = 8
FOX_HEAD_DIM = 64
FOX_WIDTH = FOX_HEADS * FOX_HEAD_DIM
SSD_HEADS = 4
SSD_HEAD_DIM = 64
SSD_WIDTH = SSD_HEADS * SSD_HEAD_DIM
SSD_GROUPS = 2
D_STATE = 128
SSD_CONV_DIM = SSD_WIDTH + 2 * SSD_GROUPS * D_STATE
IN_SIZES = (LRU_WIDTH, LRU_WIDTH, FOX_WIDTH, FOX_WIDTH, FOX_WIDTH, FOX_HEADS, SSD_WIDTH, SSD_CONV_DIM, SSD_HEADS)
D_FF = 2816
N_SUB = 3

LANES = 128
SUBLANES = 8
HEAD_PAIR = LANES // FOX_HEAD_DIM

F_LANE0 = 0
DT_LANE0 = FOX_HEADS
COL_LRU_X = 0
COL_LRU_G = COL_LRU_X + LRU_WIDTH
COL_Q = COL_LRU_G + LRU_WIDTH
COL_K = COL_Q + FOX_WIDTH
COL_V = COL_K + FOX_WIDTH
COL_Z = COL_V + FOX_WIDTH
COL_XBC = COL_Z + SSD_WIDTH
COL_SMALL = COL_XBC + SSD_CONV_DIM
D_IN_PAD = COL_SMALL + LANES

ROW_TILE = 512
FF_CHUNK = 512
ATTN_BLOCK = 512
ATTN_ROWS = 128
SCAN_BLOCK = 256
NEG_BIG = -1e30

_HI = lax.Precision.HIGHEST


def _dot(a, b):
    return jnp.dot(a, b, preferred_element_type=F32)


def _dot_nt(a, b):
    return lax.dot_general(a, b, (((1,), (1,)), ((), ())), preferred_element_type=F32)


def _dot_tn(a, b):
    return lax.dot_general(a, b, (((0,), (0,)), ((), ())), preferred_element_type=F32)


def _silu(x):
    return x * jax.nn.sigmoid(x)


def _softplus(x):
    return jnp.maximum(x, 0.0) + jnp.log1p(jnp.exp(-jnp.abs(x)))


def _rms(x):
    return x * lax.rsqrt(jnp.mean(x * x, axis=-1, keepdims=True) + EPS)


def _per_seq(rows, per_seq, fn):
    g = per_seq[0].shape[0]
    if g == 1:
        return fn(rows, *per_seq)
    tm, d = rows.shape
    out = fn(rows.reshape(g, tm // g, d), *[p[:, None, :] for p in per_seq])
    return out.reshape(tm, d)


def _pre_norm(x, npre, mod_ref):
    h = _rms(x) * npre
    return _per_seq(h, (mod_ref[:, 1, :], mod_ref[:, 0, :]), lambda r, sc, sh: r * (1.0 + sc) + sh)


def _post_norm(x, y, npost, mod_ref, w):
    yn = _rms(y) * npost
    return x + _per_seq(yn, (mod_ref[:, 2, :],), lambda r, gt: (w * gt) * r)


def _seq_grouping(rows_per_seq, tm):
    if rows_per_seq % tm == 0:
        return 1, rows_per_seq // tm
    assert tm % rows_per_seq == 0
    return tm // rows_per_seq, 1


def _mod_spec(g, tiles_per_seq, sub):
    if g == 1:
        return pl.BlockSpec((1, None, 3, D_MODEL), lambda i: (i // tiles_per_seq, sub, 0, 0))
    return pl.BlockSpec((g, None, 3, D_MODEL), lambda i: (i, sub, 0, 0))


def _params(vmem_mb, sem):
    return pltpu.CompilerParams(dimension_semantics=sem, vmem_limit_bytes=vmem_mb << 20)


def _mod_kernel(c_ref, w_ref, b_ref, o_ref):
    a = _silu(c_ref[...]).astype(BF16)
    o_ref[...] = _dot(a, w_ref[...].astype(BF16)) + b_ref[...]


def _mod_call(c_all, w_mod, b_mod):
    nseq = c_all.shape[0]
    width = N_SUB * 3 * D_MODEL
    tn = 1024
    return pl.pallas_call(
        _mod_kernel,
        grid=(DEPTH, width // tn),
        in_specs=[
            pl.BlockSpec((nseq, D_MODEL), lambda l, n: (0, 0)),
            pl.BlockSpec((None, D_MODEL, tn), lambda l, n: (l, 0, n)),
            pl.BlockSpec((None, 1, tn), lambda l, n: (l, 0, n)),
        ],
        out_specs=pl.BlockSpec((None, nseq, tn), lambda l, n: (l, 0, n)),
        out_shape=jax.ShapeDtypeStruct((DEPTH, nseq, width), F32),
        compiler_params=_params(24, ("arbitrary", "arbitrary")),
        name="adaln_mod",
    )(c_all, w_mod, b_mod.reshape(DEPTH, 1, width))


def _ffn_kernel(x_ref, mod_ref, npre_ref, npost_ref, wg_ref, wu_ref, wd_ref, o_ref):
    x = x_ref[...]
    h = _pre_norm(x, npre_ref[...], mod_ref).astype(BF16)
    acc = None
    for off in range(0, D_FF, FF_CHUNK):
        fc = min(FF_CHUNK, D_FF - off)
        g = _dot(h, wg_ref[:, off:off + fc])
        u = _dot(h, wu_ref[:, off:off + fc])
        a = (_silu(g) * u).astype(BF16)
        d = _dot(a, wd_ref[off:off + fc, :])
        acc = d if acc is None else acc + d
    o_ref[...] = _post_norm(x, acc, npost_ref[...], mod_ref, 0.5)


def _ffn_call(x, mod4, npre, npost, wg, wu, wd, layer, sub, ffn_idx, rows_per_seq):
    m = x.shape[0]
    tm = min(ROW_TILE, m)
    g, tps = _seq_grouping(rows_per_seq, tm)
    wspec = lambda shape: pl.BlockSpec((None, None) + shape, lambda i: (layer, ffn_idx, 0, 0),
                                       pipeline_mode=pl.Buffered(1))
    nspec = pl.BlockSpec((None, None, 1, D_MODEL), lambda i: (layer, sub, 0, 0))
    return pl.pallas_call(
        _ffn_kernel,
        grid=(m // tm,),
        in_specs=[
            pl.BlockSpec((tm, D_MODEL), lambda i: (i, 0)),
            _mod_spec(g, tps, sub),
            nspec, nspec,
            wspec((D_MODEL, D_FF)), wspec((D_MODEL, D_FF)), wspec((D_FF, D_MODEL)),
        ],
        out_specs=pl.BlockSpec((tm, D_MODEL), lambda i: (i, 0)),
        out_shape=jax.ShapeDtypeStruct((m, D_MODEL), F32),
        compiler_params=_params(48, ("arbitrary",)),
        name="ffn",
    )(x, mod4, npre, npost, wg, wu, wd)


def _inproj_kernel(x_ref, mod_ref, npre_ref, w_ref, fbias_ref,
                   lrux_ref, lrug_ref, q_ref, k_ref, v_ref, z_ref, xbc_ref, small_ref):
    h = _pre_norm(x_ref[...], npre_ref[...], mod_ref).astype(BF16)
    col = lambda start, width: _dot(h, w_ref[:, start:start + width])
    lrux_ref[...] = col(COL_LRU_X, LRU_WIDTH)
    lrug_ref[...] = col(COL_LRU_G, LRU_WIDTH)
    q_ref[...] = (col(COL_Q, FOX_WIDTH) * (FOX_HEAD_DIM ** -0.5)).astype(BF16)
    k_ref[...] = col(COL_K, FOX_WIDTH)
    v_ref[...] = col(COL_V, FOX_WIDTH)
    z_ref[...] = col(COL_Z, SSD_WIDTH)
    xbc_ref[...] = col(COL_XBC, SSD_CONV_DIM)
    small = col(COL_SMALL, LANES)
    t = small + fbias_ref[...]
    logf = jnp.minimum(t, 0.0) - jnp.log1p(jnp.exp(-jnp.abs(t)))
    lane = lax.broadcasted_iota(jnp.int32, small.shape, 1)
    small_ref[...] = jnp.where(lane < DT_LANE0, logf, small)


def _inproj_call(x, mod4, npre, w_in_p, fbias, layer, rows_per_seq):
    m = x.shape[0]
    tm = min(ROW_TILE, m)
    g, tps = _seq_grouping(rows_per_seq, tm)
    widths = (LRU_WIDTH, LRU_WIDTH, FOX_WIDTH, FOX_WIDTH, FOX_WIDTH, SSD_WIDTH, SSD_CONV_DIM, LANES)
    dtypes = (F32, F32, BF16, F32, F32, F32, F32, F32)
    return pl.pallas_call(
        _inproj_kernel,
        grid=(m // tm,),
        in_specs=[
            pl.BlockSpec((tm, D_MODEL), lambda i: (i, 0)),
            _mod_spec(g, tps, 1),
            pl.BlockSpec((None, None, 1, D_MODEL), lambda i: (layer, 1, 0, 0)),
            pl.BlockSpec((None, D_MODEL, D_IN_PAD), lambda i: (layer, 0, 0)),
            pl.BlockSpec((None, 1, LANES), lambda i: (layer, 0, 0)),
        ],
        out_specs=[pl.BlockSpec((tm, w), lambda i: (i, 0)) for w in widths],
        out_shape=[jax.ShapeDtypeStruct((m, w), dt) for w, dt in zip(widths, dtypes)],
        compiler_params=_params(48, ("arbitrary",)),
        name="inproj",
    )(x, mod4, npre, w_in_p, fbias)


def _outproj_kernel(x_ref, ya_ref, yb_ref, yc_ref, mod_ref, npost_ref, w_ref, o_ref):
    y = (_dot(ya_ref[...], w_ref[0:LRU_WIDTH, :])
         + _dot(yb_ref[...], w_ref[LRU_WIDTH:LRU_WIDTH + FOX_WIDTH, :])
         + _dot(yc_ref[...], w_ref[LRU_WIDTH + FOX_WIDTH:, :]))
    o_ref[...] = _post_norm(x_ref[...], y, npost_ref[...], mod_ref, 1.0)


def _outproj_call(x, ya, yb, yc, mod4, npost, w_out, layer, rows_per_seq):
    m = x.shape[0]
    tm = min(ROW_TILE, m)
    g, tps = _seq_grouping(rows_per_seq, tm)
    row = lambda w: pl.BlockSpec((tm, w), lambda i: (i, 0))
    return pl.pallas_call(
        _outproj_kernel,
        grid=(m // tm,),
        in_specs=[
            row(D_MODEL), row(LRU_WIDTH), row(FOX_WIDTH), row(SSD_WIDTH),
            _mod_spec(g, tps, 1),
            pl.BlockSpec((None, None, 1, D_MODEL), lambda i: (layer, 1, 0, 0)),
            pl.BlockSpec((None, D_MODEL, D_MODEL), lambda i: (layer, 0, 0)),
        ],
        out_specs=row(D_MODEL),
        out_shape=jax.ShapeDtypeStruct((m, D_MODEL), F32),
        compiler_params=_params(32, ("arbitrary",)),
        name="outproj",
    )(x, ya, yb, yc, mod4, npost, w_out)


def _causal_conv(xx, cw, cb, n):
    u = cb + cw[0:1] * pltpu.roll(xx, 3, 0)[SUBLANES:SUBLANES + n]
    u = u + cw[1:2] * pltpu.roll(xx, 2, 0)[SUBLANES:SUBLANES + n]
    u = u + cw[2:3] * pltpu.roll(xx, 1, 0)[SUBLANES:SUBLANES + n]
    return u + cw[3:4] * xx[SUBLANES:SUBLANES + n]


def _last_rows(xx, k):
    return pltpu.roll(xx, k, 0)[0:SUBLANES][0:k]


def _tri(n, lower):
    r = lax.broadcasted_iota(jnp.int32, (n, n), 0)
    c = lax.broadcasted_iota(jnp.int32, (n, n), 1)
    return ((r >= c) if lower else (r <= c)).astype(F32)


def _lru_kernel(x_ref, g_ref, prev_ref, h0_ref, cw_ref, cb_ref, wgate_ref, bgate_ref, lam_ref,
                y_ref, convnew_ref, hnew_ref, a_scr, b_scr):
    n = x_ref.shape[0]
    xx = jnp.concatenate([prev_ref[...], x_ref[...]], axis=0)
    convnew_ref[...] = _last_rows(xx, CONV_W - 1)
    u = _causal_conv(xx, cw_ref[...], cb_ref[...], n)
    gates = _dot(u.astype(BF16), wgate_ref[...]) + bgate_ref[...]
    r = jax.nn.sigmoid(gates[:, :LRU_WIDTH])
    i = jax.nn.sigmoid(gates[:, LRU_WIDTH:])
    log_a = (-LRU_C * r) * _softplus(-lam_ref[...])
    a = jnp.exp(log_a)
    b = jnp.sqrt(-jnp.tanh(log_a) * (a * a + 1.0)) * (i * u)
    row = lax.broadcasted_iota(jnp.int32, (n, 1), 0) % SUBLANES
    for d in (1, 2, 4):
        keep = row >= d
        b = jnp.where(keep, a * pltpu.roll(b, d, 0) + b, b)
        a = jnp.where(keep, a * pltpu.roll(a, d, 0), a)
    a_scr[...] = a
    b_scr[...] = b

    def group(j, h):
        off = pl.multiple_of(j * SUBLANES, SUBLANES)
        hb = a_scr[pl.ds(off, SUBLANES), :] * h + b_scr[pl.ds(off, SUBLANES), :]
        b_scr[pl.ds(off, SUBLANES), :] = hb
        return jnp.broadcast_to(hb[SUBLANES - 1:SUBLANES, :], hb.shape)

    h_last = lax.fori_loop(0, n // SUBLANES, group,
                           jnp.broadcast_to(h0_ref[...], (SUBLANES, LRU_WIDTH)), unroll=4)
    hnew_ref[...] = h_last[0:1]
    y_ref[...] = (b_scr[...] * jax.nn.gelu(g_ref[...])).astype(BF16)


def _lru_call(lru_x, lru_g, prev8, h0, cw, cb, wgate, bgate, lam, layer, state_layer):
    bsz, n, _ = lru_x.shape
    seq = pl.BlockSpec((None, n, LRU_WIDTH), lambda b: (b, 0, 0))
    if state_layer is None:
        prev_spec = pl.BlockSpec((None, SUBLANES, LRU_WIDTH), lambda b: (0, 0, 0))
        h0_spec = pl.BlockSpec((None, 1, LRU_WIDTH), lambda b: (0, 0, 0))
    else:
        prev_spec = pl.BlockSpec((None, None, SUBLANES, LRU_WIDTH), lambda b: (state_layer, b, 0, 0))
        h0_spec = pl.BlockSpec((None, None, 1, LRU_WIDTH), lambda b: (state_layer, b, 0, 0))
    par = lambda r, w: pl.BlockSpec((None, r, w), lambda b: (layer, 0, 0))
    return pl.pallas_call(
        _lru_kernel,
        grid=(bsz,),
        in_specs=[seq, seq, prev_spec, h0_spec, par(CONV_W, LRU_WIDTH), par(1, LRU_WIDTH),
                  par(LRU_WIDTH, 2 * LRU_WIDTH), par(1, 2 * LRU_WIDTH), par(1, LRU_WIDTH)],
        out_specs=[seq,
                   pl.BlockSpec((None, CONV_W - 1, LRU_WIDTH), lambda b: (b, 0, 0)),
                   pl.BlockSpec((None, 1, LRU_WIDTH), lambda b: (b, 0, 0))],
        out_shape=[jax.ShapeDtypeStruct((bsz, n, LRU_WIDTH), BF16),
                   jax.ShapeDtypeStruct((bsz, CONV_W - 1, LRU_WIDTH), F32),
                   jax.ShapeDtypeStruct((bsz, 1, LRU_WIDTH), F32)],
        scratch_shapes=[pltpu.VMEM((n, LRU_WIDTH), F32), pltpu.VMEM((n, LRU_WIDTH), F32)],
        compiler_params=_params(40, ("arbitrary",)),
        name="rg_lru",
    )(lru_x, lru_g, prev8, h0, cw, cb, wgate, bgate, lam)


def _ssd_kernel(xbc_ref, z_ref, small_ref, prev_ref, h0_ref, cw_ref, cb_ref, dtb_ref, alog_ref, dvec_ref, nw_ref,
                y_ref, convnew_ref, h_ref, tail_scr):
    n = xbc_ref.shape[0]
    c = pl.program_id(1)

    @pl.when(c == 0)
    def _():
        tail_scr[...] = prev_ref[...]
        h_ref[...] = h0_ref[...]

    xx = jnp.concatenate([tail_scr[...], xbc_ref[...]], axis=0)
    tail_scr[...] = xx[n:n + SUBLANES]
    convnew_ref[...] = _last_rows(xx, CONV_W - 1)
    act = _silu(_causal_conv(xx, cw_ref[...], cb_ref[...], n))
    xs = act[:, :SSD_WIDTH]
    bm = act[:, SSD_WIDTH:SSD_WIDTH + SSD_GROUPS * D_STATE]
    cm = act[:, SSD_WIDTH + SSD_GROUPS * D_STATE:]

    dt = _softplus(small_ref[...] + dtb_ref[...])
    dta = dt * (-jnp.exp(alog_ref[...]))
    cum = jnp.dot(_tri(n, True), dta, preferred_element_type=F32, precision=_HI)
    cum_t = cum.T
    rr = lax.broadcasted_iota(jnp.int32, (n, n), 0)
    cc = lax.broadcasted_iota(jnp.int32, (n, n), 1)
    causal = rr >= cc
    lo_lane = lax.broadcasted_iota(jnp.int32, (1, LANES), 1) < SSD_HEAD_DIM
    lo_row = lax.broadcasted_iota(jnp.int32, (LANES, 1), 0) < SSD_HEAD_DIM
    dvec = dvec_ref[...]

    ys = []
    for g in range(SSD_GROUPS):
        sl = slice(g * LANES, (g + 1) * LANES)
        xg, bg, cg = xs[:, sl], bm[:, sl].astype(BF16), cm[:, sl].astype(BF16)
        heads = (2 * g, 2 * g + 1)
        col = lambda a, h: a[:, DT_LANE0 + h:DT_LANE0 + h + 1]
        pick = lambda f: jnp.where(lo_lane, f(heads[0]), f(heads[1]))
        dx = xg * pick(lambda h: col(dt, h))
        dxb = dx.astype(BF16)
        cb_mat = _dot_nt(cg, bg)
        yd = []
        for h in heads:
            seg = col(cum, h) - cum_t[DT_LANE0 + h:DT_LANE0 + h + 1, :]
            lmat = jnp.exp(jnp.where(causal, seg, NEG_BIG))
            yd.append(_dot((cb_mat * lmat).astype(BF16), dxb))
        y_diag = jnp.where(lo_lane, yd[0], yd[1])
        last = lambda h: col(cum, h)[n - 1:n, :]
        decay_end = pick(lambda h: jnp.exp(last(h) - col(cum, h)))
        states = _dot_tn((dx * decay_end).astype(BF16), bg)
        h_prev = h_ref[sl, :]
        y_off = _dot_nt(cg, h_prev.astype(BF16)) * pick(lambda h: jnp.exp(col(cum, h)))
        chunk_decay = jnp.where(lo_row, jnp.exp(last(heads[0])), jnp.exp(last(heads[1])))
        h_ref[sl, :] = chunk_decay * h_prev + states
        ys.append(y_diag + y_off + dvec[:, sl] * xg)
    y = jnp.concatenate(ys, axis=1)
    y_ref[...] = (_rms(y * _silu(z_ref[...])) * nw_ref[...]).astype(BF16)


def _ssd_call(xbc, z, small, prev8, h0, cw, cb, dtb, alog, dvec, nw, layer, state_layer, chunk):
    bsz, n, _ = xbc.shape
    nc = n // chunk
    seq = lambda w: pl.BlockSpec((None, chunk, w), lambda b, c: (b, c, 0))
    if state_layer is None:
        prev_spec = pl.BlockSpec((None, SUBLANES, SSD_CONV_DIM), lambda b, c: (0, 0, 0))
        h0_spec = pl.BlockSpec((None, SSD_WIDTH, D_STATE), lambda b, c: (0, 0, 0))
    else:
        prev_spec = pl.BlockSpec((None, None, SUBLANES, SSD_CONV_DIM), lambda b, c: (state_layer, b, 0, 0))
        h0_spec = pl.BlockSpec((None, None, SSD_WIDTH, D_STATE), lambda b, c: (state_layer, b, 0, 0))
    par = lambda r, w: pl.BlockSpec((None, r, w), lambda b, c: (layer, 0, 0))
    return pl.pallas_call(
        _ssd_kernel,
        grid=(bsz, nc),
        in_specs=[seq(SSD_CONV_DIM), seq(SSD_WIDTH), seq(LANES), prev_spec, h0_spec,
                  par(CONV_W, SSD_CONV_DIM), par(1, SSD_CONV_DIM), par(1, LANES), par(1, LANES),
                  par(1, SSD_WIDTH), par(1, SSD_WIDTH)],
        out_specs=[seq(SSD_WIDTH),
                   pl.BlockSpec((None, CONV_W - 1, SSD_CONV_DIM), lambda b, c: (b, 0, 0)),
                   pl.BlockSpec((None, SSD_WIDTH, D_STATE), lambda b, c: (b, 0, 0))],
        out_shape=[jax.ShapeDtypeStruct((bsz, n, SSD_WIDTH), BF16),
                   jax.ShapeDtypeStruct((bsz, CONV_W - 1, SSD_CONV_DIM), F32),
                   jax.ShapeDtypeStruct((bsz, SSD_WIDTH, D_STATE), F32)],
        scratch_shapes=[pltpu.VMEM((SUBLANES, SSD_CONV_DIM), F32)],
        compiler_params=_params(32, ("arbitrary", "arbitrary")),
        name="ssd",
    )(xbc, z, small, prev8, h0, cw, cb, dtb, alog, dvec, nw)


def _fox_prompt_kernel(q_ref, k_ref, v_ref, logf_ref, o_ref, kb_scr, vb_scr, f_scr, ft_scr):
    s = k_ref.shape[0]
    tq = q_ref.shape[0]
    p = pl.program_id(1)
    qi = pl.program_id(2)

    @pl.when(qi == 0)
    def _():
        kb_scr[...] = k_ref[...].astype(BF16)
        vb_scr[...] = v_ref[...].astype(BF16)

    @pl.when((qi == 0) & (p == 0))
    def _():
        tri = _tri(SCAN_BLOCK, True)
        carry = jnp.zeros((1, LANES), F32)
        for c in range(s // SCAN_BLOCK):
            rows = slice(c * SCAN_BLOCK, (c + 1) * SCAN_BLOCK)
            fc = jnp.dot(tri, logf_ref[rows, :], preferred_element_type=F32, precision=_HI) + carry
            f_scr[rows, :] = fc
            per = tq // SCAN_BLOCK
            ft_scr[c // per, :, (c % per) * SCAN_BLOCK:(c % per + 1) * SCAN_BLOCK] = fc.T
            carry = fc[SCAN_BLOCK - 1:SCAN_BLOCK, :]

    lane = lax.broadcasted_iota(jnp.int32, (1, LANES), 1)
    lo_lane = lane < FOX_HEAD_DIM
    heads = (HEAD_PAIR * p, HEAD_PAIR * p + 1)
    q0 = pl.multiple_of(qi * tq, tq)
    rb = min(ATTN_ROWS, tq)
    nrb = tq // rb
    q_blocks, fq_blocks = [], []
    for r in range(nrb):
        q = q_ref[r * rb:(r + 1) * rb, :]
        q_blocks.append((jnp.where(lo_lane, q, jnp.zeros_like(q)), jnp.where(lo_lane, jnp.zeros_like(q), q)))
        f_rows = f_scr[pl.ds(pl.multiple_of(q0 + r * rb, rb), rb), :]
        fq_blocks.append([jnp.sum(jnp.where(lane == F_LANE0 + h, f_rows, 0.0), axis=1, keepdims=True) for h in heads])

    def softmax_unit(dots, f_q, f_k, state, row0):
        m_old, l_old = state
        t = dots - f_k
        if row0 is not None:
            rr = lax.broadcasted_iota(jnp.int32, t.shape, 0) + row0
            cc = lax.broadcasted_iota(jnp.int32, t.shape, 1)
            t = jnp.where(cc <= rr, t, NEG_BIG)
        m_new = jnp.maximum(m_old, jnp.max(t, axis=1, keepdims=True) + f_q)
        alpha = jnp.exp(m_old - m_new)
        pr = jnp.exp(t - (m_new - f_q))
        l_new = alpha * l_old + jnp.sum(pr, axis=1, keepdims=True)
        return m_new, l_new, alpha, pr.astype(BF16)

    def step(j, carry, diagonal):
        k0 = pl.multiple_of(j * tq, tq)
        f_keys = [ft_scr[j, pl.ds(F_LANE0 + h, 1), :] for h in heads]
        nks = [(r + 1) * rb if diagonal else tq for r in range(nrb)]
        units = [(r, hl) for r in range(nrb) for hl in range(HEAD_PAIR)]
        dots = {u: _dot_nt(q_blocks[u[0]][u[1]], kb_scr[pl.ds(k0, nks[u[0]]), :]) for u in units}
        soft = {u: softmax_unit(dots[u], fq_blocks[u[0]][u[1]], f_keys[u[1]][:, 0:nks[u[0]]],
                                carry[u[0]][2 * u[1]:2 * u[1] + 2], u[0] * rb if diagonal else None)
                for u in units}
        pvs = {u: _dot(soft[u][3], vb_scr[pl.ds(k0, nks[u[0]]), :]) for u in units}
        new = []
        for r in range(nrb):
            a, b = soft[(r, 0)], soft[(r, 1)]
            acc = jnp.where(lo_lane, a[2], b[2]) * carry[r][4] + jnp.where(lo_lane, pvs[(r, 0)], pvs[(r, 1)])
            new.append((a[0], a[1], b[0], b[1], acc))
        return tuple(new)

    init_rows = (jnp.full((rb, 1), NEG_BIG, F32), jnp.zeros((rb, 1), F32)) * HEAD_PAIR + (jnp.zeros((rb, LANES), F32),)
    carry = lax.fori_loop(0, qi, lambda j, cr: step(j, cr, False), (init_rows,) * nrb)
    carry = step(qi, carry, True)
    for r in range(nrb):
        o_ref[r * rb:(r + 1) * rb, :] = (carry[r][4] / jnp.where(lo_lane, carry[r][1], carry[r][3])).astype(BF16)


def _fox_prompt_call(q, k, v, small):
    bsz, s, _ = q.shape
    tq = min(ATTN_BLOCK, s)
    blk = pl.BlockSpec((None, tq, LANES), lambda b, p, i: (b, i, p))
    full = pl.BlockSpec((None, s, LANES), lambda b, p, i: (b, 0, p))
    return pl.pallas_call(
        _fox_prompt_kernel,
        grid=(bsz, FOX_WIDTH // LANES, s // tq),
        in_specs=[blk, full, full, pl.BlockSpec((None, s, LANES), lambda b, p, i: (b, 0, 0))],
        out_specs=blk,
        out_shape=jax.ShapeDtypeStruct((bsz, s, FOX_WIDTH), BF16),
        scratch_shapes=[pltpu.VMEM((s, LANES), BF16), pltpu.VMEM((s, LANES), BF16),
                        pltpu.VMEM((s, LANES), F32), pltpu.VMEM((s // tq, LANES, tq), F32)],
        compiler_params=_params(40, ("arbitrary", "arbitrary", "arbitrary")),
        name="fox_prompt",
    )(q, k, v, small)


def _fox_sample_kernel(q_ref, k_ref, v_ref, logf_ref, ck_ref, cv_ref, clogft_ref, o_ref,
                       m_scr, l_scr, acc_scr, fk_scr, fq_scr, fnew_scr):
    t = q_ref.shape[0]
    c = pl.program_id(1)
    nchunk, _, chunk = fk_scr.shape

    @pl.when(c == 0)
    def _():
        blk = min(SCAN_BLOCK, chunk)
        triu = _tri(blk, False)
        carry = jnp.zeros((FOX_HEADS, 1), F32)
        for i in range(nchunk):
            for j in range(chunk // blk):
                cols = slice(j * blk, (j + 1) * blk)
                fc = jnp.dot(clogft_ref[:, i * chunk + j * blk:i * chunk + (j + 1) * blk], triu,
                             preferred_element_type=F32, precision=_HI) + carry
                fk_scr[i, :, cols] = fc
                carry = fc[:, blk - 1:blk]
        sub = lax.broadcasted_iota(jnp.int32, (FOX_HEADS, LANES), 0)
        ln = lax.broadcasted_iota(jnp.int32, (FOX_HEADS, LANES), 1)
        total_row = jnp.sum(jnp.where(sub == ln - F_LANE0, carry, 0.0), axis=0, keepdims=True)
        f_new = jnp.dot(_tri(t, True), logf_ref[...], preferred_element_type=F32, precision=_HI) + total_row
        fnew_scr[...] = f_new.T[F_LANE0:F_LANE0 + FOX_HEADS, :]
        fq_scr[...] = jnp.concatenate([f_new[:, F_LANE0 + h:F_LANE0 + h + 1] for h in range(FOX_HEADS)], axis=0)
        m_scr[...] = jnp.full(m_scr.shape, NEG_BIG, F32)
        l_scr[...] = jnp.zeros(l_scr.shape, F32)
        acc_scr[...] = jnp.zeros(acc_scr.shape, F32)

    q = q_ref[...]
    q_heads = [q[:, h * FOX_HEAD_DIM:(h + 1) * FOX_HEAD_DIM] for h in range(FOX_HEADS)]
    f_q = fq_scr[...]

    def update(keys, values, f_k, mask):
        tt = jnp.concatenate([_dot_nt(q_heads[h], keys[h]) - f_k[h:h + 1, :] for h in range(FOX_HEADS)], axis=0)
        if mask is not None:
            tt = jnp.where(mask, tt, NEG_BIG)
        m_old = m_scr[...]
        m_new = jnp.maximum(m_old, jnp.max(tt, axis=1, keepdims=True) + f_q)
        alpha = jnp.exp(m_old - m_new)
        pr = jnp.exp(tt - (m_new - f_q))
        l_scr[...] = alpha * l_scr[...] + jnp.sum(pr, axis=1, keepdims=True)
        prb = pr.astype(BF16)
        pv = jnp.concatenate([_dot(prb[h * t:(h + 1) * t, :], values[h]) for h in range(FOX_HEADS)], axis=0)
        acc_scr[...] = alpha * acc_scr[...] + pv
        m_scr[...] = m_new

    head_rows = lambda ref: [ref[pl.ds(h, chunk, stride=FOX_HEADS), :].astype(BF16) for h in range(FOX_HEADS)]
    update(head_rows(ck_ref), head_rows(cv_ref), fk_scr[c], None)

    @pl.when(c == nchunk - 1)
    def _():
        k_new, v_new = k_ref[...].astype(BF16), v_ref[...].astype(BF16)
        cols = lambda a: [a[:, h * FOX_HEAD_DIM:(h + 1) * FOX_HEAD_DIM] for h in range(FOX_HEADS)]
        rr = lax.broadcasted_iota(jnp.int32, (FOX_HEADS * t, t), 0) % t
        cc = lax.broadcasted_iota(jnp.int32, (FOX_HEADS * t, t), 1)
        update(cols(k_new), cols(v_new), fnew_scr[...], cc <= rr)
        out = acc_scr[...] / l_scr[...]
        o_ref[...] = jnp.concatenate([out[h * t:(h + 1) * t, :] for h in range(FOX_HEADS)], axis=1).astype(BF16)


def _fox_sample_call(q, k, v, small, cache_k, cache_v, cache_logf_t, layer):
    bsz, t, _ = q.shape
    past = cache_logf_t.shape[3]
    chunk = min(1024, past)
    nchunk = past // chunk
    new = lambda w: pl.BlockSpec((None, t, w), lambda b, c: (b, 0, 0))
    cache = pl.BlockSpec((None, None, chunk * FOX_HEADS, FOX_HEAD_DIM), lambda b, c: (layer, b, c, 0))
    rows = FOX_HEADS * t
    return pl.pallas_call(
        _fox_sample_kernel,
        grid=(bsz, nchunk),
        in_specs=[new(FOX_WIDTH), new(FOX_WIDTH), new(FOX_WIDTH), new(LANES), cache, cache,
                  pl.BlockSpec((None, None, FOX_HEADS, past), lambda b, c: (layer, b, 0, 0))],
        out_specs=new(FOX_WIDTH),
        out_shape=jax.ShapeDtypeStruct((bsz, t, FOX_WIDTH), BF16),
        scratch_shapes=[pltpu.VMEM((rows, 1), F32), pltpu.VMEM((rows, 1), F32), pltpu.VMEM((rows, FOX_HEAD_DIM), F32),
                        pltpu.VMEM((nchunk, FOX_HEADS, chunk), F32), pltpu.VMEM((rows, 1), F32),
                        pltpu.VMEM((FOX_HEADS, t), F32)],
        compiler_params=_params(40, ("arbitrary", "arbitrary")),
        name="fox_sample",
    )(q, k, v, small, cache_k, cache_v, cache_logf_t)


def _permute_w_in(w_in):
    offs = np.concatenate([[0], np.cumsum(IN_SIZES)])
    seg = lambda i: w_in[..., int(offs[i]):int(offs[i + 1])]
    pad = jnp.zeros(w_in.shape[:-1] + (LANES - FOX_HEADS - SSD_HEADS,), w_in.dtype)
    small = jnp.concatenate([seg(5), seg(8), pad], axis=-1)
    return jnp.concatenate([seg(0), seg(1), seg(2), seg(3), seg(4), seg(6), seg(7), small], axis=-1).astype(BF16)


def _block_diag(w):
    d, h, b, _ = w.shape
    eye = jnp.eye(h, dtype=w.dtype)
    return jnp.einsum("dhij,hg->dhigj", w, eye).reshape(d, h * b, h * b)


def _lane_slab(v, lane0):
    d, k = v.shape
    return jnp.zeros((d, 1, LANES), v.dtype).at[:, 0, lane0:lane0 + k].set(v)


def _pad_history(state):
    return jnp.pad(state, ((0, 0), (0, 0), (SUBLANES - (CONV_W - 1), 0), (0, 0)))


def _trunk(x, mod_group, caches, prm, ssd_chunk):
    bsz, n, _ = x.shape
    x = x.reshape(bsz * n, D_MODEL)
    states = {name: [] for name in ("fox_k", "fox_v", "fox_logf", "lru_conv", "lru_h", "ssd_conv", "ssd_h")}
    for l in range(DEPTH):
        mod4 = mod_group[l].reshape(bsz, N_SUB, 3, D_MODEL)
        x = _ffn_call(x, mod4, prm["npre"], prm["npost"], prm["wg"], prm["wu"], prm["wd"], l, 0, 0, n)
        lrux, lrug, q, k, v, z, xbc, small = _inproj_call(x, mod4, prm["npre"], prm["w_in"], prm["fbias"], l, n)
        per_seq = lambda a: a.reshape(bsz, n, a.shape[-1])
        state_layer = None if caches is None else l
        src = prm["zero_state"] if caches is None else caches
        ya, lru_conv, lru_h = _lru_call(per_seq(lrux), per_seq(lrug), src["lru_conv"], src["lru_h"],
                                        prm["lru_cw"], prm["lru_cb"], prm["lru_wgate"], prm["lru_bgate"],
                                        prm["lru_lam"], l, state_layer)
        if caches is None:
            yb = _fox_prompt_call(per_seq(q), per_seq(k), per_seq(v), per_seq(small))
        else:
            yb = _fox_sample_call(per_seq(q), per_seq(k), per_seq(v), per_seq(small),
                                  caches["fox_k"], caches["fox_v"], caches["fox_logf_t"], l)
        yc, ssd_conv, ssd_h = _ssd_call(per_seq(xbc), per_seq(z), per_seq(small), src["ssd_conv"], src["ssd_h"],
                                        prm["ssd_cw"], prm["ssd_cb"], prm["ssd_dtb"], prm["ssd_alog"],
                                        prm["ssd_dvec"], prm["ssd_nw"], l, state_layer, ssd_chunk)
        x = _outproj_call(x, ya.reshape(bsz * n, -1), yb.reshape(bsz * n, -1), yc.reshape(bsz * n, -1),
                          mod4, prm["npost"], prm["w_out"], l, n)
        x = _ffn_call(x, mod4, prm["npre"], prm["npost"], prm["wg"], prm["wu"], prm["wd"], l, 2, 1, n)
        states["fox_k"].append(k.reshape(bsz, n, FOX_HEADS, FOX_HEAD_DIM))
        states["fox_v"].append(v.reshape(bsz, n, FOX_HEADS, FOX_HEAD_DIM))
        states["fox_logf"].append(small.reshape(bsz, n, LANES)[:, :, F_LANE0:F_LANE0 + FOX_HEADS])
        states["lru_conv"].append(lru_conv)
        states["lru_h"].append(lru_h.reshape(bsz, LRU_WIDTH))
        states["ssd_conv"].append(ssd_conv)
        states["ssd_h"].append(ssd_h.reshape(bsz, SSD_HEADS, SSD_HEAD_DIM, D_STATE))
    return x.reshape(bsz, n, D_MODEL), {name: jnp.stack(vals, axis=0) for name, vals in states.items()}


def kernel(x_prompt, x_sample, c_prompt, c_sample, cache_fox_k, cache_fox_v, cache_fox_logf, state_lru_conv, state_lru_h, state_ssd_conv, state_ssd_h, w_mod, b_mod, norm_pre, norm_post, ffn_w_gate, ffn_w_up, ffn_w_down, w_in, w_out, lru_conv_w, lru_conv_b, lru_wa, lru_ba, lru_wx, lru_bx, lru_lambda, fox_f_bias, ssd_conv_w, ssd_conv_b, ssd_dt_bias, ssd_a_log, ssd_d, ssd_norm_w):
    n_prompt, n_sample = x_prompt.shape[0], x_sample.shape[0]
    prm = {
        "npre": norm_pre.reshape(DEPTH, N_SUB, 1, D_MODEL),
        "npost": norm_post.reshape(DEPTH, N_SUB, 1, D_MODEL),
        "wg": ffn_w_gate.astype(BF16), "wu": ffn_w_up.astype(BF16), "wd": ffn_w_down.astype(BF16),
        "w_in": _permute_w_in(w_in), "w_out": w_out.astype(BF16),
        "fbias": _lane_slab(fox_f_bias, F_LANE0),
        "lru_cw": lru_conv_w, "lru_cb": lru_conv_b.reshape(DEPTH, 1, LRU_WIDTH),
        "lru_wgate": jnp.concatenate([_block_diag(lru_wa), _block_diag(lru_wx)], axis=-1).astype(BF16),
        "lru_bgate": jnp.concatenate([lru_ba, lru_bx], axis=-1).reshape(DEPTH, 1, 2 * LRU_WIDTH),
        "lru_lam": lru_lambda.reshape(DEPTH, 1, LRU_WIDTH),
        "ssd_cw": ssd_conv_w, "ssd_cb": ssd_conv_b.reshape(DEPTH, 1, SSD_CONV_DIM),
        "ssd_dtb": _lane_slab(ssd_dt_bias, DT_LANE0), "ssd_alog": _lane_slab(ssd_a_log, DT_LANE0),
        "ssd_dvec": jnp.repeat(ssd_d, SSD_HEAD_DIM, axis=-1).reshape(DEPTH, 1, SSD_WIDTH),
        "ssd_nw": ssd_norm_w.reshape(DEPTH, 1, SSD_WIDTH),
        "zero_state": {
            "lru_conv": jnp.zeros((1, SUBLANES, LRU_WIDTH), F32), "lru_h": jnp.zeros((1, 1, LRU_WIDTH), F32),
            "ssd_conv": jnp.zeros((1, SUBLANES, SSD_CONV_DIM), F32), "ssd_h": jnp.zeros((1, SSD_WIDTH, D_STATE), F32),
        },
    }
    caches = {
        "fox_k": cache_fox_k.reshape(cache_fox_k.shape[:2] + (-1, FOX_HEAD_DIM)),
        "fox_v": cache_fox_v.reshape(cache_fox_v.shape[:2] + (-1, FOX_HEAD_DIM)),
        "fox_logf_t": jnp.swapaxes(cache_fox_logf, 2, 3),
        "lru_conv": _pad_history(state_lru_conv),
        "lru_h": state_lru_h.reshape(DEPTH, n_sample, 1, LRU_WIDTH),
        "ssd_conv": _pad_history(state_ssd_conv),
        "ssd_h": state_ssd_h.reshape(DEPTH, n_sample, SSD_WIDTH, D_STATE),
    }
    mod = _mod_call(jnp.concatenate([c_prompt, c_sample], axis=0), w_mod, b_mod)
    y_prompt, sp = _trunk(x_prompt, mod[:, :n_prompt], None, prm, ssd_chunk=256)
    y_sample, ss = _trunk(x_sample, mod[:, n_prompt:], caches, prm, ssd_chunk=x_sample.shape[1])
    names = ("fox_k", "fox_v", "fox_logf", "lru_conv", "lru_h", "ssd_conv", "ssd_h")
    return (y_prompt, y_sample) + tuple(sp[n] for n in names) + tuple(ss[n] for n in names)
```

```python
import functools

import numpy as np
import jax
import jax.numpy as jnp
from jax import lax
from jax.experimental import pallas as pl
from jax.experimental.pallas import tpu as pltpu

F32 = jnp.float32
BF16 = jnp.bfloat16

D_MODEL = 1024
DEPTH = 2
CONV_W = 4
EPS = 1e-6
LRU_WIDTH = 256
LRU_HEADS = 4
LRU_BLOCK = LRU_WIDTH // LRU_HEADS
LRU_C = 8.0
FOX_HEADS = 8
FOX_HEAD_DIM = 64
FOX_WIDTH = FOX_HEADS * FOX_HEAD_DIM
SSD_HEADS = 4
SSD_HEAD_DIM = 64
SSD_WIDTH = SSD_HEADS * SSD_HEAD_DIM
SSD_GROUPS = 2
D_STATE = 128
SSD_CONV_DIM = SSD_WIDTH + 2 * SSD_GROUPS * D_STATE
IN_SIZES = (LRU_WIDTH, LRU_WIDTH, FOX_WIDTH, FOX_WIDTH, FOX_WIDTH, FOX_HEADS, SSD_WIDTH, SSD_CONV_DIM, SSD_HEADS)
D_FF = 2816
N_SUB = 3

LANES = 128
SUBLANES = 8
HEAD_PAIR = LANES // FOX_HEAD_DIM

F_LANE0 = 0
DT_LANE0 = FOX_HEADS
COL_LRU_X = 0
COL_LRU_G = COL_LRU_X + LRU_WIDTH
COL_Q = COL_LRU_G + LRU_WIDTH
COL_K = COL_Q + FOX_WIDTH
COL_V = COL_K + FOX_WIDTH
COL_Z = COL_V + FOX_WIDTH
COL_XBC = COL_Z + SSD_WIDTH
COL_SMALL = COL_XBC + SSD_CONV_DIM
D_IN_PAD = COL_SMALL + LANES

ROW_TILE = 512
FF_CHUNK = 512
ATTN_BLOCK = 512
ATTN_ROWS = 128
SCAN_BLOCK = 256
NEG_BIG = -1e30

_HI = lax.Precision.HIGHEST


def _dot(a, b):
    return jnp.dot(a, b, preferred_element_type=F32)


def _dot_nt(a, b):
    return lax.dot_general(a, b, (((1,), (1,)), ((), ())), preferred_element_type=F32)


def _dot_tn(a, b):
    return lax.dot_general(a, b, (((0,), (0,)), ((), ())), preferred_element_type=F32)


def _silu(x):
    return x * jax.nn.sigmoid(x)


def _softplus(x):
    return jnp.maximum(x, 0.0) + jnp.log1p(jnp.exp(-jnp.abs(x)))


def _rms(x):
    return x * lax.rsqrt(jnp.mean(x * x, axis=-1, keepdims=True) + EPS)


def _per_seq(rows, per_seq, fn):
    g = per_seq[0].shape[0]
    if g == 1:
        return fn(rows, *per_seq)
    tm, d = rows.shape
    out = fn(rows.reshape(g, tm // g, d), *[p[:, None, :] for p in per_seq])
    return out.reshape(tm, d)


def _pre_norm(x, npre, mod_ref):
    h = _rms(x) * npre
    return _per_seq(h, (mod_ref[:, 1, :], mod_ref[:, 0, :]), lambda r, sc, sh: r * (1.0 + sc) + sh)


def _post_norm(x, y, npost, mod_ref, w):
    yn = _rms(y) * npost
    return x + _per_seq(yn, (mod_ref[:, 2, :],), lambda r, gt: (w * gt) * r)


def _seq_grouping(rows_per_seq, tm):
    if rows_per_seq % tm == 0:
        return 1, rows_per_seq // tm
    assert tm % rows_per_seq == 0
    return tm // rows_per_seq, 1


def _mod_spec(g, tiles_per_seq, sub):
    if g == 1:
        return pl.BlockSpec((1, None, 3, D_MODEL), lambda i: (i // tiles_per_seq, sub, 0, 0))
    return pl.BlockSpec((g, None, 3, D_MODEL), lambda i: (i, sub, 0, 0))


def _params(vmem_mb, sem):
    return pltpu.CompilerParams(dimension_semantics=sem, vmem_limit_bytes=vmem_mb << 20)


def _mod_kernel(c_ref, w_ref, b_ref, o_ref):
    a = _silu(c_ref[...]).astype(BF16)
    o_ref[...] = _dot(a, w_ref[...].astype(BF16)) + b_ref[...]


def _mod_call(c_all, w_mod, b_mod):
    nseq = c_all.shape[0]
    width = N_SUB * 3 * D_MODEL
    tn = 1024
    return pl.pallas_call(
        _mod_kernel,
        grid=(DEPTH, width // tn),
        in_specs=[
            pl.BlockSpec((nseq, D_MODEL), lambda l, n: (0, 0)),
            pl.BlockSpec((None, D_MODEL, tn), lambda l, n: (l, 0, n)),
            pl.BlockSpec((None, 1, tn), lambda l, n: (l, 0, n)),
        ],
        out_specs=pl.BlockSpec((None, nseq, tn), lambda l, n: (l, 0, n)),
        out_shape=jax.ShapeDtypeStruct((DEPTH, nseq, width), F32),
        compiler_params=_params(24, ("arbitrary", "arbitrary")),
        name="adaln_mod",
    )(c_all, w_mod, b_mod.reshape(DEPTH, 1, width))


def _ffn_kernel(x_ref, mod_ref, npre_ref, npost_ref, wg_ref, wu_ref, wd_ref, o_ref):
    x = x_ref[...]
    h = _pre_norm(x, npre_ref[...], mod_ref).astype(BF16)
    acc = None
    for off in range(0, D_FF, FF_CHUNK):
        fc = min(FF_CHUNK, D_FF - off)
        g = _dot(h, wg_ref[:, off:off + fc])
        u = _dot(h, wu_ref[:, off:off + fc])
        a = (_silu(g) * u).astype(BF16)
        d = _dot(a, wd_ref[off:off + fc, :])
        acc = d if acc is None else acc + d
    o_ref[...] = _post_norm(x, acc, npost_ref[...], mod_ref, 0.5)


def _ffn_call(x, mod4, npre, npost, wg, wu, wd, layer, sub, ffn_idx, rows_per_seq):
    m = x.shape[0]
    tm = min(ROW_TILE, m)
    g, tps = _seq_grouping(rows_per_seq, tm)
    wspec = lambda shape: pl.BlockSpec((None, None) + shape, lambda i: (layer, ffn_idx, 0, 0),
                                       pipeline_mode=pl.Buffered(1))
    nspec = pl.BlockSpec((None, None, 1, D_MODEL), lambda i: (layer, sub, 0, 0))
    return pl.pallas_call(
        _ffn_kernel,
        grid=(m // tm,),
        in_specs=[
            pl.BlockSpec((tm, D_MODEL), lambda i: (i, 0)),
            _mod_spec(g, tps, sub),
            nspec, nspec,
            wspec((D_MODEL, D_FF)), wspec((D_MODEL, D_FF)), wspec((D_FF, D_MODEL)),
        ],
        out_specs=pl.BlockSpec((tm, D_MODEL), lambda i: (i, 0)),
        out_shape=jax.ShapeDtypeStruct((m, D_MODEL), F32),
        compiler_params=_params(48, ("arbitrary",)),
        name="ffn",
    )(x, mod4, npre, npost, wg, wu, wd)


def _inproj_kernel(x_ref, mod_ref, npre_ref, wt_ref, fbias_ref,
                   lrux_ref, lrug_ref, q_ref, k_ref, v_ref, z_ref, xbc_ref, small_ref, *rest, transposed):
    h = _pre_norm(x_ref[...], npre_ref[...], mod_ref).astype(BF16)
    col = lambda start, width: _dot_nt(h, wt_ref[start:start + width, :])
    col_t = lambda start, width: _dot_nt(wt_ref[start:start + width, :], h)
    lrux_ref[...] = col(COL_LRU_X, LRU_WIDTH)
    lrug_ref[...] = col(COL_LRU_G, LRU_WIDTH)
    q_ref[...] = (col(COL_Q, FOX_WIDTH) * (FOX_HEAD_DIM ** -0.5)).astype(BF16)
    if transposed:
        k_ref[...] = col_t(COL_K, FOX_WIDTH)
        v_ref[...] = col_t(COL_V, FOX_WIDTH)
    else:
        k_ref[...] = col(COL_K, FOX_WIDTH)
        v_ref[...] = col(COL_V, FOX_WIDTH)
    z_ref[...] = col(COL_Z, SSD_WIDTH)
    xbc_ref[...] = col(COL_XBC, SSD_CONV_DIM)
    small = col(COL_SMALL, LANES)
    t = small + fbias_ref[...]
    logf = jnp.minimum(t, 0.0) - jnp.log1p(jnp.exp(-jnp.abs(t)))
    lane = lax.broadcasted_iota(jnp.int32, small.shape, 1)
    small = jnp.where(lane < DT_LANE0, logf, small)
    small_ref[...] = small
    if transposed:
        rest[0][...] = small.T


def _inproj_call(x, mod4, npre, w_in_t, fbias, layer, rows_per_seq, transposed):
    m = x.shape[0]
    tm = min(ROW_TILE, m)
    g, tps = _seq_grouping(rows_per_seq, tm)
    widths = (LRU_WIDTH, LRU_WIDTH, FOX_WIDTH, FOX_WIDTH, FOX_WIDTH, SSD_WIDTH, SSD_CONV_DIM, LANES)
    dtypes = (F32, F32, BF16, F32, F32, F32, F32, F32)
    out_specs = [pl.BlockSpec((tm, w), lambda i: (i, 0)) for w in widths]
    out_shape = [jax.ShapeDtypeStruct((m, w), dt) for w, dt in zip(widths, dtypes)]
    if transposed:
        assert g == 1
        nseq = m // rows_per_seq
        for idx, w in ((3, FOX_WIDTH), (4, FOX_WIDTH), (None, LANES)):
            spec = pl.BlockSpec((None, w, tm), lambda i: (i // tps, 0, i % tps))
            shape = jax.ShapeDtypeStruct((nseq, w, rows_per_seq), F32)
            if idx is None:
                out_specs.append(spec)
                out_shape.append(shape)
            else:
                out_specs[idx], out_shape[idx] = spec, shape
    return pl.pallas_call(
        functools.partial(_inproj_kernel, transposed=transposed),
        grid=(m // tm,),
        in_specs=[
            pl.BlockSpec((tm, D_MODEL), lambda i: (i, 0)),
            _mod_spec(g, tps, 1),
            pl.BlockSpec((None, None, 1, D_MODEL), lambda i: (layer, 1, 0, 0)),
            pl.BlockSpec((None, D_IN_PAD, D_MODEL), lambda i: (layer, 0, 0)),
            pl.BlockSpec((None, 1, LANES), lambda i: (layer, 0, 0)),
        ],
        out_specs=out_specs,
        out_shape=out_shape,
        compiler_params=_params(48, ("arbitrary",)),
        name="inproj",
    )(x, mod4, npre, w_in_t, fbias)


def _outproj_kernel(x_ref, ya_ref, yb_ref, yc_ref, mod_ref, npost_ref, w_ref, o_ref):
    y = (_dot(ya_ref[...], w_ref[0:LRU_WIDTH, :])
         + _dot(yb_ref[...], w_ref[LRU_WIDTH:LRU_WIDTH + FOX_WIDTH, :])
         + _dot(yc_ref[...], w_ref[LRU_WIDTH + FOX_WIDTH:, :]))
    o_ref[...] = _post_norm(x_ref[...], y, npost_ref[...], mod_ref, 1.0)


def _outproj_call(x, ya, yb, yc, mod4, npost, w_out, layer, rows_per_seq):
    m = x.shape[0]
    tm = min(ROW_TILE, m)
    g, tps = _seq_grouping(rows_per_seq, tm)
    row = lambda w: pl.BlockSpec((tm, w), lambda i: (i, 0))
    return pl.pallas_call(
        _outproj_kernel,
        grid=(m // tm,),
        in_specs=[
            row(D_MODEL), row(LRU_WIDTH), row(FOX_WIDTH), row(SSD_WIDTH),
            _mod_spec(g, tps, 1),
            pl.BlockSpec((None, None, 1, D_MODEL), lambda i: (layer, 1, 0, 0)),
            pl.BlockSpec((None, D_MODEL, D_MODEL), lambda i: (layer, 0, 0)),
        ],
        out_specs=row(D_MODEL),
        out_shape=jax.ShapeDtypeStruct((m, D_MODEL), F32),
        compiler_params=_params(32, ("arbitrary",)),
        name="outproj",
    )(x, ya, yb, yc, mod4, npost, w_out)


def _causal_conv(xx, cw, cb, n):
    u = cb + cw[0:1] * pltpu.roll(xx, 3, 0)[SUBLANES:SUBLANES + n]
    u = u + cw[1:2] * pltpu.roll(xx, 2, 0)[SUBLANES:SUBLANES + n]
    u = u + cw[2:3] * pltpu.roll(xx, 1, 0)[SUBLANES:SUBLANES + n]
    return u + cw[3:4] * xx[SUBLANES:SUBLANES + n]


def _last_rows(xx, k):
    return pltpu.roll(xx, k, 0)[0:SUBLANES][0:k]


def _tri(n, lower):
    r = lax.broadcasted_iota(jnp.int32, (n, n), 0)
    c = lax.broadcasted_iota(jnp.int32, (n, n), 1)
    return ((r >= c) if lower else (r <= c)).astype(F32)


def _lru_kernel(x_ref, g_ref, prev_ref, h0_ref, cw_ref, cb_ref, wgate_ref, bgate_ref, lam_ref,
                y_ref, convnew_ref, hnew_ref, a_scr, b_scr):
    n = x_ref.shape[0]
    xx = jnp.concatenate([prev_ref[...], x_ref[...]], axis=0)
    convnew_ref[...] = _last_rows(xx, CONV_W - 1)
    u = _causal_conv(xx, cw_ref[...], cb_ref[...], n)
    gates = _dot(u.astype(BF16), wgate_ref[...]) + bgate_ref[...]
    r = jax.nn.sigmoid(gates[:, :LRU_WIDTH])
    i = jax.nn.sigmoid(gates[:, LRU_WIDTH:])
    log_a = (-LRU_C * r) * _softplus(-lam_ref[...])
    a = jnp.exp(log_a)
    b = jnp.sqrt(-jnp.tanh(log_a) * (a * a + 1.0)) * (i * u)
    row = lax.broadcasted_iota(jnp.int32, (n, 1), 0) % SUBLANES
    for d in (1, 2, 4):
        keep = row >= d
        b = jnp.where(keep, a * pltpu.roll(b, d, 0) + b, b)
        a = jnp.where(keep, a * pltpu.roll(a, d, 0), a)
    a_scr[...] = a
    b_scr[...] = b

    def group(j, h):
        off = pl.multiple_of(j * SUBLANES, SUBLANES)
        hb = a_scr[pl.ds(off, SUBLANES), :] * h + b_scr[pl.ds(off, SUBLANES), :]
        b_scr[pl.ds(off, SUBLANES), :] = hb
        return jnp.broadcast_to(hb[SUBLANES - 1:SUBLANES, :], hb.shape)

    h_last = lax.fori_loop(0, n // SUBLANES, group,
                           jnp.broadcast_to(h0_ref[...], (SUBLANES, LRU_WIDTH)), unroll=4)
    hnew_ref[...] = h_last[0:1]
    y_ref[...] = (b_scr[...] * jax.nn.gelu(g_ref[...])).astype(BF16)


def _lru_call(lru_x, lru_g, prev8, h0, cw, cb, wgate, bgate, lam, layer, state_layer):
    bsz, n, _ = lru_x.shape
    seq = pl.BlockSpec((None, n, LRU_WIDTH), lambda b: (b, 0, 0))
    if state_layer is None:
        prev_spec = pl.BlockSpec((None, SUBLANES, LRU_WIDTH), lambda b: (0, 0, 0))
        h0_spec = pl.BlockSpec((None, 1, LRU_WIDTH), lambda b: (0, 0, 0))
    else:
        prev_spec = pl.BlockSpec((None, None, SUBLANES, LRU_WIDTH), lambda b: (state_layer, b, 0, 0))
        h0_spec = pl.BlockSpec((None, None, 1, LRU_WIDTH), lambda b: (state_layer, b, 0, 0))
    par = lambda r, w: pl.BlockSpec((None, r, w), lambda b: (layer, 0, 0))
    return pl.pallas_call(
        _lru_kernel,
        grid=(bsz,),
        in_specs=[seq, seq, prev_spec, h0_spec, par(CONV_W, LRU_WIDTH), par(1, LRU_WIDTH),
                  par(LRU_WIDTH, 2 * LRU_WIDTH), par(1, 2 * LRU_WIDTH), par(1, LRU_WIDTH)],
        out_specs=[seq,
                   pl.BlockSpec((None, CONV_W - 1, LRU_WIDTH), lambda b: (b, 0, 0)),
                   pl.BlockSpec((None, 1, LRU_WIDTH), lambda b: (b, 0, 0))],
        out_shape=[jax.ShapeDtypeStruct((bsz, n, LRU_WIDTH), BF16),
                   jax.ShapeDtypeStruct((bsz, CONV_W - 1, LRU_WIDTH), F32),
                   jax.ShapeDtypeStruct((bsz, 1, LRU_WIDTH), F32)],
        scratch_shapes=[pltpu.VMEM((n, LRU_WIDTH), F32), pltpu.VMEM((n, LRU_WIDTH), F32)],
        compiler_params=_params(40, ("arbitrary",)),
        name="rg_lru",
    )(lru_x, lru_g, prev8, h0, cw, cb, wgate, bgate, lam)


def _ssd_kernel(xbc_ref, z_ref, small_ref, prev_ref, h0_ref, cw_ref, cb_ref, dtb_ref, alog_ref, dvec_ref, nw_ref,
                y_ref, convnew_ref, h_ref, tail_scr):
    n = xbc_ref.shape[0]
    c = pl.program_id(1)

    @pl.when(c == 0)
    def _():
        tail_scr[...] = prev_ref[...]
        h_ref[...] = h0_ref[...]

    xx = jnp.concatenate([tail_scr[...], xbc_ref[...]], axis=0)
    tail_scr[...] = xx[n:n + SUBLANES]
    convnew_ref[...] = _last_rows(xx, CONV_W - 1)
    act = _silu(_causal_conv(xx, cw_ref[...], cb_ref[...], n))
    xs = act[:, :SSD_WIDTH]
    bm = act[:, SSD_WIDTH:SSD_WIDTH + SSD_GROUPS * D_STATE]
    cm = act[:, SSD_WIDTH + SSD_GROUPS * D_STATE:]

    dt = _softplus(small_ref[...] + dtb_ref[...])
    dta = dt * (-jnp.exp(alog_ref[...]))
    cum = jnp.dot(_tri(n, True), dta, preferred_element_type=F32, precision=_HI)
    cum_t = cum.T
    rr = lax.broadcasted_iota(jnp.int32, (n, n), 0)
    cc = lax.broadcasted_iota(jnp.int32, (n, n), 1)
    causal = rr >= cc
    lo_lane = lax.broadcasted_iota(jnp.int32, (1, LANES), 1) < SSD_HEAD_DIM
    lo_row = lax.broadcasted_iota(jnp.int32, (LANES, 1), 0) < SSD_HEAD_DIM
    dvec = dvec_ref[...]

    ys = []
    for g in range(SSD_GROUPS):
        sl = slice(g * LANES, (g + 1) * LANES)
        xg, bg, cg = xs[:, sl], bm[:, sl].astype(BF16), cm[:, sl].astype(BF16)
        heads = (2 * g, 2 * g + 1)
        col = lambda a, h: a[:, DT_LANE0 + h:DT_LANE0 + h + 1]
        pick = lambda f: jnp.where(lo_lane, f(heads[0]), f(heads[1]))
        dx = xg * pick(lambda h: col(dt, h))
        dxb = dx.astype(BF16)
        cb_mat = _dot_nt(cg, bg)
        yd = []
        for h in heads:
            seg = col(cum, h) - cum_t[DT_LANE0 + h:DT_LANE0 + h + 1, :]
            lmat = jnp.exp(jnp.where(causal, seg, NEG_BIG))
            yd.append(_dot((cb_mat * lmat).astype(BF16), dxb))
        y_diag = jnp.where(lo_lane, yd[0], yd[1])
        last = lambda h: col(cum, h)[n - 1:n, :]
        decay_end = pick(lambda h: jnp.exp(last(h) - col(cum, h)))
        states = _dot_tn((dx * decay_end).astype(BF16), bg)
        h_prev = h_ref[sl, :]
        y_off = _dot_nt(cg, h_prev.astype(BF16)) * pick(lambda h: jnp.exp(col(cum, h)))
        chunk_decay = jnp.where(lo_row, jnp.exp(last(heads[0])), jnp.exp(last(heads[1])))
        h_ref[sl, :] = chunk_decay * h_prev + states
        ys.append(y_diag + y_off + dvec[:, sl] * xg)
    y = jnp.concatenate(ys, axis=1)
    y_ref[...] = (_rms(y * _silu(z_ref[...])) * nw_ref[...]).astype(BF16)


def _ssd_call(xbc, z, small, prev8, h0, cw, cb, dtb, alog, dvec, nw, layer, state_layer, chunk):
    bsz, n, _ = xbc.shape
    nc = n // chunk
    seq = lambda w: pl.BlockSpec((None, chunk, w), lambda b, c: (b, c, 0))
    if state_layer is None:
        prev_spec = pl.BlockSpec((None, SUBLANES, SSD_CONV_DIM), lambda b, c: (0, 0, 0))
        h0_spec = pl.BlockSpec((None, SSD_WIDTH, D_STATE), lambda b, c: (0, 0, 0))
    else:
        prev_spec = pl.BlockSpec((None, None, SUBLANES, SSD_CONV_DIM), lambda b, c: (state_layer, b, 0, 0))
        h0_spec = pl.BlockSpec((None, None, SSD_WIDTH, D_STATE), lambda b, c: (state_layer, b, 0, 0))
    par = lambda r, w: pl.BlockSpec((None, r, w), lambda b, c: (layer, 0, 0))
    return pl.pallas_call(
        _ssd_kernel,
        grid=(bsz, nc),
        in_specs=[seq(SSD_CONV_DIM), seq(SSD_WIDTH), seq(LANES), prev_spec, h0_spec,
                  par(CONV_W, SSD_CONV_DIM), par(1, SSD_CONV_DIM), par(1, LANES), par(1, LANES),
                  par(1, SSD_WIDTH), par(1, SSD_WIDTH)],
        out_specs=[seq(SSD_WIDTH),
                   pl.BlockSpec((None, CONV_W - 1, SSD_CONV_DIM), lambda b, c: (b, 0, 0)),
                   pl.BlockSpec((None, SSD_WIDTH, D_STATE), lambda b, c: (b, 0, 0))],
        out_shape=[jax.ShapeDtypeStruct((bsz, n, SSD_WIDTH), BF16),
                   jax.ShapeDtypeStruct((bsz, CONV_W - 1, SSD_CONV_DIM), F32),
                   jax.ShapeDtypeStruct((bsz, SSD_WIDTH, D_STATE), F32)],
        scratch_shapes=[pltpu.VMEM((SUBLANES, SSD_CONV_DIM), F32)],
        compiler_params=_params(32, ("arbitrary", "arbitrary")),
        name="ssd",
    )(xbc, z, small, prev8, h0, cw, cb, dtb, alog, dvec, nw)


def _fox_prompt_kernel(q_ref, kt_ref, vt_ref, logft_ref, o_ref, kb_scr, vb_scr, f_scr, ft_scr):
    s = kt_ref.shape[1]
    tq = q_ref.shape[0]
    p = pl.program_id(1)
    qi = pl.program_id(2)

    @pl.when(qi == 0)
    def _():
        for j in range(s // tq):
            kb_scr[j] = kt_ref[:, j * tq:(j + 1) * tq].astype(BF16)
            vb_scr[j] = vt_ref[:, j * tq:(j + 1) * tq].astype(BF16)

    @pl.when((qi == 0) & (p == 0))
    def _():
        triu = _tri(SCAN_BLOCK, False)
        carry = jnp.zeros((LANES, 1), F32)
        per = tq // SCAN_BLOCK
        for c in range(s // SCAN_BLOCK):
            cols = slice(c * SCAN_BLOCK, (c + 1) * SCAN_BLOCK)
            fc = jnp.dot(logft_ref[:, cols], triu, preferred_element_type=F32, precision=_HI) + carry
            ft_scr[c // per, :, (c % per) * SCAN_BLOCK:(c % per + 1) * SCAN_BLOCK] = fc
            f_scr[cols, :] = fc.T
            carry = fc[:, SCAN_BLOCK - 1:SCAN_BLOCK]

    lane = lax.broadcasted_iota(jnp.int32, (1, LANES), 1)
    lo_lane = lane < FOX_HEAD_DIM
    heads = (HEAD_PAIR * p, HEAD_PAIR * p + 1)
    q0 = pl.multiple_of(qi * tq, tq)
    rb = min(ATTN_ROWS, tq)
    nrb = tq // rb
    q_blocks, fq_blocks = [], []
    for r in range(nrb):
        q = q_ref[r * rb:(r + 1) * rb, :]
        q_blocks.append((jnp.where(lo_lane, q, jnp.zeros_like(q)), jnp.where(lo_lane, jnp.zeros_like(q), q)))
        f_rows = f_scr[pl.ds(pl.multiple_of(q0 + r * rb, rb), rb), :]
        fq_blocks.append([jnp.sum(jnp.where(lane == F_LANE0 + h, f_rows, 0.0), axis=1, keepdims=True) for h in heads])

    def softmax_unit(dots, f_q, f_k, state, row0):
        m_old, l_old = state
        t = dots - f_k
        if row0 is not None:
            rr = lax.broadcasted_iota(jnp.int32, t.shape, 0) + row0
            cc = lax.broadcasted_iota(jnp.int32, t.shape, 1)
            t = jnp.where(cc <= rr, t, NEG_BIG)
        m_new = jnp.maximum(m_old, jnp.max(t, axis=1, keepdims=True) + f_q)
        alpha = jnp.exp(m_old - m_new)
        pr = jnp.exp(t - (m_new - f_q))
        l_new = alpha * l_old + jnp.sum(pr, axis=1, keepdims=True)
        return m_new, l_new, alpha, pr.astype(BF16)

    def step(j, carry, diagonal):
        f_keys = [ft_scr[j, pl.ds(F_LANE0 + h, 1), :] for h in heads]
        nks = [(r + 1) * rb if diagonal else tq for r in range(nrb)]
        units = [(r, hl) for r in range(nrb) for hl in range(HEAD_PAIR)]
        dots = {u: _dot(q_blocks[u[0]][u[1]], kb_scr[j, :, 0:nks[u[0]]]) for u in units}
        soft = {u: softmax_unit(dots[u], fq_blocks[u[0]][u[1]], f_keys[u[1]][:, 0:nks[u[0]]],
                                carry[u[0]][2 * u[1]:2 * u[1] + 2], u[0] * rb if diagonal else None)
                for u in units}
        pvs = {u: _dot_nt(soft[u][3], vb_scr[j, :, 0:nks[u[0]]]) for u in units}
        new = []
        for r in range(nrb):
            a, b = soft[(r, 0)], soft[(r, 1)]
            acc = jnp.where(lo_lane, a[2], b[2]) * carry[r][4] + jnp.where(lo_lane, pvs[(r, 0)], pvs[(r, 1)])
            new.append((a[0], a[1], b[0], b[1], acc))
        return tuple(new)

    init_rows = (jnp.full((rb, 1), NEG_BIG, F32), jnp.zeros((rb, 1), F32)) * HEAD_PAIR + (jnp.zeros((rb, LANES), F32),)
    carry = lax.fori_loop(0, qi, lambda j, cr: step(j, cr, False), (init_rows,) * nrb)
    carry = step(qi, carry, True)
    for r in range(nrb):
        o_ref[r * rb:(r + 1) * rb, :] = (carry[r][4] / jnp.where(lo_lane, carry[r][1], carry[r][3])).astype(BF16)


def _fox_prompt_call(q, kt, vt, small_t):
    bsz, s, _ = q.shape
    tq = min(ATTN_BLOCK, s)
    blk = pl.BlockSpec((None, tq, LANES), lambda b, p, i: (b, i, p))
    full = pl.BlockSpec((None, LANES, s), lambda b, p, i: (b, p, 0))
    return pl.pallas_call(
        _fox_prompt_kernel,
        grid=(bsz, FOX_WIDTH // LANES, s // tq),
        in_specs=[blk, full, full, pl.BlockSpec((None, LANES, s), lambda b, p, i: (b, 0, 0))],
        out_specs=blk,
        out_shape=jax.ShapeDtypeStruct((bsz, s, FOX_WIDTH), BF16),
        scratch_shapes=[pltpu.VMEM((s // tq, LANES, tq), BF16), pltpu.VMEM((s // tq, LANES, tq), BF16),
                        pltpu.VMEM((s, LANES), F32), pltpu.VMEM((s // tq, LANES, tq), F32)],
        compiler_params=_params(40, ("arbitrary", "arbitrary", "arbitrary")),
        name="fox_prompt",
    )(q, kt, vt, small_t)


def _fox_sample_kernel(q_ref, k_ref, v_ref, logf_ref, ck_ref, cv_ref, clogft_ref, o_ref,
                       m_scr, l_scr, acc_scr, fk_scr, fq_scr, fnew_scr):
    t = q_ref.shape[0]
    c = pl.program_id(1)
    nchunk, _, chunk = fk_scr.shape

    @pl.when(c == 0)
    def _():
        blk = min(SCAN_BLOCK, chunk)
        triu = _tri(blk, False)
        carry = jnp.zeros((FOX_HEADS, 1), F32)
        for i in range(nchunk):
            for j in range(chunk // blk):
                cols = slice(j * blk, (j + 1) * blk)
                fc = jnp.dot(clogft_ref[:, i * chunk + j * blk:i * chunk + (j + 1) * blk], triu,
                             preferred_element_type=F32, precision=_HI) + carry
                fk_scr[i, :, cols] = fc
                carry = fc[:, blk - 1:blk]
        sub = lax.broadcasted_iota(jnp.int32, (FOX_HEADS, LANES), 0)
        ln = lax.broadcasted_iota(jnp.int32, (FOX_HEADS, LANES), 1)
        total_row = jnp.sum(jnp.where(sub == ln - F_LANE0, carry, 0.0), axis=0, keepdims=True)
        f_new = jnp.dot(_tri(t, True), logf_ref[...], preferred_element_type=F32, precision=_HI) + total_row
        fnew_scr[...] = f_new.T[F_LANE0:F_LANE0 + FOX_HEADS, :]
        fq_scr[...] = jnp.concatenate([f_new[:, F_LANE0 + h:F_LANE0 + h + 1] for h in range(FOX_HEADS)], axis=0)
        m_scr[...] = jnp.full(m_scr.shape, NEG_BIG, F32)
        l_scr[...] = jnp.zeros(l_scr.shape, F32)
        acc_scr[...] = jnp.zeros(acc_scr.shape, F32)

    q = q_ref[...]
    q_heads = [q[:, h * FOX_HEAD_DIM:(h + 1) * FOX_HEAD_DIM] for h in range(FOX_HEADS)]
    f_q = fq_scr[...]

    def update(keys, values, f_k, mask, channels_first):
        qk = _dot if channels_first else _dot_nt
        pv_dot = _dot_nt if channels_first else _dot
        tt = jnp.concatenate([qk(q_heads[h], keys[h]) - f_k[h:h + 1, :] for h in range(FOX_HEADS)], axis=0)
        if mask is not None:
            tt = jnp.where(mask, tt, NEG_BIG)
        m_old = m_scr[...]
        m_new = jnp.maximum(m_old, jnp.max(tt, axis=1, keepdims=True) + f_q)
        alpha = jnp.exp(m_old - m_new)
        pr = jnp.exp(tt - (m_new - f_q))
        l_scr[...] = alpha * l_scr[...] + jnp.sum(pr, axis=1, keepdims=True)
        prb = pr.astype(BF16)
        pv = jnp.concatenate([pv_dot(prb[h * t:(h + 1) * t, :], values[h]) for h in range(FOX_HEADS)], axis=0)
        acc_scr[...] = alpha * acc_scr[...] + pv
        m_scr[...] = m_new

    head_rows = lambda ref: [ref[h * FOX_HEAD_DIM:(h + 1) * FOX_HEAD_DIM, :].astype(BF16) for h in range(FOX_HEADS)]
    update(head_rows(ck_ref), head_rows(cv_ref), fk_scr[c], None, True)

    @pl.when(c == nchunk - 1)
    def _():
        k_new, v_new = k_ref[...].astype(BF16), v_ref[...].astype(BF16)
        cols = lambda a: [a[:, h * FOX_HEAD_DIM:(h + 1) * FOX_HEAD_DIM] for h in range(FOX_HEADS)]
        rr = lax.broadcasted_iota(jnp.int32, (FOX_HEADS * t, t), 0) % t
        cc = lax.broadcasted_iota(jnp.int32, (FOX_HEADS * t, t), 1)
        update(cols(k_new), cols(v_new), fnew_scr[...], cc <= rr, False)
        out = acc_scr[...] / l_scr[...]
        o_ref[...] = jnp.concatenate([out[h * t:(h + 1) * t, :] for h in range(FOX_HEADS)], axis=1).astype(BF16)


def _fox_sample_call(q, k, v, small, cache_k, cache_v, cache_logf_t, layer):
    bsz, t, _ = q.shape
    past = cache_logf_t.shape[3]
    chunk = min(1024, past)
    nchunk = past // chunk
    new = lambda w: pl.BlockSpec((None, t, w), lambda b, c: (b, 0, 0))
    cache = pl.BlockSpec((None, None, FOX_WIDTH, chunk), lambda b, c: (layer, b, 0, c))
    rows = FOX_HEADS * t
    return pl.pallas_call(
        _fox_sample_kernel,
        grid=(bsz, nchunk),
        in_specs=[new(FOX_WIDTH), new(FOX_WIDTH), new(FOX_WIDTH), new(LANES), cache, cache,
                  pl.BlockSpec((None, None, FOX_HEADS, past), lambda b, c: (layer, b, 0, 0))],
        out_specs=new(FOX_WIDTH),
        out_shape=jax.ShapeDtypeStruct((bsz, t, FOX_WIDTH), BF16),
        scratch_shapes=[pltpu.VMEM((rows, 1), F32), pltpu.VMEM((rows, 1), F32), pltpu.VMEM((rows, FOX_HEAD_DIM), F32),
                        pltpu.VMEM((nchunk, FOX_HEADS, chunk), F32), pltpu.VMEM((rows, 1), F32),
                        pltpu.VMEM((FOX_HEADS, t), F32)],
        compiler_params=_params(40, ("arbitrary", "arbitrary")),
        name="fox_sample",
    )(q, k, v, small, cache_k, cache_v, cache_logf_t)


def _permute_w_in_t(w_in):
    w_t = jnp.swapaxes(w_in, 1, 2)
    offs = np.concatenate([[0], np.cumsum(IN_SIZES)])
    seg = lambda i: w_t[:, int(offs[i]):int(offs[i + 1]), :]
    pad = jnp.zeros((w_t.shape[0], LANES - FOX_HEADS - SSD_HEADS, w_t.shape[2]), w_t.dtype)
    return jnp.concatenate([seg(0), seg(1), seg(2), seg(3), seg(4), seg(6), seg(7), seg(5), seg(8), pad],
                           axis=1).astype(BF16)


def _block_diag(w):
    d, h, b, _ = w.shape
    eye = jnp.eye(h, dtype=w.dtype)
    return jnp.einsum("dhij,hg->dhigj", w, eye).reshape(d, h * b, h * b)


def _lane_slab(v, lane0):
    d, k = v.shape
    return jnp.zeros((d, 1, LANES), v.dtype).at[:, 0, lane0:lane0 + k].set(v)


def _pad_history(state):
    return jnp.pad(state, ((0, 0), (0, 0), (SUBLANES - (CONV_W - 1), 0), (0, 0)))


def _trunk(x, mod_group, caches, prm, ssd_chunk):
    bsz, n, _ = x.shape
    x = x.reshape(bsz * n, D_MODEL)
    states = {name: [] for name in ("fox_k", "fox_v", "fox_logf", "lru_conv", "lru_h", "ssd_conv", "ssd_h")}
    for l in range(DEPTH):
        mod4 = mod_group[l].reshape(bsz, N_SUB, 3, D_MODEL)
        x = _ffn_call(x, mod4, prm["npre"], prm["npost"], prm["wg"], prm["wu"], prm["wd"], l, 0, 0, n)
        prompt = caches is None
        proj = _inproj_call(x, mod4, prm["npre"], prm["w_in"], prm["fbias"], l, n, transposed=prompt)
        lrux, lrug, q, k, v, z, xbc, small = proj[:8]
        per_seq = lambda a: a.reshape(bsz, n, a.shape[-1])
        state_layer = None if prompt else l
        src = prm["zero_state"] if prompt else caches
        ya, lru_conv, lru_h = _lru_call(per_seq(lrux), per_seq(lrug), src["lru_conv"], src["lru_h"],
                                        prm["lru_cw"], prm["lru_cb"], prm["lru_wgate"], prm["lru_bgate"],
                                        prm["lru_lam"], l, state_layer)
        if prompt:
            small_t = proj[8]
            yb = _fox_prompt_call(per_seq(q), k, v, small_t)
            heads_last = lambda a: jnp.transpose(a.reshape(bsz, FOX_HEADS, FOX_HEAD_DIM, n), (0, 3, 1, 2))
            k_out, v_out = heads_last(k), heads_last(v)
            logf_out = jnp.swapaxes(small_t[:, F_LANE0:F_LANE0 + FOX_HEADS, :], 1, 2)
        else:
            yb = _fox_sample_call(per_seq(q), per_seq(k), per_seq(v), per_seq(small),
                                  caches["fox_k"], caches["fox_v"], caches["fox_logf_t"], l)
            k_out = k.reshape(bsz, n, FOX_HEADS, FOX_HEAD_DIM)
            v_out = v.reshape(bsz, n, FOX_HEADS, FOX_HEAD_DIM)
            logf_out = small.reshape(bsz, n, LANES)[:, :, F_LANE0:F_LANE0 + FOX_HEADS]
        yc, ssd_conv, ssd_h = _ssd_call(per_seq(xbc), per_seq(z), per_seq(small), src["ssd_conv"], src["ssd_h"],
                                        prm["ssd_cw"], prm["ssd_cb"], prm["ssd_dtb"], prm["ssd_alog"],
                                        prm["ssd_dvec"], prm["ssd_nw"], l, state_layer, ssd_chunk)
        x = _outproj_call(x, ya.reshape(bsz * n, -1), yb.reshape(bsz * n, -1), yc.reshape(bsz * n, -1),
                          mod4, prm["npost"], prm["w_out"], l, n)
        x = _ffn_call(x, mod4, prm["npre"], prm["npost"], prm["wg"], prm["wu"], prm["wd"], l, 2, 1, n)
        states["fox_k"].append(k_out)
        states["fox_v"].append(v_out)
        states["fox_logf"].append(logf_out)
        states["lru_conv"].append(lru_conv)
        states["lru_h"].append(lru_h.reshape(bsz, LRU_WIDTH))
        states["ssd_conv"].append(ssd_conv)
        states["ssd_h"].append(ssd_h.reshape(bsz, SSD_HEADS, SSD_HEAD_DIM, D_STATE))
    return x.reshape(bsz, n, D_MODEL), {name: jnp.stack(vals, axis=0) for name, vals in states.items()}


def kernel(x_prompt, x_sample, c_prompt, c_sample, cache_fox_k, cache_fox_v, cache_fox_logf, state_lru_conv, state_lru_h, state_ssd_conv, state_ssd_h, w_mod, b_mod, norm_pre, norm_post, ffn_w_gate, ffn_w_up, ffn_w_down, w_in, w_out, lru_conv_w, lru_conv_b, lru_wa, lru_ba, lru_wx, lru_bx, lru_lambda, fox_f_bias, ssd_conv_w, ssd_conv_b, ssd_dt_bias, ssd_a_log, ssd_d, ssd_norm_w):
    n_prompt, n_sample = x_prompt.shape[0], x_sample.shape[0]
    prm = {
        "npre": norm_pre.reshape(DEPTH, N_SUB, 1, D_MODEL),
        "npost": norm_post.reshape(DEPTH, N_SUB, 1, D_MODEL),
        "wg": ffn_w_gate.astype(BF16), "wu": ffn_w_up.astype(BF16), "wd": ffn_w_down.astype(BF16),
        "w_in": _permute_w_in_t(w_in), "w_out": w_out.astype(BF16),
        "fbias": _lane_slab(fox_f_bias, F_LANE0),
        "lru_cw": lru_conv_w, "lru_cb": lru_conv_b.reshape(DEPTH, 1, LRU_WIDTH),
        "lru_wgate": jnp.concatenate([_block_diag(lru_wa), _block_diag(lru_wx)], axis=-1).astype(BF16),
        "lru_bgate": jnp.concatenate([lru_ba, lru_bx], axis=-1).reshape(DEPTH, 1, 2 * LRU_WIDTH),
        "lru_lam": lru_lambda.reshape(DEPTH, 1, LRU_WIDTH),
        "ssd_cw": ssd_conv_w, "ssd_cb": ssd_conv_b.reshape(DEPTH, 1, SSD_CONV_DIM),
        "ssd_dtb": _lane_slab(ssd_dt_bias, DT_LANE0), "ssd_alog": _lane_slab(ssd_a_log, DT_LANE0),
        "ssd_dvec": jnp.repeat(ssd_d, SSD_HEAD_DIM, axis=-1).reshape(DEPTH, 1, SSD_WIDTH),
        "ssd_nw": ssd_norm_w.reshape(DEPTH, 1, SSD_WIDTH),
        "zero_state": {
            "lru_conv": jnp.zeros((1, SUBLANES, LRU_WIDTH), F32), "lru_h": jnp.zeros((1, 1, LRU_WIDTH), F32),
            "ssd_conv": jnp.zeros((1, SUBLANES, SSD_CONV_DIM), F32), "ssd_h": jnp.zeros((1, SSD_WIDTH, D_STATE), F32),
        },
    }
    caches = {
        "fox_k": jnp.transpose(cache_fox_k, (0, 1, 3, 4, 2)).reshape(DEPTH, n_sample, FOX_WIDTH, -1),
        "fox_v": jnp.transpose(cache_fox_v, (0, 1, 3, 4, 2)).reshape(DEPTH, n_sample, FOX_WIDTH, -1),
        "fox_logf_t": jnp.swapaxes(cache_fox_logf, 2, 3),
        "lru_conv": _pad_history(state_lru_conv),
        "lru_h": state_lru_h.reshape(DEPTH, n_sample, 1, LRU_WIDTH),
        "ssd_conv": _pad_history(state_ssd_conv),
        "ssd_h": state_ssd_h.reshape(DEPTH, n_sample, SSD_WIDTH, D_STATE),
    }
    mod = _mod_call(jnp.concatenate([c_prompt, c_sample], axis=0), w_mod, b_mod)
    y_prompt, sp = _trunk(x_prompt, mod[:, :n_prompt], None, prm, ssd_chunk=256)
    y_sample, ss = _trunk(x_sample, mod[:, n_prompt:], caches, prm, ssd_chunk=x_sample.shape[1])
    names = ("fox_k", "fox_v", "fox_logf", "lru_conv", "lru_h", "ssd_conv", "ssd_h")
    return (y_prompt, y_sample) + tuple(sp[n] for n in names) + tuple(ss[n] for n in names)
```

```python
import functools

import numpy as np
import jax
import jax.numpy as jnp
from jax import lax
from jax.experimental import pallas as pl
from jax.experimental.pallas import tpu as pltpu

F32 = jnp.float32
BF16 = jnp.bfloat16

D_MODEL = 1024
DEPTH = 2
CONV_W = 4
EPS = 1e-6
LRU_WIDTH = 256
LRU_HEADS = 4
LRU_BLOCK = LRU_WIDTH // LRU_HEADS
LRU_C = 8.0
FOX_HEADS = 8
FOX_HEAD_DIM = 64
FOX_WIDTH = FOX_HEADS * FOX_HEAD_DIM
SSD_HEADS = 4
SSD_HEAD_DIM = 64
SSD_WIDTH = SSD_HEADS * SSD_HEAD_DIM
SSD_GROUPS = 2
D_STATE = 128
SSD_CONV_DIM = SSD_WIDTH + 2 * SSD_GROUPS * D_STATE
IN_SIZES = (LRU_WIDTH, LRU_WIDTH, FOX_WIDTH, FOX_WIDTH, FOX_WIDTH, FOX_HEADS, SSD_WIDTH, SSD_CONV_DIM, SSD_HEADS)
D_FF = 2816
N_SUB = 3

LANES = 128
SUBLANES = 8
HEAD_PAIR = LANES // FOX_HEAD_DIM

F_LANE0 = 0
DT_LANE0 = FOX_HEADS
COL_LRU_X = 0
COL_LRU_G = COL_LRU_X + LRU_WIDTH
COL_Q = COL_LRU_G + LRU_WIDTH
COL_K = COL_Q + FOX_WIDTH
COL_V = COL_K + FOX_WIDTH
COL_Z = COL_V + FOX_WIDTH
COL_XBC = COL_Z + SSD_WIDTH
COL_SMALL = COL_XBC + SSD_CONV_DIM
D_IN_PAD = COL_SMALL + LANES

ROW_TILE = 512
FF_CHUNK = 512
ATTN_BLOCK = 512
ATTN_ROWS = 128
SCAN_BLOCK = 256
NEG_BIG = -1e30
LOG2E = 1.4426950408889634
Q_SCALE = LOG2E * FOX_HEAD_DIM ** -0.5


def _dot(a, b):
    return jnp.dot(a, b, preferred_element_type=F32)


def _dot_nt(a, b):
    return lax.dot_general(a, b, (((1,), (1,)), ((), ())), preferred_element_type=F32)


def _dot_tn(a, b):
    return lax.dot_general(a, b, (((0,), (0,)), ((), ())), preferred_element_type=F32)


def _silu(x):
    return x * jax.nn.sigmoid(x)


def _softplus(x):
    return jnp.maximum(x, 0.0) + jnp.log1p(jnp.exp(-jnp.abs(x)))


def _rms(x):
    return x * lax.rsqrt(jnp.mean(x * x, axis=-1, keepdims=True) + EPS)


def _per_seq(rows, per_seq, fn):
    g = per_seq[0].shape[0]
    if g == 1:
        return fn(rows, *per_seq)
    tm, d = rows.shape
    out = fn(rows.reshape(g, tm // g, d), *[p[:, None, :] for p in per_seq])
    return out.reshape(tm, d)


def _pre_norm(x, npre, mod_ref):
    h = _rms(x) * npre
    return _per_seq(h, (mod_ref[:, 1, :], mod_ref[:, 0, :]), lambda r, sc, sh: r * (1.0 + sc) + sh)


def _post_norm(x, y, npost, mod_ref, w):
    yn = _rms(y) * npost
    return x + _per_seq(yn, (mod_ref[:, 2, :],), lambda r, gt: (w * gt) * r)


def _seq_grouping(rows_per_seq, tm):
    if rows_per_seq % tm == 0:
        return 1, rows_per_seq // tm
    assert tm % rows_per_seq == 0
    return tm // rows_per_seq, 1


def _mod_spec(g, tiles_per_seq, sub):
    if g == 1:
        return pl.BlockSpec((1, None, 3, D_MODEL), lambda i: (i // tiles_per_seq, sub, 0, 0))
    return pl.BlockSpec((g, None, 3, D_MODEL), lambda i: (i, sub, 0, 0))


def _params(vmem_mb, sem):
    return pltpu.CompilerParams(dimension_semantics=sem, vmem_limit_bytes=vmem_mb << 20)


def _mod_kernel(c_ref, w_ref, b_ref, o_ref):
    a = _silu(c_ref[...]).astype(BF16)
    o_ref[...] = _dot(a, w_ref[...].astype(BF16)) + b_ref[...]


def _mod_call(c_all, w_mod, b_mod):
    nseq = c_all.shape[0]
    width = N_SUB * 3 * D_MODEL
    tn = 1024
    return pl.pallas_call(
        _mod_kernel,
        grid=(DEPTH, width // tn),
        in_specs=[
            pl.BlockSpec((nseq, D_MODEL), lambda l, n: (0, 0)),
            pl.BlockSpec((None, D_MODEL, tn), lambda l, n: (l, 0, n)),
            pl.BlockSpec((None, 1, tn), lambda l, n: (l, 0, n)),
        ],
        out_specs=pl.BlockSpec((None, nseq, tn), lambda l, n: (l, 0, n)),
        out_shape=jax.ShapeDtypeStruct((DEPTH, nseq, width), F32),
        compiler_params=_params(24, ("arbitrary", "arbitrary")),
        name="adaln_mod",
    )(c_all, w_mod, b_mod.reshape(DEPTH, 1, width))


def _ffn_kernel(x_ref, mod_ref, npre_ref, npost_ref, wg_ref, wu_ref, wd_ref, o_ref):
    x = x_ref[...]
    h = _pre_norm(x, npre_ref[...], mod_ref).astype(BF16)
    acc = None
    for off in range(0, D_FF, FF_CHUNK):
        fc = min(FF_CHUNK, D_FF - off)
        g = _dot(h, wg_ref[:, off:off + fc])
        u = _dot(h, wu_ref[:, off:off + fc])
        a = (_silu(g) * u).astype(BF16)
        d = _dot(a, wd_ref[off:off + fc, :])
        acc = d if acc is None else acc + d
    o_ref[...] = _post_norm(x, acc, npost_ref[...], mod_ref, 0.5)


def _ffn_call(x, mod4, npre, npost, wg, wu, wd, layer, sub, ffn_idx, rows_per_seq):
    m = x.shape[0]
    tm = min(ROW_TILE, m)
    g, tps = _seq_grouping(rows_per_seq, tm)
    wspec = lambda shape: pl.BlockSpec((None, None) + shape, lambda i: (layer, ffn_idx, 0, 0),
                                       pipeline_mode=pl.Buffered(1))
    nspec = pl.BlockSpec((None, None, 1, D_MODEL), lambda i: (layer, sub, 0, 0))
    return pl.pallas_call(
        _ffn_kernel,
        grid=(m // tm,),
        in_specs=[
            pl.BlockSpec((tm, D_MODEL), lambda i: (i, 0)),
            _mod_spec(g, tps, sub),
            nspec, nspec,
            wspec((D_MODEL, D_FF)), wspec((D_MODEL, D_FF)), wspec((D_FF, D_MODEL)),
        ],
        out_specs=pl.BlockSpec((tm, D_MODEL), lambda i: (i, 0)),
        out_shape=jax.ShapeDtypeStruct((m, D_MODEL), F32),
        compiler_params=_params(48, ("arbitrary",)),
        name="ffn",
    )(x, mod4, npre, npost, wg, wu, wd)


def _inproj_kernel(x_ref, mod_ref, npre_ref, wt_ref, fbias_ref,
                   lrux_ref, lrug_ref, q_ref, k_ref, v_ref, z_ref, xbc_ref, small_ref, *rest, transposed):
    h = _pre_norm(x_ref[...], npre_ref[...], mod_ref).astype(BF16)
    col = lambda start, width: _dot_nt(h, wt_ref[start:start + width, :])
    col_t = lambda start, width: _dot_nt(wt_ref[start:start + width, :], h)
    lrux_ref[...] = col(COL_LRU_X, LRU_WIDTH)
    lrug_ref[...] = col(COL_LRU_G, LRU_WIDTH)
    q_ref[...] = (col(COL_Q, FOX_WIDTH) * Q_SCALE).astype(BF16)
    if transposed:
        k_ref[...] = col_t(COL_K, FOX_WIDTH)
        v_ref[...] = col_t(COL_V, FOX_WIDTH)
    else:
        k_ref[...] = col(COL_K, FOX_WIDTH)
        v_ref[...] = col(COL_V, FOX_WIDTH)
    z_ref[...] = col(COL_Z, SSD_WIDTH)
    xbc_ref[...] = col(COL_XBC, SSD_CONV_DIM)
    small = col(COL_SMALL, LANES)
    t = small + fbias_ref[...]
    logf = jnp.minimum(t, 0.0) - jnp.log1p(jnp.exp(-jnp.abs(t)))
    lane = lax.broadcasted_iota(jnp.int32, small.shape, 1)
    small = jnp.where(lane < DT_LANE0, logf, small)
    small_ref[...] = small
    if transposed:
        rest[0][...] = small.T


def _inproj_call(x, mod4, npre, w_in_t, fbias, layer, rows_per_seq, transposed):
    m = x.shape[0]
    tm = min(ROW_TILE, m)
    g, tps = _seq_grouping(rows_per_seq, tm)
    widths = (LRU_WIDTH, LRU_WIDTH, FOX_WIDTH, FOX_WIDTH, FOX_WIDTH, SSD_WIDTH, SSD_CONV_DIM, LANES)
    dtypes = (F32, F32, BF16, F32, F32, F32, F32, F32)
    out_specs = [pl.BlockSpec((tm, w), lambda i: (i, 0)) for w in widths]
    out_shape = [jax.ShapeDtypeStruct((m, w), dt) for w, dt in zip(widths, dtypes)]
    in_specs = [
        pl.BlockSpec((tm, D_MODEL), lambda i: (i, 0)),
        _mod_spec(g, tps, 1),
        pl.BlockSpec((None, None, 1, D_MODEL), lambda i: (layer, 1, 0, 0)),
        pl.BlockSpec((None, D_IN_PAD, D_MODEL), lambda i: (layer, 0, 0)),
        pl.BlockSpec((None, 1, LANES), lambda i: (layer, 0, 0)),
    ]
    if transposed:
        assert g == 1
        nseq = m // rows_per_seq
        for idx, w in ((3, FOX_WIDTH), (4, FOX_WIDTH), (len(widths), LANES)):
            spec = pl.BlockSpec((None, w, tm), lambda i: (i // tps, 0, i % tps))
            shape = jax.ShapeDtypeStruct((nseq, w, rows_per_seq), F32)
            out_specs[idx:idx + 1], out_shape[idx:idx + 1] = [spec], [shape]
    return pl.pallas_call(
        functools.partial(_inproj_kernel, transposed=transposed),
        grid=(m // tm,),
        in_specs=in_specs,
        out_specs=out_specs,
        out_shape=out_shape,
        compiler_params=_params(48, ("arbitrary",)),
        name="inproj",
    )(x, mod4, npre, w_in_t, fbias)


def _outproj_kernel(x_ref, ya_ref, yb_ref, yc_ref, mod_ref, npost_ref, w_ref, o_ref):
    y = (_dot(ya_ref[...], w_ref[0:LRU_WIDTH, :])
         + _dot(yb_ref[...], w_ref[LRU_WIDTH:LRU_WIDTH + FOX_WIDTH, :])
         + _dot(yc_ref[...], w_ref[LRU_WIDTH + FOX_WIDTH:, :]))
    o_ref[...] = _post_norm(x_ref[...], y, npost_ref[...], mod_ref, 1.0)


def _outproj_call(x, ya, yb, yc, mod4, npost, w_out, layer, rows_per_seq):
    m = x.shape[0]
    tm = min(ROW_TILE, m)
    g, tps = _seq_grouping(rows_per_seq, tm)
    row = lambda w: pl.BlockSpec((tm, w), lambda i: (i, 0))
    return pl.pallas_call(
        _outproj_kernel,
        grid=(m // tm,),
        in_specs=[
            row(D_MODEL), row(LRU_WIDTH), row(FOX_WIDTH), row(SSD_WIDTH),
            _mod_spec(g, tps, 1),
            pl.BlockSpec((None, None, 1, D_MODEL), lambda i: (layer, 1, 0, 0)),
            pl.BlockSpec((None, D_MODEL, D_MODEL), lambda i: (layer, 0, 0)),
        ],
        out_specs=row(D_MODEL),
        out_shape=jax.ShapeDtypeStruct((m, D_MODEL), F32),
        compiler_params=_params(32, ("arbitrary",)),
        name="outproj",
    )(x, ya, yb, yc, mod4, npost, w_out)


def _causal_conv(xx, cw, cb, n):
    u = cb + cw[0:1] * pltpu.roll(xx, 3, 0)[SUBLANES:SUBLANES + n]
    u = u + cw[1:2] * pltpu.roll(xx, 2, 0)[SUBLANES:SUBLANES + n]
    u = u + cw[2:3] * pltpu.roll(xx, 1, 0)[SUBLANES:SUBLANES + n]
    return u + cw[3:4] * xx[SUBLANES:SUBLANES + n]


def _last_rows(xx, k):
    return pltpu.roll(xx, k, 0)[0:SUBLANES][0:k]


def _tri(n, lower):
    r = lax.broadcasted_iota(jnp.int32, (n, n), 0)
    c = lax.broadcasted_iota(jnp.int32, (n, n), 1)
    return ((r >= c) if lower else (r <= c)).astype(F32).astype(BF16)


def _split3(x):
    hi = x.astype(BF16)
    r1 = x - hi.astype(F32)
    mid = r1.astype(BF16)
    lo = (r1 - mid.astype(F32)).astype(BF16)
    return hi, mid, lo


def _cumsum_lanes(x, triu):
    hi, mid, lo = _split3(x)
    return _dot(hi, triu) + _dot(mid, triu) + _dot(lo, triu)


def _cumsum_rows(x, tril):
    hi, mid, lo = _split3(x)
    return _dot(tril, hi) + _dot(tril, mid) + _dot(tril, lo)


def _lru_kernel(x_ref, g_ref, prev_ref, h0_ref, cw_ref, cb_ref, wgate_ref, bgate_ref, lam_ref,
                y_ref, convnew_ref, hnew_ref, a_scr, b_scr):
    n = x_ref.shape[0]
    xx = jnp.concatenate([prev_ref[...], x_ref[...]], axis=0)
    convnew_ref[...] = _last_rows(xx, CONV_W - 1)
    u = _causal_conv(xx, cw_ref[...], cb_ref[...], n)
    gates = _dot(u.astype(BF16), wgate_ref[...]) + bgate_ref[...]
    r = jax.nn.sigmoid(gates[:, :LRU_WIDTH])
    i = jax.nn.sigmoid(gates[:, LRU_WIDTH:])
    log_a = (-LRU_C * r) * _softplus(-lam_ref[...])
    a = jnp.exp(log_a)
    b = jnp.sqrt(-jnp.tanh(log_a) * (a * a + 1.0)) * (i * u)
    row = lax.broadcasted_iota(jnp.int32, (n, 1), 0) % SUBLANES
    for d in (1, 2, 4):
        keep = row >= d
        b = jnp.where(keep, a * pltpu.roll(b, d, 0) + b, b)
        a = jnp.where(keep, a * pltpu.roll(a, d, 0), a)
    a_scr[...] = a
    b_scr[...] = b

    def group(j, h):
        off = pl.multiple_of(j * SUBLANES, SUBLANES)
        hb = a_scr[pl.ds(off, SUBLANES), :] * h + b_scr[pl.ds(off, SUBLANES), :]
        b_scr[pl.ds(off, SUBLANES), :] = hb
        return jnp.broadcast_to(hb[SUBLANES - 1:SUBLANES, :], hb.shape)

    h_last = lax.fori_loop(0, n // SUBLANES, group,
                           jnp.broadcast_to(h0_ref[...], (SUBLANES, LRU_WIDTH)), unroll=4)
    hnew_ref[...] = h_last[0:1]
    y_ref[...] = (b_scr[...] * jax.nn.gelu(g_ref[...])).astype(BF16)


def _lru_call(lru_x, lru_g, prev8, h0, cw, cb, wgate, bgate, lam, layer, state_layer):
    bsz, n, _ = lru_x.shape
    seq = pl.BlockSpec((None, n, LRU_WIDTH), lambda b: (b, 0, 0))
    if state_layer is None:
        prev_spec = pl.BlockSpec((None, SUBLANES, LRU_WIDTH), lambda b: (0, 0, 0))
        h0_spec = pl.BlockSpec((None, 1, LRU_WIDTH), lambda b: (0, 0, 0))
    else:
        prev_spec = pl.BlockSpec((None, None, SUBLANES, LRU_WIDTH), lambda b: (state_layer, b, 0, 0))
        h0_spec = pl.BlockSpec((None, None, 1, LRU_WIDTH), lambda b: (state_layer, b, 0, 0))
    par = lambda r, w: pl.BlockSpec((None, r, w), lambda b: (layer, 0, 0))
    return pl.pallas_call(
        _lru_kernel,
        grid=(bsz,),
        in_specs=[seq, seq, prev_spec, h0_spec, par(CONV_W, LRU_WIDTH), par(1, LRU_WIDTH),
                  par(LRU_WIDTH, 2 * LRU_WIDTH), par(1, 2 * LRU_WIDTH), par(1, LRU_WIDTH)],
        out_specs=[seq,
                   pl.BlockSpec((None, CONV_W - 1, LRU_WIDTH), lambda b: (b, 0, 0)),
                   pl.BlockSpec((None, 1, LRU_WIDTH), lambda b: (b, 0, 0))],
        out_shape=[jax.ShapeDtypeStruct((bsz, n, LRU_WIDTH), BF16),
                   jax.ShapeDtypeStruct((bsz, CONV_W - 1, LRU_WIDTH), F32),
                   jax.ShapeDtypeStruct((bsz, 1, LRU_WIDTH), F32)],
        scratch_shapes=[pltpu.VMEM((n, LRU_WIDTH), F32), pltpu.VMEM((n, LRU_WIDTH), F32)],
        compiler_params=_params(40, ("arbitrary",)),
        name="rg_lru",
    )(lru_x, lru_g, prev8, h0, cw, cb, wgate, bgate, lam)


def _ssd_kernel(xbc_ref, z_ref, small_ref, prev_ref, h0_ref, cw_ref, cb_ref, dtb_ref, alog_ref, dvec_ref, nw_ref,
                y_ref, convnew_ref, h_ref, tail_scr):
    n = xbc_ref.shape[0]
    c = pl.program_id(1)

    @pl.when(c == 0)
    def _():
        tail_scr[...] = prev_ref[...]
        h_ref[...] = h0_ref[...]

    xx = jnp.concatenate([tail_scr[...], xbc_ref[...]], axis=0)
    tail_scr[...] = xx[n:n + SUBLANES]
    convnew_ref[...] = _last_rows(xx, CONV_W - 1)
    act = _silu(_causal_conv(xx, cw_ref[...], cb_ref[...], n))
    xs = act[:, :SSD_WIDTH]
    bm = act[:, SSD_WIDTH:SSD_WIDTH + SSD_GROUPS * D_STATE]
    cm = act[:, SSD_WIDTH + SSD_GROUPS * D_STATE:]

    dt = _softplus(small_ref[...] + dtb_ref[...])
    dta = dt * (-jnp.exp(alog_ref[...]))
    cum = _cumsum_rows(dta, _tri(n, True))
    cum_t = cum.T
    rr = lax.broadcasted_iota(jnp.int32, (n, n), 0)
    cc = lax.broadcasted_iota(jnp.int32, (n, n), 1)
    causal = rr >= cc
    lo_lane = lax.broadcasted_iota(jnp.int32, (1, LANES), 1) < SSD_HEAD_DIM
    lo_row = lax.broadcasted_iota(jnp.int32, (LANES, 1), 0) < SSD_HEAD_DIM
    dvec = dvec_ref[...]

    ys = []
    for g in range(SSD_GROUPS):
        sl = slice(g * LANES, (g + 1) * LANES)
        xg, bg, cg = xs[:, sl], bm[:, sl].astype(BF16), cm[:, sl].astype(BF16)
        heads = (2 * g, 2 * g + 1)
        col = lambda a, h: a[:, DT_LANE0 + h:DT_LANE0 + h + 1]
        pick = lambda f: jnp.where(lo_lane, f(heads[0]), f(heads[1]))
        dx = xg * pick(lambda h: col(dt, h))
        dxb = dx.astype(BF16)
        cb_mat = _dot_nt(cg, bg)
        yd = []
        for h in heads:
            seg = col(cum, h) - cum_t[DT_LANE0 + h:DT_LANE0 + h + 1, :]
            lmat = jnp.exp(jnp.where(causal, seg, NEG_BIG))
            yd.append(_dot((cb_mat * lmat).astype(BF16), dxb))
        y_diag = jnp.where(lo_lane, yd[0], yd[1])
        last = lambda h: col(cum, h)[n - 1:n, :]
        decay_end = pick(lambda h: jnp.exp(last(h) - col(cum, h)))
        states = _dot_tn((dx * decay_end).astype(BF16), bg)
        h_prev = h_ref[sl, :]
        y_off = _dot_nt(cg, h_prev.astype(BF16)) * pick(lambda h: jnp.exp(col(cum, h)))
        chunk_decay = jnp.where(lo_row, jnp.exp(last(heads[0])), jnp.exp(last(heads[1])))
        h_ref[sl, :] = chunk_decay * h_prev + states
        ys.append(y_diag + y_off + dvec[:, sl] * xg)
    y = jnp.concatenate(ys, axis=1)
    y_ref[...] = (_rms(y * _silu(z_ref[...])) * nw_ref[...]).astype(BF16)


def _ssd_call(xbc, z, small, prev8, h0, cw, cb, dtb, alog, dvec, nw, layer, state_layer, chunk):
    bsz, n, _ = xbc.shape
    nc = n // chunk
    seq = lambda w: pl.BlockSpec((None, chunk, w), lambda b, c: (b, c, 0))
    if state_layer is None:
        prev_spec = pl.BlockSpec((None, SUBLANES, SSD_CONV_DIM), lambda b, c: (0, 0, 0))
        h0_spec = pl.BlockSpec((None, SSD_WIDTH, D_STATE), lambda b, c: (0, 0, 0))
    else:
        prev_spec = pl.BlockSpec((None, None, SUBLANES, SSD_CONV_DIM), lambda b, c: (state_layer, b, 0, 0))
        h0_spec = pl.BlockSpec((None, None, SSD_WIDTH, D_STATE), lambda b, c: (state_layer, b, 0, 0))
    par = lambda r, w: pl.BlockSpec((None, r, w), lambda b, c: (layer, 0, 0))
    return pl.pallas_call(
        _ssd_kernel,
        grid=(bsz, nc),
        in_specs=[seq(SSD_CONV_DIM), seq(SSD_WIDTH), seq(LANES), prev_spec, h0_spec,
                  par(CONV_W, SSD_CONV_DIM), par(1, SSD_CONV_DIM), par(1, LANES), par(1, LANES),
                  par(1, SSD_WIDTH), par(1, SSD_WIDTH)],
        out_specs=[seq(SSD_WIDTH),
                   pl.BlockSpec((None, CONV_W - 1, SSD_CONV_DIM), lambda b, c: (b, 0, 0)),
                   pl.BlockSpec((None, SSD_WIDTH, D_STATE), lambda b, c: (b, 0, 0))],
        out_shape=[jax.ShapeDtypeStruct((bsz, n, SSD_WIDTH), BF16),
                   jax.ShapeDtypeStruct((bsz, CONV_W - 1, SSD_CONV_DIM), F32),
                   jax.ShapeDtypeStruct((bsz, SSD_WIDTH, D_STATE), F32)],
        scratch_shapes=[pltpu.VMEM((SUBLANES, SSD_CONV_DIM), F32)],
        compiler_params=_params(32, ("arbitrary", "arbitrary")),
        name="ssd",
    )(xbc, z, small, prev8, h0, cw, cb, dtb, alog, dvec, nw)


def _fox_prompt_kernel(q_ref, kt_ref, vt_ref, logft_ref, o_ref, k_scr, v_scr, f_scr, ft_scr):
    s = kt_ref.shape[1]
    tq = q_ref.shape[0]
    p = pl.program_id(1)
    qi = pl.program_id(2)
    heads = (HEAD_PAIR * p, HEAD_PAIR * p + 1)
    spare = (FOX_HEAD_DIM, 0)

    @pl.when((qi == 0) & (p == 0))
    def _():
        triu = _tri(SCAN_BLOCK, False)
        carry = jnp.zeros((LANES, 1), F32)
        per = tq // SCAN_BLOCK
        for c in range(s // SCAN_BLOCK):
            cols = slice(c * SCAN_BLOCK, (c + 1) * SCAN_BLOCK)
            fc = _cumsum_lanes(logft_ref[:, cols], triu) + carry
            ft_scr[c // per, :, (c % per) * SCAN_BLOCK:(c % per + 1) * SCAN_BLOCK] = fc
            f_scr[cols, :] = fc.T
            carry = fc[:, SCAN_BLOCK - 1:SCAN_BLOCK]

    @pl.when(qi == 0)
    def _():
        row = lax.broadcasted_iota(jnp.int32, (LANES, 1), 0)
        for j in range(s // tq):
            kt = kt_ref[:, j * tq:(j + 1) * tq]
            vt = vt_ref[:, j * tq:(j + 1) * tq]
            for hl in range(HEAD_PAIR):
                own = (row < FOX_HEAD_DIM) if hl == 0 else (row >= FOX_HEAD_DIM)
                f_k = ft_scr[j, pl.ds(F_LANE0 + heads[hl], 1), :] * LOG2E
                kc = jnp.where(own, kt, 0.0).astype(BF16)
                for i, part in enumerate(_split3(-f_k)):
                    kc = jnp.where(row == spare[hl] + i, part, kc)
                k_scr[hl, j] = kc
                v_scr[hl, j] = jnp.where(own, vt, jnp.where(row == spare[hl], 1.0, 0.0)).astype(BF16)

    lane = lax.broadcasted_iota(jnp.int32, (1, LANES), 1)
    lo_lane = lane < FOX_HEAD_DIM
    q0 = pl.multiple_of(qi * tq, tq)
    rb = min(ATTN_ROWS, tq)
    nrb = tq // rb
    q_blocks, fq_blocks = [], []
    for r in range(nrb):
        q = q_ref[r * rb:(r + 1) * rb, :].astype(F32)
        ones_at = lambda l0: jnp.where((lane >= l0) & (lane < l0 + 3), 1.0, 0.0)
        q_blocks.append((jnp.where(lo_lane, q, ones_at(spare[0])).astype(BF16),
                         jnp.where(lo_lane, ones_at(spare[1]), q).astype(BF16)))
        f_rows = f_scr[pl.ds(pl.multiple_of(q0 + r * rb, rb), rb), :]
        fq_blocks.append([jnp.sum(jnp.where(lane == F_LANE0 + h, f_rows, 0.0), axis=1, keepdims=True) * LOG2E
                          for h in heads])

    def softmax_unit(t, f_q, m_old, row0):
        if row0 is not None:
            rr = lax.broadcasted_iota(jnp.int32, t.shape, 0) + row0
            cc = lax.broadcasted_iota(jnp.int32, t.shape, 1)
            t = jnp.where(cc <= rr, t, NEG_BIG)
        m_new = jnp.maximum(m_old, jnp.max(t, axis=1, keepdims=True) + f_q)
        return m_new, jnp.exp2(m_old - m_new), jnp.exp2(t - (m_new - f_q)).astype(BF16)

    def step(j, carry, diagonal):
        nks = [(r + 1) * rb if diagonal else tq for r in range(nrb)]
        units = [(r, hl) for r in range(nrb) for hl in range(HEAD_PAIR)]
        dots = {(r, hl): _dot(q_blocks[r][hl], k_scr[hl, j, :, 0:nks[r]]) for r, hl in units}
        soft = {(r, hl): softmax_unit(dots[(r, hl)], fq_blocks[r][hl], carry[r][2 * hl], r * rb if diagonal else None)
                for r, hl in units}
        pvs = {(r, hl): _dot_nt(soft[(r, hl)][2], v_scr[hl, j, :, 0:nks[r]]) for r, hl in units}
        new = []
        for r in range(nrb):
            row_state = ()
            for hl in range(HEAD_PAIR):
                m_new, alpha, _ = soft[(r, hl)]
                row_state += (m_new, alpha * carry[r][2 * hl + 1] + pvs[(r, hl)])
            new.append(row_state)
        return tuple(new)

    init_rows = (jnp.full((rb, 1), NEG_BIG, F32), jnp.zeros((rb, LANES), F32)) * HEAD_PAIR
    carry = lax.fori_loop(0, qi, lambda j, cr: step(j, cr, False), (init_rows,) * nrb)
    carry = step(qi, carry, True)
    for r in range(nrb):
        acc = [carry[r][2 * hl + 1] for hl in range(HEAD_PAIR)]
        out = [a / a[:, spare[hl]:spare[hl] + 1] for hl, a in enumerate(acc)]
        o_ref[r * rb:(r + 1) * rb, :] = jnp.where(lo_lane, out[0], out[1]).astype(BF16)


def _fox_prompt_call(q, kt, vt, small_t):
    bsz, s, _ = q.shape
    tq = min(ATTN_BLOCK, s)
    blk = pl.BlockSpec((None, tq, LANES), lambda b, p, i: (b, i, p))
    full = pl.BlockSpec((None, LANES, s), lambda b, p, i: (b, p, 0))
    return pl.pallas_call(
        _fox_prompt_kernel,
        grid=(bsz, FOX_WIDTH // LANES, s // tq),
        in_specs=[blk, full, full, pl.BlockSpec((None, LANES, s), lambda b, p, i: (b, 0, 0))],
        out_specs=blk,
        out_shape=jax.ShapeDtypeStruct((bsz, s, FOX_WIDTH), BF16),
        scratch_shapes=[pltpu.VMEM((HEAD_PAIR, s // tq, LANES, tq), BF16),
                        pltpu.VMEM((HEAD_PAIR, s // tq, LANES, tq), BF16),
                        pltpu.VMEM((s, LANES), F32), pltpu.VMEM((s // tq, LANES, tq), F32)],
        compiler_params=_params(40, ("arbitrary", "arbitrary", "arbitrary")),
        name="fox_prompt",
    )(q, kt, vt, small_t)


def _fox_sample_kernel(q_ref, k_ref, v_ref, logf_ref, ck_ref, cv_ref, clogft_ref, o_ref,
                       m_scr, l_scr, acc_scr, fk_scr, fq_scr, fnew_scr):
    t = q_ref.shape[0]
    c = pl.program_id(1)
    nchunk, _, chunk = fk_scr.shape

    @pl.when(c == 0)
    def _():
        blk = min(SCAN_BLOCK, chunk)
        triu = _tri(blk, False)
        carry = jnp.zeros((FOX_HEADS, 1), F32)
        for i in range(nchunk):
            for j in range(chunk // blk):
                cols = slice(j * blk, (j + 1) * blk)
                fc = _cumsum_lanes(clogft_ref[:, i * chunk + j * blk:i * chunk + (j + 1) * blk], triu) + carry
                fk_scr[i, :, cols] = fc * LOG2E
                carry = fc[:, blk - 1:blk]
        sub = lax.broadcasted_iota(jnp.int32, (FOX_HEADS, LANES), 0)
        ln = lax.broadcasted_iota(jnp.int32, (FOX_HEADS, LANES), 1)
        total_row = jnp.sum(jnp.where(sub == ln - F_LANE0, carry, 0.0), axis=0, keepdims=True)
        f_new = _cumsum_rows(logf_ref[...], _tri(t, True)) + total_row
        f_new = f_new * LOG2E
        fnew_scr[...] = f_new.T[F_LANE0:F_LANE0 + FOX_HEADS, :]
        fq_scr[...] = jnp.concatenate([f_new[:, F_LANE0 + h:F_LANE0 + h + 1] for h in range(FOX_HEADS)], axis=0)
        m_scr[...] = jnp.full(m_scr.shape, NEG_BIG, F32)
        l_scr[...] = jnp.zeros(l_scr.shape, F32)
        acc_scr[...] = jnp.zeros(acc_scr.shape, F32)

    q = q_ref[...]
    q_heads = [q[:, h * FOX_HEAD_DIM:(h + 1) * FOX_HEAD_DIM] for h in range(FOX_HEADS)]
    f_q = fq_scr[...]

    def update(keys, values, f_k, mask, channels_first):
        qk = _dot if channels_first else _dot_nt
        pv_dot = _dot_nt if channels_first else _dot
        tt = jnp.concatenate([qk(q_heads[h], keys[h]) - f_k[h:h + 1, :] for h in range(FOX_HEADS)], axis=0)
        if mask is not None:
            tt = jnp.where(mask, tt, NEG_BIG)
        m_old = m_scr[...]
        m_new = jnp.maximum(m_old, jnp.max(tt, axis=1, keepdims=True) + f_q)
        alpha = jnp.exp2(m_old - m_new)
        pr = jnp.exp2(tt - (m_new - f_q))
        l_scr[...] = alpha * l_scr[...] + jnp.sum(pr, axis=1, keepdims=True)
        prb = pr.astype(BF16)
        pv = jnp.concatenate([pv_dot(prb[h * t:(h + 1) * t, :], values[h]) for h in range(FOX_HEADS)], axis=0)
        acc_scr[...] = alpha * acc_scr[...] + pv
        m_scr[...] = m_new

    head_rows = lambda ref: [ref[h * FOX_HEAD_DIM:(h + 1) * FOX_HEAD_DIM, :].astype(BF16) for h in range(FOX_HEADS)]
    update(head_rows(ck_ref), head_rows(cv_ref), fk_scr[c], None, True)

    @pl.when(c == nchunk - 1)
    def _():
        k_new, v_new = k_ref[...].astype(BF16), v_ref[...].astype(BF16)
        cols = lambda a: [a[:, h * FOX_HEAD_DIM:(h + 1) * FOX_HEAD_DIM] for h in range(FOX_HEADS)]
        rr = lax.broadcasted_iota(jnp.int32, (FOX_HEADS * t, t), 0) % t
        cc = lax.broadcasted_iota(jnp.int32, (FOX_HEADS * t, t), 1)
        update(cols(k_new), cols(v_new), fnew_scr[...], cc <= rr, False)
        out = acc_scr[...] / l_scr[...]
        o_ref[...] = jnp.concatenate([out[h * t:(h + 1) * t, :] for h in range(FOX_HEADS)], axis=1).astype(BF16)


def _fox_sample_call(q, k, v, small, cache_k, cache_v, cache_logf_t, layer):
    bsz, t, _ = q.shape
    past = cache_logf_t.shape[3]
    chunk = min(1024, past)
    nchunk = past // chunk
    new = lambda w: pl.BlockSpec((None, t, w), lambda b, c: (b, 0, 0))
    cache = pl.BlockSpec((None, None, FOX_WIDTH, chunk), lambda b, c: (layer, b, 0, c))
    rows = FOX_HEADS * t
    return pl.pallas_call(
        _fox_sample_kernel,
        grid=(bsz, nchunk),
        in_specs=[new(FOX_WIDTH), new(FOX_WIDTH), new(FOX_WIDTH), new(LANES), cache, cache,
                  pl.BlockSpec((None, None, FOX_HEADS, past), lambda b, c: (layer, b, 0, 0))],
        out_specs=new(FOX_WIDTH),
        out_shape=jax.ShapeDtypeStruct((bsz, t, FOX_WIDTH), BF16),
        scratch_shapes=[pltpu.VMEM((rows, 1), F32), pltpu.VMEM((rows, 1), F32), pltpu.VMEM((rows, FOX_HEAD_DIM), F32),
                        pltpu.VMEM((nchunk, FOX_HEADS, chunk), F32), pltpu.VMEM((rows, 1), F32),
                        pltpu.VMEM((FOX_HEADS, t), F32)],
        compiler_params=_params(40, ("arbitrary", "arbitrary")),
        name="fox_sample",
    )(q, k, v, small, cache_k, cache_v, cache_logf_t)


def _permute_w_in_t(w_in):
    w_t = jnp.swapaxes(w_in, 1, 2)
    offs = np.concatenate([[0], np.cumsum(IN_SIZES)])
    seg = lambda i: w_t[:, int(offs[i]):int(offs[i + 1]), :]
    pad = jnp.zeros((w_t.shape[0], LANES - FOX_HEADS - SSD_HEADS, w_t.shape[2]), w_t.dtype)
    return jnp.concatenate([seg(0), seg(1), seg(2), seg(3), seg(4), seg(6), seg(7), seg(5), seg(8), pad],
                           axis=1).astype(BF16)


def _block_diag(w):
    d, h, b, _ = w.shape
    eye = jnp.eye(h, dtype=w.dtype)
    return jnp.einsum("dhij,hg->dhigj", w, eye).reshape(d, h * b, h * b)


def _lane_slab(v, lane0):
    d, k = v.shape
    return jnp.zeros((d, 1, LANES), v.dtype).at[:, 0, lane0:lane0 + k].set(v)


def _pad_history(state):
    return jnp.pad(state, ((0, 0), (0, 0), (SUBLANES - (CONV_W - 1), 0), (0, 0)))


def _trunk(x, mod_group, caches, prm, ssd_chunk):
    bsz, n, _ = x.shape
    x = x.reshape(bsz * n, D_MODEL)
    states = {name: [] for name in ("fox_k", "fox_v", "fox_logf", "lru_conv", "lru_h", "ssd_conv", "ssd_h")}
    for l in range(DEPTH):
        mod4 = mod_group[l].reshape(bsz, N_SUB, 3, D_MODEL)
        x = _ffn_call(x, mod4, prm["npre"], prm["npost"], prm["wg"], prm["wu"], prm["wd"], l, 0, 0, n)
        prompt = caches is None
        proj = _inproj_call(x, mod4, prm["npre"], prm["w_in"], prm["fbias"], l, n, transposed=prompt)
        lrux, lrug, q, k, v, z, xbc, small = proj[:8]
        per_seq = lambda a: a.reshape(bsz, n, a.shape[-1])
        state_layer = None if prompt else l
        src = prm["zero_state"] if prompt else caches
        ya, lru_conv, lru_h = _lru_call(per_seq(lrux), per_seq(lrug), src["lru_conv"], src["lru_h"],
                                        prm["lru_cw"], prm["lru_cb"], prm["lru_wgate"], prm["lru_bgate"],
                                        prm["lru_lam"], l, state_layer)
        if prompt:
            small_t = proj[8]
            yb = _fox_prompt_call(per_seq(q), k, v, small_t)
            heads_last = lambda a: jnp.transpose(a.reshape(bsz, FOX_HEADS, FOX_HEAD_DIM, n), (0, 3, 1, 2))
            k_out, v_out = heads_last(k), heads_last(v)
            logf_out = jnp.swapaxes(small_t[:, F_LANE0:F_LANE0 + FOX_HEADS, :], 1, 2)
        else:
            yb = _fox_sample_call(per_seq(q), per_seq(k), per_seq(v), per_seq(small),
                                  caches["fox_k"], caches["fox_v"], caches["fox_logf_t"], l)
            k_out = k.reshape(bsz, n, FOX_HEADS, FOX_HEAD_DIM)
            v_out = v.reshape(bsz, n, FOX_HEADS, FOX_HEAD_DIM)
            logf_out = small.reshape(bsz, n, LANES)[:, :, F_LANE0:F_LANE0 + FOX_HEADS]
        yc, ssd_conv, ssd_h = _ssd_call(per_seq(xbc), per_seq(z), per_seq(small), src["ssd_conv"], src["ssd_h"],
                                        prm["ssd_cw"], prm["ssd_cb"], prm["ssd_dtb"], prm["ssd_alog"],
                                        prm["ssd_dvec"], prm["ssd_nw"], l, state_layer, ssd_chunk)
        x = _outproj_call(x, ya.reshape(bsz * n, -1), yb.reshape(bsz * n, -1), yc.reshape(bsz * n, -1),
                          mod4, prm["npost"], prm["w_out"], l, n)
        x = _ffn_call(x, mod4, prm["npre"], prm["npost"], prm["wg"], prm["wu"], prm["wd"], l, 2, 1, n)
        states["fox_k"].append(k_out)
        states["fox_v"].append(v_out)
        states["fox_logf"].append(logf_out)
        states["lru_conv"].append(lru_conv)
        states["lru_h"].append(lru_h.reshape(bsz, LRU_WIDTH))
        states["ssd_conv"].append(ssd_conv)
        states["ssd_h"].append(ssd_h.reshape(bsz, SSD_HEADS, SSD_HEAD_DIM, D_STATE))
    return x.reshape(bsz, n, D_MODEL), {name: jnp.stack(vals, axis=0) for name, vals in states.items()}


def kernel(x_prompt, x_sample, c_prompt, c_sample, cache_fox_k, cache_fox_v, cache_fox_logf, state_lru_conv, state_lru_h, state_ssd_conv, state_ssd_h, w_mod, b_mod, norm_pre, norm_post, ffn_w_gate, ffn_w_up, ffn_w_down, w_in, w_out, lru_conv_w, lru_conv_b, lru_wa, lru_ba, lru_wx, lru_bx, lru_lambda, fox_f_bias, ssd_conv_w, ssd_conv_b, ssd_dt_bias, ssd_a_log, ssd_d, ssd_norm_w):
    n_prompt, n_sample = x_prompt.shape[0], x_sample.shape[0]
    prm = {
        "npre": norm_pre.reshape(DEPTH, N_SUB, 1, D_MODEL),
        "npost": norm_post.reshape(DEPTH, N_SUB, 1, D_MODEL),
        "wg": ffn_w_gate.astype(BF16), "wu": ffn_w_up.astype(BF16), "wd": ffn_w_down.astype(BF16),
        "w_in": _permute_w_in_t(w_in), "w_out": w_out.astype(BF16),
        "fbias": _lane_slab(fox_f_bias, F_LANE0),
        "lru_cw": lru_conv_w, "lru_cb": lru_conv_b.reshape(DEPTH, 1, LRU_WIDTH),
        "lru_wgate": jnp.concatenate([_block_diag(lru_wa), _block_diag(lru_wx)], axis=-1).astype(BF16),
        "lru_bgate": jnp.concatenate([lru_ba, lru_bx], axis=-1).reshape(DEPTH, 1, 2 * LRU_WIDTH),
        "lru_lam": lru_lambda.reshape(DEPTH, 1, LRU_WIDTH),
        "ssd_cw": ssd_conv_w, "ssd_cb": ssd_conv_b.reshape(DEPTH, 1, SSD_CONV_DIM),
        "ssd_dtb": _lane_slab(ssd_dt_bias, DT_LANE0), "ssd_alog": _lane_slab(ssd_a_log, DT_LANE0),
        "ssd_dvec": jnp.repeat(ssd_d, SSD_HEAD_DIM, axis=-1).reshape(DEPTH, 1, SSD_WIDTH),
        "ssd_nw": ssd_norm_w.reshape(DEPTH, 1, SSD_WIDTH),
        "zero_state": {
            "lru_conv": jnp.zeros((1, SUBLANES, LRU_WIDTH), F32), "lru_h": jnp.zeros((1, 1, LRU_WIDTH), F32),
            "ssd_conv": jnp.zeros((1, SUBLANES, SSD_CONV_DIM), F32), "ssd_h": jnp.zeros((1, SSD_WIDTH, D_STATE), F32),
        },
    }
    caches = {
        "fox_k": jnp.transpose(cache_fox_k, (0, 1, 3, 4, 2)).reshape(DEPTH, n_sample, FOX_WIDTH, -1),
        "fox_v": jnp.transpose(cache_fox_v, (0, 1, 3, 4, 2)).reshape(DEPTH, n_sample, FOX_WIDTH, -1),
        "fox_logf_t": jnp.swapaxes(cache_fox_logf, 2, 3),
        "lru_conv": _pad_history(state_lru_conv),
        "lru_h": state_lru_h.reshape(DEPTH, n_sample, 1, LRU_WIDTH),
        "ssd_conv": _pad_history(state_ssd_conv),
        "ssd_h": state_ssd_h.reshape(DEPTH, n_sample, SSD_WIDTH, D_STATE),
    }
    mod = _mod_call(jnp.concatenate([c_prompt, c_sample], axis=0), w_mod, b_mod)
    y_prompt, sp = _trunk(x_prompt, mod[:, :n_prompt], None, prm, ssd_chunk=256)
    y_sample, ss = _trunk(x_sample, mod[:, n_prompt:], caches, prm, ssd_chunk=x_sample.shape[1])
    names = ("fox_k", "fox_v", "fox_logf", "lru_conv", "lru_h", "ssd_conv", "ssd_h")
    return (y_prompt, y_sample) + tuple(sp[n] for n in names) + tuple(ss[n] for n in names)
```

```python
import functools

import numpy as np
import jax
import jax.numpy as jnp
from jax import lax
from jax.experimental import pallas as pl
from jax.experimental.pallas import tpu as pltpu

F32 = jnp.float32
BF16 = jnp.bfloat16

D_MODEL = 1024
DEPTH = 2
CONV_W = 4
EPS = 1e-6
LRU_WIDTH = 256
LRU_HEADS = 4
LRU_BLOCK = LRU_WIDTH // LRU_HEADS
LRU_C = 8.0
FOX_HEADS = 8
FOX_HEAD_DIM = 64
FOX_WIDTH = FOX_HEADS * FOX_HEAD_DIM
SSD_HEADS = 4
SSD_HEAD_DIM = 64
SSD_WIDTH = SSD_HEADS * SSD_HEAD_DIM
SSD_GROUPS = 2
D_STATE = 128
SSD_CONV_DIM = SSD_WIDTH + 2 * SSD_GROUPS * D_STATE
IN_SIZES = (LRU_WIDTH, LRU_WIDTH, FOX_WIDTH, FOX_WIDTH, FOX_WIDTH, FOX_HEADS, SSD_WIDTH, SSD_CONV_DIM, SSD_HEADS)
D_FF = 2816
N_SUB = 3

LANES = 128
SUBLANES = 8
HEAD_PAIR = LANES // FOX_HEAD_DIM

F_LANE0 = 0
DT_LANE0 = FOX_HEADS
COL_LRU_X = 0
COL_LRU_G = COL_LRU_X + LRU_WIDTH
COL_Q = COL_LRU_G + LRU_WIDTH
COL_K = COL_Q + FOX_WIDTH
COL_V = COL_K + FOX_WIDTH
COL_Z = COL_V + FOX_WIDTH
COL_XBC = COL_Z + SSD_WIDTH
COL_SMALL = COL_XBC + SSD_CONV_DIM
D_IN_PAD = COL_SMALL + LANES
D_IN = sum(IN_SIZES)

ROW_TILE = 512
FFN_ROW_TILE = 1024
FF_CHUNK = 512
ATTN_BLOCK = 512
ATTN_ROWS = 128
SCAN_BLOCK = 256
SAMPLE_CHUNK = 2048
NEG_BIG = -1e30
LOG2E = 1.4426950408889634
Q_SCALE = LOG2E * FOX_HEAD_DIM ** -0.5


def _dot(a, b):
    return jnp.dot(a, b, preferred_element_type=F32)


def _dot_nt(a, b):
    return lax.dot_general(a, b, (((1,), (1,)), ((), ())), preferred_element_type=F32)


def _dot_tn(a, b):
    return lax.dot_general(a, b, (((0,), (0,)), ((), ())), preferred_element_type=F32)


def _silu(x):
    return x * jax.nn.sigmoid(x)


def _softplus(x):
    return jnp.maximum(x, 0.0) + jnp.log1p(jnp.exp(-jnp.abs(x)))


def _rms(x):
    return x * lax.rsqrt(jnp.mean(x * x, axis=-1, keepdims=True) + EPS)


def _per_seq(rows, per_seq, fn):
    g = per_seq[0].shape[0]
    if g == 1:
        return fn(rows, *per_seq)
    tm, d = rows.shape
    out = fn(rows.reshape(g, tm // g, d), *[p[:, None, :] for p in per_seq])
    return out.reshape(tm, d)


def _pre_norm(x, npre, mod_ref, seqs=slice(None)):
    h = _rms(x) * npre
    return _per_seq(h, (mod_ref[seqs, 1, :], mod_ref[seqs, 0, :]), lambda r, sc, sh: r * (1.0 + sc) + sh)


def _post_norm(x, y, npost, mod_ref, w, seqs=slice(None)):
    yn = _rms(y) * npost
    return x + _per_seq(yn, (mod_ref[seqs, 2, :],), lambda r, gt: (w * gt) * r)


def _seq_grouping(rows_per_seq, tm):
    if rows_per_seq % tm == 0:
        return 1, rows_per_seq // tm
    assert tm % rows_per_seq == 0
    return tm // rows_per_seq, 1


def _mod_spec(g, tiles_per_seq, sub):
    if g == 1:
        return pl.BlockSpec((1, None, 3, D_MODEL), lambda i: (i // tiles_per_seq, sub, 0, 0))
    return pl.BlockSpec((g, None, 3, D_MODEL), lambda i: (i, sub, 0, 0))


def _params(vmem_mb, sem):
    return pltpu.CompilerParams(dimension_semantics=sem, vmem_limit_bytes=vmem_mb << 20)


def _mod_kernel(c_ref, w_ref, b_ref, o_ref):
    a = _silu(c_ref[...]).astype(BF16)
    o_ref[...] = _dot(a, w_ref[...].astype(BF16)) + b_ref[...]


def _mod_call(c_all, w_mod, b_mod):
    nseq = c_all.shape[0]
    width = N_SUB * 3 * D_MODEL
    tn = 1024
    return pl.pallas_call(
        _mod_kernel,
        grid=(DEPTH, width // tn),
        in_specs=[
            pl.BlockSpec((nseq, D_MODEL), lambda l, n: (0, 0)),
            pl.BlockSpec((None, D_MODEL, tn), lambda l, n: (l, 0, n)),
            pl.BlockSpec((None, 1, tn), lambda l, n: (l, 0, n)),
        ],
        out_specs=pl.BlockSpec((None, nseq, tn), lambda l, n: (l, 0, n)),
        out_shape=jax.ShapeDtypeStruct((DEPTH, nseq, width), F32),
        compiler_params=_params(24, ("arbitrary", "arbitrary")),
        name="adaln_mod",
    )(c_all, w_mod, b_mod.reshape(DEPTH, 1, width))


def _ffn_kernel(x_ref, mod_ref, npre_ref, npost_ref, wg_ref, wu_ref, wd_ref, o_ref):
    tm = x_ref.shape[0]
    half = tm // 2
    nseq = mod_ref.shape[0]
    seqs = [slice(0, 1)] * 2 if nseq == 1 else [slice(i * nseq // 2, (i + 1) * nseq // 2) for i in range(2)]
    xs = [x_ref[i * half:(i + 1) * half, :] for i in range(2)]
    hs = [_pre_norm(x, npre_ref[...], mod_ref, sq).astype(BF16) for x, sq in zip(xs, seqs)]
    accs = [None, None]
    for off in range(0, D_FF, FF_CHUNK):
        fc = min(FF_CHUNK, D_FF - off)
        for i in range(2):
            g = _dot(hs[i], wg_ref[:, off:off + fc])
            u = _dot(hs[i], wu_ref[:, off:off + fc])
            a = (_silu(g) * u).astype(BF16)
            d = _dot(a, wd_ref[off:off + fc, :])
            accs[i] = d if accs[i] is None else accs[i] + d
    for i in range(2):
        o_ref[i * half:(i + 1) * half, :] = _post_norm(xs[i], accs[i], npost_ref[...], mod_ref, 0.5, seqs[i])


def _ffn_call(x, mod4, npre, npost, wg, wu, wd, layer, sub, ffn_idx, rows_per_seq):
    m = x.shape[0]
    tm = min(FFN_ROW_TILE, m)
    g, tps = _seq_grouping(rows_per_seq, tm)
    wspec = lambda shape: pl.BlockSpec((None, None) + shape, lambda i: (layer, ffn_idx, 0, 0),
                                       pipeline_mode=pl.Buffered(1))
    nspec = pl.BlockSpec((None, None, 1, D_MODEL), lambda i: (layer, sub, 0, 0))
    return pl.pallas_call(
        _ffn_kernel,
        grid=(m // tm,),
        in_specs=[
            pl.BlockSpec((tm, D_MODEL), lambda i: (i, 0)),
            _mod_spec(g, tps, sub),
            nspec, nspec,
            wspec((D_MODEL, D_FF)), wspec((D_MODEL, D_FF)), wspec((D_FF, D_MODEL)),
        ],
        out_specs=pl.BlockSpec((tm, D_MODEL), lambda i: (i, 0)),
        out_shape=jax.ShapeDtypeStruct((m, D_MODEL), F32),
        compiler_params=_params(56, ("arbitrary",)),
        name="ffn",
    )(x, mod4, npre, npost, wg, wu, wd)


def _inproj_kernel(x_ref, mod_ref, npre_ref, w_ref, wsmall_ref, fbias_ref,
                   lrux_ref, lrug_ref, q_ref, k_ref, v_ref, z_ref, xbc_ref, small_ref, *rest, transposed):
    wt_ref = rest[-1]

    @pl.when(pl.program_id(0) == 0)
    def _():
        src = int(np.cumsum(IN_SIZES)[4])
        assert src == COL_Z
        moves = [(0, 0, src), (src + FOX_HEADS, COL_Z, SSD_WIDTH), (src + FOX_HEADS + SSD_WIDTH, COL_XBC, SSD_CONV_DIM)]
        for s0, d0, n in moves:
            for off in range(0, n, LRU_WIDTH):
                wt_ref[d0 + off:d0 + off + LRU_WIDTH, :] = w_ref[s0 + off:s0 + off + LRU_WIDTH, :].astype(BF16)
        wt_ref[COL_SMALL:COL_SMALL + LANES, :] = wsmall_ref[...].astype(BF16)

    h = _pre_norm(x_ref[...], npre_ref[...], mod_ref).astype(BF16)
    col = lambda start, width: _dot_nt(h, wt_ref[start:start + width, :])
    col_t = lambda start, width: _dot_nt(wt_ref[start:start + width, :], h)
    lrux_ref[...] = col(COL_LRU_X, LRU_WIDTH)
    lrug_ref[...] = col(COL_LRU_G, LRU_WIDTH)
    q_ref[...] = (col(COL_Q, FOX_WIDTH) * Q_SCALE).astype(BF16)
    if transposed:
        k_ref[...] = col_t(COL_K, FOX_WIDTH)
        v_ref[...] = col_t(COL_V, FOX_WIDTH)
    else:
        k_ref[...] = col(COL_K, FOX_WIDTH)
        v_ref[...] = col(COL_V, FOX_WIDTH)
    z_ref[...] = col(COL_Z, SSD_WIDTH)
    xbc_ref[...] = col(COL_XBC, SSD_CONV_DIM)
    small = col(COL_SMALL, LANES)
    t = small + fbias_ref[...]
    logf = jnp.minimum(t, 0.0) - jnp.log1p(jnp.exp(-jnp.abs(t)))
    lane = lax.broadcasted_iota(jnp.int32, small.shape, 1)
    small = jnp.where(lane < DT_LANE0, logf, small)
    small_ref[...] = small
    if transposed:
        rest[0][...] = small.T


def _inproj_call(x, mod4, npre, w_in_t, w_small, fbias, layer, rows_per_seq, transposed):
    m = x.shape[0]
    tm = min(ROW_TILE, m)
    g, tps = _seq_grouping(rows_per_seq, tm)
    widths = (LRU_WIDTH, LRU_WIDTH, FOX_WIDTH, FOX_WIDTH, FOX_WIDTH, SSD_WIDTH, SSD_CONV_DIM, LANES)
    dtypes = (F32, F32, BF16, F32, F32, F32, F32, F32)
    out_specs = [pl.BlockSpec((tm, w), lambda i: (i, 0)) for w in widths]
    out_shape = [jax.ShapeDtypeStruct((m, w), dt) for w, dt in zip(widths, dtypes)]
    in_specs = [
        pl.BlockSpec((tm, D_MODEL), lambda i: (i, 0)),
        _mod_spec(g, tps, 1),
        pl.BlockSpec((None, None, 1, D_MODEL), lambda i: (layer, 1, 0, 0)),
        pl.BlockSpec((None, D_IN, D_MODEL), lambda i: (layer, 0, 0), pipeline_mode=pl.Buffered(1)),
        pl.BlockSpec((None, LANES, D_MODEL), lambda i: (layer, 0, 0)),
        pl.BlockSpec((None, 1, LANES), lambda i: (layer, 0, 0)),
    ]
    if transposed:
        assert g == 1
        nseq = m // rows_per_seq
        for idx, w in ((3, FOX_WIDTH), (4, FOX_WIDTH), (len(widths), LANES)):
            spec = pl.BlockSpec((None, w, tm), lambda i: (i // tps, 0, i % tps))
            shape = jax.ShapeDtypeStruct((nseq, w, rows_per_seq), F32)
            out_specs[idx:idx + 1], out_shape[idx:idx + 1] = [spec], [shape]
    return pl.pallas_call(
        functools.partial(_inproj_kernel, transposed=transposed),
        grid=(m // tm,),
        in_specs=in_specs,
        out_specs=out_specs,
        out_shape=out_shape,
        scratch_shapes=[pltpu.VMEM((D_IN_PAD, D_MODEL), BF16)],
        compiler_params=_params(52, ("arbitrary",)),
        name="inproj",
    )(x, mod4, npre, w_in_t, w_small, fbias)


def _outproj_kernel(x_ref, ya_ref, yb_ref, yc_ref, mod_ref, npost_ref, w_ref, o_ref):
    y = (_dot(ya_ref[...], w_ref[0:LRU_WIDTH, :])
         + _dot(yb_ref[...], w_ref[LRU_WIDTH:LRU_WIDTH + FOX_WIDTH, :])
         + _dot(yc_ref[...], w_ref[LRU_WIDTH + FOX_WIDTH:, :]))
    o_ref[...] = _post_norm(x_ref[...], y, npost_ref[...], mod_ref, 1.0)


def _outproj_call(x, ya, yb, yc, mod4, npost, w_out, layer, rows_per_seq):
    m = x.shape[0]
    tm = min(ROW_TILE, m)
    g, tps = _seq_grouping(rows_per_seq, tm)
    row = lambda w: pl.BlockSpec((tm, w), lambda i: (i, 0))
    return pl.pallas_call(
        _outproj_kernel,
        grid=(m // tm,),
        in_specs=[
            row(D_MODEL), row(LRU_WIDTH), row(FOX_WIDTH), row(SSD_WIDTH),
            _mod_spec(g, tps, 1),
            pl.BlockSpec((None, None, 1, D_MODEL), lambda i: (layer, 1, 0, 0)),
            pl.BlockSpec((None, D_MODEL, D_MODEL), lambda i: (layer, 0, 0)),
        ],
        out_specs=row(D_MODEL),
        out_shape=jax.ShapeDtypeStruct((m, D_MODEL), F32),
        compiler_params=_params(32, ("arbitrary",)),
        name="outproj",
    )(x, ya, yb, yc, mod4, npost, w_out)


def _causal_conv(xx, cw, cb, n):
    u = cb + cw[0:1] * pltpu.roll(xx, 3, 0)[SUBLANES:SUBLANES + n]
    u = u + cw[1:2] * pltpu.roll(xx, 2, 0)[SUBLANES:SUBLANES + n]
    u = u + cw[2:3] * pltpu.roll(xx, 1, 0)[SUBLANES:SUBLANES + n]
    return u + cw[3:4] * xx[SUBLANES:SUBLANES + n]


def _last_rows(xx, k):
    return pltpu.roll(xx, k, 0)[0:SUBLANES][0:k]


def _tri(n, lower):
    r = lax.broadcasted_iota(jnp.int32, (n, n), 0)
    c = lax.broadcasted_iota(jnp.int32, (n, n), 1)
    return ((r >= c) if lower else (r <= c)).astype(F32).astype(BF16)


def _split3(x):
    hi = x.astype(BF16)
    r1 = x - hi.astype(F32)
    mid = r1.astype(BF16)
    lo = (r1 - mid.astype(F32)).astype(BF16)
    return hi, mid, lo


def _cumsum_lanes(x, triu):
    hi, mid, lo = _split3(x)
    return _dot(hi, triu) + _dot(mid, triu) + _dot(lo, triu)


def _cumsum_rows(x, tril):
    hi, mid, lo = _split3(x)
    return _dot(tril, hi) + _dot(tril, mid) + _dot(tril, lo)


def _lru_kernel(x_ref, g_ref, prev_ref, h0_ref, cw_ref, cb_ref, wgate_ref, bgate_ref, lam_ref,
                y_ref, convnew_ref, hnew_ref, a_scr, b_scr):
    n = x_ref.shape[0]
    xx = jnp.concatenate([prev_ref[...], x_ref[...]], axis=0)
    convnew_ref[...] = _last_rows(xx, CONV_W - 1)
    u = _causal_conv(xx, cw_ref[...], cb_ref[...], n)
    gates = _dot(u.astype(BF16), wgate_ref[...]) + bgate_ref[...]
    r = jax.nn.sigmoid(gates[:, :LRU_WIDTH])
    i = jax.nn.sigmoid(gates[:, LRU_WIDTH:])
    log_a = (-LRU_C * r) * _softplus(-lam_ref[...])
    a = jnp.exp(log_a)
    b = jnp.sqrt(-jnp.tanh(log_a) * (a * a + 1.0)) * (i * u)
    row = lax.broadcasted_iota(jnp.int32, (n, 1), 0) % SUBLANES
    for d in (1, 2, 4):
        keep = row >= d
        b = jnp.where(keep, a * pltpu.roll(b, d, 0) + b, b)
        a = jnp.where(keep, a * pltpu.roll(a, d, 0), a)
    a_scr[...] = a
    b_scr[...] = b

    def group(j, h):
        off = pl.multiple_of(j * SUBLANES, SUBLANES)
        hb = a_scr[pl.ds(off, SUBLANES), :] * h + b_scr[pl.ds(off, SUBLANES), :]
        b_scr[pl.ds(off, SUBLANES), :] = hb
        return jnp.broadcast_to(hb[SUBLANES - 1:SUBLANES, :], hb.shape)

    h_last = lax.fori_loop(0, n // SUBLANES, group,
                           jnp.broadcast_to(h0_ref[...], (SUBLANES, LRU_WIDTH)), unroll=4)
    hnew_ref[...] = h_last[0:1]
    y_ref[...] = (b_scr[...] * jax.nn.gelu(g_ref[...])).astype(BF16)


def _lru_call(lru_x, lru_g, prev8, h0, cw, cb, wgate, bgate, lam, layer, state_layer):
    bsz, n, _ = lru_x.shape
    seq = pl.BlockSpec((None, n, LRU_WIDTH), lambda b: (b, 0, 0))
    if state_layer is None:
        prev_spec = pl.BlockSpec((None, SUBLANES, LRU_WIDTH), lambda b: (0, 0, 0))
        h0_spec = pl.BlockSpec((None, 1, LRU_WIDTH), lambda b: (0, 0, 0))
    else:
        prev_spec = pl.BlockSpec((None, None, SUBLANES, LRU_WIDTH), lambda b: (state_layer, b, 0, 0))
        h0_spec = pl.BlockSpec((None, None, 1, LRU_WIDTH), lambda b: (state_layer, b, 0, 0))
    par = lambda r, w: pl.BlockSpec((None, r, w), lambda b: (layer, 0, 0))
    return pl.pallas_call(
        _lru_kernel,
        grid=(bsz,),
        in_specs=[seq, seq, prev_spec, h0_spec, par(CONV_W, LRU_WIDTH), par(1, LRU_WIDTH),
                  par(LRU_WIDTH, 2 * LRU_WIDTH), par(1, 2 * LRU_WIDTH), par(1, LRU_WIDTH)],
        out_specs=[seq,
                   pl.BlockSpec((None, CONV_W - 1, LRU_WIDTH), lambda b: (b, 0, 0)),
                   pl.BlockSpec((None, 1, LRU_WIDTH), lambda b: (b, 0, 0))],
        out_shape=[jax.ShapeDtypeStruct((bsz, n, LRU_WIDTH), BF16),
                   jax.ShapeDtypeStruct((bsz, CONV_W - 1, LRU_WIDTH), F32),
                   jax.ShapeDtypeStruct((bsz, 1, LRU_WIDTH), F32)],
        scratch_shapes=[pltpu.VMEM((n, LRU_WIDTH), F32), pltpu.VMEM((n, LRU_WIDTH), F32)],
        compiler_params=_params(40, ("arbitrary",)),
        name="rg_lru",
    )(lru_x, lru_g, prev8, h0, cw, cb, wgate, bgate, lam)


def _ssd_kernel(xbc_ref, z_ref, small_ref, prev_ref, h0_ref, cw_ref, cb_ref, dtb_ref, alog_ref, dvec_ref, nw_ref,
                y_ref, convnew_ref, h_ref, tail_scr):
    n = xbc_ref.shape[0]
    c = pl.program_id(1)

    @pl.when(c == 0)
    def _():
        tail_scr[...] = prev_ref[...]
        h_ref[...] = h0_ref[...]

    xx = jnp.concatenate([tail_scr[...], xbc_ref[...]], axis=0)
    tail_scr[...] = xx[n:n + SUBLANES]
    convnew_ref[...] = _last_rows(xx, CONV_W - 1)
    act = _silu(_causal_conv(xx, cw_ref[...], cb_ref[...], n))
    xs = act[:, :SSD_WIDTH]
    bm = act[:, SSD_WIDTH:SSD_WIDTH + SSD_GROUPS * D_STATE]
    cm = act[:, SSD_WIDTH + SSD_GROUPS * D_STATE:]

    dt = _softplus(small_ref[...] + dtb_ref[...])
    dta = dt * (-jnp.exp(alog_ref[...]))
    cum = _cumsum_rows(dta, _tri(n, True))
    cum_t = cum.T
    rr = lax.broadcasted_iota(jnp.int32, (n, n), 0)
    cc = lax.broadcasted_iota(jnp.int32, (n, n), 1)
    causal = rr >= cc
    lo_lane = lax.broadcasted_iota(jnp.int32, (1, LANES), 1) < SSD_HEAD_DIM
    lo_row = lax.broadcasted_iota(jnp.int32, (LANES, 1), 0) < SSD_HEAD_DIM
    dvec = dvec_ref[...]

    ys = []
    for g in range(SSD_GROUPS):
        sl = slice(g * LANES, (g + 1) * LANES)
        xg, bg, cg = xs[:, sl], bm[:, sl].astype(BF16), cm[:, sl].astype(BF16)
        heads = (2 * g, 2 * g + 1)
        col = lambda a, h: a[:, DT_LANE0 + h:DT_LANE0 + h + 1]
        pick = lambda f: jnp.where(lo_lane, f(heads[0]), f(heads[1]))
        dx = xg * pick(lambda h: col(dt, h))
        dxb = dx.astype(BF16)
        cb_mat = _dot_nt(cg, bg)
        yd = []
        for h in heads:
            seg = col(cum, h) - cum_t[DT_LANE0 + h:DT_LANE0 + h + 1, :]
            lmat = jnp.exp(jnp.where(causal, seg, NEG_BIG))
            yd.append(_dot((cb_mat * lmat).astype(BF16), dxb))
        y_diag = jnp.where(lo_lane, yd[0], yd[1])
        last = lambda h: col(cum, h)[n - 1:n, :]
        decay_end = pick(lambda h: jnp.exp(last(h) - col(cum, h)))
        states = _dot_tn((dx * decay_end).astype(BF16), bg)
        h_prev = h_ref[sl, :]
        y_off = _dot_nt(cg, h_prev.astype(BF16)) * pick(lambda h: jnp.exp(col(cum, h)))
        chunk_decay = jnp.where(lo_row, jnp.exp(last(heads[0])), jnp.exp(last(heads[1])))
        h_ref[sl, :] = chunk_decay * h_prev + states
        ys.append(y_diag + y_off + dvec[:, sl] * xg)
    y = jnp.concatenate(ys, axis=1)
    y_ref[...] = (_rms(y * _silu(z_ref[...])) * nw_ref[...]).astype(BF16)


def _ssd_call(xbc, z, small, prev8, h0, cw, cb, dtb, alog, dvec, nw, layer, state_layer, chunk):
    bsz, n, _ = xbc.shape
    nc = n // chunk
    seq = lambda w: pl.BlockSpec((None, chunk, w), lambda b, c: (b, c, 0))
    if state_layer is None:
        prev_spec = pl.BlockSpec((None, SUBLANES, SSD_CONV_DIM), lambda b, c: (0, 0, 0))
        h0_spec = pl.BlockSpec((None, SSD_WIDTH, D_STATE), lambda b, c: (0, 0, 0))
    else:
        prev_spec = pl.BlockSpec((None, None, SUBLANES, SSD_CONV_DIM), lambda b, c: (state_layer, b, 0, 0))
        h0_spec = pl.BlockSpec((None, None, SSD_WIDTH, D_STATE), lambda b, c: (state_layer, b, 0, 0))
    par = lambda r, w: pl.BlockSpec((None, r, w), lambda b, c: (layer, 0, 0))
    return pl.pallas_call(
        _ssd_kernel,
        grid=(bsz, nc),
        in_specs=[seq(SSD_CONV_DIM), seq(SSD_WIDTH), seq(LANES), prev_spec, h0_spec,
                  par(CONV_W, SSD_CONV_DIM), par(1, SSD_CONV_DIM), par(1, LANES), par(1, LANES),
                  par(1, SSD_WIDTH), par(1, SSD_WIDTH)],
        out_specs=[seq(SSD_WIDTH),
                   pl.BlockSpec((None, CONV_W - 1, SSD_CONV_DIM), lambda b, c: (b, 0, 0)),
                   pl.BlockSpec((None, SSD_WIDTH, D_STATE), lambda b, c: (b, 0, 0))],
        out_shape=[jax.ShapeDtypeStruct((bsz, n, SSD_WIDTH), BF16),
                   jax.ShapeDtypeStruct((bsz, CONV_W - 1, SSD_CONV_DIM), F32),
                   jax.ShapeDtypeStruct((bsz, SSD_WIDTH, D_STATE), F32)],
        scratch_shapes=[pltpu.VMEM((SUBLANES, SSD_CONV_DIM), F32)],
        compiler_params=_params(32, ("arbitrary", "arbitrary")),
        name="ssd",
    )(xbc, z, small, prev8, h0, cw, cb, dtb, alog, dvec, nw)


def _fox_prompt_kernel(q_ref, kt_ref, vt_ref, logft_ref, o_ref, k_scr, v_scr, f_scr, ft_scr):
    s = kt_ref.shape[1]
    tq = k_scr.shape[3]
    p = pl.program_id(1)
    heads = (HEAD_PAIR * p, HEAD_PAIR * p + 1)
    spare = (FOX_HEAD_DIM, 0)

    @pl.when(p == 0)
    def _():
        triu = _tri(SCAN_BLOCK, False)
        carry = jnp.zeros((LANES, 1), F32)
        per = tq // SCAN_BLOCK
        for c in range(s // SCAN_BLOCK):
            cols = slice(c * SCAN_BLOCK, (c + 1) * SCAN_BLOCK)
            fc = _cumsum_lanes(logft_ref[:, cols], triu) + carry
            ft_scr[c // per, :, (c % per) * SCAN_BLOCK:(c % per + 1) * SCAN_BLOCK] = fc
            f_scr[cols, :] = fc.T
            carry = fc[:, SCAN_BLOCK - 1:SCAN_BLOCK]

    row = lax.broadcasted_iota(jnp.int32, (LANES, 1), 0)
    for j in range(s // tq):
        kt = kt_ref[:, j * tq:(j + 1) * tq]
        vt = vt_ref[:, j * tq:(j + 1) * tq]
        for hl in range(HEAD_PAIR):
            own = (row < FOX_HEAD_DIM) if hl == 0 else (row >= FOX_HEAD_DIM)
            f_k = ft_scr[j, pl.ds(F_LANE0 + heads[hl], 1), :] * LOG2E
            kc = jnp.where(own, kt, 0.0).astype(BF16)
            for i, part in enumerate(_split3(-f_k)):
                kc = jnp.where(row == spare[hl] + i, part, kc)
            k_scr[hl, j] = kc
            v_scr[hl, j] = jnp.where(own, vt, jnp.where(row == spare[hl], 1.0, 0.0)).astype(BF16)

    lane = lax.broadcasted_iota(jnp.int32, (1, LANES), 1)
    lo_lane = lane < FOX_HEAD_DIM
    rb = min(ATTN_ROWS, tq)
    nrb = tq // rb

    def query_block(qi, _):
        rows = [pl.ds(pl.multiple_of(qi * tq + r * rb, rb), rb) for r in range(nrb)]
        q_blocks, fq_blocks = [], []
        for r in range(nrb):
            q = q_ref[rows[r], :].astype(F32)
            ones_at = lambda l0: jnp.where((lane >= l0) & (lane < l0 + 3), 1.0, 0.0)
            q_blocks.append((jnp.where(lo_lane, q, ones_at(spare[0])).astype(BF16),
                             jnp.where(lo_lane, ones_at(spare[1]), q).astype(BF16)))
            f_rows = f_scr[rows[r], :]
            fq_blocks.append([jnp.sum(jnp.where(lane == F_LANE0 + h, f_rows, 0.0), axis=1, keepdims=True) * LOG2E
                              for h in heads])
        carry = lax.fori_loop(0, qi, lambda j, cr: step(j, cr, q_blocks, fq_blocks, False), (init_rows,) * nrb)
        carry = step(qi, carry, q_blocks, fq_blocks, True)
        for r in range(nrb):
            acc = [carry[r][2 * hl + 1] for hl in range(HEAD_PAIR)]
            out = [a / a[:, spare[hl]:spare[hl] + 1] for hl, a in enumerate(acc)]
            o_ref[rows[r], :] = jnp.where(lo_lane, out[0], out[1]).astype(BF16)
        return 0

    def softmax_unit(t, f_q, m_old, row0):
        if row0 is not None:
            rr = lax.broadcasted_iota(jnp.int32, t.shape, 0) + row0
            cc = lax.broadcasted_iota(jnp.int32, t.shape, 1)
            t = jnp.where(cc <= rr, t, NEG_BIG)
        m_new = jnp.maximum(m_old, jnp.max(t, axis=1, keepdims=True) + f_q)
        return m_new, jnp.exp2(m_old - m_new), jnp.exp2(t - (m_new - f_q)).astype(BF16)

    def step(j, carry, q_blocks, fq_blocks, diagonal):
        nks = [(r + 1) * rb if diagonal else tq for r in range(nrb)]
        units = [(r, hl) for r in range(nrb) for hl in range(HEAD_PAIR)]
        dots = {(r, hl): _dot(q_blocks[r][hl], k_scr[hl, j, :, 0:nks[r]]) for r, hl in units}
        soft = {(r, hl): softmax_unit(dots[(r, hl)], fq_blocks[r][hl], carry[r][2 * hl], r * rb if diagonal else None)
                for r, hl in units}
        pvs = {(r, hl): _dot_nt(soft[(r, hl)][2], v_scr[hl, j, :, 0:nks[r]]) for r, hl in units}
        new = []
        for r in range(nrb):
            row_state = ()
            for hl in range(HEAD_PAIR):
                m_new, alpha, _ = soft[(r, hl)]
                row_state += (m_new, alpha * carry[r][2 * hl + 1] + pvs[(r, hl)])
            new.append(row_state)
        return tuple(new)

    init_rows = (jnp.full((rb, 1), NEG_BIG, F32), jnp.zeros((rb, LANES), F32)) * HEAD_PAIR
    lax.fori_loop(0, s // tq, query_block, 0)


def _fox_prompt_call(q, kt, vt, small_t):
    bsz, s, _ = q.shape
    tq = min(ATTN_BLOCK, s)
    blk = pl.BlockSpec((None, s, LANES), lambda b, p: (b, 0, p))
    full = pl.BlockSpec((None, LANES, s), lambda b, p: (b, p, 0))
    return pl.pallas_call(
        _fox_prompt_kernel,
        grid=(bsz, FOX_WIDTH // LANES),
        in_specs=[blk, full, full, pl.BlockSpec((None, LANES, s), lambda b, p: (b, 0, 0))],
        out_specs=blk,
        out_shape=jax.ShapeDtypeStruct((bsz, s, FOX_WIDTH), BF16),
        scratch_shapes=[pltpu.VMEM((HEAD_PAIR, s // tq, LANES, tq), BF16),
                        pltpu.VMEM((HEAD_PAIR, s // tq, LANES, tq), BF16),
                        pltpu.VMEM((s, LANES), F32), pltpu.VMEM((s // tq, LANES, tq), F32)],
        compiler_params=_params(40, ("arbitrary", "arbitrary")),
        name="fox_prompt",
    )(q, kt, vt, small_t)


def _fox_sample_kernel(q_ref, k_ref, v_ref, logf_ref, ck_ref, cv_ref, clogft_ref, o_ref,
                       m_scr, l_scr, acc_scr, fk_scr, fq_scr, fnew_scr):
    t = q_ref.shape[0]
    c = pl.program_id(1)
    nchunk, _, chunk = fk_scr.shape

    @pl.when(c == 0)
    def _():
        blk = min(SCAN_BLOCK, chunk)
        triu = _tri(blk, False)
        carry = jnp.zeros((FOX_HEADS, 1), F32)
        for i in range(nchunk):
            for j in range(chunk // blk):
                cols = slice(j * blk, (j + 1) * blk)
                fc = _cumsum_lanes(clogft_ref[:, i * chunk + j * blk:i * chunk + (j + 1) * blk], triu) + carry
                fk_scr[i, :, cols] = fc * LOG2E
                carry = fc[:, blk - 1:blk]
        sub = lax.broadcasted_iota(jnp.int32, (FOX_HEADS, LANES), 0)
        ln = lax.broadcasted_iota(jnp.int32, (FOX_HEADS, LANES), 1)
        total_row = jnp.sum(jnp.where(sub == ln - F_LANE0, carry, 0.0), axis=0, keepdims=True)
        f_new = _cumsum_rows(logf_ref[...], _tri(t, True)) + total_row
        f_new = f_new * LOG2E
        fnew_scr[...] = f_new.T[F_LANE0:F_LANE0 + FOX_HEADS, :]
        fq_scr[...] = jnp.concatenate([f_new[:, F_LANE0 + h:F_LANE0 + h + 1] for h in range(FOX_HEADS)], axis=0)
        m_scr[...] = jnp.full(m_scr.shape, NEG_BIG, F32)
        l_scr[...] = jnp.zeros(l_scr.shape, F32)
        acc_scr[...] = jnp.zeros(acc_scr.shape, F32)

    q = q_ref[...]
    q_heads = [q[:, h * FOX_HEAD_DIM:(h + 1) * FOX_HEAD_DIM] for h in range(FOX_HEADS)]
    f_q = fq_scr[...]

    def update(keys, values, f_k, mask, channels_first):
        qk = _dot if channels_first else _dot_nt
        pv_dot = _dot_nt if channels_first else _dot
        tt = jnp.concatenate([qk(q_heads[h], keys[h]) - f_k[h:h + 1, :] for h in range(FOX_HEADS)], axis=0)
        if mask is not None:
            tt = jnp.where(mask, tt, NEG_BIG)
        m_old = m_scr[...]
        m_new = jnp.maximum(m_old, jnp.max(tt, axis=1, keepdims=True) + f_q)
        alpha = jnp.exp2(m_old - m_new)
        pr = jnp.exp2(tt - (m_new - f_q))
        l_scr[...] = alpha * l_scr[...] + jnp.sum(pr, axis=1, keepdims=True)
        prb = pr.astype(BF16)
        pv = jnp.concatenate([pv_dot(prb[h * t:(h + 1) * t, :], values[h]) for h in range(FOX_HEADS)], axis=0)
        acc_scr[...] = alpha * acc_scr[...] + pv
        m_scr[...] = m_new

    head_rows = lambda ref: [ref[h * FOX_HEAD_DIM:(h + 1) * FOX_HEAD_DIM, :].astype(BF16) for h in range(FOX_HEADS)]
    update(head_rows(ck_ref), head_rows(cv_ref), fk_scr[c], None, True)

    @pl.when(c == nchunk - 1)
    def _():
        k_new, v_new = k_ref[...].astype(BF16), v_ref[...].astype(BF16)
        cols = lambda a: [a[:, h * FOX_HEAD_DIM:(h + 1) * FOX_HEAD_DIM] for h in range(FOX_HEADS)]
        rr = lax.broadcasted_iota(jnp.int32, (FOX_HEADS * t, t), 0) % t
        cc = lax.broadcasted_iota(jnp.int32, (FOX_HEADS * t, t), 1)
        update(cols(k_new), cols(v_new), fnew_scr[...], cc <= rr, False)
        out = acc_scr[...] / l_scr[...]
        o_ref[...] = jnp.concatenate([out[h * t:(h + 1) * t, :] for h in range(FOX_HEADS)], axis=1).astype(BF16)


def _fox_sample_call(q, k, v, small, cache_k, cache_v, cache_logf_t, layer):
    bsz, t, _ = q.shape
    past = cache_logf_t.shape[3]
    chunk = min(SAMPLE_CHUNK, past)
    nchunk = past // chunk
    new = lambda w: pl.BlockSpec((None, t, w), lambda b, c: (b, 0, 0))
    cache = pl.BlockSpec((None, None, FOX_WIDTH, chunk), lambda b, c: (layer, b, 0, c))
    rows = FOX_HEADS * t
    return pl.pallas_call(
        _fox_sample_kernel,
        grid=(bsz, nchunk),
        in_specs=[new(FOX_WIDTH), new(FOX_WIDTH), new(FOX_WIDTH), new(LANES), cache, cache,
                  pl.BlockSpec((None, None, FOX_HEADS, past), lambda b, c: (layer, b, 0, 0))],
        out_specs=new(FOX_WIDTH),
        out_shape=jax.ShapeDtypeStruct((bsz, t, FOX_WIDTH), BF16),
        scratch_shapes=[pltpu.VMEM((rows, 1), F32), pltpu.VMEM((rows, 1), F32), pltpu.VMEM((rows, FOX_HEAD_DIM), F32),
                        pltpu.VMEM((nchunk, FOX_HEADS, chunk), F32), pltpu.VMEM((rows, 1), F32),
                        pltpu.VMEM((FOX_HEADS, t), F32)],
        compiler_params=_params(40, ("arbitrary", "arbitrary")),
        name="fox_sample",
    )(q, k, v, small, cache_k, cache_v, cache_logf_t)


def _small_rows(w_in_t):
    offs = np.concatenate([[0], np.cumsum(IN_SIZES)])
    seg = lambda i: w_in_t[:, int(offs[i]):int(offs[i + 1]), :]
    pad = jnp.zeros((w_in_t.shape[0], LANES - FOX_HEADS - SSD_HEADS, w_in_t.shape[2]), w_in_t.dtype)
    return jnp.concatenate([seg(5), seg(8), pad], axis=1)


def _block_diag(w):
    d, h, b, _ = w.shape
    eye = jnp.eye(h, dtype=w.dtype)
    return jnp.einsum("dhij,hg->dhigj", w, eye).reshape(d, h * b, h * b)


def _lane_slab(v, lane0):
    d, k = v.shape
    return jnp.zeros((d, 1, LANES), v.dtype).at[:, 0, lane0:lane0 + k].set(v)


def _pad_history(state):
    return jnp.pad(state, ((0, 0), (0, 0), (SUBLANES - (CONV_W - 1), 0), (0, 0)))


def _trunk(x, mod_group, caches, prm, ssd_chunk):
    bsz, n, _ = x.shape
    x = x.reshape(bsz * n, D_MODEL)
    states = {name: [] for name in ("fox_k", "fox_v", "fox_logf", "lru_conv", "lru_h", "ssd_conv", "ssd_h")}
    for l in range(DEPTH):
        mod4 = mod_group[l].reshape(bsz, N_SUB, 3, D_MODEL)
        x = _ffn_call(x, mod4, prm["npre"], prm["npost"], prm["wg"], prm["wu"], prm["wd"], l, 0, 0, n)
        prompt = caches is None
        proj = _inproj_call(x, mod4, prm["npre"], prm["w_in_t"], prm["w_small"], prm["fbias"], l, n,
                            transposed=prompt)
        lrux, lrug, q, k, v, z, xbc, small = proj[:8]
        per_seq = lambda a: a.reshape(bsz, n, a.shape[-1])
        state_layer = None if prompt else l
        src = prm["zero_state"] if prompt else caches
        ya, lru_conv, lru_h = _lru_call(per_seq(lrux), per_seq(lrug), src["lru_conv"], src["lru_h"],
                                        prm["lru_cw"], prm["lru_cb"], prm["lru_wgate"], prm["lru_bgate"],
                                        prm["lru_lam"], l, state_layer)
        if prompt:
            small_t = proj[8]
            yb = _fox_prompt_call(per_seq(q), k, v, small_t)
            heads_last = lambda a: jnp.transpose(a.reshape(bsz, FOX_HEADS, FOX_HEAD_DIM, n), (0, 3, 1, 2))
            k_out, v_out = heads_last(k), heads_last(v)
            logf_out = jnp.swapaxes(small_t[:, F_LANE0:F_LANE0 + FOX_HEADS, :], 1, 2)
        else:
            yb = _fox_sample_call(per_seq(q), per_seq(k), per_seq(v), per_seq(small),
                                  caches["fox_k"], caches["fox_v"], caches["fox_logf_t"], l)
            k_out = k.reshape(bsz, n, FOX_HEADS, FOX_HEAD_DIM)
            v_out = v.reshape(bsz, n, FOX_HEADS, FOX_HEAD_DIM)
            logf_out = small.reshape(bsz, n, LANES)[:, :, F_LANE0:F_LANE0 + FOX_HEADS]
        yc, ssd_conv, ssd_h = _ssd_call(per_seq(xbc), per_seq(z), per_seq(small), src["ssd_conv"], src["ssd_h"],
                                        prm["ssd_cw"], prm["ssd_cb"], prm["ssd_dtb"], prm["ssd_alog"],
                                        prm["ssd_dvec"], prm["ssd_nw"], l, state_layer, ssd_chunk)
        x = _outproj_call(x, ya.reshape(bsz * n, -1), yb.reshape(bsz * n, -1), yc.reshape(bsz * n, -1),
                          mod4, prm["npost"], prm["w_out"], l, n)
        x = _ffn_call(x, mod4, prm["npre"], prm["npost"], prm["wg"], prm["wu"], prm["wd"], l, 2, 1, n)
        states["fox_k"].append(k_out)
        states["fox_v"].append(v_out)
        states["fox_logf"].append(logf_out)
        states["lru_conv"].append(lru_conv)
        states["lru_h"].append(lru_h.reshape(bsz, LRU_WIDTH))
        states["ssd_conv"].append(ssd_conv)
        states["ssd_h"].append(ssd_h.reshape(bsz, SSD_HEADS, SSD_HEAD_DIM, D_STATE))
    return x.reshape(bsz, n, D_MODEL), {name: jnp.stack(vals, axis=0) for name, vals in states.items()}


def kernel(x_prompt, x_sample, c_prompt, c_sample, cache_fox_k, cache_fox_v, cache_fox_logf, state_lru_conv, state_lru_h, state_ssd_conv, state_ssd_h, w_mod, b_mod, norm_pre, norm_post, ffn_w_gate, ffn_w_up, ffn_w_down, w_in, w_out, lru_conv_w, lru_conv_b, lru_wa, lru_ba, lru_wx, lru_bx, lru_lambda, fox_f_bias, ssd_conv_w, ssd_conv_b, ssd_dt_bias, ssd_a_log, ssd_d, ssd_norm_w):
    n_prompt, n_sample = x_prompt.shape[0], x_sample.shape[0]
    w_in_t = jnp.swapaxes(w_in, 1, 2)
    prm = {
        "npre": norm_pre.reshape(DEPTH, N_SUB, 1, D_MODEL),
        "npost": norm_post.reshape(DEPTH, N_SUB, 1, D_MODEL),
        "wg": ffn_w_gate.astype(BF16), "wu": ffn_w_up.astype(BF16), "wd": ffn_w_down.astype(BF16),
        "w_in_t": w_in_t, "w_small": _small_rows(w_in_t), "w_out": w_out.astype(BF16),
        "fbias": _lane_slab(fox_f_bias, F_LANE0),
        "lru_cw": lru_conv_w, "lru_cb": lru_conv_b.reshape(DEPTH, 1, LRU_WIDTH),
        "lru_wgate": jnp.concatenate([_block_diag(lru_wa), _block_diag(lru_wx)], axis=-1).astype(BF16),
        "lru_bgate": jnp.concatenate([lru_ba, lru_bx], axis=-1).reshape(DEPTH, 1, 2 * LRU_WIDTH),
        "lru_lam": lru_lambda.reshape(DEPTH, 1, LRU_WIDTH),
        "ssd_cw": ssd_conv_w, "ssd_cb": ssd_conv_b.reshape(DEPTH, 1, SSD_CONV_DIM),
        "ssd_dtb": _lane_slab(ssd_dt_bias, DT_LANE0), "ssd_alog": _lane_slab(ssd_a_log, DT_LANE0),
        "ssd_dvec": jnp.repeat(ssd_d, SSD_HEAD_DIM, axis=-1).reshape(DEPTH, 1, SSD_WIDTH),
        "ssd_nw": ssd_norm_w.reshape(DEPTH, 1, SSD_WIDTH),
        "zero_state": {
            "lru_conv": jnp.zeros((1, SUBLANES, LRU_WIDTH), F32), "lru_h": jnp.zeros((1, 1, LRU_WIDTH), F32),
            "ssd_conv": jnp.zeros((1, SUBLANES, SSD_CONV_DIM), F32), "ssd_h": jnp.zeros((1, SSD_WIDTH, D_STATE), F32),
        },
    }
    caches = {
        "fox_k": jnp.transpose(cache_fox_k, (0, 1, 3, 4, 2)).reshape(DEPTH, n_sample, FOX_WIDTH, -1),
        "fox_v": jnp.transpose(cache_fox_v, (0, 1, 3, 4, 2)).reshape(DEPTH, n_sample, FOX_WIDTH, -1),
        "fox_logf_t": jnp.swapaxes(cache_fox_logf, 2, 3),
        "lru_conv": _pad_history(state_lru_conv),
        "lru_h": state_lru_h.reshape(DEPTH, n_sample, 1, LRU_WIDTH),
        "ssd_conv": _pad_history(state_ssd_conv),
        "ssd_h": state_ssd_h.reshape(DEPTH, n_sample, SSD_WIDTH, D_STATE),
    }
    mod = _mod_call(jnp.concatenate([c_prompt, c_sample], axis=0), w_mod, b_mod)
    y_prompt, sp = _trunk(x_prompt, mod[:, :n_prompt], None, prm, ssd_chunk=256)
    y_sample, ss = _trunk(x_sample, mod[:, n_prompt:], caches, prm, ssd_chunk=x_sample.shape[1])
    names = ("fox_k", "fox_v", "fox_logf", "lru_conv", "lru_h", "ssd_conv", "ssd_h")
    return (y_prompt, y_sample) + tuple(sp[n] for n in names) + tuple(ss[n] for n in names)
```

```python
import functools

import numpy as np
import jax
import jax.numpy as jnp
from jax import lax
from jax.experimental import pallas as pl
from jax.experimental.pallas import tpu as pltpu

F32 = jnp.float32
BF16 = jnp.bfloat16

D_MODEL = 1024
DEPTH = 2
CONV_W = 4
EPS = 1e-6
LRU_WIDTH = 256
LRU_HEADS = 4
LRU_BLOCK = LRU_WIDTH // LRU_HEADS
LRU_C = 8.0
FOX_HEADS = 8
FOX_HEAD_DIM = 64
FOX_WIDTH = FOX_HEADS * FOX_HEAD_DIM
SSD_HEADS = 4
SSD_HEAD_DIM = 64
SSD_WIDTH = SSD_HEADS * SSD_HEAD_DIM
SSD_GROUPS = 2
D_STATE = 128
SSD_CONV_DIM = SSD_WIDTH + 2 * SSD_GROUPS * D_STATE
IN_SIZES = (LRU_WIDTH, LRU_WIDTH, FOX_WIDTH, FOX_WIDTH, FOX_WIDTH, FOX_HEADS, SSD_WIDTH, SSD_CONV_DIM, SSD_HEADS)
D_FF = 2816
N_SUB = 3

LANES = 128
SUBLANES = 8
HEAD_PAIR = LANES // FOX_HEAD_DIM

F_LANE0 = 0
DT_LANE0 = FOX_HEADS
COL_LRU_X = 0
COL_LRU_G = COL_LRU_X + LRU_WIDTH
COL_Q = COL_LRU_G + LRU_WIDTH
COL_K = COL_Q + FOX_WIDTH
COL_V = COL_K + FOX_WIDTH
COL_Z = COL_V + FOX_WIDTH
COL_XBC = COL_Z + SSD_WIDTH
COL_SMALL = COL_XBC + SSD_CONV_DIM
D_IN_PAD = COL_SMALL + LANES
D_IN = sum(IN_SIZES)

ROW_TILE = 512
FFN_ROW_TILE = 1024
FF_CHUNK = 512
ATTN_BLOCK = 1024
ATTN_ROWS = 128
ATTN_GROUP = 8
SCAN_BLOCK = 256
SAMPLE_CHUNK = 2048
NEG_BIG = -1e30
LOG2E = 1.4426950408889634
Q_SCALE = LOG2E * FOX_HEAD_DIM ** -0.5


def _dot(a, b):
    return jnp.dot(a, b, preferred_element_type=F32)


def _dot_nt(a, b):
    return lax.dot_general(a, b, (((1,), (1,)), ((), ())), preferred_element_type=F32)


def _dot_tn(a, b):
    return lax.dot_general(a, b, (((0,), (0,)), ((), ())), preferred_element_type=F32)


def _silu(x):
    return x * jax.nn.sigmoid(x)


def _softplus(x):
    return jnp.maximum(x, 0.0) + jnp.log1p(jnp.exp(-jnp.abs(x)))


def _rms(x):
    return x * lax.rsqrt(jnp.mean(x * x, axis=-1, keepdims=True) + EPS)


def _per_seq(rows, per_seq, fn):
    g = per_seq[0].shape[0]
    if g == 1:
        return fn(rows, *per_seq)
    tm, d = rows.shape
    out = fn(rows.reshape(g, tm // g, d), *[p[:, None, :] for p in per_seq])
    return out.reshape(tm, d)


def _pre_norm(x, npre, mod_ref, seqs=slice(None)):
    h = _rms(x) * npre
    return _per_seq(h, (mod_ref[seqs, 1, :], mod_ref[seqs, 0, :]), lambda r, sc, sh: r * (1.0 + sc) + sh)


def _post_norm(x, y, npost, mod_ref, w, seqs=slice(None)):
    yn = _rms(y) * npost
    return x + _per_seq(yn, (mod_ref[seqs, 2, :],), lambda r, gt: (w * gt) * r)


def _seq_grouping(rows_per_seq, tm):
    if rows_per_seq % tm == 0:
        return 1, rows_per_seq // tm
    assert tm % rows_per_seq == 0
    return tm // rows_per_seq, 1


def _mod_spec(g, tiles_per_seq, sub):
    if g == 1:
        return pl.BlockSpec((1, None, 3, D_MODEL), lambda i: (i // tiles_per_seq, sub, 0, 0))
    return pl.BlockSpec((g, None, 3, D_MODEL), lambda i: (i, sub, 0, 0))


def _params(vmem_mb, sem):
    return pltpu.CompilerParams(dimension_semantics=sem, vmem_limit_bytes=vmem_mb << 20)


def _mod_kernel(c_ref, w_ref, b_ref, o_ref):
    a = _silu(c_ref[...]).astype(BF16)
    o_ref[...] = _dot(a, w_ref[...].astype(BF16)) + b_ref[...]


def _mod_call(c_all, w_mod, b_mod):
    nseq = c_all.shape[0]
    width = N_SUB * 3 * D_MODEL
    tn = 1024
    return pl.pallas_call(
        _mod_kernel,
        grid=(DEPTH, width // tn),
        in_specs=[
            pl.BlockSpec((nseq, D_MODEL), lambda l, n: (0, 0)),
            pl.BlockSpec((None, D_MODEL, tn), lambda l, n: (l, 0, n)),
            pl.BlockSpec((None, 1, tn), lambda l, n: (l, 0, n)),
        ],
        out_specs=pl.BlockSpec((None, nseq, tn), lambda l, n: (l, 0, n)),
        out_shape=jax.ShapeDtypeStruct((DEPTH, nseq, width), F32),
        compiler_params=_params(24, ("arbitrary", "arbitrary")),
        name="adaln_mod",
    )(c_all, w_mod, b_mod.reshape(DEPTH, 1, width))


def _ffn_kernel(x_ref, mod_ref, npre_ref, npost_ref, wg_ref, wu_ref, wd_ref, o_ref):
    tm = x_ref.shape[0]
    half = tm // 2
    nseq = mod_ref.shape[0]
    seqs = [slice(0, 1)] * 2 if nseq == 1 else [slice(i * nseq // 2, (i + 1) * nseq // 2) for i in range(2)]
    xs = [x_ref[i * half:(i + 1) * half, :] for i in range(2)]
    hs = [_pre_norm(x, npre_ref[...], mod_ref, sq).astype(BF16) for x, sq in zip(xs, seqs)]
    accs = [None, None]
    for off in range(0, D_FF, FF_CHUNK):
        fc = min(FF_CHUNK, D_FF - off)
        for i in range(2):
            g = _dot(hs[i], wg_ref[:, off:off + fc])
            u = _dot(hs[i], wu_ref[:, off:off + fc])
            a = (_silu(g) * u).astype(BF16)
            d = _dot(a, wd_ref[off:off + fc, :])
            accs[i] = d if accs[i] is None else accs[i] + d
    for i in range(2):
        o_ref[i * half:(i + 1) * half, :] = _post_norm(xs[i], accs[i], npost_ref[...], mod_ref, 0.5, seqs[i])


def _ffn_call(x, mod4, npre, npost, wg, wu, wd, layer, sub, ffn_idx, rows_per_seq):
    m = x.shape[0]
    tm = min(FFN_ROW_TILE, m)
    g, tps = _seq_grouping(rows_per_seq, tm)
    wspec = lambda shape: pl.BlockSpec((None, None) + shape, lambda i: (layer, ffn_idx, 0, 0),
                                       pipeline_mode=pl.Buffered(1))
    nspec = pl.BlockSpec((None, None, 1, D_MODEL), lambda i: (layer, sub, 0, 0))
    return pl.pallas_call(
        _ffn_kernel,
        grid=(m // tm,),
        in_specs=[
            pl.BlockSpec((tm, D_MODEL), lambda i: (i, 0)),
            _mod_spec(g, tps, sub),
            nspec, nspec,
            wspec((D_MODEL, D_FF)), wspec((D_MODEL, D_FF)), wspec((D_FF, D_MODEL)),
        ],
        out_specs=pl.BlockSpec((tm, D_MODEL), lambda i: (i, 0)),
        out_shape=jax.ShapeDtypeStruct((m, D_MODEL), F32),
        compiler_params=_params(56, ("arbitrary",)),
        name="ffn",
    )(x, mod4, npre, npost, wg, wu, wd)


def _inproj_kernel(*refs, transposed, stacked):
    n_in = 8 if stacked else 6
    x_ref, mod_ref, npre_ref, w_ref, wsmall_ref, fbias_ref = refs[:6]
    lrux_ref, lrug_ref, q_ref, k_ref, v_ref, z_ref, xbc_ref, small_ref, *rest = refs[n_in:]
    wt_ref = rest[-1]

    @pl.when(pl.program_id(0) == 0)
    def _():
        src = int(np.cumsum(IN_SIZES)[4])
        assert src == COL_Z
        moves = [(0, 0, src), (src + FOX_HEADS, COL_Z, SSD_WIDTH), (src + FOX_HEADS + SSD_WIDTH, COL_XBC, SSD_CONV_DIM)]
        for s0, d0, n in moves:
            for off in range(0, n, LRU_WIDTH):
                wt_ref[d0 + off:d0 + off + LRU_WIDTH, :] = w_ref[s0 + off:s0 + off + LRU_WIDTH, :].astype(BF16)
        wt_ref[COL_SMALL:COL_SMALL + LANES, :] = wsmall_ref[...].astype(BF16)

    h = _pre_norm(x_ref[...], npre_ref[...], mod_ref).astype(BF16)
    col = lambda start, width: _dot_nt(h, wt_ref[start:start + width, :])
    col_t = lambda start, width: _dot_nt(wt_ref[start:start + width, :], h)
    lrux_ref[...] = col(COL_LRU_X, LRU_WIDTH)
    lrug_ref[...] = col(COL_LRU_G, LRU_WIDTH)
    q_ref[...] = (col(COL_Q, FOX_WIDTH) * Q_SCALE).astype(BF16)
    if transposed:
        kv_t = col_t(COL_K, 2 * FOX_WIDTH)
        if stacked:
            k_ref[0], v_ref[0] = refs[6][...], refs[7][...]
            k_ref[1], v_ref[1] = kv_t[:FOX_WIDTH], kv_t[FOX_WIDTH:]
        else:
            k_ref[...] = kv_t[:FOX_WIDTH]
            v_ref[...] = kv_t[FOX_WIDTH:]
    else:
        k_ref[...] = col(COL_K, FOX_WIDTH)
        v_ref[...] = col(COL_V, FOX_WIDTH)
    z_ref[...] = col(COL_Z, SSD_WIDTH)
    xbc_ref[...] = col(COL_XBC, SSD_CONV_DIM)
    small = col(COL_SMALL, LANES)
    t = small + fbias_ref[...]
    logf = jnp.minimum(t, 0.0) - jnp.log1p(jnp.exp(-jnp.abs(t)))
    lane = lax.broadcasted_iota(jnp.int32, small.shape, 1)
    small = jnp.where(lane < DT_LANE0, logf, small)
    small_ref[...] = small
    if transposed:
        rest[0][...] = small.T


def _inproj_call(x, mod4, npre, w_in_t, w_small, fbias, layer, rows_per_seq, transposed, prev_kv=None):
    m = x.shape[0]
    tm = min(ROW_TILE, m)
    g, tps = _seq_grouping(rows_per_seq, tm)
    widths = (LRU_WIDTH, LRU_WIDTH, FOX_WIDTH, FOX_WIDTH, FOX_WIDTH, SSD_WIDTH, SSD_CONV_DIM, LANES)
    dtypes = (F32, F32, BF16, F32, F32, F32, F32, F32)
    out_specs = [pl.BlockSpec((tm, w), lambda i: (i, 0)) for w in widths]
    out_shape = [jax.ShapeDtypeStruct((m, w), dt) for w, dt in zip(widths, dtypes)]
    in_specs = [
        pl.BlockSpec((tm, D_MODEL), lambda i: (i, 0)),
        _mod_spec(g, tps, 1),
        pl.BlockSpec((None, None, 1, D_MODEL), lambda i: (layer, 1, 0, 0)),
        pl.BlockSpec((None, D_IN, D_MODEL), lambda i: (layer, 0, 0), pipeline_mode=pl.Buffered(1)),
        pl.BlockSpec((None, LANES, D_MODEL), lambda i: (layer, 0, 0)),
        pl.BlockSpec((None, 1, LANES), lambda i: (layer, 0, 0)),
    ]
    if transposed:
        assert g == 1
        nseq = m // rows_per_seq
        for idx, w in ((3, FOX_WIDTH), (4, FOX_WIDTH), (len(widths), LANES)):
            spec = pl.BlockSpec((None, w, tm), lambda i: (i // tps, 0, i % tps))
            shape = jax.ShapeDtypeStruct((nseq, w, rows_per_seq), F32)
            out_specs[idx:idx + 1], out_shape[idx:idx + 1] = [spec], [shape]
    extra = ()
    if prev_kv is not None:
        assert transposed and layer == 1 and DEPTH == 2
        extra = tuple(prev_kv)
        in_specs += [pl.BlockSpec((None, FOX_WIDTH, tm), lambda i: (i // tps, 0, i % tps))] * 2
        for idx in (3, 4):
            out_specs[idx] = pl.BlockSpec((DEPTH, None, FOX_WIDTH, tm), lambda i: (0, i // tps, 0, i % tps))
            out_shape[idx] = jax.ShapeDtypeStruct((DEPTH, nseq, FOX_WIDTH, rows_per_seq), F32)
    return pl.pallas_call(
        functools.partial(_inproj_kernel, transposed=transposed, stacked=prev_kv is not None),
        grid=(m // tm,),
        in_specs=in_specs,
        out_specs=out_specs,
        out_shape=out_shape,
        scratch_shapes=[pltpu.VMEM((D_IN_PAD, D_MODEL), BF16)],
        compiler_params=_params(56, ("arbitrary",)),
        name="inproj",
    )(x, mod4, npre, w_in_t, w_small, fbias, *extra)


def _outproj_kernel(x_ref, ya_ref, yb_ref, yc_ref, mod_ref, npost_ref, w_ref, o_ref):
    y = (_dot(ya_ref[...], w_ref[0:LRU_WIDTH, :])
         + _dot(yb_ref[...], w_ref[LRU_WIDTH:LRU_WIDTH + FOX_WIDTH, :])
         + _dot(yc_ref[...], w_ref[LRU_WIDTH + FOX_WIDTH:, :]))
    o_ref[...] = _post_norm(x_ref[...], y, npost_ref[...], mod_ref, 1.0)


def _outproj_call(x, ya, yb, yc, mod4, npost, w_out, layer, rows_per_seq):
    m = x.shape[0]
    tm = min(ROW_TILE, m)
    g, tps = _seq_grouping(rows_per_seq, tm)
    row = lambda w: pl.BlockSpec((tm, w), lambda i: (i, 0))
    return pl.pallas_call(
        _outproj_kernel,
        grid=(m // tm,),
        in_specs=[
            row(D_MODEL), row(LRU_WIDTH), row(FOX_WIDTH), row(SSD_WIDTH),
            _mod_spec(g, tps, 1),
            pl.BlockSpec((None, None, 1, D_MODEL), lambda i: (layer, 1, 0, 0)),
            pl.BlockSpec((None, D_MODEL, D_MODEL), lambda i: (layer, 0, 0)),
        ],
        out_specs=row(D_MODEL),
        out_shape=jax.ShapeDtypeStruct((m, D_MODEL), F32),
        compiler_params=_params(32, ("arbitrary",)),
        name="outproj",
    )(x, ya, yb, yc, mod4, npost, w_out)


def _causal_conv(xx, cw, cb, n):
    u = cb + cw[0:1] * pltpu.roll(xx, 3, 0)[SUBLANES:SUBLANES + n]
    u = u + cw[1:2] * pltpu.roll(xx, 2, 0)[SUBLANES:SUBLANES + n]
    u = u + cw[2:3] * pltpu.roll(xx, 1, 0)[SUBLANES:SUBLANES + n]
    return u + cw[3:4] * xx[SUBLANES:SUBLANES + n]


def _last_rows(xx, k):
    return pltpu.roll(xx, k, 0)[0:SUBLANES][0:k]


def _tri(n, lower):
    r = lax.broadcasted_iota(jnp.int32, (n, n), 0)
    c = lax.broadcasted_iota(jnp.int32, (n, n), 1)
    return ((r >= c) if lower else (r <= c)).astype(F32).astype(BF16)


def _split3(x):
    hi = x.astype(BF16)
    r1 = x - hi.astype(F32)
    mid = r1.astype(BF16)
    lo = (r1 - mid.astype(F32)).astype(BF16)
    return hi, mid, lo


def _cumsum_lanes(x, triu):
    hi, mid, lo = _split3(x)
    return _dot(hi, triu) + _dot(mid, triu) + _dot(lo, triu)


def _cumsum_rows(x, tril):
    hi, mid, lo = _split3(x)
    return _dot(tril, hi) + _dot(tril, mid) + _dot(tril, lo)


def _lru_kernel(x_ref, g_ref, prev_ref, h0_ref, cw_ref, cb_ref, wgate_ref, bgate_ref, lam_ref,
                y_ref, convnew_ref, hnew_ref, a_scr, b_scr):
    n = x_ref.shape[0]
    xx = jnp.concatenate([prev_ref[...], x_ref[...]], axis=0)
    convnew_ref[...] = _last_rows(xx, CONV_W - 1)
    u = _causal_conv(xx, cw_ref[...], cb_ref[...], n)
    gates = _dot(u.astype(BF16), wgate_ref[...]) + bgate_ref[...]
    r = jax.nn.sigmoid(gates[:, :LRU_WIDTH])
    i = jax.nn.sigmoid(gates[:, LRU_WIDTH:])
    log_a = (-LRU_C * r) * _softplus(-lam_ref[...])
    a = jnp.exp(log_a)
    b = jnp.sqrt(-jnp.tanh(log_a) * (a * a + 1.0)) * (i * u)
    row = lax.broadcasted_iota(jnp.int32, (n, 1), 0) % SUBLANES
    for d in (1, 2, 4):
        keep = row >= d
        b = jnp.where(keep, a * pltpu.roll(b, d, 0) + b, b)
        a = jnp.where(keep, a * pltpu.roll(a, d, 0), a)
    a_scr[...] = a
    b_scr[...] = b

    def group(j, h):
        off = pl.multiple_of(j * SUBLANES, SUBLANES)
        hb = a_scr[pl.ds(off, SUBLANES), :] * h + b_scr[pl.ds(off, SUBLANES), :]
        b_scr[pl.ds(off, SUBLANES), :] = hb
        return jnp.broadcast_to(hb[SUBLANES - 1:SUBLANES, :], hb.shape)

    h_last = lax.fori_loop(0, n // SUBLANES, group,
                           jnp.broadcast_to(h0_ref[...], (SUBLANES, LRU_WIDTH)), unroll=4)
    hnew_ref[...] = h_last[0:1]
    y_ref[...] = (b_scr[...] * jax.nn.gelu(g_ref[...])).astype(BF16)


def _lru_call(lru_x, lru_g, prev8, h0, cw, cb, wgate, bgate, lam, layer, state_layer):
    bsz, n, _ = lru_x.shape
    seq = pl.BlockSpec((None, n, LRU_WIDTH), lambda b: (b, 0, 0))
    if state_layer is None:
        prev_spec = pl.BlockSpec((None, SUBLANES, LRU_WIDTH), lambda b: (0, 0, 0))
        h0_spec = pl.BlockSpec((None, 1, LRU_WIDTH), lambda b: (0, 0, 0))
    else:
        prev_spec = pl.BlockSpec((None, None, SUBLANES, LRU_WIDTH), lambda b: (state_layer, b, 0, 0))
        h0_spec = pl.BlockSpec((None, None, 1, LRU_WIDTH), lambda b: (state_layer, b, 0, 0))
    par = lambda r, w: pl.BlockSpec((None, r, w), lambda b: (layer, 0, 0))
    return pl.pallas_call(
        _lru_kernel,
        grid=(bsz,),
        in_specs=[seq, seq, prev_spec, h0_spec, par(CONV_W, LRU_WIDTH), par(1, LRU_WIDTH),
                  par(LRU_WIDTH, 2 * LRU_WIDTH), par(1, 2 * LRU_WIDTH), par(1, LRU_WIDTH)],
        out_specs=[seq,
                   pl.BlockSpec((None, CONV_W - 1, LRU_WIDTH), lambda b: (b, 0, 0)),
                   pl.BlockSpec((None, 1, LRU_WIDTH), lambda b: (b, 0, 0))],
        out_shape=[jax.ShapeDtypeStruct((bsz, n, LRU_WIDTH), BF16),
                   jax.ShapeDtypeStruct((bsz, CONV_W - 1, LRU_WIDTH), F32),
                   jax.ShapeDtypeStruct((bsz, 1, LRU_WIDTH), F32)],
        scratch_shapes=[pltpu.VMEM((n, LRU_WIDTH), F32), pltpu.VMEM((n, LRU_WIDTH), F32)],
        compiler_params=_params(40, ("arbitrary",)),
        name="rg_lru",
    )(lru_x, lru_g, prev8, h0, cw, cb, wgate, bgate, lam)


def _ssd_kernel(xbc_ref, z_ref, small_ref, prev_ref, h0_ref, cw_ref, cb_ref, dtb_ref, alog_ref, dvec_ref, nw_ref,
                y_ref, convnew_ref, h_ref, tail_scr):
    n = xbc_ref.shape[0]
    c = pl.program_id(1)

    @pl.when(c == 0)
    def _():
        tail_scr[...] = prev_ref[...]
        h_ref[...] = h0_ref[...]

    xx = jnp.concatenate([tail_scr[...], xbc_ref[...]], axis=0)
    tail_scr[...] = xx[n:n + SUBLANES]
    convnew_ref[...] = _last_rows(xx, CONV_W - 1)
    act = _silu(_causal_conv(xx, cw_ref[...], cb_ref[...], n))
    xs = act[:, :SSD_WIDTH]
    bm = act[:, SSD_WIDTH:SSD_WIDTH + SSD_GROUPS * D_STATE]
    cm = act[:, SSD_WIDTH + SSD_GROUPS * D_STATE:]

    dt = _softplus(small_ref[...] + dtb_ref[...])
    dta = dt * (-jnp.exp(alog_ref[...]))
    cum = _cumsum_rows(dta, _tri(n, True))
    cum_t = cum.T
    rr = lax.broadcasted_iota(jnp.int32, (n, n), 0)
    cc = lax.broadcasted_iota(jnp.int32, (n, n), 1)
    causal = rr >= cc
    lo_lane = lax.broadcasted_iota(jnp.int32, (1, LANES), 1) < SSD_HEAD_DIM
    lo_row = lax.broadcasted_iota(jnp.int32, (LANES, 1), 0) < SSD_HEAD_DIM
    dvec = dvec_ref[...]

    ys = []
    for g in range(SSD_GROUPS):
        sl = slice(g * LANES, (g + 1) * LANES)
        xg, bg, cg = xs[:, sl], bm[:, sl].astype(BF16), cm[:, sl].astype(BF16)
        heads = (2 * g, 2 * g + 1)
        col = lambda a, h: a[:, DT_LANE0 + h:DT_LANE0 + h + 1]
        pick = lambda f: jnp.where(lo_lane, f(heads[0]), f(heads[1]))
        dx = xg * pick(lambda h: col(dt, h))
        dxb = dx.astype(BF16)
        cb_mat = _dot_nt(cg, bg)
        yd = []
        for h in heads:
            seg = col(cum, h) - cum_t[DT_LANE0 + h:DT_LANE0 + h + 1, :]
            lmat = jnp.exp(jnp.where(causal, seg, NEG_BIG))
            yd.append(_dot((cb_mat * lmat).astype(BF16), dxb))
        y_diag = jnp.where(lo_lane, yd[0], yd[1])
        last = lambda h: col(cum, h)[n - 1:n, :]
        decay_end = pick(lambda h: jnp.exp(last(h) - col(cum, h)))
        states = _dot_tn((dx * decay_end).astype(BF16), bg)
        h_prev = h_ref[sl, :]
        y_off = _dot_nt(cg, h_prev.astype(BF16)) * pick(lambda h: jnp.exp(col(cum, h)))
        chunk_decay = jnp.where(lo_row, jnp.exp(last(heads[0])), jnp.exp(last(heads[1])))
        h_ref[sl, :] = chunk_decay * h_prev + states
        ys.append(y_diag + y_off + dvec[:, sl] * xg)
    y = jnp.concatenate(ys, axis=1)
    y_ref[...] = (_rms(y * _silu(z_ref[...])) * nw_ref[...]).astype(BF16)


def _ssd_call(xbc, z, small, prev8, h0, cw, cb, dtb, alog, dvec, nw, layer, state_layer, chunk):
    bsz, n, _ = xbc.shape
    nc = n // chunk
    seq = lambda w: pl.BlockSpec((None, chunk, w), lambda b, c: (b, c, 0))
    if state_layer is None:
        prev_spec = pl.BlockSpec((None, SUBLANES, SSD_CONV_DIM), lambda b, c: (0, 0, 0))
        h0_spec = pl.BlockSpec((None, SSD_WIDTH, D_STATE), lambda b, c: (0, 0, 0))
    else:
        prev_spec = pl.BlockSpec((None, None, SUBLANES, SSD_CONV_DIM), lambda b, c: (state_layer, b, 0, 0))
        h0_spec = pl.BlockSpec((None, None, SSD_WIDTH, D_STATE), lambda b, c: (state_layer, b, 0, 0))
    par = lambda r, w: pl.BlockSpec((None, r, w), lambda b, c: (layer, 0, 0))
    return pl.pallas_call(
        _ssd_kernel,
        grid=(bsz, nc),
        in_specs=[seq(SSD_CONV_DIM), seq(SSD_WIDTH), seq(LANES), prev_spec, h0_spec,
                  par(CONV_W, SSD_CONV_DIM), par(1, SSD_CONV_DIM), par(1, LANES), par(1, LANES),
                  par(1, SSD_WIDTH), par(1, SSD_WIDTH)],
        out_specs=[seq(SSD_WIDTH),
                   pl.BlockSpec((None, CONV_W - 1, SSD_CONV_DIM), lambda b, c: (b, 0, 0)),
                   pl.BlockSpec((None, SSD_WIDTH, D_STATE), lambda b, c: (b, 0, 0))],
        out_shape=[jax.ShapeDtypeStruct((bsz, n, SSD_WIDTH), BF16),
                   jax.ShapeDtypeStruct((bsz, CONV_W - 1, SSD_CONV_DIM), F32),
                   jax.ShapeDtypeStruct((bsz, SSD_WIDTH, D_STATE), F32)],
        scratch_shapes=[pltpu.VMEM((SUBLANES, SSD_CONV_DIM), F32)],
        compiler_params=_params(32, ("arbitrary", "arbitrary")),
        name="ssd",
    )(xbc, z, small, prev8, h0, cw, cb, dtb, alog, dvec, nw)


def _fox_prompt_kernel(q_ref, kt_ref, vt_ref, logft_ref, o_ref, k_scr, v_scr, f_scr, ft_scr):
    s = kt_ref.shape[1]
    tq = k_scr.shape[3]
    p = pl.program_id(1)
    heads = (HEAD_PAIR * p, HEAD_PAIR * p + 1)
    spare = (FOX_HEAD_DIM, 0)

    @pl.when(p == 0)
    def _():
        triu = _tri(SCAN_BLOCK, False)
        carry = jnp.zeros((LANES, 1), F32)
        per = tq // SCAN_BLOCK
        for c in range(s // SCAN_BLOCK):
            cols = slice(c * SCAN_BLOCK, (c + 1) * SCAN_BLOCK)
            fc = _cumsum_lanes(logft_ref[:, cols], triu) + carry
            ft_scr[c // per, :, (c % per) * SCAN_BLOCK:(c % per + 1) * SCAN_BLOCK] = fc
            f_scr[cols, :] = fc.T
            carry = fc[:, SCAN_BLOCK - 1:SCAN_BLOCK]

    row = lax.broadcasted_iota(jnp.int32, (LANES, 1), 0)
    for j in range(s // tq):
        kt = kt_ref[:, j * tq:(j + 1) * tq]
        vt = vt_ref[:, j * tq:(j + 1) * tq]
        for hl in range(HEAD_PAIR):
            own = (row < FOX_HEAD_DIM) if hl == 0 else (row >= FOX_HEAD_DIM)
            f_k = ft_scr[j, pl.ds(F_LANE0 + heads[hl], 1), :] * LOG2E
            kc = jnp.where(own, kt, 0.0).astype(BF16)
            for i, part in enumerate(_split3(-f_k)):
                kc = jnp.where(row == spare[hl] + i, part, kc)
            k_scr[hl, j] = kc
            v_scr[hl, j] = jnp.where(own, vt, jnp.where(row == spare[hl], 1.0, 0.0)).astype(BF16)

    lane = lax.broadcasted_iota(jnp.int32, (1, LANES), 1)
    lo_lane = lane < FOX_HEAD_DIM
    rb = min(ATTN_ROWS, tq)
    nrb = tq // rb

    def query_block(qi, _):
        rows = [pl.ds(pl.multiple_of(qi * tq + r * rb, rb), rb) for r in range(nrb)]
        q_blocks, fq_blocks = [], []
        for r in range(nrb):
            q = q_ref[rows[r], :].astype(F32)
            ones_at = lambda l0: jnp.where((lane >= l0) & (lane < l0 + 3), 1.0, 0.0)
            q_blocks.append((jnp.where(lo_lane, q, ones_at(spare[0])).astype(BF16),
                             jnp.where(lo_lane, ones_at(spare[1]), q).astype(BF16)))
            f_rows = f_scr[rows[r], :]
            fq_blocks.append([jnp.sum(jnp.where(lane == F_LANE0 + h, f_rows, 0.0), axis=1, keepdims=True) * LOG2E
                              for h in heads])
        for g0 in range(0, nrb, ATTN_GROUP):
            rs = tuple(range(g0, min(g0 + ATTN_GROUP, nrb)))
            carry = lax.fori_loop(0, qi, lambda j, cr: step(j, cr, rs, q_blocks, fq_blocks, False),
                                  (init_rows,) * len(rs))
            carry = step(qi, carry, rs, q_blocks, fq_blocks, True)
            for i, r in enumerate(rs):
                acc = [carry[i][2 * hl + 1] for hl in range(HEAD_PAIR)]
                out = [a / a[:, spare[hl]:spare[hl] + 1] for hl, a in enumerate(acc)]
                o_ref[rows[r], :] = jnp.where(lo_lane, out[0], out[1]).astype(BF16)
        return 0

    def softmax_unit(t, f_q, m_old, row0):
        if row0 is not None:
            rr = lax.broadcasted_iota(jnp.int32, t.shape, 0) + row0
            cc = lax.broadcasted_iota(jnp.int32, t.shape, 1)
            t = jnp.where(cc <= rr, t, NEG_BIG)
        m_new = jnp.maximum(m_old, jnp.max(t, axis=1, keepdims=True) + f_q)
        return m_new, jnp.exp2(m_old - m_new), jnp.exp2(t - (m_new - f_q)).astype(BF16)

    def step(j, carry, rs, q_blocks, fq_blocks, diagonal):
        nks = {r: (r + 1) * rb if diagonal else tq for r in rs}
        units = [(i, r, hl) for i, r in enumerate(rs) for hl in range(HEAD_PAIR)]
        dots = {(r, hl): _dot(q_blocks[r][hl], k_scr[hl, j, :, 0:nks[r]]) for _, r, hl in units}
        soft = {(r, hl): softmax_unit(dots[(r, hl)], fq_blocks[r][hl], carry[i][2 * hl], r * rb if diagonal else None)
                for i, r, hl in units}
        pvs = {(r, hl): _dot_nt(soft[(r, hl)][2], v_scr[hl, j, :, 0:nks[r]]) for _, r, hl in units}
        new = []
        for i, r in enumerate(rs):
            row_state = ()
            for hl in range(HEAD_PAIR):
                m_new, alpha, _ = soft[(r, hl)]
                row_state += (m_new, alpha * carry[i][2 * hl + 1] + pvs[(r, hl)])
            new.append(row_state)
        return tuple(new)

    init_rows = (jnp.full((rb, 1), NEG_BIG, F32), jnp.zeros((rb, LANES), F32)) * HEAD_PAIR
    lax.fori_loop(0, s // tq, query_block, 0)


def _fox_prompt_call(q, kt, vt, small_t, layer=None):
    bsz, s, _ = q.shape
    tq = min(ATTN_BLOCK, s)
    blk = pl.BlockSpec((None, s, LANES), lambda b, p: (b, 0, p))
    if layer is None:
        full = pl.BlockSpec((None, LANES, s), lambda b, p: (b, p, 0))
    else:
        full = pl.BlockSpec((None, None, LANES, s), lambda b, p: (layer, b, p, 0))
    return pl.pallas_call(
        _fox_prompt_kernel,
        grid=(bsz, FOX_WIDTH // LANES),
        in_specs=[blk, full, full, pl.BlockSpec((None, LANES, s), lambda b, p: (b, 0, 0))],
        out_specs=blk,
        out_shape=jax.ShapeDtypeStruct((bsz, s, FOX_WIDTH), BF16),
        scratch_shapes=[pltpu.VMEM((HEAD_PAIR, s // tq, LANES, tq), BF16),
                        pltpu.VMEM((HEAD_PAIR, s // tq, LANES, tq), BF16),
                        pltpu.VMEM((s, LANES), F32), pltpu.VMEM((s // tq, LANES, tq), F32)],
        compiler_params=_params(40, ("arbitrary", "arbitrary")),
        name="fox_prompt",
    )(q, kt, vt, small_t)


def _fox_sample_kernel(q_ref, k_ref, v_ref, logf_ref, ck_ref, cv_ref, clogft_ref, o_ref,
                       m_scr, l_scr, acc_scr, fk_scr, fq_scr, fnew_scr):
    t = q_ref.shape[0]
    c = pl.program_id(1)
    nchunk, _, chunk = fk_scr.shape

    @pl.when(c == 0)
    def _():
        blk = min(SCAN_BLOCK, chunk)
        triu = _tri(blk, False)
        carry = jnp.zeros((FOX_HEADS, 1), F32)
        for i in range(nchunk):
            for j in range(chunk // blk):
                cols = slice(j * blk, (j + 1) * blk)
                fc = _cumsum_lanes(clogft_ref[:, i * chunk + j * blk:i * chunk + (j + 1) * blk], triu) + carry
                fk_scr[i, :, cols] = fc * LOG2E
                carry = fc[:, blk - 1:blk]
        sub = lax.broadcasted_iota(jnp.int32, (FOX_HEADS, LANES), 0)
        ln = lax.broadcasted_iota(jnp.int32, (FOX_HEADS, LANES), 1)
        total_row = jnp.sum(jnp.where(sub == ln - F_LANE0, carry, 0.0), axis=0, keepdims=True)
        f_new = _cumsum_rows(logf_ref[...], _tri(t, True)) + total_row
        f_new = f_new * LOG2E
        fnew_scr[...] = f_new.T[F_LANE0:F_LANE0 + FOX_HEADS, :]
        fq_scr[...] = jnp.concatenate([f_new[:, F_LANE0 + h:F_LANE0 + h + 1] for h in range(FOX_HEADS)], axis=0)
        m_scr[...] = jnp.full(m_scr.shape, NEG_BIG, F32)
        l_scr[...] = jnp.zeros(l_scr.shape, F32)
        acc_scr[...] = jnp.zeros(acc_scr.shape, F32)

    q = q_ref[...]
    q_heads = [q[:, h * FOX_HEAD_DIM:(h + 1) * FOX_HEAD_DIM] for h in range(FOX_HEADS)]
    f_q = fq_scr[...]

    def update(keys, values, f_k, mask, channels_first):
        qk = _dot if channels_first else _dot_nt
        pv_dot = _dot_nt if channels_first else _dot
        tt = jnp.concatenate([qk(q_heads[h], keys[h]) - f_k[h:h + 1, :] for h in range(FOX_HEADS)], axis=0)
        if mask is not None:
            tt = jnp.where(mask, tt, NEG_BIG)
        m_old = m_scr[...]
        m_new = jnp.maximum(m_old, jnp.max(tt, axis=1, keepdims=True) + f_q)
        alpha = jnp.exp2(m_old - m_new)
        pr = jnp.exp2(tt - (m_new - f_q))
        l_scr[...] = alpha * l_scr[...] + jnp.sum(pr, axis=1, keepdims=True)
        prb = pr.astype(BF16)
        pv = jnp.concatenate([pv_dot(prb[h * t:(h + 1) * t, :], values[h]) for h in range(FOX_HEADS)], axis=0)
        acc_scr[...] = alpha * acc_scr[...] + pv
        m_scr[...] = m_new

    head_rows = lambda ref: [ref[h * FOX_HEAD_DIM:(h + 1) * FOX_HEAD_DIM, :].astype(BF16) for h in range(FOX_HEADS)]
    update(head_rows(ck_ref), head_rows(cv_ref), fk_scr[c], None, True)

    @pl.when(c == nchunk - 1)
    def _():
        k_new, v_new = k_ref[...].astype(BF16), v_ref[...].astype(BF16)
        cols = lambda a: [a[:, h * FOX_HEAD_DIM:(h + 1) * FOX_HEAD_DIM] for h in range(FOX_HEADS)]
        rr = lax.broadcasted_iota(jnp.int32, (FOX_HEADS * t, t), 0) % t
        cc = lax.broadcasted_iota(jnp.int32, (FOX_HEADS * t, t), 1)
        update(cols(k_new), cols(v_new), fnew_scr[...], cc <= rr, False)
        out = acc_scr[...] / l_scr[...]
        o_ref[...] = jnp.concatenate([out[h * t:(h + 1) * t, :] for h in range(FOX_HEADS)], axis=1).astype(BF16)


def _fox_sample_call(q, k, v, small, cache_k, cache_v, cache_logf_t, layer):
    bsz, t, _ = q.shape
    past = cache_logf_t.shape[3]
    chunk = min(SAMPLE_CHUNK, past)
    nchunk = past // chunk
    new = lambda w: pl.BlockSpec((None, t, w), lambda b, c: (b, 0, 0))
    cache = pl.BlockSpec((None, None, FOX_WIDTH, chunk), lambda b, c: (layer, b, 0, c))
    rows = FOX_HEADS * t
    return pl.pallas_call(
        _fox_sample_kernel,
        grid=(bsz, nchunk),
        in_specs=[new(FOX_WIDTH), new(FOX_WIDTH), new(FOX_WIDTH), new(LANES), cache, cache,
                  pl.BlockSpec((None, None, FOX_HEADS, past), lambda b, c: (layer, b, 0, 0))],
        out_specs=new(FOX_WIDTH),
        out_shape=jax.ShapeDtypeStruct((bsz, t, FOX_WIDTH), BF16),
        scratch_shapes=[pltpu.VMEM((rows, 1), F32), pltpu.VMEM((rows, 1), F32), pltpu.VMEM((rows, FOX_HEAD_DIM), F32),
                        pltpu.VMEM((nchunk, FOX_HEADS, chunk), F32), pltpu.VMEM((rows, 1), F32),
                        pltpu.VMEM((FOX_HEADS, t), F32)],
        compiler_params=_params(40, ("arbitrary", "arbitrary")),
        name="fox_sample",
    )(q, k, v, small, cache_k, cache_v, cache_logf_t)


def _small_rows(w_in_t):
    offs = np.concatenate([[0], np.cumsum(IN_SIZES)])
    seg = lambda i: w_in_t[:, int(offs[i]):int(offs[i + 1]), :]
    pad = jnp.zeros((w_in_t.shape[0], LANES - FOX_HEADS - SSD_HEADS, w_in_t.shape[2]), w_in_t.dtype)
    return jnp.concatenate([seg(5), seg(8), pad], axis=1)


def _block_diag(w):
    d, h, b, _ = w.shape
    eye = jnp.eye(h, dtype=w.dtype)
    return jnp.einsum("dhij,hg->dhigj", w, eye).reshape(d, h * b, h * b)


def _lane_slab(v, lane0):
    d, k = v.shape
    return jnp.zeros((d, 1, LANES), v.dtype).at[:, 0, lane0:lane0 + k].set(v)


def _pad_history(state):
    return jnp.pad(state, ((0, 0), (0, 0), (SUBLANES - (CONV_W - 1), 0), (0, 0)))


def _trunk(x, mod_group, caches, prm, ssd_chunk):
    bsz, n, _ = x.shape
    x = x.reshape(bsz * n, D_MODEL)
    states = {name: [] for name in ("fox_k", "fox_v", "fox_logf", "lru_conv", "lru_h", "ssd_conv", "ssd_h")}
    prev_kv = None
    for l in range(DEPTH):
        mod4 = mod_group[l].reshape(bsz, N_SUB, 3, D_MODEL)
        x = _ffn_call(x, mod4, prm["npre"], prm["npost"], prm["wg"], prm["wu"], prm["wd"], l, 0, 0, n)
        prompt = caches is None
        stacking = prompt and l == DEPTH - 1
        proj = _inproj_call(x, mod4, prm["npre"], prm["w_in_t"], prm["w_small"], prm["fbias"], l, n,
                            transposed=prompt, prev_kv=prev_kv if stacking else None)
        lrux, lrug, q, k, v, z, xbc, small = proj[:8]
        per_seq = lambda a: a.reshape(bsz, n, a.shape[-1])
        state_layer = None if prompt else l
        src = prm["zero_state"] if prompt else caches
        ya, lru_conv, lru_h = _lru_call(per_seq(lrux), per_seq(lrug), src["lru_conv"], src["lru_h"],
                                        prm["lru_cw"], prm["lru_cb"], prm["lru_wgate"], prm["lru_bgate"],
                                        prm["lru_lam"], l, state_layer)
        if prompt:
            small_t = proj[8]
            yb = _fox_prompt_call(per_seq(q), k, v, small_t, l if stacking else None)
            prev_kv = (k, v)
            logf_out = jnp.swapaxes(small_t[:, F_LANE0:F_LANE0 + FOX_HEADS, :], 1, 2)
        else:
            yb = _fox_sample_call(per_seq(q), per_seq(k), per_seq(v), per_seq(small),
                                  caches["fox_k"], caches["fox_v"], caches["fox_logf_t"], l)
            k_out = k.reshape(bsz, n, FOX_HEADS, FOX_HEAD_DIM)
            v_out = v.reshape(bsz, n, FOX_HEADS, FOX_HEAD_DIM)
            logf_out = small.reshape(bsz, n, LANES)[:, :, F_LANE0:F_LANE0 + FOX_HEADS]
        yc, ssd_conv, ssd_h = _ssd_call(per_seq(xbc), per_seq(z), per_seq(small), src["ssd_conv"], src["ssd_h"],
                                        prm["ssd_cw"], prm["ssd_cb"], prm["ssd_dtb"], prm["ssd_alog"],
                                        prm["ssd_dvec"], prm["ssd_nw"], l, state_layer, ssd_chunk)
        x = _outproj_call(x, ya.reshape(bsz * n, -1), yb.reshape(bsz * n, -1), yc.reshape(bsz * n, -1),
                          mod4, prm["npost"], prm["w_out"], l, n)
        x = _ffn_call(x, mod4, prm["npre"], prm["npost"], prm["wg"], prm["wu"], prm["wd"], l, 2, 1, n)
        if not prompt:
            states["fox_k"].append(k_out)
            states["fox_v"].append(v_out)
        states["fox_logf"].append(logf_out)
        states["lru_conv"].append(lru_conv)
        states["lru_h"].append(lru_h.reshape(bsz, LRU_WIDTH))
        states["ssd_conv"].append(ssd_conv)
        states["ssd_h"].append(ssd_h.reshape(bsz, SSD_HEADS, SSD_HEAD_DIM, D_STATE))
    out = {name: jnp.stack(vals, axis=0) for name, vals in states.items() if vals}
    if prev_kv is not None:
        heads_last = lambda a: jnp.transpose(a.reshape(DEPTH, bsz, FOX_HEADS, FOX_HEAD_DIM, n), (0, 1, 4, 2, 3))
        out["fox_k"], out["fox_v"] = heads_last(prev_kv[0]), heads_last(prev_kv[1])
    return x.reshape(bsz, n, D_MODEL), out


def kernel(x_prompt, x_sample, c_prompt, c_sample, cache_fox_k, cache_fox_v, cache_fox_logf, state_lru_conv, state_lru_h, state_ssd_conv, state_ssd_h, w_mod, b_mod, norm_pre, norm_post, ffn_w_gate, ffn_w_up, ffn_w_down, w_in, w_out, lru_conv_w, lru_conv_b, lru_wa, lru_ba, lru_wx, lru_bx, lru_lambda, fox_f_bias, ssd_conv_w, ssd_conv_b, ssd_dt_bias, ssd_a_log, ssd_d, ssd_norm_w):
    n_prompt, n_sample = x_prompt.shape[0], x_sample.shape[0]
    w_in_t = jnp.swapaxes(w_in, 1, 2)
    prm = {
        "npre": norm_pre.reshape(DEPTH, N_SUB, 1, D_MODEL),
        "npost": norm_post.reshape(DEPTH, N_SUB, 1, D_MODEL),
        "wg": ffn_w_gate.astype(BF16), "wu": ffn_w_up.astype(BF16), "wd": ffn_w_down.astype(BF16),
        "w_in_t": w_in_t, "w_small": _small_rows(w_in_t), "w_out": w_out.astype(BF16),
        "fbias": _lane_slab(fox_f_bias, F_LANE0),
        "lru_cw": lru_conv_w, "lru_cb": lru_conv_b.reshape(DEPTH, 1, LRU_WIDTH),
        "lru_wgate": jnp.concatenate([_block_diag(lru_wa), _block_diag(lru_wx)], axis=-1).astype(BF16),
        "lru_bgate": jnp.concatenate([lru_ba, lru_bx], axis=-1).reshape(DEPTH, 1, 2 * LRU_WIDTH),
        "lru_lam": lru_lambda.reshape(DEPTH, 1, LRU_WIDTH),
        "ssd_cw": ssd_conv_w, "ssd_cb": ssd_conv_b.reshape(DEPTH, 1, SSD_CONV_DIM),
        "ssd_dtb": _lane_slab(ssd_dt_bias, DT_LANE0), "ssd_alog": _lane_slab(ssd_a_log, DT_LANE0),
        "ssd_dvec": jnp.repeat(ssd_d, SSD_HEAD_DIM, axis=-1).reshape(DEPTH, 1, SSD_WIDTH),
        "ssd_nw": ssd_norm_w.reshape(DEPTH, 1, SSD_WIDTH),
        "zero_state": {
            "lru_conv": jnp.zeros((1, SUBLANES, LRU_WIDTH), F32), "lru_h": jnp.zeros((1, 1, LRU_WIDTH), F32),
            "ssd_conv": jnp.zeros((1, SUBLANES, SSD_CONV_DIM), F32), "ssd_h": jnp.zeros((1, SSD_WIDTH, D_STATE), F32),
        },
    }
    caches = {
        "fox_k": jnp.transpose(cache_fox_k, (0, 1, 3, 4, 2)).reshape(DEPTH, n_sample, FOX_WIDTH, -1),
        "fox_v": jnp.transpose(cache_fox_v, (0, 1, 3, 4, 2)).reshape(DEPTH, n_sample, FOX_WIDTH, -1),
        "fox_logf_t": jnp.swapaxes(cache_fox_logf, 2, 3),
        "lru_conv": _pad_history(state_lru_conv),
        "lru_h": state_lru_h.reshape(DEPTH, n_sample, 1, LRU_WIDTH),
        "ssd_conv": _pad_history(state_ssd_conv),
        "ssd_h": state_ssd_h.reshape(DEPTH, n_sample, SSD_WIDTH, D_STATE),
    }
    mod = _mod_call(jnp.concatenate([c_prompt, c_sample], axis=0), w_mod, b_mod)
    y_prompt, sp = _trunk(x_prompt, mod[:, :n_prompt], None, prm, ssd_chunk=256)
    y_sample, ss = _trunk(x_sample, mod[:, n_prompt:], caches, prm, ssd_chunk=x_sample.shape[1])
    names = ("fox_k", "fox_v", "fox_logf", "lru_conv", "lru_h", "ssd_conv", "ssd_h")
    return (y_prompt, y_sample) + tuple(sp[n] for n in names) + tuple(ss[n] for n in names)
```

```python
import functools

import numpy as np
import jax
import jax.numpy as jnp
from jax import lax
from jax.experimental import pallas as pl
from jax.experimental.pallas import tpu as pltpu

F32 = jnp.float32
BF16 = jnp.bfloat16

D_MODEL = 1024
DEPTH = 2
CONV_W = 4
EPS = 1e-6
LRU_WIDTH = 256
LRU_HEADS = 4
LRU_BLOCK = LRU_WIDTH // LRU_HEADS
LRU_C = 8.0
FOX_HEADS = 8
FOX_HEAD_DIM = 64
FOX_WIDTH = FOX_HEADS * FOX_HEAD_DIM
SSD_HEADS = 4
SSD_HEAD_DIM = 64
SSD_WIDTH = SSD_HEADS * SSD_HEAD_DIM
SSD_GROUPS = 2
D_STATE = 128
SSD_CONV_DIM = SSD_WIDTH + 2 * SSD_GROUPS * D_STATE
IN_SIZES = (LRU_WIDTH, LRU_WIDTH, FOX_WIDTH, FOX_WIDTH, FOX_WIDTH, FOX_HEADS, SSD_WIDTH, SSD_CONV_DIM, SSD_HEADS)
D_FF = 2816
N_SUB = 3

LANES = 128
SUBLANES = 8
HEAD_PAIR = LANES // FOX_HEAD_DIM

F_LANE0 = 0
DT_LANE0 = FOX_HEADS
COL_LRU_X = 0
COL_LRU_G = COL_LRU_X + LRU_WIDTH
COL_Q = COL_LRU_G + LRU_WIDTH
COL_K = COL_Q + FOX_WIDTH
COL_V = COL_K + FOX_WIDTH
COL_Z = COL_V + FOX_WIDTH
COL_XBC = COL_Z + SSD_WIDTH
COL_SMALL = COL_XBC + SSD_CONV_DIM
D_IN_PAD = COL_SMALL + LANES
D_IN = sum(IN_SIZES)

ROW_TILE = 512
FFN_ROW_TILE = 1024
FF_CHUNK = 512
ATTN_BLOCK = 1024
ATTN_ROWS = 128
ATTN_GROUP = 8
SCAN_BLOCK = 256
SAMPLE_CHUNK = 2048
NEG_BIG = -1e30
LOG2E = 1.4426950408889634
Q_SCALE = LOG2E * FOX_HEAD_DIM ** -0.5


def _dot(a, b):
    return jnp.dot(a, b, preferred_element_type=F32)


def _dot_nt(a, b):
    return lax.dot_general(a, b, (((1,), (1,)), ((), ())), preferred_element_type=F32)


def _dot_tn(a, b):
    return lax.dot_general(a, b, (((0,), (0,)), ((), ())), preferred_element_type=F32)


def _silu(x):
    return x * jax.nn.sigmoid(x)


def _softplus(x):
    return jnp.maximum(x, 0.0) + jnp.log1p(jnp.exp(-jnp.abs(x)))


def _rms(x):
    return x * lax.rsqrt(jnp.mean(x * x, axis=-1, keepdims=True) + EPS)


def _per_seq(rows, per_seq, fn):
    g = per_seq[0].shape[0]
    if g == 1:
        return fn(rows, *per_seq)
    tm, d = rows.shape
    out = fn(rows.reshape(g, tm // g, d), *[p[:, None, :] for p in per_seq])
    return out.reshape(tm, d)


def _pre_norm(x, npre, mod_ref, seqs=slice(None)):
    h = _rms(x) * npre
    return _per_seq(h, (mod_ref[seqs, 1, :], mod_ref[seqs, 0, :]), lambda r, sc, sh: r * (1.0 + sc) + sh)


def _post_norm(x, y, npost, mod_ref, w, seqs=slice(None)):
    yn = _rms(y) * npost
    return x + _per_seq(yn, (mod_ref[seqs, 2, :],), lambda r, gt: (w * gt) * r)


def _seq_grouping(rows_per_seq, tm):
    if rows_per_seq % tm == 0:
        return 1, rows_per_seq // tm
    assert tm % rows_per_seq == 0
    return tm // rows_per_seq, 1


def _mod_spec(g, tiles_per_seq, sub):
    if g == 1:
        return pl.BlockSpec((1, None, 3, D_MODEL), lambda i: (i // tiles_per_seq, sub, 0, 0))
    return pl.BlockSpec((g, None, 3, D_MODEL), lambda i: (i, sub, 0, 0))


def _params(vmem_mb, sem):
    return pltpu.CompilerParams(dimension_semantics=sem, vmem_limit_bytes=vmem_mb << 20)


def _mod_kernel(c_ref, w_ref, b_ref, o_ref):
    a = _silu(c_ref[...]).astype(BF16)
    o_ref[...] = _dot(a, w_ref[...].astype(BF16)) + b_ref[...]


def _mod_call(c_all, w_mod, b_mod):
    nseq = c_all.shape[0]
    width = N_SUB * 3 * D_MODEL
    tn = 1024
    return pl.pallas_call(
        _mod_kernel,
        grid=(DEPTH, width // tn),
        in_specs=[
            pl.BlockSpec((nseq, D_MODEL), lambda l, n: (0, 0)),
            pl.BlockSpec((None, D_MODEL, tn), lambda l, n: (l, 0, n)),
            pl.BlockSpec((None, 1, tn), lambda l, n: (l, 0, n)),
        ],
        out_specs=pl.BlockSpec((None, nseq, tn), lambda l, n: (l, 0, n)),
        out_shape=jax.ShapeDtypeStruct((DEPTH, nseq, width), F32),
        compiler_params=_params(24, ("arbitrary", "arbitrary")),
        name="adaln_mod",
    )(c_all, w_mod, b_mod.reshape(DEPTH, 1, width))


def _halves(tm, nseq):
    half = tm // 2
    rows = [slice(i * half, (i + 1) * half) for i in range(2)]
    seqs = [slice(0, 1)] * 2 if nseq == 1 else [slice(i * nseq // 2, (i + 1) * nseq // 2) for i in range(2)]
    return rows, seqs


def _swiglu(hs, wg_ref, wu_ref, wd_ref):
    accs = [None] * len(hs)
    for off in range(0, D_FF, FF_CHUNK):
        fc = min(FF_CHUNK, D_FF - off)
        for i, h in enumerate(hs):
            g = _dot(h, wg_ref[:, off:off + fc])
            u = _dot(h, wu_ref[:, off:off + fc])
            a = (_silu(g) * u).astype(BF16)
            d = _dot(a, wd_ref[off:off + fc, :])
            accs[i] = d if accs[i] is None else accs[i] + d
    return accs


def _ffn_kernel(x_ref, mod_ref, npre_ref, npost_ref, wg_ref, wu_ref, wd_ref, o_ref):
    rows, seqs = _halves(x_ref.shape[0], mod_ref.shape[0])
    xs = [x_ref[r, :] for r in rows]
    hs = [_pre_norm(x, npre_ref[...], mod_ref, sq).astype(BF16) for x, sq in zip(xs, seqs)]
    accs = _swiglu(hs, wg_ref, wu_ref, wd_ref)
    for i, r in enumerate(rows):
        o_ref[r, :] = _post_norm(xs[i], accs[i], npost_ref[...], mod_ref, 0.5, seqs[i])


def _outffn_kernel(x_ref, ya_ref, yb_ref, yc_ref, mod1_ref, mod2_ref, npost1_ref, npre2_ref, npost2_ref,
                   wo_ref, wg_ref, wu_ref, wd_ref, o_ref):
    rows, seqs = _halves(x_ref.shape[0], mod1_ref.shape[0])
    xs = []
    for r, sq in zip(rows, seqs):
        y = (_dot(ya_ref[r, :], wo_ref[0:LRU_WIDTH, :])
             + _dot(yb_ref[r, :], wo_ref[LRU_WIDTH:LRU_WIDTH + FOX_WIDTH, :])
             + _dot(yc_ref[r, :], wo_ref[LRU_WIDTH + FOX_WIDTH:, :]))
        xs.append(_post_norm(x_ref[r, :], y, npost1_ref[...], mod1_ref, 1.0, sq))
    hs = [_pre_norm(x, npre2_ref[...], mod2_ref, sq).astype(BF16) for x, sq in zip(xs, seqs)]
    accs = _swiglu(hs, wg_ref, wu_ref, wd_ref)
    for i, r in enumerate(rows):
        o_ref[r, :] = _post_norm(xs[i], accs[i], npost2_ref[...], mod2_ref, 0.5, seqs[i])


def _outffn_call(x, ya, yb, yc, mod4, npre, npost, w_out, wg, wu, wd, layer, rows_per_seq):
    m = x.shape[0]
    tm = min(FFN_ROW_TILE, m)
    g, tps = _seq_grouping(rows_per_seq, tm)
    row = lambda w: pl.BlockSpec((tm, w), lambda i: (i, 0))
    once = dict(pipeline_mode=pl.Buffered(1))
    wspec = lambda shape: pl.BlockSpec((None, None) + shape, lambda i: (layer, 1, 0, 0), **once)
    norm = lambda sub: pl.BlockSpec((None, None, 1, D_MODEL), lambda i: (layer, sub, 0, 0))
    return pl.pallas_call(
        _outffn_kernel,
        grid=(m // tm,),
        in_specs=[
            row(D_MODEL), row(LRU_WIDTH), row(FOX_WIDTH), row(SSD_WIDTH),
            _mod_spec(g, tps, 1), _mod_spec(g, tps, 2),
            norm(1), norm(2), norm(2),
            pl.BlockSpec((None, D_MODEL, D_MODEL), lambda i: (layer, 0, 0), **once),
            wspec((D_MODEL, D_FF)), wspec((D_MODEL, D_FF)), wspec((D_FF, D_MODEL)),
        ],
        out_specs=row(D_MODEL),
        out_shape=jax.ShapeDtypeStruct((m, D_MODEL), F32),
        compiler_params=_params(57, ("arbitrary",)),
        name="outproj_ffn",
    )(x, ya, yb, yc, mod4, mod4, npost, npre, npost, w_out, wg, wu, wd)


def _ffn_call(x, mod4, npre, npost, wg, wu, wd, layer, sub, ffn_idx, rows_per_seq):
    m = x.shape[0]
    tm = min(FFN_ROW_TILE, m)
    g, tps = _seq_grouping(rows_per_seq, tm)
    wspec = lambda shape: pl.BlockSpec((None, None) + shape, lambda i: (layer, ffn_idx, 0, 0),
                                       pipeline_mode=pl.Buffered(1))
    nspec = pl.BlockSpec((None, None, 1, D_MODEL), lambda i: (layer, sub, 0, 0))
    return pl.pallas_call(
        _ffn_kernel,
        grid=(m // tm,),
        in_specs=[
            pl.BlockSpec((tm, D_MODEL), lambda i: (i, 0)),
            _mod_spec(g, tps, sub),
            nspec, nspec,
            wspec((D_MODEL, D_FF)), wspec((D_MODEL, D_FF)), wspec((D_FF, D_MODEL)),
        ],
        out_specs=pl.BlockSpec((tm, D_MODEL), lambda i: (i, 0)),
        out_shape=jax.ShapeDtypeStruct((m, D_MODEL), F32),
        compiler_params=_params(56, ("arbitrary",)),
        name="ffn",
    )(x, mod4, npre, npost, wg, wu, wd)


def _inproj_kernel(*refs, transposed, stacked):
    n_in = 8 if stacked else 6
    x_ref, mod_ref, npre_ref, w_ref, wsmall_ref, fbias_ref = refs[:6]
    lrux_ref, lrug_ref, q_ref, k_ref, v_ref, z_ref, xbc_ref, small_ref, *rest = refs[n_in:]
    wt_ref = rest[-1]

    @pl.when(pl.program_id(0) == 0)
    def _():
        src = int(np.cumsum(IN_SIZES)[4])
        assert src == COL_Z
        moves = [(0, 0, src), (src + FOX_HEADS, COL_Z, SSD_WIDTH), (src + FOX_HEADS + SSD_WIDTH, COL_XBC, SSD_CONV_DIM)]
        for s0, d0, n in moves:
            for off in range(0, n, LRU_WIDTH):
                wt_ref[d0 + off:d0 + off + LRU_WIDTH, :] = w_ref[s0 + off:s0 + off + LRU_WIDTH, :].astype(BF16)
        wt_ref[COL_SMALL:COL_SMALL + LANES, :] = wsmall_ref[...].astype(BF16)

    h = _pre_norm(x_ref[...], npre_ref[...], mod_ref).astype(BF16)
    col = lambda start, width: _dot_nt(h, wt_ref[start:start + width, :])
    col_t = lambda start, width: _dot_nt(wt_ref[start:start + width, :], h)
    lrux_ref[...] = col(COL_LRU_X, LRU_WIDTH)
    lrug_ref[...] = col(COL_LRU_G, LRU_WIDTH)
    q_ref[...] = (col(COL_Q, FOX_WIDTH) * Q_SCALE).astype(BF16)
    if transposed:
        kv_t = col_t(COL_K, 2 * FOX_WIDTH)
        if stacked:
            k_ref[0], v_ref[0] = refs[6][...], refs[7][...]
            k_ref[1], v_ref[1] = kv_t[:FOX_WIDTH], kv_t[FOX_WIDTH:]
        else:
            k_ref[...] = kv_t[:FOX_WIDTH]
            v_ref[...] = kv_t[FOX_WIDTH:]
    else:
        k_ref[...] = col(COL_K, FOX_WIDTH)
        v_ref[...] = col(COL_V, FOX_WIDTH)
    z_ref[...] = col(COL_Z, SSD_WIDTH)
    xbc_ref[...] = col(COL_XBC, SSD_CONV_DIM)
    small = col(COL_SMALL, LANES)
    t = small + fbias_ref[...]
    logf = jnp.minimum(t, 0.0) - jnp.log1p(jnp.exp(-jnp.abs(t)))
    lane = lax.broadcasted_iota(jnp.int32, small.shape, 1)
    small = jnp.where(lane < DT_LANE0, logf, small)
    small_ref[...] = small
    if transposed:
        rest[0][...] = small.T


def _inproj_call(x, mod4, npre, w_in_t, w_small, fbias, layer, rows_per_seq, transposed, prev_kv=None):
    m = x.shape[0]
    tm = min(ROW_TILE, m)
    g, tps = _seq_grouping(rows_per_seq, tm)
    widths = (LRU_WIDTH, LRU_WIDTH, FOX_WIDTH, FOX_WIDTH, FOX_WIDTH, SSD_WIDTH, SSD_CONV_DIM, LANES)
    dtypes = (F32, F32, BF16, F32, F32, F32, F32, F32)
    out_specs = [pl.BlockSpec((tm, w), lambda i: (i, 0)) for w in widths]
    out_shape = [jax.ShapeDtypeStruct((m, w), dt) for w, dt in zip(widths, dtypes)]
    in_specs = [
        pl.BlockSpec((tm, D_MODEL), lambda i: (i, 0)),
        _mod_spec(g, tps, 1),
        pl.BlockSpec((None, None, 1, D_MODEL), lambda i: (layer, 1, 0, 0)),
        pl.BlockSpec((None, D_IN, D_MODEL), lambda i: (layer, 0, 0), pipeline_mode=pl.Buffered(1)),
        pl.BlockSpec((None, LANES, D_MODEL), lambda i: (layer, 0, 0)),
        pl.BlockSpec((None, 1, LANES), lambda i: (layer, 0, 0)),
    ]
    if transposed:
        assert g == 1
        nseq = m // rows_per_seq
        for idx, w in ((3, FOX_WIDTH), (4, FOX_WIDTH), (len(widths), LANES)):
            spec = pl.BlockSpec((None, w, tm), lambda i: (i // tps, 0, i % tps))
            shape = jax.ShapeDtypeStruct((nseq, w, rows_per_seq), F32)
            out_specs[idx:idx + 1], out_shape[idx:idx + 1] = [spec], [shape]
    extra = ()
    if prev_kv is not None:
        assert transposed and layer == 1 and DEPTH == 2
        extra = tuple(prev_kv)
        in_specs += [pl.BlockSpec((None, FOX_WIDTH, tm), lambda i: (i // tps, 0, i % tps))] * 2
        for idx in (3, 4):
            out_specs[idx] = pl.BlockSpec((DEPTH, None, FOX_WIDTH, tm), lambda i: (0, i // tps, 0, i % tps))
            out_shape[idx] = jax.ShapeDtypeStruct((DEPTH, nseq, FOX_WIDTH, rows_per_seq), F32)
    return pl.pallas_call(
        functools.partial(_inproj_kernel, transposed=transposed, stacked=prev_kv is not None),
        grid=(m // tm,),
        in_specs=in_specs,
        out_specs=out_specs,
        out_shape=out_shape,
        scratch_shapes=[pltpu.VMEM((D_IN_PAD, D_MODEL), BF16)],
        compiler_params=_params(56, ("arbitrary",)),
        name="inproj",
    )(x, mod4, npre, w_in_t, w_small, fbias, *extra)


def _causal_conv(xx, cw, cb, n):
    u = cb + cw[0:1] * pltpu.roll(xx, 3, 0)[SUBLANES:SUBLANES + n]
    u = u + cw[1:2] * pltpu.roll(xx, 2, 0)[SUBLANES:SUBLANES + n]
    u = u + cw[2:3] * pltpu.roll(xx, 1, 0)[SUBLANES:SUBLANES + n]
    return u + cw[3:4] * xx[SUBLANES:SUBLANES + n]


def _last_rows(xx, k):
    return pltpu.roll(xx, k, 0)[0:SUBLANES][0:k]


def _tri(n, lower):
    r = lax.broadcasted_iota(jnp.int32, (n, n), 0)
    c = lax.broadcasted_iota(jnp.int32, (n, n), 1)
    return ((r >= c) if lower else (r <= c)).astype(F32).astype(BF16)


def _split3(x):
    hi = x.astype(BF16)
    r1 = x - hi.astype(F32)
    mid = r1.astype(BF16)
    lo = (r1 - mid.astype(F32)).astype(BF16)
    return hi, mid, lo


def _cumsum_lanes(x, triu):
    hi, mid, lo = _split3(x)
    return _dot(hi, triu) + _dot(mid, triu) + _dot(lo, triu)


def _cumsum_rows(x, tril):
    hi, mid, lo = _split3(x)
    return _dot(tril, hi) + _dot(tril, mid) + _dot(tril, lo)


def _lru_kernel(x_ref, g_ref, prev_ref, h0_ref, cw_ref, cb_ref, wgate_ref, bgate_ref, lam_ref,
                y_ref, convnew_ref, hnew_ref, a_scr, b_scr):
    n = x_ref.shape[0]
    xx = jnp.concatenate([prev_ref[...], x_ref[...]], axis=0)
    convnew_ref[...] = _last_rows(xx, CONV_W - 1)
    u = _causal_conv(xx, cw_ref[...], cb_ref[...], n)
    gates = _dot(u.astype(BF16), wgate_ref[...]) + bgate_ref[...]
    r = jax.nn.sigmoid(gates[:, :LRU_WIDTH])
    i = jax.nn.sigmoid(gates[:, LRU_WIDTH:])
    log_a = (-LRU_C * r) * _softplus(-lam_ref[...])
    a = jnp.exp(log_a)
    b = jnp.sqrt(-jnp.tanh(log_a) * (a * a + 1.0)) * (i * u)
    row = lax.broadcasted_iota(jnp.int32, (n, 1), 0) % SUBLANES
    for d in (1, 2, 4):
        keep = row >= d
        b = jnp.where(keep, a * pltpu.roll(b, d, 0) + b, b)
        a = jnp.where(keep, a * pltpu.roll(a, d, 0), a)
    a_scr[...] = a
    b_scr[...] = b

    def group(j, h):
        off = pl.multiple_of(j * SUBLANES, SUBLANES)
        hb = a_scr[pl.ds(off, SUBLANES), :] * h + b_scr[pl.ds(off, SUBLANES), :]
        b_scr[pl.ds(off, SUBLANES), :] = hb
        return jnp.broadcast_to(hb[SUBLANES - 1:SUBLANES, :], hb.shape)

    h_last = lax.fori_loop(0, n // SUBLANES, group,
                           jnp.broadcast_to(h0_ref[...], (SUBLANES, LRU_WIDTH)), unroll=4)
    hnew_ref[...] = h_last[0:1]
    y_ref[...] = (b_scr[...] * jax.nn.gelu(g_ref[...])).astype(BF16)


def _lru_call(lru_x, lru_g, prev8, h0, cw, cb, wgate, bgate, lam, layer, state_layer):
    bsz, n, _ = lru_x.shape
    seq = pl.BlockSpec((None, n, LRU_WIDTH), lambda b: (b, 0, 0))
    if state_layer is None:
        prev_spec = pl.BlockSpec((None, SUBLANES, LRU_WIDTH), lambda b: (0, 0, 0))
        h0_spec = pl.BlockSpec((None, 1, LRU_WIDTH), lambda b: (0, 0, 0))
    else:
        prev_spec = pl.BlockSpec((None, None, SUBLANES, LRU_WIDTH), lambda b: (state_layer, b, 0, 0))
        h0_spec = pl.BlockSpec((None, None, 1, LRU_WIDTH), lambda b: (state_layer, b, 0, 0))
    par = lambda r, w: pl.BlockSpec((None, r, w), lambda b: (layer, 0, 0))
    return pl.pallas_call(
        _lru_kernel,
        grid=(bsz,),
        in_specs=[seq, seq, prev_spec, h0_spec, par(CONV_W, LRU_WIDTH), par(1, LRU_WIDTH),
                  par(LRU_WIDTH, 2 * LRU_WIDTH), par(1, 2 * LRU_WIDTH), par(1, LRU_WIDTH)],
        out_specs=[seq,
                   pl.BlockSpec((None, CONV_W - 1, LRU_WIDTH), lambda b: (b, 0, 0)),
                   pl.BlockSpec((None, 1, LRU_WIDTH), lambda b: (b, 0, 0))],
        out_shape=[jax.ShapeDtypeStruct((bsz, n, LRU_WIDTH), BF16),
                   jax.ShapeDtypeStruct((bsz, CONV_W - 1, LRU_WIDTH), F32),
                   jax.ShapeDtypeStruct((bsz, 1, LRU_WIDTH), F32)],
        scratch_shapes=[pltpu.VMEM((n, LRU_WIDTH), F32), pltpu.VMEM((n, LRU_WIDTH), F32)],
        compiler_params=_params(40, ("arbitrary",)),
        name="rg_lru",
    )(lru_x, lru_g, prev8, h0, cw, cb, wgate, bgate, lam)


def _ssd_kernel(xbc_ref, z_ref, small_ref, prev_ref, h0_ref, cw_ref, cb_ref, dtb_ref, alog_ref, dvec_ref, nw_ref,
                y_ref, convnew_ref, h_ref, tail_scr):
    n = xbc_ref.shape[0]
    c = pl.program_id(1)

    @pl.when(c == 0)
    def _():
        tail_scr[...] = prev_ref[...]
        h_ref[...] = h0_ref[...]

    xx = jnp.concatenate([tail_scr[...], xbc_ref[...]], axis=0)
    tail_scr[...] = xx[n:n + SUBLANES]
    convnew_ref[...] = _last_rows(xx, CONV_W - 1)
    act = _silu(_causal_conv(xx, cw_ref[...], cb_ref[...], n))
    xs = act[:, :SSD_WIDTH]
    bm = act[:, SSD_WIDTH:SSD_WIDTH + SSD_GROUPS * D_STATE]
    cm = act[:, SSD_WIDTH + SSD_GROUPS * D_STATE:]

    dt = _softplus(small_ref[...] + dtb_ref[...])
    dta = dt * (-jnp.exp(alog_ref[...]))
    cum = _cumsum_rows(dta, _tri(n, True))
    cum_t = cum.T
    rr = lax.broadcasted_iota(jnp.int32, (n, n), 0)
    cc = lax.broadcasted_iota(jnp.int32, (n, n), 1)
    causal = rr >= cc
    lo_lane = lax.broadcasted_iota(jnp.int32, (1, LANES), 1) < SSD_HEAD_DIM
    lo_row = lax.broadcasted_iota(jnp.int32, (LANES, 1), 0) < SSD_HEAD_DIM
    dvec = dvec_ref[...]

    ys = []
    for g in range(SSD_GROUPS):
        sl = slice(g * LANES, (g + 1) * LANES)
        xg, bg, cg = xs[:, sl], bm[:, sl].astype(BF16), cm[:, sl].astype(BF16)
        heads = (2 * g, 2 * g + 1)
        col = lambda a, h: a[:, DT_LANE0 + h:DT_LANE0 + h + 1]
        pick = lambda f: jnp.where(lo_lane, f(heads[0]), f(heads[1]))
        dx = xg * pick(lambda h: col(dt, h))
        dxb = dx.astype(BF16)
        cb_mat = _dot_nt(cg, bg)
        yd = []
        for h in heads:
            seg = col(cum, h) - cum_t[DT_LANE0 + h:DT_LANE0 + h + 1, :]
            lmat = jnp.exp(jnp.where(causal, seg, NEG_BIG))
            yd.append(_dot((cb_mat * lmat).astype(BF16), dxb))
        y_diag = jnp.where(lo_lane, yd[0], yd[1])
        last = lambda h: col(cum, h)[n - 1:n, :]
        decay_end = pick(lambda h: jnp.exp(last(h) - col(cum, h)))
        states = _dot_tn((dx * decay_end).astype(BF16), bg)
        h_prev = h_ref[sl, :]
        y_off = _dot_nt(cg, h_prev.astype(BF16)) * pick(lambda h: jnp.exp(col(cum, h)))
        chunk_decay = jnp.where(lo_row, jnp.exp(last(heads[0])), jnp.exp(last(heads[1])))
        h_ref[sl, :] = chunk_decay * h_prev + states
        ys.append(y_diag + y_off + dvec[:, sl] * xg)
    y = jnp.concatenate(ys, axis=1)
    y_ref[...] = (_rms(y * _silu(z_ref[...])) * nw_ref[...]).astype(BF16)


def _ssd_call(xbc, z, small, prev8, h0, cw, cb, dtb, alog, dvec, nw, layer, state_layer, chunk):
    bsz, n, _ = xbc.shape
    nc = n // chunk
    seq = lambda w: pl.BlockSpec((None, chunk, w), lambda b, c: (b, c, 0))
    if state_layer is None:
        prev_spec = pl.BlockSpec((None, SUBLANES, SSD_CONV_DIM), lambda b, c: (0, 0, 0))
        h0_spec = pl.BlockSpec((None, SSD_WIDTH, D_STATE), lambda b, c: (0, 0, 0))
    else:
        prev_spec = pl.BlockSpec((None, None, SUBLANES, SSD_CONV_DIM), lambda b, c: (state_layer, b, 0, 0))
        h0_spec = pl.BlockSpec((None, None, SSD_WIDTH, D_STATE), lambda b, c: (state_layer, b, 0, 0))
    par = lambda r, w: pl.BlockSpec((None, r, w), lambda b, c: (layer, 0, 0))
    return pl.pallas_call(
        _ssd_kernel,
        grid=(bsz, nc),
        in_specs=[seq(SSD_CONV_DIM), seq(SSD_WIDTH), seq(LANES), prev_spec, h0_spec,
                  par(CONV_W, SSD_CONV_DIM), par(1, SSD_CONV_DIM), par(1, LANES), par(1, LANES),
                  par(1, SSD_WIDTH), par(1, SSD_WIDTH)],
        out_specs=[seq(SSD_WIDTH),
                   pl.BlockSpec((None, CONV_W - 1, SSD_CONV_DIM), lambda b, c: (b, 0, 0)),
                   pl.BlockSpec((None, SSD_WIDTH, D_STATE), lambda b, c: (b, 0, 0))],
        out_shape=[jax.ShapeDtypeStruct((bsz, n, SSD_WIDTH), BF16),
                   jax.ShapeDtypeStruct((bsz, CONV_W - 1, SSD_CONV_DIM), F32),
                   jax.ShapeDtypeStruct((bsz, SSD_WIDTH, D_STATE), F32)],
        scratch_shapes=[pltpu.VMEM((SUBLANES, SSD_CONV_DIM), F32)],
        compiler_params=_params(32, ("arbitrary", "arbitrary")),
        name="ssd",
    )(xbc, z, small, prev8, h0, cw, cb, dtb, alog, dvec, nw)


def _fox_prompt_kernel(q_ref, kt_ref, vt_ref, logft_ref, o_ref, k_scr, v_scr, f_scr, ft_scr):
    s = kt_ref.shape[1]
    tq = k_scr.shape[3]
    p = pl.program_id(1)
    heads = (HEAD_PAIR * p, HEAD_PAIR * p + 1)
    spare = (FOX_HEAD_DIM, 0)

    @pl.when(p == 0)
    def _():
        triu = _tri(SCAN_BLOCK, False)
        carry = jnp.zeros((LANES, 1), F32)
        per = tq // SCAN_BLOCK
        for c in range(s // SCAN_BLOCK):
            cols = slice(c * SCAN_BLOCK, (c + 1) * SCAN_BLOCK)
            fc = _cumsum_lanes(logft_ref[:, cols], triu) + carry
            ft_scr[c // per, :, (c % per) * SCAN_BLOCK:(c % per + 1) * SCAN_BLOCK] = fc
            f_scr[cols, :] = fc.T
            carry = fc[:, SCAN_BLOCK - 1:SCAN_BLOCK]

    row = lax.broadcasted_iota(jnp.int32, (LANES, 1), 0)
    for j in range(s // tq):
        kt = kt_ref[:, j * tq:(j + 1) * tq]
        vt = vt_ref[:, j * tq:(j + 1) * tq]
        for hl in range(HEAD_PAIR):
            own = (row < FOX_HEAD_DIM) if hl == 0 else (row >= FOX_HEAD_DIM)
            f_k = ft_scr[j, pl.ds(F_LANE0 + heads[hl], 1), :] * LOG2E
            kc = jnp.where(own, kt, 0.0).astype(BF16)
            for i, part in enumerate(_split3(-f_k)):
                kc = jnp.where(row == spare[hl] + i, part, kc)
            k_scr[hl, j] = kc
            v_scr[hl, j] = jnp.where(own, vt, jnp.where(row == spare[hl], 1.0, 0.0)).astype(BF16)

    lane = lax.broadcasted_iota(jnp.int32, (1, LANES), 1)
    lo_lane = lane < FOX_HEAD_DIM
    rb = min(ATTN_ROWS, tq)
    nrb = tq // rb

    def query_block(qi, _):
        rows = [pl.ds(pl.multiple_of(qi * tq + r * rb, rb), rb) for r in range(nrb)]
        q_blocks, fq_blocks = [], []
        for r in range(nrb):
            q = q_ref[rows[r], :].astype(F32)
            ones_at = lambda l0: jnp.where((lane >= l0) & (lane < l0 + 3), 1.0, 0.0)
            q_blocks.append((jnp.where(lo_lane, q, ones_at(spare[0])).astype(BF16),
                             jnp.where(lo_lane, ones_at(spare[1]), q).astype(BF16)))
            f_rows = f_scr[rows[r], :]
            fq_blocks.append([jnp.sum(jnp.where(lane == F_LANE0 + h, f_rows, 0.0), axis=1, keepdims=True) * LOG2E
                              for h in heads])
        for g0 in range(0, nrb, ATTN_GROUP):
            rs = tuple(range(g0, min(g0 + ATTN_GROUP, nrb)))
            carry = lax.fori_loop(0, qi, lambda j, cr: step(j, cr, rs, q_blocks, fq_blocks, False),
                                  (init_rows,) * len(rs))
            carry = step(qi, carry, rs, q_blocks, fq_blocks, True)
            for i, r in enumerate(rs):
                acc = [carry[i][2 * hl + 1] for hl in range(HEAD_PAIR)]
                out = [a / a[:, spare[hl]:spare[hl] + 1] for hl, a in enumerate(acc)]
                o_ref[rows[r], :] = jnp.where(lo_lane, out[0], out[1]).astype(BF16)
        return 0

    def softmax_unit(t, f_q, m_old, row0):
        if row0 is not None:
            rr = lax.broadcasted_iota(jnp.int32, t.shape, 0) + row0
            cc = lax.broadcasted_iota(jnp.int32, t.shape, 1)
            t = jnp.where(cc <= rr, t, NEG_BIG)
        m_new = jnp.maximum(m_old, jnp.max(t, axis=1, keepdims=True) + f_q)
        return m_new, jnp.exp2(m_old - m_new), jnp.exp2(t - (m_new - f_q)).astype(BF16)

    def step(j, carry, rs, q_blocks, fq_blocks, diagonal):
        nks = {r: (r + 1) * rb if diagonal else tq for r in rs}
        units = [(i, r, hl) for i, r in enumerate(rs) for hl in range(HEAD_PAIR)]
        dots = {(r, hl): _dot(q_blocks[r][hl], k_scr[hl, j, :, 0:nks[r]]) for _, r, hl in units}
        soft = {(r, hl): softmax_unit(dots[(r, hl)], fq_blocks[r][hl], carry[i][2 * hl], r * rb if diagonal else None)
                for i, r, hl in units}
        pvs = {(r, hl): _dot_nt(soft[(r, hl)][2], v_scr[hl, j, :, 0:nks[r]]) for _, r, hl in units}
        new = []
        for i, r in enumerate(rs):
            row_state = ()
            for hl in range(HEAD_PAIR):
                m_new, alpha, _ = soft[(r, hl)]
                row_state += (m_new, alpha * carry[i][2 * hl + 1] + pvs[(r, hl)])
            new.append(row_state)
        return tuple(new)

    init_rows = (jnp.full((rb, 1), NEG_BIG, F32), jnp.zeros((rb, LANES), F32)) * HEAD_PAIR
    lax.fori_loop(0, s // tq, query_block, 0)


def _fox_prompt_call(q, kt, vt, small_t, layer=None):
    bsz, s, _ = q.shape
    tq = min(ATTN_BLOCK, s)
    blk = pl.BlockSpec((None, s, LANES), lambda b, p: (b, 0, p))
    if layer is None:
        full = pl.BlockSpec((None, LANES, s), lambda b, p: (b, p, 0))
    else:
        full = pl.BlockSpec((None, None, LANES, s), lambda b, p: (layer, b, p, 0))
    return pl.pallas_call(
        _fox_prompt_kernel,
        grid=(bsz, FOX_WIDTH // LANES),
        in_specs=[blk, full, full, pl.BlockSpec((None, LANES, s), lambda b, p: (b, 0, 0))],
        out_specs=blk,
        out_shape=jax.ShapeDtypeStruct((bsz, s, FOX_WIDTH), BF16),
        scratch_shapes=[pltpu.VMEM((HEAD_PAIR, s // tq, LANES, tq), BF16),
                        pltpu.VMEM((HEAD_PAIR, s // tq, LANES, tq), BF16),
                        pltpu.VMEM((s, LANES), F32), pltpu.VMEM((s // tq, LANES, tq), F32)],
        compiler_params=_params(40, ("arbitrary", "arbitrary")),
        name="fox_prompt",
    )(q, kt, vt, small_t)


def _fox_sample_kernel(q_ref, k_ref, v_ref, logf_ref, ck_ref, cv_ref, clogft_ref, o_ref,
                       m_scr, l_scr, acc_scr, fk_scr, fq_scr, fnew_scr):
    t = q_ref.shape[0]
    c = pl.program_id(1)
    nchunk, _, chunk = fk_scr.shape

    @pl.when(c == 0)
    def _():
        blk = min(SCAN_BLOCK, chunk)
        per = chunk // blk
        nblk = nchunk * per
        stacked = jnp.concatenate([clogft_ref[:, i * blk:(i + 1) * blk] for i in range(nblk)], axis=0)
        local = _cumsum_lanes(stacked, _tri(blk, False))
        nrow = nblk * FOX_HEADS
        rr = lax.broadcasted_iota(jnp.int32, (nrow, nrow), 0)
        cc = lax.broadcasted_iota(jnp.int32, (nrow, nrow), 1)
        earlier = ((rr % FOX_HEADS == cc % FOX_HEADS) & (cc < rr - rr % FOX_HEADS)).astype(F32).astype(BF16)
        totals = jnp.broadcast_to(local[:, blk - 1:blk], (nrow, LANES))
        f_all = local + _cumsum_rows(totals, earlier)[:, 0:1]
        for i in range(nblk):
            fk_scr[i // per, :, (i % per) * blk:(i % per + 1) * blk] = f_all[i * FOX_HEADS:(i + 1) * FOX_HEADS, :] * LOG2E
        carry = f_all[nrow - FOX_HEADS:, blk - 1:blk]
        sub = lax.broadcasted_iota(jnp.int32, (FOX_HEADS, LANES), 0)
        ln = lax.broadcasted_iota(jnp.int32, (FOX_HEADS, LANES), 1)
        total_row = jnp.sum(jnp.where(sub == ln - F_LANE0, carry, 0.0), axis=0, keepdims=True)
        f_new = _cumsum_rows(logf_ref[...], _tri(t, True)) + total_row
        f_new = f_new * LOG2E
        fnew_scr[...] = f_new.T[F_LANE0:F_LANE0 + FOX_HEADS, :]
        fq_scr[...] = jnp.concatenate([f_new[:, F_LANE0 + h:F_LANE0 + h + 1] for h in range(FOX_HEADS)], axis=0)
        m_scr[...] = jnp.full(m_scr.shape, NEG_BIG, F32)
        l_scr[...] = jnp.zeros(l_scr.shape, F32)
        acc_scr[...] = jnp.zeros(acc_scr.shape, F32)

    q = q_ref[...]
    q_heads = [q[:, h * FOX_HEAD_DIM:(h + 1) * FOX_HEAD_DIM] for h in range(FOX_HEADS)]
    f_q = fq_scr[...]

    def update(keys, values, f_k, mask, channels_first):
        qk = _dot if channels_first else _dot_nt
        pv_dot = _dot_nt if channels_first else _dot
        tt = jnp.concatenate([qk(q_heads[h], keys[h]) - f_k[h:h + 1, :] for h in range(FOX_HEADS)], axis=0)
        if mask is not None:
            tt = jnp.where(mask, tt, NEG_BIG)
        m_old = m_scr[...]
        m_new = jnp.maximum(m_old, jnp.max(tt, axis=1, keepdims=True) + f_q)
        alpha = jnp.exp2(m_old - m_new)
        pr = jnp.exp2(tt - (m_new - f_q))
        l_scr[...] = alpha * l_scr[...] + jnp.sum(pr, axis=1, keepdims=True)
        prb = pr.astype(BF16)
        pv = jnp.concatenate([pv_dot(prb[h * t:(h + 1) * t, :], values[h]) for h in range(FOX_HEADS)], axis=0)
        acc_scr[...] = alpha * acc_scr[...] + pv
        m_scr[...] = m_new

    head_rows = lambda ref: [ref[h * FOX_HEAD_DIM:(h + 1) * FOX_HEAD_DIM, :].astype(BF16) for h in range(FOX_HEADS)]
    update(head_rows(ck_ref), head_rows(cv_ref), fk_scr[c], None, True)

    @pl.when(c == nchunk - 1)
    def _():
        k_new, v_new = k_ref[...].astype(BF16), v_ref[...].astype(BF16)
        cols = lambda a: [a[:, h * FOX_HEAD_DIM:(h + 1) * FOX_HEAD_DIM] for h in range(FOX_HEADS)]
        rr = lax.broadcasted_iota(jnp.int32, (FOX_HEADS * t, t), 0) % t
        cc = lax.broadcasted_iota(jnp.int32, (FOX_HEADS * t, t), 1)
        update(cols(k_new), cols(v_new), fnew_scr[...], cc <= rr, False)
        out = acc_scr[...] / l_scr[...]
        o_ref[...] = jnp.concatenate([out[h * t:(h + 1) * t, :] for h in range(FOX_HEADS)], axis=1).astype(BF16)


def _fox_sample_call(q, k, v, small, cache_k, cache_v, cache_logf_t, layer):
    bsz, t, _ = q.shape
    past = cache_logf_t.shape[3]
    chunk = min(SAMPLE_CHUNK, past)
    nchunk = past // chunk
    new = lambda w: pl.BlockSpec((None, t, w), lambda b, c: (b, 0, 0))
    cache = pl.BlockSpec((None, None, FOX_WIDTH, chunk), lambda b, c: (layer, b, 0, c))
    rows = FOX_HEADS * t
    return pl.pallas_call(
        _fox_sample_kernel,
        grid=(bsz, nchunk),
        in_specs=[new(FOX_WIDTH), new(FOX_WIDTH), new(FOX_WIDTH), new(LANES), cache, cache,
                  pl.BlockSpec((None, None, FOX_HEADS, past), lambda b, c: (layer, b, 0, 0))],
        out_specs=new(FOX_WIDTH),
        out_shape=jax.ShapeDtypeStruct((bsz, t, FOX_WIDTH), BF16),
        scratch_shapes=[pltpu.VMEM((rows, 1), F32), pltpu.VMEM((rows, 1), F32), pltpu.VMEM((rows, FOX_HEAD_DIM), F32),
                        pltpu.VMEM((nchunk, FOX_HEADS, chunk), F32), pltpu.VMEM((rows, 1), F32),
                        pltpu.VMEM((FOX_HEADS, t), F32)],
        compiler_params=_params(40, ("arbitrary", "arbitrary")),
        name="fox_sample",
    )(q, k, v, small, cache_k, cache_v, cache_logf_t)


def _small_rows(w_in_t):
    offs = np.concatenate([[0], np.cumsum(IN_SIZES)])
    seg = lambda i: w_in_t[:, int(offs[i]):int(offs[i + 1]), :]
    pad = jnp.zeros((w_in_t.shape[0], LANES - FOX_HEADS - SSD_HEADS, w_in_t.shape[2]), w_in_t.dtype)
    return jnp.concatenate([seg(5), seg(8), pad], axis=1)


def _block_diag(w):
    d, h, b, _ = w.shape
    eye = jnp.eye(h, dtype=w.dtype)
    return jnp.einsum("dhij,hg->dhigj", w, eye).reshape(d, h * b, h * b)


def _lane_slab(v, lane0):
    d, k = v.shape
    return jnp.zeros((d, 1, LANES), v.dtype).at[:, 0, lane0:lane0 + k].set(v)


def _pad_history(state):
    return jnp.pad(state, ((0, 0), (0, 0), (SUBLANES - (CONV_W - 1), 0), (0, 0)))


def _trunk(x, mod_group, caches, prm, ssd_chunk):
    bsz, n, _ = x.shape
    x = x.reshape(bsz * n, D_MODEL)
    states = {name: [] for name in ("fox_k", "fox_v", "fox_logf", "lru_conv", "lru_h", "ssd_conv", "ssd_h")}
    prev_kv = None
    for l in range(DEPTH):
        mod4 = mod_group[l].reshape(bsz, N_SUB, 3, D_MODEL)
        x = _ffn_call(x, mod4, prm["npre"], prm["npost"], prm["wg"], prm["wu"], prm["wd"], l, 0, 0, n)
        prompt = caches is None
        stacking = prompt and l == DEPTH - 1
        proj = _inproj_call(x, mod4, prm["npre"], prm["w_in_t"], prm["w_small"], prm["fbias"], l, n,
                            transposed=prompt, prev_kv=prev_kv if stacking else None)
        lrux, lrug, q, k, v, z, xbc, small = proj[:8]
        per_seq = lambda a: a.reshape(bsz, n, a.shape[-1])
        state_layer = None if prompt else l
        src = prm["zero_state"] if prompt else caches
        ya, lru_conv, lru_h = _lru_call(per_seq(lrux), per_seq(lrug), src["lru_conv"], src["lru_h"],
                                        prm["lru_cw"], prm["lru_cb"], prm["lru_wgate"], prm["lru_bgate"],
                                        prm["lru_lam"], l, state_layer)
        if prompt:
            small_t = proj[8]
            yb = _fox_prompt_call(per_seq(q), k, v, small_t, l if stacking else None)
            prev_kv = (k, v)
            logf_out = jnp.swapaxes(small_t[:, F_LANE0:F_LANE0 + FOX_HEADS, :], 1, 2)
        else:
            yb = _fox_sample_call(per_seq(q), per_seq(k), per_seq(v), per_seq(small),
                                  caches["fox_k"], caches["fox_v"], caches["fox_logf_t"], l)
            k_out = k.reshape(bsz, n, FOX_HEADS, FOX_HEAD_DIM)
            v_out = v.reshape(bsz, n, FOX_HEADS, FOX_HEAD_DIM)
            logf_out = small.reshape(bsz, n, LANES)[:, :, F_LANE0:F_LANE0 + FOX_HEADS]
        yc, ssd_conv, ssd_h = _ssd_call(per_seq(xbc), per_seq(z), per_seq(small), src["ssd_conv"], src["ssd_h"],
                                        prm["ssd_cw"], prm["ssd_cb"], prm["ssd_dtb"], prm["ssd_alog"],
                                        prm["ssd_dvec"], prm["ssd_nw"], l, state_layer, ssd_chunk)
        x = _outffn_call(x, ya.reshape(bsz * n, -1), yb.reshape(bsz * n, -1), yc.reshape(bsz * n, -1),
                         mod4, prm["npre"], prm["npost"], prm["w_out"], prm["wg"], prm["wu"], prm["wd"], l, n)
        if not prompt:
            states["fox_k"].append(k_out)
            states["fox_v"].append(v_out)
        states["fox_logf"].append(logf_out)
        states["lru_conv"].append(lru_conv)
        states["lru_h"].append(lru_h.reshape(bsz, LRU_WIDTH))
        states["ssd_conv"].append(ssd_conv)
        states["ssd_h"].append(ssd_h.reshape(bsz, SSD_HEADS, SSD_HEAD_DIM, D_STATE))
    out = {name: jnp.stack(vals, axis=0) for name, vals in states.items() if vals}
    if prev_kv is not None:
        heads_last = lambda a: jnp.transpose(a.reshape(DEPTH, bsz, FOX_HEADS, FOX_HEAD_DIM, n), (0, 1, 4, 2, 3))
        out["fox_k"], out["fox_v"] = heads_last(prev_kv[0]), heads_last(prev_kv[1])
    return x.reshape(bsz, n, D_MODEL), out


def kernel(x_prompt, x_sample, c_prompt, c_sample, cache_fox_k, cache_fox_v, cache_fox_logf, state_lru_conv, state_lru_h, state_ssd_conv, state_ssd_h, w_mod, b_mod, norm_pre, norm_post, ffn_w_gate, ffn_w_up, ffn_w_down, w_in, w_out, lru_conv_w, lru_conv_b, lru_wa, lru_ba, lru_wx, lru_bx, lru_lambda, fox_f_bias, ssd_conv_w, ssd_conv_b, ssd_dt_bias, ssd_a_log, ssd_d, ssd_norm_w):
    n_prompt, n_sample = x_prompt.shape[0], x_sample.shape[0]
    w_in_t = jnp.swapaxes(w_in, 1, 2)
    prm = {
        "npre": norm_pre.reshape(DEPTH, N_SUB, 1, D_MODEL),
        "npost": norm_post.reshape(DEPTH, N_SUB, 1, D_MODEL),
        "wg": ffn_w_gate.astype(BF16), "wu": ffn_w_up.astype(BF16), "wd": ffn_w_down.astype(BF16),
        "w_in_t": w_in_t, "w_small": _small_rows(w_in_t), "w_out": w_out.astype(BF16),
        "fbias": _lane_slab(fox_f_bias, F_LANE0),
        "lru_cw": lru_conv_w, "lru_cb": lru_conv_b.reshape(DEPTH, 1, LRU_WIDTH),
        "lru_wgate": jnp.concatenate([_block_diag(lru_wa), _block_diag(lru_wx)], axis=-1).astype(BF16),
        "lru_bgate": jnp.concatenate([lru_ba, lru_bx], axis=-1).reshape(DEPTH, 1, 2 * LRU_WIDTH),
        "lru_lam": lru_lambda.reshape(DEPTH, 1, LRU_WIDTH),
        "ssd_cw": ssd_conv_w, "ssd_cb": ssd_conv_b.reshape(DEPTH, 1, SSD_CONV_DIM),
        "ssd_dtb": _lane_slab(ssd_dt_bias, DT_LANE0), "ssd_alog": _lane_slab(ssd_a_log, DT_LANE0),
        "ssd_dvec": jnp.repeat(ssd_d, SSD_HEAD_DIM, axis=-1).reshape(DEPTH, 1, SSD_WIDTH),
        "ssd_nw": ssd_norm_w.reshape(DEPTH, 1, SSD_WIDTH),
        "zero_state": {
            "lru_conv": jnp.zeros((1, SUBLANES, LRU_WIDTH), F32), "lru_h": jnp.zeros((1, 1, LRU_WIDTH), F32),
            "ssd_conv": jnp.zeros((1, SUBLANES, SSD_CONV_DIM), F32), "ssd_h": jnp.zeros((1, SSD_WIDTH, D_STATE), F32),
        },
    }
    caches = {
        "fox_k": jnp.transpose(cache_fox_k, (0, 1, 3, 4, 2)).reshape(DEPTH, n_sample, FOX_WIDTH, -1),
        "fox_v": jnp.transpose(cache_fox_v, (0, 1, 3, 4, 2)).reshape(DEPTH, n_sample, FOX_WIDTH, -1),
        "fox_logf_t": jnp.swapaxes(cache_fox_logf, 2, 3),
        "lru_conv": _pad_history(state_lru_conv),
        "lru_h": state_lru_h.reshape(DEPTH, n_sample, 1, LRU_WIDTH),
        "ssd_conv": _pad_history(state_ssd_conv),
        "ssd_h": state_ssd_h.reshape(DEPTH, n_sample, SSD_WIDTH, D_STATE),
    }
    mod = _mod_call(jnp.concatenate([c_prompt, c_sample], axis=0), w_mod, b_mod)
    y_prompt, sp = _trunk(x_prompt, mod[:, :n_prompt], None, prm, ssd_chunk=256)
    y_sample, ss = _trunk(x_sample, mod[:, n_prompt:], caches, prm, ssd_chunk=x_sample.shape[1])
    names = ("fox_k", "fox_v", "fox_logf", "lru_conv", "lru_h", "ssd_conv", "ssd_h")
    return (y_prompt, y_sample) + tuple(sp[n] for n in names) + tuple(ss[n] for n in names)
```

```python
import functools

import numpy as np
import jax
import jax.numpy as jnp
from jax import lax
from jax.experimental import pallas as pl
from jax.experimental.pallas import tpu as pltpu

F32 = jnp.float32
BF16 = jnp.bfloat16

D_MODEL = 1024
DEPTH = 2
CONV_W = 4
EPS = 1e-6
LRU_WIDTH = 256
LRU_HEADS = 4
LRU_BLOCK = LRU_WIDTH // LRU_HEADS
LRU_C = 8.0
FOX_HEADS = 8
FOX_HEAD_DIM = 64
FOX_WIDTH = FOX_HEADS * FOX_HEAD_DIM
SSD_HEADS = 4
SSD_HEAD_DIM = 64
SSD_WIDTH = SSD_HEADS * SSD_HEAD_DIM
SSD_GROUPS = 2
D_STATE = 128
SSD_CONV_DIM = SSD_WIDTH + 2 * SSD_GROUPS * D_STATE
IN_SIZES = (LRU_WIDTH, LRU_WIDTH, FOX_WIDTH, FOX_WIDTH, FOX_WIDTH, FOX_HEADS, SSD_WIDTH, SSD_CONV_DIM, SSD_HEADS)
D_FF = 2816
N_SUB = 3

LANES = 128
SUBLANES = 8
HEAD_PAIR = LANES // FOX_HEAD_DIM

F_LANE0 = 0
DT_LANE0 = FOX_HEADS
COL_LRU_X = 0
COL_LRU_G = COL_LRU_X + LRU_WIDTH
COL_Q = COL_LRU_G + LRU_WIDTH
COL_K = COL_Q + FOX_WIDTH
COL_V = COL_K + FOX_WIDTH
COL_Z = COL_V + FOX_WIDTH
COL_XBC = COL_Z + SSD_WIDTH
COL_SMALL = COL_XBC + SSD_CONV_DIM
D_IN_PAD = COL_SMALL + LANES
D_IN = sum(IN_SIZES)

ROW_TILE = 512
FFN_ROW_TILE = 1024
FF_CHUNK = 512
ATTN_BLOCK = 1024
ATTN_ROWS = 128
ATTN_GROUP = 8
SCAN_BLOCK = 256
SAMPLE_CHUNK = 2048
NEG_BIG = -1e30
LOG2E = 1.4426950408889634
Q_SCALE = LOG2E * FOX_HEAD_DIM ** -0.5


def _dot(a, b):
    return jnp.dot(a, b, preferred_element_type=F32)


def _dot_nt(a, b):
    return lax.dot_general(a, b, (((1,), (1,)), ((), ())), preferred_element_type=F32)


def _dot_tn(a, b):
    return lax.dot_general(a, b, (((0,), (0,)), ((), ())), preferred_element_type=F32)


def _silu(x):
    return x * jax.nn.sigmoid(x)


def _softplus(x):
    return jnp.maximum(x, 0.0) + jnp.log1p(jnp.exp(-jnp.abs(x)))


def _rms(x):
    return x * lax.rsqrt(jnp.mean(x * x, axis=-1, keepdims=True) + EPS)


def _per_seq(rows, per_seq, fn):
    g = per_seq[0].shape[0]
    if g == 1:
        return fn(rows, *per_seq)
    tm, d = rows.shape
    out = fn(rows.reshape(g, tm // g, d), *[p[:, None, :] for p in per_seq])
    return out.reshape(tm, d)


def _pre_norm(x, npre, mod_ref, seqs=slice(None)):
    h = _rms(x) * npre
    return _per_seq(h, (mod_ref[seqs, 1, :], mod_ref[seqs, 0, :]), lambda r, sc, sh: r * (1.0 + sc) + sh)


def _post_norm(x, y, npost, mod_ref, w, seqs=slice(None)):
    yn = _rms(y) * npost
    return x + _per_seq(yn, (mod_ref[seqs, 2, :],), lambda r, gt: (w * gt) * r)


def _seq_grouping(rows_per_seq, tm):
    if rows_per_seq % tm == 0:
        return 1, rows_per_seq // tm
    assert tm % rows_per_seq == 0
    return tm // rows_per_seq, 1


def _mod_spec(g, tiles_per_seq, sub):
    if g == 1:
        return pl.BlockSpec((1, None, 3, D_MODEL), lambda i: (i // tiles_per_seq, sub, 0, 0))
    return pl.BlockSpec((g, None, 3, D_MODEL), lambda i: (i, sub, 0, 0))


def _params(vmem_mb, sem):
    return pltpu.CompilerParams(dimension_semantics=sem, vmem_limit_bytes=vmem_mb << 20)


def _mod_kernel(c_ref, w_ref, b_ref, o_ref):
    a = _silu(c_ref[...]).astype(BF16)
    o_ref[...] = _dot(a, w_ref[...].astype(BF16)) + b_ref[...]


def _mod_call(c_all, w_mod, b_mod):
    nseq = c_all.shape[0]
    width = N_SUB * 3 * D_MODEL
    tn = 1024
    return pl.pallas_call(
        _mod_kernel,
        grid=(DEPTH, width // tn),
        in_specs=[
            pl.BlockSpec((nseq, D_MODEL), lambda l, n: (0, 0)),
            pl.BlockSpec((None, D_MODEL, tn), lambda l, n: (l, 0, n)),
            pl.BlockSpec((None, 1, tn), lambda l, n: (l, 0, n)),
        ],
        out_specs=pl.BlockSpec((None, nseq, tn), lambda l, n: (l, 0, n)),
        out_shape=jax.ShapeDtypeStruct((DEPTH, nseq, width), F32),
        compiler_params=_params(24, ("arbitrary", "arbitrary")),
        name="adaln_mod",
    )(c_all, w_mod, b_mod.reshape(DEPTH, 1, width))


def _halves(tm, nseq):
    half = tm // 2
    rows = [slice(i * half, (i + 1) * half) for i in range(2)]
    seqs = [slice(0, 1)] * 2 if nseq == 1 else [slice(i * nseq // 2, (i + 1) * nseq // 2) for i in range(2)]
    return rows, seqs


def _swiglu(hs, wg_ref, wu_ref, wd_ref):
    accs = [None] * len(hs)
    for off in range(0, D_FF, FF_CHUNK):
        fc = min(FF_CHUNK, D_FF - off)
        for i, h in enumerate(hs):
            g = _dot(h, wg_ref[:, off:off + fc])
            u = _dot(h, wu_ref[:, off:off + fc])
            a = (_silu(g) * u).astype(BF16)
            d = _dot(a, wd_ref[off:off + fc, :])
            accs[i] = d if accs[i] is None else accs[i] + d
    return accs


def _ffn_kernel(x_ref, mod_ref, npre_ref, npost_ref, wg_ref, wu_ref, wd_ref, o_ref):
    rows, seqs = _halves(x_ref.shape[0], mod_ref.shape[0])
    xs = [x_ref[r, :] for r in rows]
    hs = [_pre_norm(x, npre_ref[...], mod_ref, sq).astype(BF16) for x, sq in zip(xs, seqs)]
    accs = _swiglu(hs, wg_ref, wu_ref, wd_ref)
    for i, r in enumerate(rows):
        o_ref[r, :] = _post_norm(xs[i], accs[i], npost_ref[...], mod_ref, 0.5, seqs[i])


def _outffn_kernel(x_ref, ya_ref, yb_ref, yc_ref, mod1_ref, mod2_ref, npost1_ref, npre2_ref, npost2_ref,
                   wo_ref, wg_ref, wu_ref, wd_ref, o_ref):
    rows, seqs = _halves(x_ref.shape[0], mod1_ref.shape[0])
    xs = []
    for r, sq in zip(rows, seqs):
        y = (_dot(ya_ref[r, :], wo_ref[0:LRU_WIDTH, :])
             + _dot(yb_ref[r, :], wo_ref[LRU_WIDTH:LRU_WIDTH + FOX_WIDTH, :])
             + _dot(yc_ref[r, :], wo_ref[LRU_WIDTH + FOX_WIDTH:, :]))
        xs.append(_post_norm(x_ref[r, :], y, npost1_ref[...], mod1_ref, 1.0, sq))
    hs = [_pre_norm(x, npre2_ref[...], mod2_ref, sq).astype(BF16) for x, sq in zip(xs, seqs)]
    accs = _swiglu(hs, wg_ref, wu_ref, wd_ref)
    for i, r in enumerate(rows):
        o_ref[r, :] = _post_norm(xs[i], accs[i], npost2_ref[...], mod2_ref, 0.5, seqs[i])


def _outffn_call(x, ya, yb, yc, mod4, npre, npost, w_out, wg, wu, wd, layer, rows_per_seq):
    m = x.shape[0]
    tm = min(FFN_ROW_TILE, m)
    g, tps = _seq_grouping(rows_per_seq, tm)
    row = lambda w: pl.BlockSpec((tm, w), lambda i: (i, 0))
    once = dict(pipeline_mode=pl.Buffered(1))
    wspec = lambda shape: pl.BlockSpec((None, None) + shape, lambda i: (layer, 1, 0, 0), **once)
    norm = lambda sub: pl.BlockSpec((None, None, 1, D_MODEL), lambda i: (layer, sub, 0, 0))
    return pl.pallas_call(
        _outffn_kernel,
        grid=(m // tm,),
        in_specs=[
            row(D_MODEL), row(LRU_WIDTH), row(FOX_WIDTH), row(SSD_WIDTH),
            _mod_spec(g, tps, 1), _mod_spec(g, tps, 2),
            norm(1), norm(2), norm(2),
            pl.BlockSpec((None, D_MODEL, D_MODEL), lambda i: (layer, 0, 0), **once),
            wspec((D_MODEL, D_FF)), wspec((D_MODEL, D_FF)), wspec((D_FF, D_MODEL)),
        ],
        out_specs=row(D_MODEL),
        out_shape=jax.ShapeDtypeStruct((m, D_MODEL), F32),
        compiler_params=_params(57, ("arbitrary",)),
        name="outproj_ffn",
    )(x, ya, yb, yc, mod4, mod4, npost, npre, npost, w_out, wg, wu, wd)


def _ffn_call(x, mod4, npre, npost, wg, wu, wd, layer, sub, ffn_idx, rows_per_seq):
    m = x.shape[0]
    tm = min(FFN_ROW_TILE, m)
    g, tps = _seq_grouping(rows_per_seq, tm)
    wspec = lambda shape: pl.BlockSpec((None, None) + shape, lambda i: (layer, ffn_idx, 0, 0),
                                       pipeline_mode=pl.Buffered(1))
    nspec = pl.BlockSpec((None, None, 1, D_MODEL), lambda i: (layer, sub, 0, 0))
    return pl.pallas_call(
        _ffn_kernel,
        grid=(m // tm,),
        in_specs=[
            pl.BlockSpec((tm, D_MODEL), lambda i: (i, 0)),
            _mod_spec(g, tps, sub),
            nspec, nspec,
            wspec((D_MODEL, D_FF)), wspec((D_MODEL, D_FF)), wspec((D_FF, D_MODEL)),
        ],
        out_specs=pl.BlockSpec((tm, D_MODEL), lambda i: (i, 0)),
        out_shape=jax.ShapeDtypeStruct((m, D_MODEL), F32),
        compiler_params=_params(56, ("arbitrary",)),
        name="ffn",
    )(x, mod4, npre, npost, wg, wu, wd)


def _inproj_kernel(*refs, transposed, stacked):
    n_in = 8 if stacked else 6
    x_ref, mod_ref, npre_ref, w_ref, wsmall_ref, fbias_ref = refs[:6]
    lrux_ref, lrug_ref, q_ref, k_ref, v_ref, z_ref, xbc_ref, small_ref, *rest = refs[n_in:]
    wt_ref = rest[-1]

    @pl.when(pl.program_id(0) == 0)
    def _():
        src = int(np.cumsum(IN_SIZES)[4])
        assert src == COL_Z
        moves = [(0, 0, src), (src + FOX_HEADS, COL_Z, SSD_WIDTH), (src + FOX_HEADS + SSD_WIDTH, COL_XBC, SSD_CONV_DIM)]
        for s0, d0, n in moves:
            for off in range(0, n, LRU_WIDTH):
                wt_ref[d0 + off:d0 + off + LRU_WIDTH, :] = w_ref[s0 + off:s0 + off + LRU_WIDTH, :].astype(BF16)
        wt_ref[COL_SMALL:COL_SMALL + LANES, :] = wsmall_ref[...].astype(BF16)

    h = _pre_norm(x_ref[...], npre_ref[...], mod_ref).astype(BF16)
    col = lambda start, width: _dot_nt(h, wt_ref[start:start + width, :])
    col_t = lambda start, width: _dot_nt(wt_ref[start:start + width, :], h)
    lrux_ref[...] = col(COL_LRU_X, LRU_WIDTH)
    lrug_ref[...] = col(COL_LRU_G, LRU_WIDTH)
    q_ref[...] = (col(COL_Q, FOX_WIDTH) * Q_SCALE).astype(BF16)
    if transposed:
        kv_t = col_t(COL_K, 2 * FOX_WIDTH)
        if stacked:
            k_ref[0], v_ref[0] = refs[6][...], refs[7][...]
            k_ref[1], v_ref[1] = kv_t[:FOX_WIDTH], kv_t[FOX_WIDTH:]
        else:
            k_ref[...] = kv_t[:FOX_WIDTH]
            v_ref[...] = kv_t[FOX_WIDTH:]
    else:
        k_ref[...] = col(COL_K, FOX_WIDTH)
        v_ref[...] = col(COL_V, FOX_WIDTH)
    z_ref[...] = col(COL_Z, SSD_WIDTH)
    xbc_ref[...] = col(COL_XBC, SSD_CONV_DIM)
    small = col(COL_SMALL, LANES)
    t = small + fbias_ref[...]
    logf = jnp.minimum(t, 0.0) - jnp.log1p(jnp.exp(-jnp.abs(t)))
    lane = lax.broadcasted_iota(jnp.int32, small.shape, 1)
    small = jnp.where(lane < DT_LANE0, logf, small)
    small_ref[...] = small
    if transposed:
        rest[0][...] = small.T


def _inproj_call(x, mod4, npre, w_in_t, w_small, fbias, layer, rows_per_seq, transposed, prev_kv=None):
    m = x.shape[0]
    tm = min(ROW_TILE, m)
    g, tps = _seq_grouping(rows_per_seq, tm)
    widths = (LRU_WIDTH, LRU_WIDTH, FOX_WIDTH, FOX_WIDTH, FOX_WIDTH, SSD_WIDTH, SSD_CONV_DIM, LANES)
    dtypes = (F32, F32, BF16, F32, F32, F32, F32, F32)
    out_specs = [pl.BlockSpec((tm, w), lambda i: (i, 0)) for w in widths]
    out_shape = [jax.ShapeDtypeStruct((m, w), dt) for w, dt in zip(widths, dtypes)]
    in_specs = [
        pl.BlockSpec((tm, D_MODEL), lambda i: (i, 0)),
        _mod_spec(g, tps, 1),
        pl.BlockSpec((None, None, 1, D_MODEL), lambda i: (layer, 1, 0, 0)),
        pl.BlockSpec((None, D_IN, D_MODEL), lambda i: (layer, 0, 0), pipeline_mode=pl.Buffered(1)),
        pl.BlockSpec((None, LANES, D_MODEL), lambda i: (layer, 0, 0)),
        pl.BlockSpec((None, 1, LANES), lambda i: (layer, 0, 0)),
    ]
    if transposed:
        assert g == 1
        nseq = m // rows_per_seq
        for idx, w in ((3, FOX_WIDTH), (4, FOX_WIDTH), (len(widths), LANES)):
            spec = pl.BlockSpec((None, w, tm), lambda i: (i // tps, 0, i % tps))
            shape = jax.ShapeDtypeStruct((nseq, w, rows_per_seq), F32)
            out_specs[idx:idx + 1], out_shape[idx:idx + 1] = [spec], [shape]
    extra = ()
    if prev_kv is not None:
        assert transposed and layer == 1 and DEPTH == 2
        extra = tuple(prev_kv)
        in_specs += [pl.BlockSpec((None, FOX_WIDTH, tm), lambda i: (i // tps, 0, i % tps))] * 2
        for idx in (3, 4):
            out_specs[idx] = pl.BlockSpec((DEPTH, None, FOX_WIDTH, tm), lambda i: (0, i // tps, 0, i % tps))
            out_shape[idx] = jax.ShapeDtypeStruct((DEPTH, nseq, FOX_WIDTH, rows_per_seq), F32)
    return pl.pallas_call(
        functools.partial(_inproj_kernel, transposed=transposed, stacked=prev_kv is not None),
        grid=(m // tm,),
        in_specs=in_specs,
        out_specs=out_specs,
        out_shape=out_shape,
        scratch_shapes=[pltpu.VMEM((D_IN_PAD, D_MODEL), BF16)],
        compiler_params=_params(56, ("arbitrary",)),
        name="inproj",
    )(x, mod4, npre, w_in_t, w_small, fbias, *extra)


def _causal_conv(xx, cw, cb, n):
    u = cb + cw[0:1] * pltpu.roll(xx, 3, 0)[SUBLANES:SUBLANES + n]
    u = u + cw[1:2] * pltpu.roll(xx, 2, 0)[SUBLANES:SUBLANES + n]
    u = u + cw[2:3] * pltpu.roll(xx, 1, 0)[SUBLANES:SUBLANES + n]
    return u + cw[3:4] * xx[SUBLANES:SUBLANES + n]


def _last_rows(xx, k):
    return pltpu.roll(xx, k, 0)[0:SUBLANES][0:k]


def _tri(n, lower):
    r = lax.broadcasted_iota(jnp.int32, (n, n), 0)
    c = lax.broadcasted_iota(jnp.int32, (n, n), 1)
    return ((r >= c) if lower else (r <= c)).astype(F32).astype(BF16)


def _split3(x):
    hi = x.astype(BF16)
    r1 = x - hi.astype(F32)
    mid = r1.astype(BF16)
    lo = (r1 - mid.astype(F32)).astype(BF16)
    return hi, mid, lo


def _cumsum_lanes(x, triu):
    hi, mid, lo = _split3(x)
    return _dot(hi, triu) + _dot(mid, triu) + _dot(lo, triu)


def _cumsum_rows(x, tril):
    hi, mid, lo = _split3(x)
    return _dot(tril, hi) + _dot(tril, mid) + _dot(tril, lo)


def _lru_kernel(x_ref, g_ref, prev_ref, h0_ref, cw_ref, cb_ref, wgate_ref, bgate_ref, lam_ref,
                y_ref, convnew_ref, hnew_ref, a_scr, b_scr):
    n = x_ref.shape[0]
    xx = jnp.concatenate([prev_ref[...], x_ref[...]], axis=0)
    convnew_ref[...] = _last_rows(xx, CONV_W - 1)
    u = _causal_conv(xx, cw_ref[...], cb_ref[...], n)
    gates = _dot(u.astype(BF16), wgate_ref[...]) + bgate_ref[...]
    r = jax.nn.sigmoid(gates[:, :LRU_WIDTH])
    i = jax.nn.sigmoid(gates[:, LRU_WIDTH:])
    log_a = (-LRU_C * r) * _softplus(-lam_ref[...])
    a = jnp.exp(log_a)
    b = jnp.sqrt(-jnp.tanh(log_a) * (a * a + 1.0)) * (i * u)
    a = a.reshape(n // SUBLANES, SUBLANES, LRU_WIDTH)
    b = b.reshape(n // SUBLANES, SUBLANES, LRU_WIDTH)
    row = lax.broadcasted_iota(jnp.int32, (1, SUBLANES, 1), 1)
    for d in (1, 2, 4):
        keep = row >= d
        b = jnp.where(keep, a * pltpu.roll(b, d, 1) + b, b)
        a = jnp.where(keep, a * pltpu.roll(a, d, 1), a)
    a_scr[...] = a.reshape(n, LRU_WIDTH)
    b_scr[...] = b.reshape(n, LRU_WIDTH)

    def group(j, h):
        off = pl.multiple_of(j * SUBLANES, SUBLANES)
        hb = a_scr[pl.ds(off, SUBLANES), :] * h + b_scr[pl.ds(off, SUBLANES), :]
        b_scr[pl.ds(off, SUBLANES), :] = hb
        return jnp.broadcast_to(hb[SUBLANES - 1:SUBLANES, :], hb.shape)

    h_last = lax.fori_loop(0, n // SUBLANES, group,
                           jnp.broadcast_to(h0_ref[...], (SUBLANES, LRU_WIDTH)), unroll=4)
    hnew_ref[...] = h_last[0:1]
    y_ref[...] = (b_scr[...] * jax.nn.gelu(g_ref[...])).astype(BF16)


def _lru_call(lru_x, lru_g, prev8, h0, cw, cb, wgate, bgate, lam, layer, state_layer):
    bsz, n, _ = lru_x.shape
    seq = pl.BlockSpec((None, n, LRU_WIDTH), lambda b: (b, 0, 0))
    if state_layer is None:
        prev_spec = pl.BlockSpec((None, SUBLANES, LRU_WIDTH), lambda b: (0, 0, 0))
        h0_spec = pl.BlockSpec((None, 1, LRU_WIDTH), lambda b: (0, 0, 0))
    else:
        prev_spec = pl.BlockSpec((None, None, SUBLANES, LRU_WIDTH), lambda b: (state_layer, b, 0, 0))
        h0_spec = pl.BlockSpec((None, None, 1, LRU_WIDTH), lambda b: (state_layer, b, 0, 0))
    par = lambda r, w: pl.BlockSpec((None, r, w), lambda b: (layer, 0, 0))
    return pl.pallas_call(
        _lru_kernel,
        grid=(bsz,),
        in_specs=[seq, seq, prev_spec, h0_spec, par(CONV_W, LRU_WIDTH), par(1, LRU_WIDTH),
                  par(LRU_WIDTH, 2 * LRU_WIDTH), par(1, 2 * LRU_WIDTH), par(1, LRU_WIDTH)],
        out_specs=[seq,
                   pl.BlockSpec((None, CONV_W - 1, LRU_WIDTH), lambda b: (b, 0, 0)),
                   pl.BlockSpec((None, 1, LRU_WIDTH), lambda b: (b, 0, 0))],
        out_shape=[jax.ShapeDtypeStruct((bsz, n, LRU_WIDTH), BF16),
                   jax.ShapeDtypeStruct((bsz, CONV_W - 1, LRU_WIDTH), F32),
                   jax.ShapeDtypeStruct((bsz, 1, LRU_WIDTH), F32)],
        scratch_shapes=[pltpu.VMEM((n, LRU_WIDTH), F32), pltpu.VMEM((n, LRU_WIDTH), F32)],
        compiler_params=_params(40, ("arbitrary",)),
        name="rg_lru",
    )(lru_x, lru_g, prev8, h0, cw, cb, wgate, bgate, lam)


def _ssd_kernel(xbc_ref, z_ref, small_ref, prev_ref, h0_ref, cw_ref, cb_ref, dtb_ref, alog_ref, dvec_ref, nw_ref,
                y_ref, convnew_ref, h_ref, *, chunk):
    h_ref[...] = h0_ref[...]

    def one_chunk(c, tail):
        rows = pl.ds(pl.multiple_of(c * chunk, chunk), chunk)
        xx = jnp.concatenate([tail, xbc_ref[rows, :]], axis=0)
        _ssd_chunk(xx, z_ref[rows, :], small_ref[rows, :], cw_ref, cb_ref, dtb_ref, alog_ref, dvec_ref, nw_ref,
                   y_ref.at[rows, :], h_ref)
        return xx[chunk:chunk + SUBLANES]

    tail = lax.fori_loop(0, xbc_ref.shape[0] // chunk, one_chunk, prev_ref[...])
    convnew_ref[...] = _last_rows(tail, CONV_W - 1)


def _ssd_chunk(xx, z, small, cw_ref, cb_ref, dtb_ref, alog_ref, dvec_ref, nw_ref, y_ref, h_ref):
    n = xx.shape[0] - SUBLANES
    act = _silu(_causal_conv(xx, cw_ref[...], cb_ref[...], n))
    xs = act[:, :SSD_WIDTH]
    bm = act[:, SSD_WIDTH:SSD_WIDTH + SSD_GROUPS * D_STATE]
    cm = act[:, SSD_WIDTH + SSD_GROUPS * D_STATE:]

    dt = _softplus(small + dtb_ref[...])
    dta = dt * (-jnp.exp(alog_ref[...]))
    cum = _cumsum_rows(dta, _tri(n, True)) * LOG2E
    cum_t = cum.T
    rr = lax.broadcasted_iota(jnp.int32, (n, n), 0)
    cc = lax.broadcasted_iota(jnp.int32, (n, n), 1)
    causal = rr >= cc
    lo_lane = lax.broadcasted_iota(jnp.int32, (1, LANES), 1) < SSD_HEAD_DIM
    lo_row = lax.broadcasted_iota(jnp.int32, (LANES, 1), 0) < SSD_HEAD_DIM
    dvec = dvec_ref[...]

    ys = []
    for g in range(SSD_GROUPS):
        sl = slice(g * LANES, (g + 1) * LANES)
        xg, bg, cg = xs[:, sl], bm[:, sl].astype(BF16), cm[:, sl].astype(BF16)
        heads = (2 * g, 2 * g + 1)
        col = lambda a, h: a[:, DT_LANE0 + h:DT_LANE0 + h + 1]
        pick = lambda f: jnp.where(lo_lane, f(heads[0]), f(heads[1]))
        dx = xg * pick(lambda h: col(dt, h))
        dxb = dx.astype(BF16)
        cb_mat = _dot_nt(cg, bg)
        yd = []
        for h in heads:
            seg = col(cum, h) - cum_t[DT_LANE0 + h:DT_LANE0 + h + 1, :]
            lmat = jnp.exp2(jnp.where(causal, seg, NEG_BIG))
            yd.append(_dot((cb_mat * lmat).astype(BF16), dxb))
        y_diag = jnp.where(lo_lane, yd[0], yd[1])
        last = lambda h: col(cum, h)[n - 1:n, :]
        decay_end = pick(lambda h: jnp.exp2(last(h) - col(cum, h)))
        states = _dot_tn((dx * decay_end).astype(BF16), bg)
        h_prev = h_ref[sl, :]
        y_off = _dot_nt(cg, h_prev.astype(BF16)) * pick(lambda h: jnp.exp2(col(cum, h)))
        chunk_decay = jnp.where(lo_row, jnp.exp2(last(heads[0])), jnp.exp2(last(heads[1])))
        h_ref[sl, :] = chunk_decay * h_prev + states
        ys.append(y_diag + y_off + dvec[:, sl] * xg)
    y = jnp.concatenate(ys, axis=1)
    y_ref[...] = (_rms(y * _silu(z)) * nw_ref[...]).astype(BF16)


def _ssd_call(xbc, z, small, prev8, h0, cw, cb, dtb, alog, dvec, nw, layer, state_layer, chunk):
    bsz, n, _ = xbc.shape
    seq = lambda w: pl.BlockSpec((None, n, w), lambda b: (b, 0, 0))
    if state_layer is None:
        prev_spec = pl.BlockSpec((None, SUBLANES, SSD_CONV_DIM), lambda b: (0, 0, 0))
        h0_spec = pl.BlockSpec((None, SSD_WIDTH, D_STATE), lambda b: (0, 0, 0))
    else:
        prev_spec = pl.BlockSpec((None, None, SUBLANES, SSD_CONV_DIM), lambda b: (state_layer, b, 0, 0))
        h0_spec = pl.BlockSpec((None, None, SSD_WIDTH, D_STATE), lambda b: (state_layer, b, 0, 0))
    par = lambda r, w: pl.BlockSpec((None, r, w), lambda b: (layer, 0, 0))
    return pl.pallas_call(
        functools.partial(_ssd_kernel, chunk=chunk),
        grid=(bsz,),
        in_specs=[seq(SSD_CONV_DIM), seq(SSD_WIDTH), seq(LANES), prev_spec, h0_spec,
                  par(CONV_W, SSD_CONV_DIM), par(1, SSD_CONV_DIM), par(1, LANES), par(1, LANES),
                  par(1, SSD_WIDTH), par(1, SSD_WIDTH)],
        out_specs=[seq(SSD_WIDTH),
                   pl.BlockSpec((None, CONV_W - 1, SSD_CONV_DIM), lambda b: (b, 0, 0)),
                   pl.BlockSpec((None, SSD_WIDTH, D_STATE), lambda b: (b, 0, 0))],
        out_shape=[jax.ShapeDtypeStruct((bsz, n, SSD_WIDTH), BF16),
                   jax.ShapeDtypeStruct((bsz, CONV_W - 1, SSD_CONV_DIM), F32),
                   jax.ShapeDtypeStruct((bsz, SSD_WIDTH, D_STATE), F32)],
        compiler_params=_params(40, ("arbitrary",)),
        name="ssd",
    )(xbc, z, small, prev8, h0, cw, cb, dtb, alog, dvec, nw)


def _fox_prompt_kernel(q_ref, kt_ref, vt_ref, logft_ref, o_ref, k_scr, v_scr, f_scr, ft_scr):
    s = kt_ref.shape[1]
    tq = k_scr.shape[3]
    p = pl.program_id(1)
    heads = (HEAD_PAIR * p, HEAD_PAIR * p + 1)
    spare = (FOX_HEAD_DIM, 0)

    @pl.when(p == 0)
    def _():
        triu = _tri(SCAN_BLOCK, False)
        carry = jnp.zeros((LANES, 1), F32)
        per = tq // SCAN_BLOCK
        for c in range(s // SCAN_BLOCK):
            cols = slice(c * SCAN_BLOCK, (c + 1) * SCAN_BLOCK)
            fc = _cumsum_lanes(logft_ref[:, cols], triu) + carry
            ft_scr[c // per, :, (c % per) * SCAN_BLOCK:(c % per + 1) * SCAN_BLOCK] = fc
            f_scr[cols, :] = fc.T
            carry = fc[:, SCAN_BLOCK - 1:SCAN_BLOCK]

    row = lax.broadcasted_iota(jnp.int32, (LANES, 1), 0)
    for j in range(s // tq):
        kt = kt_ref[:, j * tq:(j + 1) * tq]
        vt = vt_ref[:, j * tq:(j + 1) * tq]
        for hl in range(HEAD_PAIR):
            own = (row < FOX_HEAD_DIM) if hl == 0 else (row >= FOX_HEAD_DIM)
            f_k = ft_scr[j, pl.ds(F_LANE0 + heads[hl], 1), :] * LOG2E
            kc = jnp.where(own, kt, 0.0).astype(BF16)
            for i, part in enumerate(_split3(-f_k)):
                kc = jnp.where(row == spare[hl] + i, part, kc)
            k_scr[hl, j] = kc
            v_scr[hl, j] = jnp.where(own, vt, jnp.where(row == spare[hl], 1.0, 0.0)).astype(BF16)

    lane = lax.broadcasted_iota(jnp.int32, (1, LANES), 1)
    lo_lane = lane < FOX_HEAD_DIM
    rb = min(ATTN_ROWS, tq)
    nrb = tq // rb

    def query_block(qi, _):
        rows = [pl.ds(pl.multiple_of(qi * tq + r * rb, rb), rb) for r in range(nrb)]
        q_blocks, fq_blocks = [], []
        for r in range(nrb):
            q = q_ref[rows[r], :].astype(F32)
            ones_at = lambda l0: jnp.where((lane >= l0) & (lane < l0 + 3), 1.0, 0.0)
            q_blocks.append((jnp.where(lo_lane, q, ones_at(spare[0])).astype(BF16),
                             jnp.where(lo_lane, ones_at(spare[1]), q).astype(BF16)))
            f_rows = f_scr[rows[r], :]
            fq_blocks.append([jnp.sum(jnp.where(lane == F_LANE0 + h, f_rows, 0.0), axis=1, keepdims=True) * LOG2E
                              for h in heads])
        for g0 in range(0, nrb, ATTN_GROUP):
            rs = tuple(range(g0, min(g0 + ATTN_GROUP, nrb)))
            carry = lax.fori_loop(0, qi, lambda j, cr: step(j, cr, rs, q_blocks, fq_blocks, False),
                                  (init_rows,) * len(rs))
            carry = step(qi, carry, rs, q_blocks, fq_blocks, True)
            for i, r in enumerate(rs):
                acc = [carry[i][2 * hl + 1] for hl in range(HEAD_PAIR)]
                out = [a / a[:, spare[hl]:spare[hl] + 1] for hl, a in enumerate(acc)]
                o_ref[rows[r], :] = jnp.where(lo_lane, out[0], out[1]).astype(BF16)
        return 0

    def softmax_unit(t, f_q, m_old, row0):
        if row0 is not None:
            rr = lax.broadcasted_iota(jnp.int32, t.shape, 0) + row0
            cc = lax.broadcasted_iota(jnp.int32, t.shape, 1)
            t = jnp.where(cc <= rr, t, NEG_BIG)
        m_new = jnp.maximum(m_old, jnp.max(t, axis=1, keepdims=True) + f_q)
        return m_new, jnp.exp2(m_old - m_new), jnp.exp2(t - (m_new - f_q)).astype(BF16)

    def step(j, carry, rs, q_blocks, fq_blocks, diagonal):
        nks = {r: (r + 1) * rb if diagonal else tq for r in rs}
        units = [(i, r, hl) for i, r in enumerate(rs) for hl in range(HEAD_PAIR)]
        dots = {(r, hl): _dot(q_blocks[r][hl], k_scr[hl, j, :, 0:nks[r]]) for _, r, hl in units}
        soft = {(r, hl): softmax_unit(dots[(r, hl)], fq_blocks[r][hl], carry[i][2 * hl], r * rb if diagonal else None)
                for i, r, hl in units}
        pvs = {(r, hl): _dot_nt(soft[(r, hl)][2], v_scr[hl, j, :, 0:nks[r]]) for _, r, hl in units}
        new = []
        for i, r in enumerate(rs):
            row_state = ()
            for hl in range(HEAD_PAIR):
                m_new, alpha, _ = soft[(r, hl)]
                row_state += (m_new, alpha * carry[i][2 * hl + 1] + pvs[(r, hl)])
            new.append(row_state)
        return tuple(new)

    init_rows = (jnp.full((rb, 1), NEG_BIG, F32), jnp.zeros((rb, LANES), F32)) * HEAD_PAIR
    lax.fori_loop(0, s // tq, query_block, 0)


def _fox_prompt_call(q, kt, vt, small_t, layer=None):
    bsz, s, _ = q.shape
    tq = min(ATTN_BLOCK, s)
    blk = pl.BlockSpec((None, s, LANES), lambda b, p: (b, 0, p))
    if layer is None:
        full = pl.BlockSpec((None, LANES, s), lambda b, p: (b, p, 0))
    else:
        full = pl.BlockSpec((None, None, LANES, s), lambda b, p: (layer, b, p, 0))
    return pl.pallas_call(
        _fox_prompt_kernel,
        grid=(bsz, FOX_WIDTH // LANES),
        in_specs=[blk, full, full, pl.BlockSpec((None, LANES, s), lambda b, p: (b, 0, 0))],
        out_specs=blk,
        out_shape=jax.ShapeDtypeStruct((bsz, s, FOX_WIDTH), BF16),
        scratch_shapes=[pltpu.VMEM((HEAD_PAIR, s // tq, LANES, tq), BF16),
                        pltpu.VMEM((HEAD_PAIR, s // tq, LANES, tq), BF16),
                        pltpu.VMEM((s, LANES), F32), pltpu.VMEM((s // tq, LANES, tq), F32)],
        compiler_params=_params(40, ("arbitrary", "arbitrary")),
        name="fox_prompt",
    )(q, kt, vt, small_t)


def _fox_sample_kernel(q_ref, k_ref, v_ref, logf_ref, ck_ref, cv_ref, clogft_ref, o_ref,
                       m_scr, l_scr, acc_scr, fk_scr, fq_scr, fnew_scr):
    t = q_ref.shape[0]
    c = pl.program_id(1)
    nchunk, _, chunk = fk_scr.shape

    @pl.when(c == 0)
    def _():
        blk = min(SCAN_BLOCK, chunk)
        per = chunk // blk
        nblk = nchunk * per
        stacked = jnp.concatenate([clogft_ref[:, i * blk:(i + 1) * blk] for i in range(nblk)], axis=0)
        local = _cumsum_lanes(stacked, _tri(blk, False))
        nrow = nblk * FOX_HEADS
        rr = lax.broadcasted_iota(jnp.int32, (nrow, nrow), 0)
        cc = lax.broadcasted_iota(jnp.int32, (nrow, nrow), 1)
        earlier = ((rr % FOX_HEADS == cc % FOX_HEADS) & (cc < rr - rr % FOX_HEADS)).astype(F32).astype(BF16)
        totals = jnp.broadcast_to(local[:, blk - 1:blk], (nrow, LANES))
        f_all = local + _cumsum_rows(totals, earlier)[:, 0:1]
        for i in range(nblk):
            fk_scr[i // per, :, (i % per) * blk:(i % per + 1) * blk] = f_all[i * FOX_HEADS:(i + 1) * FOX_HEADS, :] * LOG2E
        carry = f_all[nrow - FOX_HEADS:, blk - 1:blk]
        sub = lax.broadcasted_iota(jnp.int32, (FOX_HEADS, LANES), 0)
        ln = lax.broadcasted_iota(jnp.int32, (FOX_HEADS, LANES), 1)
        total_row = jnp.sum(jnp.where(sub == ln - F_LANE0, carry, 0.0), axis=0, keepdims=True)
        f_new = _cumsum_rows(logf_ref[...], _tri(t, True)) + total_row
        f_new = f_new * LOG2E
        fnew_scr[...] = f_new.T[F_LANE0:F_LANE0 + FOX_HEADS, :]
        fq_scr[...] = jnp.concatenate([f_new[:, F_LANE0 + h:F_LANE0 + h + 1] for h in range(FOX_HEADS)], axis=0)
        m_scr[...] = jnp.full(m_scr.shape, NEG_BIG, F32)
        l_scr[...] = jnp.zeros(l_scr.shape, F32)
        acc_scr[...] = jnp.zeros(acc_scr.shape, F32)

    q = q_ref[...]
    q_heads = [q[:, h * FOX_HEAD_DIM:(h + 1) * FOX_HEAD_DIM] for h in range(FOX_HEADS)]
    f_q = fq_scr[...]

    def update(keys, values, f_k, mask, channels_first):
        qk = _dot if channels_first else _dot_nt
        pv_dot = _dot_nt if channels_first else _dot
        tt = jnp.concatenate([qk(q_heads[h], keys[h]) - f_k[h:h + 1, :] for h in range(FOX_HEADS)], axis=0)
        if mask is not None:
            tt = jnp.where(mask, tt, NEG_BIG)
        m_old = m_scr[...]
        m_new = jnp.maximum(m_old, jnp.max(tt, axis=1, keepdims=True) + f_q)
        alpha = jnp.exp2(m_old - m_new)
        pr = jnp.exp2(tt - (m_new - f_q))
        l_scr[...] = alpha * l_scr[...] + jnp.sum(pr, axis=1, keepdims=True)
        prb = pr.astype(BF16)
        pv = jnp.concatenate([pv_dot(prb[h * t:(h + 1) * t, :], values[h]) for h in range(FOX_HEADS)], axis=0)
        acc_scr[...] = alpha * acc_scr[...] + pv
        m_scr[...] = m_new

    head_rows = lambda ref: [ref[h * FOX_HEAD_DIM:(h + 1) * FOX_HEAD_DIM, :].astype(BF16) for h in range(FOX_HEADS)]
    update(head_rows(ck_ref), head_rows(cv_ref), fk_scr[c], None, True)

    @pl.when(c == nchunk - 1)
    def _():
        k_new, v_new = k_ref[...].astype(BF16), v_ref[...].astype(BF16)
        cols = lambda a: [a[:, h * FOX_HEAD_DIM:(h + 1) * FOX_HEAD_DIM] for h in range(FOX_HEADS)]
        rr = lax.broadcasted_iota(jnp.int32, (FOX_HEADS * t, t), 0) % t
        cc = lax.broadcasted_iota(jnp.int32, (FOX_HEADS * t, t), 1)
        update(cols(k_new), cols(v_new), fnew_scr[...], cc <= rr, False)
        out = acc_scr[...] / l_scr[...]
        o_ref[...] = jnp.concatenate([out[h * t:(h + 1) * t, :] for h in range(FOX_HEADS)], axis=1).astype(BF16)


def _fox_sample_call(q, k, v, small, cache_k, cache_v, cache_logf_t, layer):
    bsz, t, _ = q.shape
    past = cache_logf_t.shape[3]
    chunk = min(SAMPLE_CHUNK, past)
    nchunk = past // chunk
    new = lambda w: pl.BlockSpec((None, t, w), lambda b, c: (b, 0, 0))
    cache = pl.BlockSpec((None, None, FOX_WIDTH, chunk), lambda b, c: (layer, b, 0, c))
    rows = FOX_HEADS * t
    return pl.pallas_call(
        _fox_sample_kernel,
        grid=(bsz, nchunk),
        in_specs=[new(FOX_WIDTH), new(FOX_WIDTH), new(FOX_WIDTH), new(LANES), cache, cache,
                  pl.BlockSpec((None, None, FOX_HEADS, past), lambda b, c: (layer, b, 0, 0))],
        out_specs=new(FOX_WIDTH),
        out_shape=jax.ShapeDtypeStruct((bsz, t, FOX_WIDTH), BF16),
        scratch_shapes=[pltpu.VMEM((rows, 1), F32), pltpu.VMEM((rows, 1), F32), pltpu.VMEM((rows, FOX_HEAD_DIM), F32),
                        pltpu.VMEM((nchunk, FOX_HEADS, chunk), F32), pltpu.VMEM((rows, 1), F32),
                        pltpu.VMEM((FOX_HEADS, t), F32)],
        compiler_params=_params(40, ("arbitrary", "arbitrary")),
        name="fox_sample",
    )(q, k, v, small, cache_k, cache_v, cache_logf_t)


def _small_rows(w_in_t):
    offs = np.concatenate([[0], np.cumsum(IN_SIZES)])
    seg = lambda i: w_in_t[:, int(offs[i]):int(offs[i + 1]), :]
    pad = jnp.zeros((w_in_t.shape[0], LANES - FOX_HEADS - SSD_HEADS, w_in_t.shape[2]), w_in_t.dtype)
    return jnp.concatenate([seg(5), seg(8), pad], axis=1)


def _block_diag(w):
    d, h, b, _ = w.shape
    eye = jnp.eye(h, dtype=w.dtype)
    return jnp.einsum("dhij,hg->dhigj", w, eye).reshape(d, h * b, h * b)


def _lane_slab(v, lane0):
    d, k = v.shape
    return jnp.zeros((d, 1, LANES), v.dtype).at[:, 0, lane0:lane0 + k].set(v)


def _pad_history(state):
    return jnp.pad(state, ((0, 0), (0, 0), (SUBLANES - (CONV_W - 1), 0), (0, 0)))


def _trunk(x, mod_group, caches, prm, ssd_chunk):
    bsz, n, _ = x.shape
    x = x.reshape(bsz * n, D_MODEL)
    states = {name: [] for name in ("fox_k", "fox_v", "fox_logf", "lru_conv", "lru_h", "ssd_conv", "ssd_h")}
    prev_kv = None
    for l in range(DEPTH):
        mod4 = mod_group[l].reshape(bsz, N_SUB, 3, D_MODEL)
        x = _ffn_call(x, mod4, prm["npre"], prm["npost"], prm["wg"], prm["wu"], prm["wd"], l, 0, 0, n)
        prompt = caches is None
        stacking = prompt and l == DEPTH - 1
        proj = _inproj_call(x, mod4, prm["npre"], prm["w_in_t"], prm["w_small"], prm["fbias"], l, n,
                            transposed=prompt, prev_kv=prev_kv if stacking else None)
        lrux, lrug, q, k, v, z, xbc, small = proj[:8]
        per_seq = lambda a: a.reshape(bsz, n, a.shape[-1])
        state_layer = None if prompt else l
        src = prm["zero_state"] if prompt else caches
        ya, lru_conv, lru_h = _lru_call(per_seq(lrux), per_seq(lrug), src["lru_conv"], src["lru_h"],
                                        prm["lru_cw"], prm["lru_cb"], prm["lru_wgate"], prm["lru_bgate"],
                                        prm["lru_lam"], l, state_layer)
        if prompt:
            small_t = proj[8]
            yb = _fox_prompt_call(per_seq(q), k, v, small_t, l if stacking else None)
            prev_kv = (k, v)
            logf_out = jnp.swapaxes(small_t[:, F_LANE0:F_LANE0 + FOX_HEADS, :], 1, 2)
        else:
            yb = _fox_sample_call(per_seq(q), per_seq(k), per_seq(v), per_seq(small),
                                  caches["fox_k"], caches["fox_v"], caches["fox_logf_t"], l)
            k_out = k.reshape(bsz, n, FOX_HEADS, FOX_HEAD_DIM)
            v_out = v.reshape(bsz, n, FOX_HEADS, FOX_HEAD_DIM)
            logf_out = small.reshape(bsz, n, LANES)[:, :, F_LANE0:F_LANE0 + FOX_HEADS]
        yc, ssd_conv, ssd_h = _ssd_call(per_seq(xbc), per_seq(z), per_seq(small), src["ssd_conv"], src["ssd_h"],
                                        prm["ssd_cw"], prm["ssd_cb"], prm["ssd_dtb"], prm["ssd_alog"],
                                        prm["ssd_dvec"], prm["ssd_nw"], l, state_layer, ssd_chunk)
        x = _outffn_call(x, ya.reshape(bsz * n, -1), yb.reshape(bsz * n, -1), yc.reshape(bsz * n, -1),
                         mod4, prm["npre"], prm["npost"], prm["w_out"], prm["wg"], prm["wu"], prm["wd"], l, n)
        if not prompt:
            states["fox_k"].append(k_out)
            states["fox_v"].append(v_out)
        states["fox_logf"].append(logf_out)
        states["lru_conv"].append(lru_conv)
        states["lru_h"].append(lru_h.reshape(bsz, LRU_WIDTH))
        states["ssd_conv"].append(ssd_conv)
        states["ssd_h"].append(ssd_h.reshape(bsz, SSD_HEADS, SSD_HEAD_DIM, D_STATE))
    out = {name: jnp.stack(vals, axis=0) for name, vals in states.items() if vals}
    if prev_kv is not None:
        heads_last = lambda a: jnp.transpose(a.reshape(DEPTH, bsz, FOX_HEADS, FOX_HEAD_DIM, n), (0, 1, 4, 2, 3))
        out["fox_k"], out["fox_v"] = heads_last(prev_kv[0]), heads_last(prev_kv[1])
    return x.reshape(bsz, n, D_MODEL), out


def kernel(x_prompt, x_sample, c_prompt, c_sample, cache_fox_k, cache_fox_v, cache_fox_logf, state_lru_conv, state_lru_h, state_ssd_conv, state_ssd_h, w_mod, b_mod, norm_pre, norm_post, ffn_w_gate, ffn_w_up, ffn_w_down, w_in, w_out, lru_conv_w, lru_conv_b, lru_wa, lru_ba, lru_wx, lru_bx, lru_lambda, fox_f_bias, ssd_conv_w, ssd_conv_b, ssd_dt_bias, ssd_a_log, ssd_d, ssd_norm_w):
    n_prompt, n_sample = x_prompt.shape[0], x_sample.shape[0]
    w_in_t = jnp.swapaxes(w_in, 1, 2)
    prm = {
        "npre": norm_pre.reshape(DEPTH, N_SUB, 1, D_MODEL),
        "npost": norm_post.reshape(DEPTH, N_SUB, 1, D_MODEL),
        "wg": ffn_w_gate.astype(BF16), "wu": ffn_w_up.astype(BF16), "wd": ffn_w_down.astype(BF16),
        "w_in_t": w_in_t, "w_small": _small_rows(w_in_t), "w_out": w_out.astype(BF16),
        "fbias": _lane_slab(fox_f_bias, F_LANE0),
        "lru_cw": lru_conv_w, "lru_cb": lru_conv_b.reshape(DEPTH, 1, LRU_WIDTH),
        "lru_wgate": jnp.concatenate([_block_diag(lru_wa), _block_diag(lru_wx)], axis=-1).astype(BF16),
        "lru_bgate": jnp.concatenate([lru_ba, lru_bx], axis=-1).reshape(DEPTH, 1, 2 * LRU_WIDTH),
        "lru_lam": lru_lambda.reshape(DEPTH, 1, LRU_WIDTH),
        "ssd_cw": ssd_conv_w, "ssd_cb": ssd_conv_b.reshape(DEPTH, 1, SSD_CONV_DIM),
        "ssd_dtb": _lane_slab(ssd_dt_bias, DT_LANE0), "ssd_alog": _lane_slab(ssd_a_log, DT_LANE0),
        "ssd_dvec": jnp.repeat(ssd_d, SSD_HEAD_DIM, axis=-1).reshape(DEPTH, 1, SSD_WIDTH),
        "ssd_nw": ssd_norm_w.reshape(DEPTH, 1, SSD_WIDTH),
        "zero_state": {
            "lru_conv": jnp.zeros((1, SUBLANES, LRU_WIDTH), F32), "lru_h": jnp.zeros((1, 1, LRU_WIDTH), F32),
            "ssd_conv": jnp.zeros((1, SUBLANES, SSD_CONV_DIM), F32), "ssd_h": jnp.zeros((1, SSD_WIDTH, D_STATE), F32),
        },
    }
    caches = {
        "fox_k": jnp.transpose(cache_fox_k, (0, 1, 3, 4, 2)).reshape(DEPTH, n_sample, FOX_WIDTH, -1),
        "fox_v": jnp.transpose(cache_fox_v, (0, 1, 3, 4, 2)).reshape(DEPTH, n_sample, FOX_WIDTH, -1),
        "fox_logf_t": jnp.swapaxes(cache_fox_logf, 2, 3),
        "lru_conv": _pad_history(state_lru_conv),
        "lru_h": state_lru_h.reshape(DEPTH, n_sample, 1, LRU_WIDTH),
        "ssd_conv": _pad_history(state_ssd_conv),
        "ssd_h": state_ssd_h.reshape(DEPTH, n_sample, SSD_WIDTH, D_STATE),
    }
    mod = _mod_call(jnp.concatenate([c_prompt, c_sample], axis=0), w_mod, b_mod)
    y_prompt, sp = _trunk(x_prompt, mod[:, :n_prompt], None, prm, ssd_chunk=256)
    y_sample, ss = _trunk(x_sample, mod[:, n_prompt:], caches, prm, ssd_chunk=x_sample.shape[1])
    names = ("fox_k", "fox_v", "fox_logf", "lru_conv", "lru_h", "ssd_conv", "ssd_h")
    return (y_prompt, y_sample) + tuple(sp[n] for n in names) + tuple(ss[n] for n in names)
```

```python
import functools

import numpy as np
import jax
import jax.numpy as jnp
from jax import lax
from jax.experimental import pallas as pl
from jax.experimental.pallas import tpu as pltpu

F32 = jnp.float32
BF16 = jnp.bfloat16

D_MODEL = 1024
DEPTH = 2
CONV_W = 4
EPS = 1e-6
LRU_WIDTH = 256
LRU_HEADS = 4
LRU_BLOCK = LRU_WIDTH // LRU_HEADS
LRU_C = 8.0
FOX_HEADS = 8
FOX_HEAD_DIM = 64
FOX_WIDTH = FOX_HEADS * FOX_HEAD_DIM
SSD_HEADS = 4
SSD_HEAD_DIM = 64
SSD_WIDTH = SSD_HEADS * SSD_HEAD_DIM
SSD_GROUPS = 2
D_STATE = 128
SSD_CONV_DIM = SSD_WIDTH + 2 * SSD_GROUPS * D_STATE
IN_SIZES = (LRU_WIDTH, LRU_WIDTH, FOX_WIDTH, FOX_WIDTH, FOX_WIDTH, FOX_HEADS, SSD_WIDTH, SSD_CONV_DIM, SSD_HEADS)
D_FF = 2816
N_SUB = 3

LANES = 128
SUBLANES = 8
HEAD_PAIR = LANES // FOX_HEAD_DIM

F_LANE0 = 0
DT_LANE0 = FOX_HEADS
COL_LRU_X = 0
COL_LRU_G = COL_LRU_X + LRU_WIDTH
COL_Q = COL_LRU_G + LRU_WIDTH
COL_K = COL_Q + FOX_WIDTH
COL_V = COL_K + FOX_WIDTH
COL_Z = COL_V + FOX_WIDTH
COL_XBC = COL_Z + SSD_WIDTH
COL_SMALL = COL_XBC + SSD_CONV_DIM
D_IN_PAD = COL_SMALL + LANES
D_IN = sum(IN_SIZES)

ROW_TILE = 512
FFN_ROW_TILE = 1024
FF_CHUNK = 512
ATTN_BLOCK = 1024
ATTN_ROWS = 128
ATTN_GROUP = 8
SCAN_BLOCK = 256
SAMPLE_CHUNK = 2048
NEG_BIG = -1e30
LOG2E = 1.4426950408889634
Q_SCALE = LOG2E * FOX_HEAD_DIM ** -0.5


def _dot(a, b):
    return jnp.dot(a, b, preferred_element_type=F32)


def _dot_nt(a, b):
    return lax.dot_general(a, b, (((1,), (1,)), ((), ())), preferred_element_type=F32)


def _dot_tn(a, b):
    return lax.dot_general(a, b, (((0,), (0,)), ((), ())), preferred_element_type=F32)


def _silu(x):
    return x * jax.nn.sigmoid(x)


def _softplus(x):
    return jnp.maximum(x, 0.0) + jnp.log1p(jnp.exp(-jnp.abs(x)))


def _rms(x):
    return x * lax.rsqrt(jnp.mean(x * x, axis=-1, keepdims=True) + EPS)


def _per_seq(rows, per_seq, fn):
    g = per_seq[0].shape[0]
    if g == 1:
        return fn(rows, *per_seq)
    tm, d = rows.shape
    out = fn(rows.reshape(g, tm // g, d), *[p[:, None, :] for p in per_seq])
    return out.reshape(tm, d)


def _pre_norm(x, npre, mod_ref, seqs=slice(None)):
    h = _rms(x) * npre
    return _per_seq(h, (mod_ref[seqs, 1, :], mod_ref[seqs, 0, :]), lambda r, sc, sh: r * (1.0 + sc) + sh)


def _post_norm(x, y, npost, mod_ref, w, seqs=slice(None)):
    yn = _rms(y) * npost
    return x + _per_seq(yn, (mod_ref[seqs, 2, :],), lambda r, gt: (w * gt) * r)


def _seq_grouping(rows_per_seq, tm):
    if rows_per_seq % tm == 0:
        return 1, rows_per_seq // tm
    assert tm % rows_per_seq == 0
    return tm // rows_per_seq, 1


def _mod_spec(g, tiles_per_seq, sub):
    if g == 1:
        return pl.BlockSpec((1, None, 3, D_MODEL), lambda i: (i // tiles_per_seq, sub, 0, 0))
    return pl.BlockSpec((g, None, 3, D_MODEL), lambda i: (i, sub, 0, 0))


def _params(vmem_mb, sem):
    return pltpu.CompilerParams(dimension_semantics=sem, vmem_limit_bytes=vmem_mb << 20)


def _mod_kernel(c_ref, w_ref, b_ref, o_ref):
    a = _silu(c_ref[...]).astype(BF16)
    o_ref[...] = _dot(a, w_ref[...].astype(BF16)) + b_ref[...]


def _mod_call(c_all, w_mod, b_mod):
    nseq = c_all.shape[0]
    width = N_SUB * 3 * D_MODEL
    tn = 1024
    return pl.pallas_call(
        _mod_kernel,
        grid=(DEPTH, width // tn),
        in_specs=[
            pl.BlockSpec((nseq, D_MODEL), lambda l, n: (0, 0)),
            pl.BlockSpec((None, D_MODEL, tn), lambda l, n: (l, 0, n)),
            pl.BlockSpec((None, 1, tn), lambda l, n: (l, 0, n)),
        ],
        out_specs=pl.BlockSpec((None, nseq, tn), lambda l, n: (l, 0, n)),
        out_shape=jax.ShapeDtypeStruct((DEPTH, nseq, width), F32),
        compiler_params=_params(24, ("arbitrary", "arbitrary")),
        name="adaln_mod",
    )(c_all, w_mod, b_mod.reshape(DEPTH, 1, width))


def _halves(tm, nseq):
    half = tm // 2
    rows = [slice(i * half, (i + 1) * half) for i in range(2)]
    seqs = [slice(0, 1)] * 2 if nseq == 1 else [slice(i * nseq // 2, (i + 1) * nseq // 2) for i in range(2)]
    return rows, seqs


def _swiglu(hs, wg_ref, wu_ref, wd_ref):
    accs = [None] * len(hs)
    for off in range(0, D_FF, FF_CHUNK):
        fc = min(FF_CHUNK, D_FF - off)
        for i, h in enumerate(hs):
            g = _dot(h, wg_ref[:, off:off + fc])
            u = _dot(h, wu_ref[:, off:off + fc])
            a = (_silu(g) * u).astype(BF16)
            d = _dot(a, wd_ref[off:off + fc, :])
            accs[i] = d if accs[i] is None else accs[i] + d
    return accs


def _ffn_kernel(x_ref, mod_ref, npre_ref, npost_ref, wg_ref, wu_ref, wd_ref, o_ref):
    rows, seqs = _halves(x_ref.shape[0], mod_ref.shape[0])
    xs = [x_ref[r, :] for r in rows]
    hs = [_pre_norm(x, npre_ref[...], mod_ref, sq).astype(BF16) for x, sq in zip(xs, seqs)]
    accs = _swiglu(hs, wg_ref, wu_ref, wd_ref)
    for i, r in enumerate(rows):
        o_ref[r, :] = _post_norm(xs[i], accs[i], npost_ref[...], mod_ref, 0.5, seqs[i])


def _outffn_kernel(x_ref, ya_ref, yb_ref, yc_ref, mod1_ref, mod2_ref, npost1_ref, npre2_ref, npost2_ref,
                   wo_ref, wg_ref, wu_ref, wd_ref, o_ref):
    rows, seqs = _halves(x_ref.shape[0], mod1_ref.shape[0])
    xs = []
    for r, sq in zip(rows, seqs):
        y = (_dot(ya_ref[r, :], wo_ref[0:LRU_WIDTH, :])
             + _dot(yb_ref[r, :], wo_ref[LRU_WIDTH:LRU_WIDTH + FOX_WIDTH, :])
             + _dot(yc_ref[r, :], wo_ref[LRU_WIDTH + FOX_WIDTH:, :]))
        xs.append(_post_norm(x_ref[r, :], y, npost1_ref[...], mod1_ref, 1.0, sq))
    hs = [_pre_norm(x, npre2_ref[...], mod2_ref, sq).astype(BF16) for x, sq in zip(xs, seqs)]
    accs = _swiglu(hs, wg_ref, wu_ref, wd_ref)
    for i, r in enumerate(rows):
        o_ref[r, :] = _post_norm(xs[i], accs[i], npost2_ref[...], mod2_ref, 0.5, seqs[i])


def _outffn_call(x, ya, yb, yc, mod4, npre, npost, w_out, wg, wu, wd, layer, rows_per_seq):
    m = x.shape[0]
    tm = min(FFN_ROW_TILE, m)
    g, tps = _seq_grouping(rows_per_seq, tm)
    row = lambda w: pl.BlockSpec((tm, w), lambda i: (i, 0))
    once = dict(pipeline_mode=pl.Buffered(1))
    wspec = lambda shape: pl.BlockSpec((None, None) + shape, lambda i: (layer, 1, 0, 0), **once)
    norm = lambda sub: pl.BlockSpec((None, None, 1, D_MODEL), lambda i: (layer, sub, 0, 0))
    return pl.pallas_call(
        _outffn_kernel,
        grid=(m // tm,),
        in_specs=[
            row(D_MODEL), row(LRU_WIDTH), row(FOX_WIDTH), row(SSD_WIDTH),
            _mod_spec(g, tps, 1), _mod_spec(g, tps, 2),
            norm(1), norm(2), norm(2),
            pl.BlockSpec((None, D_MODEL, D_MODEL), lambda i: (layer, 0, 0), **once),
            wspec((D_MODEL, D_FF)), wspec((D_MODEL, D_FF)), wspec((D_FF, D_MODEL)),
        ],
        out_specs=row(D_MODEL),
        out_shape=jax.ShapeDtypeStruct((m, D_MODEL), F32),
        compiler_params=_params(57, ("arbitrary",)),
        name="outproj_ffn",
    )(x, ya, yb, yc, mod4, mod4, npost, npre, npost, w_out, wg, wu, wd)


def _ffn_call(x, mod4, npre, npost, wg, wu, wd, layer, sub, ffn_idx, rows_per_seq):
    m = x.shape[0]
    tm = min(FFN_ROW_TILE, m)
    g, tps = _seq_grouping(rows_per_seq, tm)
    wspec = lambda shape: pl.BlockSpec((None, None) + shape, lambda i: (layer, ffn_idx, 0, 0),
                                       pipeline_mode=pl.Buffered(1))
    nspec = pl.BlockSpec((None, None, 1, D_MODEL), lambda i: (layer, sub, 0, 0))
    return pl.pallas_call(
        _ffn_kernel,
        grid=(m // tm,),
        in_specs=[
            pl.BlockSpec((tm, D_MODEL), lambda i: (i, 0)),
            _mod_spec(g, tps, sub),
            nspec, nspec,
            wspec((D_MODEL, D_FF)), wspec((D_MODEL, D_FF)), wspec((D_FF, D_MODEL)),
        ],
        out_specs=pl.BlockSpec((tm, D_MODEL), lambda i: (i, 0)),
        out_shape=jax.ShapeDtypeStruct((m, D_MODEL), F32),
        compiler_params=_params(56, ("arbitrary",)),
        name="ffn",
    )(x, mod4, npre, npost, wg, wu, wd)


def _inproj_kernel(*refs, transposed, stacked):
    n_in = 8 if stacked else 6
    x_ref, mod_ref, npre_ref, w_ref, wsmall_ref, fbias_ref = refs[:6]
    lrux_ref, lrug_ref, q_ref, k_ref, v_ref, z_ref, xbc_ref, small_ref, *rest = refs[n_in:]
    wt_ref = rest[-1]

    @pl.when(pl.program_id(0) == 0)
    def _():
        src = int(np.cumsum(IN_SIZES)[4])
        assert src == COL_Z
        moves = [(0, 0, src), (src + FOX_HEADS, COL_Z, SSD_WIDTH), (src + FOX_HEADS + SSD_WIDTH, COL_XBC, SSD_CONV_DIM)]
        for s0, d0, n in moves:
            for off in range(0, n, LRU_WIDTH):
                wt_ref[d0 + off:d0 + off + LRU_WIDTH, :] = w_ref[s0 + off:s0 + off + LRU_WIDTH, :].astype(BF16)
        wt_ref[COL_SMALL:COL_SMALL + LANES, :] = wsmall_ref[...].astype(BF16)

    h = _pre_norm(x_ref[...], npre_ref[...], mod_ref).astype(BF16)
    col = lambda start, width: _dot_nt(h, wt_ref[start:start + width, :])
    col_t = lambda start, width: _dot_nt(wt_ref[start:start + width, :], h)
    lrux_ref[...] = col(COL_LRU_X, LRU_WIDTH)
    lrug_ref[...] = col(COL_LRU_G, LRU_WIDTH)
    q_ref[...] = (col(COL_Q, FOX_WIDTH) * Q_SCALE).astype(BF16)
    if transposed:
        kv_t = col_t(COL_K, 2 * FOX_WIDTH)
        if stacked:
            k_ref[0], v_ref[0] = refs[6][...], refs[7][...]
            k_ref[1], v_ref[1] = kv_t[:FOX_WIDTH], kv_t[FOX_WIDTH:]
        else:
            k_ref[...] = kv_t[:FOX_WIDTH]
            v_ref[...] = kv_t[FOX_WIDTH:]
    else:
        k_ref[...] = col(COL_K, FOX_WIDTH)
        v_ref[...] = col(COL_V, FOX_WIDTH)
    z_ref[...] = col(COL_Z, SSD_WIDTH)
    xbc_ref[...] = col(COL_XBC, SSD_CONV_DIM)
    small = col(COL_SMALL, LANES)
    t = small + fbias_ref[...]
    logf = jnp.minimum(t, 0.0) - jnp.log1p(jnp.exp(-jnp.abs(t)))
    lane = lax.broadcasted_iota(jnp.int32, small.shape, 1)
    small = jnp.where(lane < DT_LANE0, logf, small)
    small_ref[...] = small
    if transposed:
        rest[0][...] = small.T


def _inproj_call(x, mod4, npre, w_in_t, w_small, fbias, layer, rows_per_seq, transposed, prev_kv=None):
    m = x.shape[0]
    tm = min(ROW_TILE, m)
    g, tps = _seq_grouping(rows_per_seq, tm)
    widths = (LRU_WIDTH, LRU_WIDTH, FOX_WIDTH, FOX_WIDTH, FOX_WIDTH, SSD_WIDTH, SSD_CONV_DIM, LANES)
    dtypes = (F32, F32, BF16, F32, F32, F32, F32, F32)
    out_specs = [pl.BlockSpec((tm, w), lambda i: (i, 0)) for w in widths]
    out_shape = [jax.ShapeDtypeStruct((m, w), dt) for w, dt in zip(widths, dtypes)]
    in_specs = [
        pl.BlockSpec((tm, D_MODEL), lambda i: (i, 0)),
        _mod_spec(g, tps, 1),
        pl.BlockSpec((None, None, 1, D_MODEL), lambda i: (layer, 1, 0, 0)),
        pl.BlockSpec((None, D_IN, D_MODEL), lambda i: (layer, 0, 0), pipeline_mode=pl.Buffered(1)),
        pl.BlockSpec((None, LANES, D_MODEL), lambda i: (layer, 0, 0)),
        pl.BlockSpec((None, 1, LANES), lambda i: (layer, 0, 0)),
    ]
    if transposed:
        assert g == 1
        nseq = m // rows_per_seq
        for idx, w in ((3, FOX_WIDTH), (4, FOX_WIDTH), (len(widths), LANES)):
            spec = pl.BlockSpec((None, w, tm), lambda i: (i // tps, 0, i % tps))
            shape = jax.ShapeDtypeStruct((nseq, w, rows_per_seq), F32)
            out_specs[idx:idx + 1], out_shape[idx:idx + 1] = [spec], [shape]
    extra = ()
    if prev_kv is not None:
        assert transposed and layer == 1 and DEPTH == 2
        extra = tuple(prev_kv)
        in_specs += [pl.BlockSpec((None, FOX_WIDTH, tm), lambda i: (i // tps, 0, i % tps))] * 2
        for idx in (3, 4):
            out_specs[idx] = pl.BlockSpec((DEPTH, None, FOX_WIDTH, tm), lambda i: (0, i // tps, 0, i % tps))
            out_shape[idx] = jax.ShapeDtypeStruct((DEPTH, nseq, FOX_WIDTH, rows_per_seq), F32)
    return pl.pallas_call(
        functools.partial(_inproj_kernel, transposed=transposed, stacked=prev_kv is not None),
        grid=(m // tm,),
        in_specs=in_specs,
        out_specs=out_specs,
        out_shape=out_shape,
        scratch_shapes=[pltpu.VMEM((D_IN_PAD, D_MODEL), BF16)],
        compiler_params=_params(56, ("arbitrary",)),
        name="inproj",
    )(x, mod4, npre, w_in_t, w_small, fbias, *extra)


def _causal_conv(xx, cw, cb, n):
    u = cb + cw[0:1] * pltpu.roll(xx, 3, 0)[SUBLANES:SUBLANES + n]
    u = u + cw[1:2] * pltpu.roll(xx, 2, 0)[SUBLANES:SUBLANES + n]
    u = u + cw[2:3] * pltpu.roll(xx, 1, 0)[SUBLANES:SUBLANES + n]
    return u + cw[3:4] * xx[SUBLANES:SUBLANES + n]


def _last_rows(xx, k):
    return pltpu.roll(xx, k, 0)[0:SUBLANES][0:k]


def _tri(n, lower):
    r = lax.broadcasted_iota(jnp.int32, (n, n), 0)
    c = lax.broadcasted_iota(jnp.int32, (n, n), 1)
    return ((r >= c) if lower else (r <= c)).astype(F32).astype(BF16)


def _split3(x):
    hi = x.astype(BF16)
    r1 = x - hi.astype(F32)
    mid = r1.astype(BF16)
    lo = (r1 - mid.astype(F32)).astype(BF16)
    return hi, mid, lo


def _cumsum_lanes(x, triu):
    hi, mid, lo = _split3(x)
    return _dot(hi, triu) + _dot(mid, triu) + _dot(lo, triu)


def _cumsum_rows(x, tril):
    hi, mid, lo = _split3(x)
    return _dot(tril, hi) + _dot(tril, mid) + _dot(tril, lo)


def _lru_kernel(x_ref, g_ref, prev_ref, h0_ref, cw_ref, cb_ref, wgate_ref, bgate_ref, lam_ref,
                y_ref, convnew_ref, hnew_ref, a_scr, b_scr):
    n = x_ref.shape[0]
    xx = jnp.concatenate([prev_ref[...], x_ref[...]], axis=0)
    convnew_ref[...] = _last_rows(xx, CONV_W - 1)
    u = _causal_conv(xx, cw_ref[...], cb_ref[...], n)
    gates = _dot(u.astype(BF16), wgate_ref[...]) + bgate_ref[...]
    r = jax.nn.sigmoid(gates[:, :LRU_WIDTH])
    i = jax.nn.sigmoid(gates[:, LRU_WIDTH:])
    log_a = (-LRU_C * r) * _softplus(-lam_ref[...])
    a = jnp.exp(log_a)
    b = jnp.sqrt(-jnp.tanh(log_a) * (a * a + 1.0)) * (i * u)
    a = a.reshape(n // SUBLANES, SUBLANES, LRU_WIDTH)
    b = b.reshape(n // SUBLANES, SUBLANES, LRU_WIDTH)
    row = lax.broadcasted_iota(jnp.int32, (1, SUBLANES, 1), 1)
    for d in (1, 2, 4):
        keep = row >= d
        b = jnp.where(keep, a * pltpu.roll(b, d, 1) + b, b)
        a = jnp.where(keep, a * pltpu.roll(a, d, 1), a)
    a_scr[...] = a.reshape(n, LRU_WIDTH)
    b_scr[...] = b.reshape(n, LRU_WIDTH)

    def group(j, h):
        off = pl.multiple_of(j * SUBLANES, SUBLANES)
        hb = a_scr[pl.ds(off, SUBLANES), :] * h + b_scr[pl.ds(off, SUBLANES), :]
        b_scr[pl.ds(off, SUBLANES), :] = hb
        return jnp.broadcast_to(hb[SUBLANES - 1:SUBLANES, :], hb.shape)

    h_last = lax.fori_loop(0, n // SUBLANES, group,
                           jnp.broadcast_to(h0_ref[...], (SUBLANES, LRU_WIDTH)), unroll=4)
    hnew_ref[...] = h_last[0:1]
    y_ref[...] = (b_scr[...] * jax.nn.gelu(g_ref[...])).astype(BF16)


def _lru_call(lru_x, lru_g, prev8, h0, cw, cb, wgate, bgate, lam, layer, state_layer):
    bsz, n, _ = lru_x.shape
    seq = pl.BlockSpec((None, n, LRU_WIDTH), lambda b: (b, 0, 0))
    if state_layer is None:
        prev_spec = pl.BlockSpec((None, SUBLANES, LRU_WIDTH), lambda b: (0, 0, 0))
        h0_spec = pl.BlockSpec((None, 1, LRU_WIDTH), lambda b: (0, 0, 0))
    else:
        prev_spec = pl.BlockSpec((None, None, SUBLANES, LRU_WIDTH), lambda b: (state_layer, b, 0, 0))
        h0_spec = pl.BlockSpec((None, None, 1, LRU_WIDTH), lambda b: (state_layer, b, 0, 0))
    par = lambda r, w: pl.BlockSpec((None, r, w), lambda b: (layer, 0, 0))
    return pl.pallas_call(
        _lru_kernel,
        grid=(bsz,),
        in_specs=[seq, seq, prev_spec, h0_spec, par(CONV_W, LRU_WIDTH), par(1, LRU_WIDTH),
                  par(LRU_WIDTH, 2 * LRU_WIDTH), par(1, 2 * LRU_WIDTH), par(1, LRU_WIDTH)],
        out_specs=[seq,
                   pl.BlockSpec((None, CONV_W - 1, LRU_WIDTH), lambda b: (b, 0, 0)),
                   pl.BlockSpec((None, 1, LRU_WIDTH), lambda b: (b, 0, 0))],
        out_shape=[jax.ShapeDtypeStruct((bsz, n, LRU_WIDTH), BF16),
                   jax.ShapeDtypeStruct((bsz, CONV_W - 1, LRU_WIDTH), F32),
                   jax.ShapeDtypeStruct((bsz, 1, LRU_WIDTH), F32)],
        scratch_shapes=[pltpu.VMEM((n, LRU_WIDTH), F32), pltpu.VMEM((n, LRU_WIDTH), F32)],
        compiler_params=_params(40, ("arbitrary",)),
        name="rg_lru",
    )(lru_x, lru_g, prev8, h0, cw, cb, wgate, bgate, lam)


def _ssd_kernel(xbc_ref, z_ref, small_ref, prev_ref, h0_ref, cw_ref, cb_ref, dtb_ref, alog_ref, dvec_ref, nw_ref,
                y_ref, convnew_ref, h_ref, *, chunk):
    h_ref[...] = h0_ref[...]

    def one_chunk(c, tail):
        rows = pl.ds(pl.multiple_of(c * chunk, chunk), chunk)
        xx = jnp.concatenate([tail, xbc_ref[rows, :]], axis=0)
        _ssd_chunk(xx, z_ref[rows, :], small_ref[rows, :], cw_ref, cb_ref, dtb_ref, alog_ref, dvec_ref, nw_ref,
                   y_ref.at[rows, :], h_ref)
        return xx[chunk:chunk + SUBLANES]

    tail = lax.fori_loop(0, xbc_ref.shape[0] // chunk, one_chunk, prev_ref[...])
    convnew_ref[...] = _last_rows(tail, CONV_W - 1)


def _ssd_chunk(xx, z, small, cw_ref, cb_ref, dtb_ref, alog_ref, dvec_ref, nw_ref, y_ref, h_ref):
    n = xx.shape[0] - SUBLANES
    act = _silu(_causal_conv(xx, cw_ref[...], cb_ref[...], n))
    xs = act[:, :SSD_WIDTH]
    bm = act[:, SSD_WIDTH:SSD_WIDTH + SSD_GROUPS * D_STATE]
    cm = act[:, SSD_WIDTH + SSD_GROUPS * D_STATE:]

    dt = _softplus(small + dtb_ref[...])
    dta = dt * (-jnp.exp(alog_ref[...]))
    cum = _cumsum_rows(dta, _tri(n, True)) * LOG2E
    cum_t = cum.T
    rr = lax.broadcasted_iota(jnp.int32, (n, n), 0)
    cc = lax.broadcasted_iota(jnp.int32, (n, n), 1)
    causal = rr >= cc
    lo_lane = lax.broadcasted_iota(jnp.int32, (1, LANES), 1) < SSD_HEAD_DIM
    lo_row = lax.broadcasted_iota(jnp.int32, (LANES, 1), 0) < SSD_HEAD_DIM
    dvec = dvec_ref[...]

    ys = []
    for g in range(SSD_GROUPS):
        sl = slice(g * LANES, (g + 1) * LANES)
        xg, bg, cg = xs[:, sl], bm[:, sl].astype(BF16), cm[:, sl].astype(BF16)
        heads = (2 * g, 2 * g + 1)
        col = lambda a, h: a[:, DT_LANE0 + h:DT_LANE0 + h + 1]
        pick = lambda f: jnp.where(lo_lane, f(heads[0]), f(heads[1]))
        dx = xg * pick(lambda h: col(dt, h))
        dxb = dx.astype(BF16)
        cb_mat = _dot_nt(cg, bg)
        yd = []
        for h in heads:
            seg = col(cum, h) - cum_t[DT_LANE0 + h:DT_LANE0 + h + 1, :]
            lmat = jnp.exp2(jnp.where(causal, seg, NEG_BIG))
            yd.append(_dot((cb_mat * lmat).astype(BF16), dxb))
        y_diag = jnp.where(lo_lane, yd[0], yd[1])
        last = lambda h: col(cum, h)[n - 1:n, :]
        decay_end = pick(lambda h: jnp.exp2(last(h) - col(cum, h)))
        states = _dot_tn((dx * decay_end).astype(BF16), bg)
        h_prev = h_ref[sl, :]
        y_off = _dot_nt(cg, h_prev.astype(BF16)) * pick(lambda h: jnp.exp2(col(cum, h)))
        chunk_decay = jnp.where(lo_row, jnp.exp2(last(heads[0])), jnp.exp2(last(heads[1])))
        h_ref[sl, :] = chunk_decay * h_prev + states
        ys.append(y_diag + y_off + dvec[:, sl] * xg)
    y = jnp.concatenate(ys, axis=1)
    y_ref[...] = (_rms(y * _silu(z)) * nw_ref[...]).astype(BF16)


def _ssd_call(xbc, z, small, prev8, h0, cw, cb, dtb, alog, dvec, nw, layer, state_layer, chunk):
    bsz, n, _ = xbc.shape
    seq = lambda w: pl.BlockSpec((None, n, w), lambda b: (b, 0, 0))
    if state_layer is None:
        prev_spec = pl.BlockSpec((None, SUBLANES, SSD_CONV_DIM), lambda b: (0, 0, 0))
        h0_spec = pl.BlockSpec((None, SSD_WIDTH, D_STATE), lambda b: (0, 0, 0))
    else:
        prev_spec = pl.BlockSpec((None, None, SUBLANES, SSD_CONV_DIM), lambda b: (state_layer, b, 0, 0))
        h0_spec = pl.BlockSpec((None, None, SSD_WIDTH, D_STATE), lambda b: (state_layer, b, 0, 0))
    par = lambda r, w: pl.BlockSpec((None, r, w), lambda b: (layer, 0, 0))
    return pl.pallas_call(
        functools.partial(_ssd_kernel, chunk=chunk),
        grid=(bsz,),
        in_specs=[seq(SSD_CONV_DIM), seq(SSD_WIDTH), seq(LANES), prev_spec, h0_spec,
                  par(CONV_W, SSD_CONV_DIM), par(1, SSD_CONV_DIM), par(1, LANES), par(1, LANES),
                  par(1, SSD_WIDTH), par(1, SSD_WIDTH)],
        out_specs=[seq(SSD_WIDTH),
                   pl.BlockSpec((None, CONV_W - 1, SSD_CONV_DIM), lambda b: (b, 0, 0)),
                   pl.BlockSpec((None, SSD_WIDTH, D_STATE), lambda b: (b, 0, 0))],
        out_shape=[jax.ShapeDtypeStruct((bsz, n, SSD_WIDTH), BF16),
                   jax.ShapeDtypeStruct((bsz, CONV_W - 1, SSD_CONV_DIM), F32),
                   jax.ShapeDtypeStruct((bsz, SSD_WIDTH, D_STATE), F32)],
        compiler_params=_params(40, ("arbitrary",)),
        name="ssd",
    )(xbc, z, small, prev8, h0, cw, cb, dtb, alog, dvec, nw)


def _fox_prompt_kernel(q_ref, kt_ref, vt_ref, logft_ref, o_ref, k_scr, v_scr, f_scr, ft_scr):
    s = kt_ref.shape[1]
    tq = k_scr.shape[3]
    p = pl.program_id(1)
    heads = (HEAD_PAIR * p, HEAD_PAIR * p + 1)
    spare = (FOX_HEAD_DIM, 0)

    @pl.when(p == 0)
    def _():
        triu = _tri(SCAN_BLOCK, False)
        carry = jnp.zeros((LANES, 1), F32)
        per = tq // SCAN_BLOCK
        for c in range(s // SCAN_BLOCK):
            cols = slice(c * SCAN_BLOCK, (c + 1) * SCAN_BLOCK)
            fc = _cumsum_lanes(logft_ref[:, cols], triu) + carry
            ft_scr[c // per, :, (c % per) * SCAN_BLOCK:(c % per + 1) * SCAN_BLOCK] = fc
            f_scr[cols, :] = fc.T
            carry = fc[:, SCAN_BLOCK - 1:SCAN_BLOCK]

    row = lax.broadcasted_iota(jnp.int32, (LANES, 1), 0)
    for j in range(s // tq):
        kt = kt_ref[:, j * tq:(j + 1) * tq]
        vt = vt_ref[:, j * tq:(j + 1) * tq]
        for hl in range(HEAD_PAIR):
            own = (row < FOX_HEAD_DIM) if hl == 0 else (row >= FOX_HEAD_DIM)
            f_k = ft_scr[j, pl.ds(F_LANE0 + heads[hl], 1), :] * LOG2E
            kc = jnp.where(own, kt, 0.0).astype(BF16)
            for i, part in enumerate(_split3(-f_k)):
                kc = jnp.where(row == spare[hl] + i, part, kc)
            k_scr[hl, j] = kc
            v_scr[hl, j] = jnp.where(own, vt, jnp.where(row == spare[hl], 1.0, 0.0)).astype(BF16)

    lane = lax.broadcasted_iota(jnp.int32, (1, LANES), 1)
    lo_lane = lane < FOX_HEAD_DIM
    rb = min(ATTN_ROWS, tq)
    nrb = tq // rb

    def query_block(qi, _):
        rows = [pl.ds(pl.multiple_of(qi * tq + r * rb, rb), rb) for r in range(nrb)]
        q_blocks, fq_blocks = [], []
        for r in range(nrb):
            q = q_ref[rows[r], :].astype(F32)
            ones_at = lambda l0: jnp.where((lane >= l0) & (lane < l0 + 3), 1.0, 0.0)
            q_blocks.append((jnp.where(lo_lane, q, ones_at(spare[0])).astype(BF16),
                             jnp.where(lo_lane, ones_at(spare[1]), q).astype(BF16)))
            f_rows = f_scr[rows[r], :]
            fq_blocks.append([jnp.sum(jnp.where(lane == F_LANE0 + h, f_rows, 0.0), axis=1, keepdims=True) * LOG2E
                              for h in heads])
        for g0 in range(0, nrb, ATTN_GROUP):
            rs = tuple(range(g0, min(g0 + ATTN_GROUP, nrb)))
            carry = lax.fori_loop(0, qi, lambda j, cr: step(j, cr, rs, q_blocks, fq_blocks, False),
                                  (init_rows,) * len(rs))
            carry = step(qi, carry, rs, q_blocks, fq_blocks, True)
            for i, r in enumerate(rs):
                acc = [carry[i][2 * hl + 1] for hl in range(HEAD_PAIR)]
                out = [a / a[:, spare[hl]:spare[hl] + 1] for hl, a in enumerate(acc)]
                o_ref[rows[r], :] = jnp.where(lo_lane, out[0], out[1]).astype(BF16)
        return 0

    def softmax_unit(t, f_q, m_old, row0):
        if row0 is not None:
            rr = lax.broadcasted_iota(jnp.int32, t.shape, 0) + row0
            cc = lax.broadcasted_iota(jnp.int32, t.shape, 1)
            t = jnp.where(cc <= rr, t, NEG_BIG)
        m_new = jnp.maximum(m_old, jnp.max(t, axis=1, keepdims=True) + f_q)
        return m_new, jnp.exp2(m_old - m_new), jnp.exp2((t - (m_new - f_q)).astype(BF16))

    def step(j, carry, rs, q_blocks, fq_blocks, diagonal):
        nks = {r: (r + 1) * rb if diagonal else tq for r in rs}
        units = [(i, r, hl) for i, r in enumerate(rs) for hl in range(HEAD_PAIR)]
        dots = {(r, hl): _dot(q_blocks[r][hl], k_scr[hl, j, :, 0:nks[r]]) for _, r, hl in units}
        soft = {(r, hl): softmax_unit(dots[(r, hl)], fq_blocks[r][hl], carry[i][2 * hl], r * rb if diagonal else None)
                for i, r, hl in units}
        pvs = {(r, hl): _dot_nt(soft[(r, hl)][2], v_scr[hl, j, :, 0:nks[r]]) for _, r, hl in units}
        new = []
        for i, r in enumerate(rs):
            row_state = ()
            for hl in range(HEAD_PAIR):
                m_new, alpha, _ = soft[(r, hl)]
                row_state += (m_new, alpha * carry[i][2 * hl + 1] + pvs[(r, hl)])
            new.append(row_state)
        return tuple(new)

    init_rows = (jnp.full((rb, 1), NEG_BIG, F32), jnp.zeros((rb, LANES), F32)) * HEAD_PAIR
    lax.fori_loop(0, s // tq, query_block, 0)


def _fox_prompt_call(q, kt, vt, small_t, layer=None):
    bsz, s, _ = q.shape
    tq = min(ATTN_BLOCK, s)
    blk = pl.BlockSpec((None, s, LANES), lambda b, p: (b, 0, p))
    if layer is None:
        full = pl.BlockSpec((None, LANES, s), lambda b, p: (b, p, 0))
    else:
        full = pl.BlockSpec((None, None, LANES, s), lambda b, p: (layer, b, p, 0))
    return pl.pallas_call(
        _fox_prompt_kernel,
        grid=(bsz, FOX_WIDTH // LANES),
        in_specs=[blk, full, full, pl.BlockSpec((None, LANES, s), lambda b, p: (b, 0, 0))],
        out_specs=blk,
        out_shape=jax.ShapeDtypeStruct((bsz, s, FOX_WIDTH), BF16),
        scratch_shapes=[pltpu.VMEM((HEAD_PAIR, s // tq, LANES, tq), BF16),
                        pltpu.VMEM((HEAD_PAIR, s // tq, LANES, tq), BF16),
                        pltpu.VMEM((s, LANES), F32), pltpu.VMEM((s // tq, LANES, tq), F32)],
        compiler_params=_params(40, ("arbitrary", "arbitrary")),
        name="fox_prompt",
    )(q, kt, vt, small_t)


def _fox_sample_kernel(q_ref, k_ref, v_ref, logf_ref, ck_ref, cv_ref, clogft_ref, o_ref,
                       m_scr, l_scr, acc_scr, fk_scr, fq_scr, fnew_scr):
    t = q_ref.shape[0]
    c = pl.program_id(1)
    nchunk, _, chunk = fk_scr.shape

    @pl.when(c == 0)
    def _():
        blk = min(SCAN_BLOCK, chunk)
        per = chunk // blk
        nblk = nchunk * per
        stacked = jnp.concatenate([clogft_ref[:, i * blk:(i + 1) * blk] for i in range(nblk)], axis=0)
        local = _cumsum_lanes(stacked, _tri(blk, False))
        nrow = nblk * FOX_HEADS
        rr = lax.broadcasted_iota(jnp.int32, (nrow, nrow), 0)
        cc = lax.broadcasted_iota(jnp.int32, (nrow, nrow), 1)
        earlier = ((rr % FOX_HEADS == cc % FOX_HEADS) & (cc < rr - rr % FOX_HEADS)).astype(F32).astype(BF16)
        totals = jnp.broadcast_to(local[:, blk - 1:blk], (nrow, LANES))
        f_all = local + _cumsum_rows(totals, earlier)[:, 0:1]
        for i in range(nblk):
            fk_scr[i // per, :, (i % per) * blk:(i % per + 1) * blk] = f_all[i * FOX_HEADS:(i + 1) * FOX_HEADS, :] * LOG2E
        carry = f_all[nrow - FOX_HEADS:, blk - 1:blk]
        sub = lax.broadcasted_iota(jnp.int32, (FOX_HEADS, LANES), 0)
        ln = lax.broadcasted_iota(jnp.int32, (FOX_HEADS, LANES), 1)
        total_row = jnp.sum(jnp.where(sub == ln - F_LANE0, carry, 0.0), axis=0, keepdims=True)
        f_new = _cumsum_rows(logf_ref[...], _tri(t, True)) + total_row
        f_new = f_new * LOG2E
        fnew_scr[...] = f_new.T[F_LANE0:F_LANE0 + FOX_HEADS, :]
        fq_scr[...] = jnp.concatenate([f_new[:, F_LANE0 + h:F_LANE0 + h + 1] for h in range(FOX_HEADS)], axis=0)
        m_scr[...] = jnp.full(m_scr.shape, NEG_BIG, F32)
        l_scr[...] = jnp.zeros(l_scr.shape, F32)
        acc_scr[...] = jnp.zeros(acc_scr.shape, F32)

    q = q_ref[...]
    q_heads = [q[:, h * FOX_HEAD_DIM:(h + 1) * FOX_HEAD_DIM] for h in range(FOX_HEADS)]
    f_q = fq_scr[...]

    def update(keys, values, f_k, mask, channels_first):
        qk = _dot if channels_first else _dot_nt
        pv_dot = _dot_nt if channels_first else _dot
        tt = jnp.concatenate([qk(q_heads[h], keys[h]) - f_k[h:h + 1, :] for h in range(FOX_HEADS)], axis=0)
        if mask is not None:
            tt = jnp.where(mask, tt, NEG_BIG)
        m_old = m_scr[...]
        m_new = jnp.maximum(m_old, jnp.max(tt, axis=1, keepdims=True) + f_q)
        alpha = jnp.exp2(m_old - m_new)
        pr = jnp.exp2(tt - (m_new - f_q))
        l_scr[...] = alpha * l_scr[...] + jnp.sum(pr, axis=1, keepdims=True)
        prb = pr.astype(BF16)
        pv = jnp.concatenate([pv_dot(prb[h * t:(h + 1) * t, :], values[h]) for h in range(FOX_HEADS)], axis=0)
        acc_scr[...] = alpha * acc_scr[...] + pv
        m_scr[...] = m_new

    head_rows = lambda ref: [ref[h * FOX_HEAD_DIM:(h + 1) * FOX_HEAD_DIM, :].astype(BF16) for h in range(FOX_HEADS)]
    update(head_rows(ck_ref), head_rows(cv_ref), fk_scr[c], None, True)

    @pl.when(c == nchunk - 1)
    def _():
        k_new, v_new = k_ref[...].astype(BF16), v_ref[...].astype(BF16)
        cols = lambda a: [a[:, h * FOX_HEAD_DIM:(h + 1) * FOX_HEAD_DIM] for h in range(FOX_HEADS)]
        rr = lax.broadcasted_iota(jnp.int32, (FOX_HEADS * t, t), 0) % t
        cc = lax.broadcasted_iota(jnp.int32, (FOX_HEADS * t, t), 1)
        update(cols(k_new), cols(v_new), fnew_scr[...], cc <= rr, False)
        out = acc_scr[...] / l_scr[...]
        o_ref[...] = jnp.concatenate([out[h * t:(h + 1) * t, :] for h in range(FOX_HEADS)], axis=1).astype(BF16)


def _fox_sample_call(q, k, v, small, cache_k, cache_v, cache_logf_t, layer):
    bsz, t, _ = q.shape
    past = cache_logf_t.shape[3]
    chunk = min(SAMPLE_CHUNK, past)
    nchunk = past // chunk
    new = lambda w: pl.BlockSpec((None, t, w), lambda b, c: (b, 0, 0))
    cache = pl.BlockSpec((None, None, FOX_WIDTH, chunk), lambda b, c: (layer, b, 0, c))
    rows = FOX_HEADS * t
    return pl.pallas_call(
        _fox_sample_kernel,
        grid=(bsz, nchunk),
        in_specs=[new(FOX_WIDTH), new(FOX_WIDTH), new(FOX_WIDTH), new(LANES), cache, cache,
                  pl.BlockSpec((None, None, FOX_HEADS, past), lambda b, c: (layer, b, 0, 0))],
        out_specs=new(FOX_WIDTH),
        out_shape=jax.ShapeDtypeStruct((bsz, t, FOX_WIDTH), BF16),
        scratch_shapes=[pltpu.VMEM((rows, 1), F32), pltpu.VMEM((rows, 1), F32), pltpu.VMEM((rows, FOX_HEAD_DIM), F32),
                        pltpu.VMEM((nchunk, FOX_HEADS, chunk), F32), pltpu.VMEM((rows, 1), F32),
                        pltpu.VMEM((FOX_HEADS, t), F32)],
        compiler_params=_params(40, ("arbitrary", "arbitrary")),
        name="fox_sample",
    )(q, k, v, small, cache_k, cache_v, cache_logf_t)


def _small_rows(w_in_t):
    offs = np.concatenate([[0], np.cumsum(IN_SIZES)])
    seg = lambda i: w_in_t[:, int(offs[i]):int(offs[i + 1]), :]
    pad = jnp.zeros((w_in_t.shape[0], LANES - FOX_HEADS - SSD_HEADS, w_in_t.shape[2]), w_in_t.dtype)
    return jnp.concatenate([seg(5), seg(8), pad], axis=1)


def _block_diag(w):
    d, h, b, _ = w.shape
    eye = jnp.eye(h, dtype=w.dtype)
    return jnp.einsum("dhij,hg->dhigj", w, eye).reshape(d, h * b, h * b)


def _lane_slab(v, lane0):
    d, k = v.shape
    return jnp.zeros((d, 1, LANES), v.dtype).at[:, 0, lane0:lane0 + k].set(v)


def _pad_history(state):
    return jnp.pad(state, ((0, 0), (0, 0), (SUBLANES - (CONV_W - 1), 0), (0, 0)))


def _trunk(x, mod_group, caches, prm, ssd_chunk):
    bsz, n, _ = x.shape
    x = x.reshape(bsz * n, D_MODEL)
    states = {name: [] for name in ("fox_k", "fox_v", "fox_logf", "lru_conv", "lru_h", "ssd_conv", "ssd_h")}
    prev_kv = None
    for l in range(DEPTH):
        mod4 = mod_group[l].reshape(bsz, N_SUB, 3, D_MODEL)
        x = _ffn_call(x, mod4, prm["npre"], prm["npost"], prm["wg"], prm["wu"], prm["wd"], l, 0, 0, n)
        prompt = caches is None
        stacking = prompt and l == DEPTH - 1
        proj = _inproj_call(x, mod4, prm["npre"], prm["w_in_t"], prm["w_small"], prm["fbias"], l, n,
                            transposed=prompt, prev_kv=prev_kv if stacking else None)
        lrux, lrug, q, k, v, z, xbc, small = proj[:8]
        per_seq = lambda a: a.reshape(bsz, n, a.shape[-1])
        state_layer = None if prompt else l
        src = prm["zero_state"] if prompt else caches
        ya, lru_conv, lru_h = _lru_call(per_seq(lrux), per_seq(lrug), src["lru_conv"], src["lru_h"],
                                        prm["lru_cw"], prm["lru_cb"], prm["lru_wgate"], prm["lru_bgate"],
                                        prm["lru_lam"], l, state_layer)
        if prompt:
            small_t = proj[8]
            yb = _fox_prompt_call(per_seq(q), k, v, small_t, l if stacking else None)
            prev_kv = (k, v)
            logf_out = jnp.swapaxes(small_t[:, F_LANE0:F_LANE0 + FOX_HEADS, :], 1, 2)
        else:
            yb = _fox_sample_call(per_seq(q), per_seq(k), per_seq(v), per_seq(small),
                                  caches["fox_k"], caches["fox_v"], caches["fox_logf_t"], l)
            k_out = k.reshape(bsz, n, FOX_HEADS, FOX_HEAD_DIM)
            v_out = v.reshape(bsz, n, FOX_HEADS, FOX_HEAD_DIM)
            logf_out = small.reshape(bsz, n, LANES)[:, :, F_LANE0:F_LANE0 + FOX_HEADS]
        yc, ssd_conv, ssd_h = _ssd_call(per_seq(xbc), per_seq(z), per_seq(small), src["ssd_conv"], src["ssd_h"],
                                        prm["ssd_cw"], prm["ssd_cb"], prm["ssd_dtb"], prm["ssd_alog"],
                                        prm["ssd_dvec"], prm["ssd_nw"], l, state_layer, ssd_chunk)
        x = _outffn_call(x, ya.reshape(bsz * n, -1), yb.reshape(bsz * n, -1), yc.reshape(bsz * n, -1),
                         mod4, prm["npre"], prm["npost"], prm["w_out"], prm["wg"], prm["wu"], prm["wd"], l, n)
        if not prompt:
            states["fox_k"].append(k_out)
            states["fox_v"].append(v_out)
        states["fox_logf"].append(logf_out)
        states["lru_conv"].append(lru_conv)
        states["lru_h"].append(lru_h.reshape(bsz, LRU_WIDTH))
        states["ssd_conv"].append(ssd_conv)
        states["ssd_h"].append(ssd_h.reshape(bsz, SSD_HEADS, SSD_HEAD_DIM, D_STATE))
    out = {name: jnp.stack(vals, axis=0) for name, vals in states.items() if vals}
    if prev_kv is not None:
        heads_last = lambda a: jnp.transpose(a.reshape(DEPTH, bsz, FOX_HEADS, FOX_HEAD_DIM, n), (0, 1, 4, 2, 3))
        out["fox_k"], out["fox_v"] = heads_last(prev_kv[0]), heads_last(prev_kv[1])
    return x.reshape(bsz, n, D_MODEL), out


def kernel(x_prompt, x_sample, c_prompt, c_sample, cache_fox_k, cache_fox_v, cache_fox_logf, state_lru_conv, state_lru_h, state_ssd_conv, state_ssd_h, w_mod, b_mod, norm_pre, norm_post, ffn_w_gate, ffn_w_up, ffn_w_down, w_in, w_out, lru_conv_w, lru_conv_b, lru_wa, lru_ba, lru_wx, lru_bx, lru_lambda, fox_f_bias, ssd_conv_w, ssd_conv_b, ssd_dt_bias, ssd_a_log, ssd_d, ssd_norm_w):
    n_prompt, n_sample = x_prompt.shape[0], x_sample.shape[0]
    w_in_t = jnp.swapaxes(w_in, 1, 2)
    prm = {
        "npre": norm_pre.reshape(DEPTH, N_SUB, 1, D_MODEL),
        "npost": norm_post.reshape(DEPTH, N_SUB, 1, D_MODEL),
        "wg": ffn_w_gate.astype(BF16), "wu": ffn_w_up.astype(BF16), "wd": ffn_w_down.astype(BF16),
        "w_in_t": w_in_t, "w_small": _small_rows(w_in_t), "w_out": w_out.astype(BF16),
        "fbias": _lane_slab(fox_f_bias, F_LANE0),
        "lru_cw": lru_conv_w, "lru_cb": lru_conv_b.reshape(DEPTH, 1, LRU_WIDTH),
        "lru_wgate": jnp.concatenate([_block_diag(lru_wa), _block_diag(lru_wx)], axis=-1).astype(BF16),
        "lru_bgate": jnp.concatenate([lru_ba, lru_bx], axis=-1).reshape(DEPTH, 1, 2 * LRU_WIDTH),
        "lru_lam": lru_lambda.reshape(DEPTH, 1, LRU_WIDTH),
        "ssd_cw": ssd_conv_w, "ssd_cb": ssd_conv_b.reshape(DEPTH, 1, SSD_CONV_DIM),
        "ssd_dtb": _lane_slab(ssd_dt_bias, DT_LANE0), "ssd_alog": _lane_slab(ssd_a_log, DT_LANE0),
        "ssd_dvec": jnp.repeat(ssd_d, SSD_HEAD_DIM, axis=-1).reshape(DEPTH, 1, SSD_WIDTH),
        "ssd_nw": ssd_norm_w.reshape(DEPTH, 1, SSD_WIDTH),
        "zero_state": {
            "lru_conv": jnp.zeros((1, SUBLANES, LRU_WIDTH), F32), "lru_h": jnp.zeros((1, 1, LRU_WIDTH), F32),
            "ssd_conv": jnp.zeros((1, SUBLANES, SSD_CONV_DIM), F32), "ssd_h": jnp.zeros((1, SSD_WIDTH, D_STATE), F32),
        },
    }
    caches = {
        "fox_k": jnp.transpose(cache_fox_k, (0, 1, 3, 4, 2)).reshape(DEPTH, n_sample, FOX_WIDTH, -1),
        "fox_v": jnp.transpose(cache_fox_v, (0, 1, 3, 4, 2)).reshape(DEPTH, n_sample, FOX_WIDTH, -1),
        "fox_logf_t": jnp.swapaxes(cache_fox_logf, 2, 3),
        "lru_conv": _pad_history(state_lru_conv),
        "lru_h": state_lru_h.reshape(DEPTH, n_sample, 1, LRU_WIDTH),
        "ssd_conv": _pad_history(state_ssd_conv),
        "ssd_h": state_ssd_h.reshape(DEPTH, n_sample, SSD_WIDTH, D_STATE),
    }
    mod = _mod_call(jnp.concatenate([c_prompt, c_sample], axis=0), w_mod, b_mod)
    y_prompt, sp = _trunk(x_prompt, mod[:, :n_prompt], None, prm, ssd_chunk=256)
    y_sample, ss = _trunk(x_sample, mod[:, n_prompt:], caches, prm, ssd_chunk=x_sample.shape[1])
    names = ("fox_k", "fox_v", "fox_logf", "lru_conv", "lru_h", "ssd_conv", "ssd_h")
    return (y_prompt, y_sample) + tuple(sp[n] for n in names) + tuple(ss[n] for n in names)
```

```python
import functools

import numpy as np
import jax
import jax.numpy as jnp
from jax import lax
from jax.experimental import pallas as pl
from jax.experimental.pallas import tpu as pltpu

F32 = jnp.float32
BF16 = jnp.bfloat16

D_MODEL = 1024
DEPTH = 2
CONV_W = 4
EPS = 1e-6
LRU_WIDTH = 256
LRU_HEADS = 4
LRU_BLOCK = LRU_WIDTH // LRU_HEADS
LRU_C = 8.0
FOX_HEADS = 8
FOX_HEAD_DIM = 64
FOX_WIDTH = FOX_HEADS * FOX_HEAD_DIM
SSD_HEADS = 4
SSD_HEAD_DIM = 64
SSD_WIDTH = SSD_HEADS * SSD_HEAD_DIM
SSD_GROUPS = 2
D_STATE = 128
SSD_CONV_DIM = SSD_WIDTH + 2 * SSD_GROUPS * D_STATE
IN_SIZES = (LRU_WIDTH, LRU_WIDTH, FOX_WIDTH, FOX_WIDTH, FOX_WIDTH, FOX_HEADS, SSD_WIDTH, SSD_CONV_DIM, SSD_HEADS)
D_FF = 2816
N_SUB = 3

LANES = 128
SUBLANES = 8
HEAD_PAIR = LANES // FOX_HEAD_DIM

F_LANE0 = 0
DT_LANE0 = FOX_HEADS
COL_LRU_X = 0
COL_LRU_G = COL_LRU_X + LRU_WIDTH
COL_Q = COL_LRU_G + LRU_WIDTH
COL_K = COL_Q + FOX_WIDTH
COL_V = COL_K + FOX_WIDTH
COL_Z = COL_V + FOX_WIDTH
COL_XBC = COL_Z + SSD_WIDTH
COL_SMALL = COL_XBC + SSD_CONV_DIM
D_IN_PAD = COL_SMALL + LANES
D_IN = sum(IN_SIZES)

ROW_TILE = 512
FFN_ROW_TILE = 1024
FF_CHUNK = 512
ATTN_BLOCK = 2048
ATTN_ROWS = 128
ATTN_GROUP = 16
SCAN_BLOCK = 256
SAMPLE_CHUNK = 2048
NEG_BIG = -1e30
LOG2E = 1.4426950408889634
Q_SCALE = LOG2E * FOX_HEAD_DIM ** -0.5


def _dot(a, b):
    return jnp.dot(a, b, preferred_element_type=F32)


def _dot_nt(a, b):
    return lax.dot_general(a, b, (((1,), (1,)), ((), ())), preferred_element_type=F32)


def _dot_tn(a, b):
    return lax.dot_general(a, b, (((0,), (0,)), ((), ())), preferred_element_type=F32)


def _silu(x):
    return x * jax.nn.sigmoid(x)


def _softplus(x):
    return jnp.maximum(x, 0.0) + jnp.log1p(jnp.exp(-jnp.abs(x)))


def _rms(x):
    return x * lax.rsqrt(jnp.mean(x * x, axis=-1, keepdims=True) + EPS)


def _per_seq(rows, per_seq, fn):
    g = per_seq[0].shape[0]
    if g == 1:
        return fn(rows, *per_seq)
    tm, d = rows.shape
    out = fn(rows.reshape(g, tm // g, d), *[p[:, None, :] for p in per_seq])
    return out.reshape(tm, d)


def _pre_norm(x, npre, mod_ref, seqs=slice(None)):
    h = _rms(x) * npre
    return _per_seq(h, (mod_ref[seqs, 1, :], mod_ref[seqs, 0, :]), lambda r, sc, sh: r * (1.0 + sc) + sh)


def _post_norm(x, y, npost, mod_ref, w, seqs=slice(None)):
    yn = _rms(y) * npost
    return x + _per_seq(yn, (mod_ref[seqs, 2, :],), lambda r, gt: (w * gt) * r)


def _seq_grouping(rows_per_seq, tm):
    if rows_per_seq % tm == 0:
        return 1, rows_per_seq // tm
    assert tm % rows_per_seq == 0
    return tm // rows_per_seq, 1


def _mod_spec(g, tiles_per_seq, sub):
    if g == 1:
        return pl.BlockSpec((1, None, 3, D_MODEL), lambda i: (i // tiles_per_seq, sub, 0, 0))
    return pl.BlockSpec((g, None, 3, D_MODEL), lambda i: (i, sub, 0, 0))


def _params(vmem_mb, sem):
    return pltpu.CompilerParams(dimension_semantics=sem, vmem_limit_bytes=vmem_mb << 20)


def _mod_kernel(c_ref, w_ref, b_ref, o_ref):
    a = _silu(c_ref[...]).astype(BF16)
    o_ref[...] = _dot(a, w_ref[...].astype(BF16)) + b_ref[...]


def _mod_call(c_all, w_mod, b_mod):
    nseq = c_all.shape[0]
    width = N_SUB * 3 * D_MODEL
    tn = 1024
    return pl.pallas_call(
        _mod_kernel,
        grid=(DEPTH, width // tn),
        in_specs=[
            pl.BlockSpec((nseq, D_MODEL), lambda l, n: (0, 0)),
            pl.BlockSpec((None, D_MODEL, tn), lambda l, n: (l, 0, n)),
            pl.BlockSpec((None, 1, tn), lambda l, n: (l, 0, n)),
        ],
        out_specs=pl.BlockSpec((None, nseq, tn), lambda l, n: (l, 0, n)),
        out_shape=jax.ShapeDtypeStruct((DEPTH, nseq, width), F32),
        compiler_params=_params(24, ("arbitrary", "arbitrary")),
        name="adaln_mod",
    )(c_all, w_mod, b_mod.reshape(DEPTH, 1, width))


def _halves(tm, nseq):
    half = tm // 2
    rows = [slice(i * half, (i + 1) * half) for i in range(2)]
    seqs = [slice(0, 1)] * 2 if nseq == 1 else [slice(i * nseq // 2, (i + 1) * nseq // 2) for i in range(2)]
    return rows, seqs


def _swiglu(hs, wg_ref, wu_ref, wd_ref):
    accs = [None] * len(hs)
    for off in range(0, D_FF, FF_CHUNK):
        fc = min(FF_CHUNK, D_FF - off)
        for i, h in enumerate(hs):
            g = _dot(h, wg_ref[:, off:off + fc])
            u = _dot(h, wu_ref[:, off:off + fc])
            a = (_silu(g) * u).astype(BF16)
            d = _dot(a, wd_ref[off:off + fc, :])
            accs[i] = d if accs[i] is None else accs[i] + d
    return accs


def _ffn_kernel(x_ref, mod_ref, npre_ref, npost_ref, wg_ref, wu_ref, wd_ref, o_ref):
    rows, seqs = _halves(x_ref.shape[0], mod_ref.shape[0])
    xs = [x_ref[r, :] for r in rows]
    hs = [_pre_norm(x, npre_ref[...], mod_ref, sq).astype(BF16) for x, sq in zip(xs, seqs)]
    accs = _swiglu(hs, wg_ref, wu_ref, wd_ref)
    for i, r in enumerate(rows):
        o_ref[r, :] = _post_norm(xs[i], accs[i], npost_ref[...], mod_ref, 0.5, seqs[i])


def _outffn_kernel(x_ref, ya_ref, yb_ref, yc_ref, mod1_ref, mod2_ref, npost1_ref, npre2_ref, npost2_ref,
                   wo_ref, wg_ref, wu_ref, wd_ref, o_ref):
    rows, seqs = _halves(x_ref.shape[0], mod1_ref.shape[0])
    xs = []
    for r, sq in zip(rows, seqs):
        y = (_dot(ya_ref[r, :], wo_ref[0:LRU_WIDTH, :])
             + _dot(yb_ref[r, :], wo_ref[LRU_WIDTH:LRU_WIDTH + FOX_WIDTH, :])
             + _dot(yc_ref[r, :], wo_ref[LRU_WIDTH + FOX_WIDTH:, :]))
        xs.append(_post_norm(x_ref[r, :], y, npost1_ref[...], mod1_ref, 1.0, sq))
    hs = [_pre_norm(x, npre2_ref[...], mod2_ref, sq).astype(BF16) for x, sq in zip(xs, seqs)]
    accs = _swiglu(hs, wg_ref, wu_ref, wd_ref)
    for i, r in enumerate(rows):
        o_ref[r, :] = _post_norm(xs[i], accs[i], npost2_ref[...], mod2_ref, 0.5, seqs[i])


def _outffn_call(x, ya, yb, yc, mod4, npre, npost, w_out, wg, wu, wd, layer, rows_per_seq):
    m = x.shape[0]
    tm = min(FFN_ROW_TILE, m)
    g, tps = _seq_grouping(rows_per_seq, tm)
    row = lambda w: pl.BlockSpec((tm, w), lambda i: (i, 0))
    once = dict(pipeline_mode=pl.Buffered(1))
    wspec = lambda shape: pl.BlockSpec((None, None) + shape, lambda i: (layer, 1, 0, 0), **once)
    norm = lambda sub: pl.BlockSpec((None, None, 1, D_MODEL), lambda i: (layer, sub, 0, 0))
    return pl.pallas_call(
        _outffn_kernel,
        grid=(m // tm,),
        in_specs=[
            row(D_MODEL), row(LRU_WIDTH), row(FOX_WIDTH), row(SSD_WIDTH),
            _mod_spec(g, tps, 1), _mod_spec(g, tps, 2),
            norm(1), norm(2), norm(2),
            pl.BlockSpec((None, D_MODEL, D_MODEL), lambda i: (layer, 0, 0), **once),
            wspec((D_MODEL, D_FF)), wspec((D_MODEL, D_FF)), wspec((D_FF, D_MODEL)),
        ],
        out_specs=row(D_MODEL),
        out_shape=jax.ShapeDtypeStruct((m, D_MODEL), F32),
        compiler_params=_params(57, ("arbitrary",)),
        name="outproj_ffn",
    )(x, ya, yb, yc, mod4, mod4, npost, npre, npost, w_out, wg, wu, wd)


def _ffn_call(x, mod4, npre, npost, wg, wu, wd, layer, sub, ffn_idx, rows_per_seq):
    m = x.shape[0]
    tm = min(FFN_ROW_TILE, m)
    g, tps = _seq_grouping(rows_per_seq, tm)
    wspec = lambda shape: pl.BlockSpec((None, None) + shape, lambda i: (layer, ffn_idx, 0, 0),
                                       pipeline_mode=pl.Buffered(1))
    nspec = pl.BlockSpec((None, None, 1, D_MODEL), lambda i: (layer, sub, 0, 0))
    return pl.pallas_call(
        _ffn_kernel,
        grid=(m // tm,),
        in_specs=[
            pl.BlockSpec((tm, D_MODEL), lambda i: (i, 0)),
            _mod_spec(g, tps, sub),
            nspec, nspec,
            wspec((D_MODEL, D_FF)), wspec((D_MODEL, D_FF)), wspec((D_FF, D_MODEL)),
        ],
        out_specs=pl.BlockSpec((tm, D_MODEL), lambda i: (i, 0)),
        out_shape=jax.ShapeDtypeStruct((m, D_MODEL), F32),
        compiler_params=_params(56, ("arbitrary",)),
        name="ffn",
    )(x, mod4, npre, npost, wg, wu, wd)


def _inproj_kernel(*refs, transposed, stacked):
    n_in = 8 if stacked else 6
    x_ref, mod_ref, npre_ref, w_ref, wsmall_ref, fbias_ref = refs[:6]
    lrux_ref, lrug_ref, q_ref, k_ref, v_ref, z_ref, xbc_ref, small_ref, *rest = refs[n_in:]
    wt_ref = rest[-1]

    @pl.when(pl.program_id(0) == 0)
    def _():
        src = int(np.cumsum(IN_SIZES)[4])
        assert src == COL_Z
        moves = [(0, 0, src), (src + FOX_HEADS, COL_Z, SSD_WIDTH), (src + FOX_HEADS + SSD_WIDTH, COL_XBC, SSD_CONV_DIM)]
        for s0, d0, n in moves:
            for off in range(0, n, LRU_WIDTH):
                wt_ref[d0 + off:d0 + off + LRU_WIDTH, :] = w_ref[s0 + off:s0 + off + LRU_WIDTH, :].astype(BF16)
        wt_ref[COL_SMALL:COL_SMALL + LANES, :] = wsmall_ref[...].astype(BF16)

    h = _pre_norm(x_ref[...], npre_ref[...], mod_ref).astype(BF16)
    col = lambda start, width: _dot_nt(h, wt_ref[start:start + width, :])
    col_t = lambda start, width: _dot_nt(wt_ref[start:start + width, :], h)
    lrux_ref[...] = col(COL_LRU_X, LRU_WIDTH)
    lrug_ref[...] = col(COL_LRU_G, LRU_WIDTH)
    q_ref[...] = (col(COL_Q, FOX_WIDTH) * Q_SCALE).astype(BF16)
    if transposed:
        kv_t = col_t(COL_K, 2 * FOX_WIDTH)
        if stacked:
            k_ref[0], v_ref[0] = refs[6][...], refs[7][...]
            k_ref[1], v_ref[1] = kv_t[:FOX_WIDTH], kv_t[FOX_WIDTH:]
        else:
            k_ref[...] = kv_t[:FOX_WIDTH]
            v_ref[...] = kv_t[FOX_WIDTH:]
    else:
        k_ref[...] = col(COL_K, FOX_WIDTH)
        v_ref[...] = col(COL_V, FOX_WIDTH)
    z_ref[...] = col(COL_Z, SSD_WIDTH)
    xbc_ref[...] = col(COL_XBC, SSD_CONV_DIM)
    small = col(COL_SMALL, LANES)
    t = small + fbias_ref[...]
    logf = jnp.minimum(t, 0.0) - jnp.log1p(jnp.exp(-jnp.abs(t)))
    lane = lax.broadcasted_iota(jnp.int32, small.shape, 1)
    small = jnp.where(lane < DT_LANE0, logf, small)
    small_ref[...] = small
    if transposed:
        rest[0][...] = small.T


def _inproj_call(x, mod4, npre, w_in_t, w_small, fbias, layer, rows_per_seq, transposed, prev_kv=None):
    m = x.shape[0]
    tm = min(ROW_TILE, m)
    g, tps = _seq_grouping(rows_per_seq, tm)
    widths = (LRU_WIDTH, LRU_WIDTH, FOX_WIDTH, FOX_WIDTH, FOX_WIDTH, SSD_WIDTH, SSD_CONV_DIM, LANES)
    dtypes = (F32, F32, BF16, F32, F32, F32, F32, F32)
    out_specs = [pl.BlockSpec((tm, w), lambda i: (i, 0)) for w in widths]
    out_shape = [jax.ShapeDtypeStruct((m, w), dt) for w, dt in zip(widths, dtypes)]
    in_specs = [
        pl.BlockSpec((tm, D_MODEL), lambda i: (i, 0)),
        _mod_spec(g, tps, 1),
        pl.BlockSpec((None, None, 1, D_MODEL), lambda i: (layer, 1, 0, 0)),
        pl.BlockSpec((None, D_IN, D_MODEL), lambda i: (layer, 0, 0), pipeline_mode=pl.Buffered(1)),
        pl.BlockSpec((None, LANES, D_MODEL), lambda i: (layer, 0, 0)),
        pl.BlockSpec((None, 1, LANES), lambda i: (layer, 0, 0)),
    ]
    if transposed:
        assert g == 1
        nseq = m // rows_per_seq
        for idx, w in ((3, FOX_WIDTH), (4, FOX_WIDTH), (len(widths), LANES)):
            spec = pl.BlockSpec((None, w, tm), lambda i: (i // tps, 0, i % tps))
            shape = jax.ShapeDtypeStruct((nseq, w, rows_per_seq), F32)
            out_specs[idx:idx + 1], out_shape[idx:idx + 1] = [spec], [shape]
    extra = ()
    if prev_kv is not None:
        assert transposed and layer == 1 and DEPTH == 2
        extra = tuple(prev_kv)
        in_specs += [pl.BlockSpec((None, FOX_WIDTH, tm), lambda i: (i // tps, 0, i % tps))] * 2
        for idx in (3, 4):
            out_specs[idx] = pl.BlockSpec((DEPTH, None, FOX_WIDTH, tm), lambda i: (0, i // tps, 0, i % tps))
            out_shape[idx] = jax.ShapeDtypeStruct((DEPTH, nseq, FOX_WIDTH, rows_per_seq), F32)
    return pl.pallas_call(
        functools.partial(_inproj_kernel, transposed=transposed, stacked=prev_kv is not None),
        grid=(m // tm,),
        in_specs=in_specs,
        out_specs=out_specs,
        out_shape=out_shape,
        scratch_shapes=[pltpu.VMEM((D_IN_PAD, D_MODEL), BF16)],
        compiler_params=_params(56, ("arbitrary",)),
        name="inproj",
    )(x, mod4, npre, w_in_t, w_small, fbias, *extra)


def _causal_conv(xx, cw, cb, n):
    u = cb + cw[0:1] * pltpu.roll(xx, 3, 0)[SUBLANES:SUBLANES + n]
    u = u + cw[1:2] * pltpu.roll(xx, 2, 0)[SUBLANES:SUBLANES + n]
    u = u + cw[2:3] * pltpu.roll(xx, 1, 0)[SUBLANES:SUBLANES + n]
    return u + cw[3:4] * xx[SUBLANES:SUBLANES + n]


def _last_rows(xx, k):
    return pltpu.roll(xx, k, 0)[0:SUBLANES][0:k]


def _tri(n, lower):
    r = lax.broadcasted_iota(jnp.int32, (n, n), 0)
    c = lax.broadcasted_iota(jnp.int32, (n, n), 1)
    return ((r >= c) if lower else (r <= c)).astype(F32).astype(BF16)


def _split3(x):
    hi = x.astype(BF16)
    r1 = x - hi.astype(F32)
    mid = r1.astype(BF16)
    lo = (r1 - mid.astype(F32)).astype(BF16)
    return hi, mid, lo


def _cumsum_lanes(x, triu):
    hi, mid, lo = _split3(x)
    return _dot(hi, triu) + _dot(mid, triu) + _dot(lo, triu)


def _cumsum_rows(x, tril):
    hi, mid, lo = _split3(x)
    return _dot(tril, hi) + _dot(tril, mid) + _dot(tril, lo)


def _lru_kernel(x_ref, g_ref, prev_ref, h0_ref, cw_ref, cb_ref, wgate_ref, bgate_ref, lam_ref,
                y_ref, convnew_ref, hnew_ref, a_scr, b_scr):
    n = x_ref.shape[0]
    xx = jnp.concatenate([prev_ref[...], x_ref[...]], axis=0)
    convnew_ref[...] = _last_rows(xx, CONV_W - 1)
    u = _causal_conv(xx, cw_ref[...], cb_ref[...], n)
    gates = _dot(u.astype(BF16), wgate_ref[...]) + bgate_ref[...]
    r = jax.nn.sigmoid(gates[:, :LRU_WIDTH])
    i = jax.nn.sigmoid(gates[:, LRU_WIDTH:])
    log_a = (-LRU_C * r) * _softplus(-lam_ref[...])
    a = jnp.exp(log_a)
    b = jnp.sqrt(-jnp.tanh(log_a) * (a * a + 1.0)) * (i * u)
    a = a.reshape(n // SUBLANES, SUBLANES, LRU_WIDTH)
    b = b.reshape(n // SUBLANES, SUBLANES, LRU_WIDTH)
    row = lax.broadcasted_iota(jnp.int32, (1, SUBLANES, 1), 1)
    for d in (1, 2, 4):
        keep = row >= d
        b = jnp.where(keep, a * pltpu.roll(b, d, 1) + b, b)
        a = jnp.where(keep, a * pltpu.roll(a, d, 1), a)
    a_scr[...] = a.reshape(n, LRU_WIDTH)
    b_scr[...] = b.reshape(n, LRU_WIDTH)

    def group(j, h):
        off = pl.multiple_of(j * SUBLANES, SUBLANES)
        hb = a_scr[pl.ds(off, SUBLANES), :] * h + b_scr[pl.ds(off, SUBLANES), :]
        b_scr[pl.ds(off, SUBLANES), :] = hb
        return jnp.broadcast_to(hb[SUBLANES - 1:SUBLANES, :], hb.shape)

    h_last = lax.fori_loop(0, n // SUBLANES, group,
                           jnp.broadcast_to(h0_ref[...], (SUBLANES, LRU_WIDTH)), unroll=4)
    hnew_ref[...] = h_last[0:1]
    y_ref[...] = (b_scr[...] * jax.nn.gelu(g_ref[...])).astype(BF16)


def _lru_call(lru_x, lru_g, prev8, h0, cw, cb, wgate, bgate, lam, layer, state_layer):
    bsz, n, _ = lru_x.shape
    seq = pl.BlockSpec((None, n, LRU_WIDTH), lambda b: (b, 0, 0))
    if state_layer is None:
        prev_spec = pl.BlockSpec((None, SUBLANES, LRU_WIDTH), lambda b: (0, 0, 0))
        h0_spec = pl.BlockSpec((None, 1, LRU_WIDTH), lambda b: (0, 0, 0))
    else:
        prev_spec = pl.BlockSpec((None, None, SUBLANES, LRU_WIDTH), lambda b: (state_layer, b, 0, 0))
        h0_spec = pl.BlockSpec((None, None, 1, LRU_WIDTH), lambda b: (state_layer, b, 0, 0))
    par = lambda r, w: pl.BlockSpec((None, r, w), lambda b: (layer, 0, 0))
    return pl.pallas_call(
        _lru_kernel,
        grid=(bsz,),
        in_specs=[seq, seq, prev_spec, h0_spec, par(CONV_W, LRU_WIDTH), par(1, LRU_WIDTH),
                  par(LRU_WIDTH, 2 * LRU_WIDTH), par(1, 2 * LRU_WIDTH), par(1, LRU_WIDTH)],
        out_specs=[seq,
                   pl.BlockSpec((None, CONV_W - 1, LRU_WIDTH), lambda b: (b, 0, 0)),
                   pl.BlockSpec((None, 1, LRU_WIDTH), lambda b: (b, 0, 0))],
        out_shape=[jax.ShapeDtypeStruct((bsz, n, LRU_WIDTH), BF16),
                   jax.ShapeDtypeStruct((bsz, CONV_W - 1, LRU_WIDTH), F32),
                   jax.ShapeDtypeStruct((bsz, 1, LRU_WIDTH), F32)],
        scratch_shapes=[pltpu.VMEM((n, LRU_WIDTH), F32), pltpu.VMEM((n, LRU_WIDTH), F32)],
        compiler_params=_params(40, ("arbitrary",)),
        name="rg_lru",
    )(lru_x, lru_g, prev8, h0, cw, cb, wgate, bgate, lam)


def _ssd_kernel(xbc_ref, z_ref, small_ref, prev_ref, h0_ref, cw_ref, cb_ref, dtb_ref, alog_ref, dvec_ref, nw_ref,
                y_ref, convnew_ref, h_ref, *, chunk):
    h_ref[...] = h0_ref[...]

    def one_chunk(c, tail):
        rows = pl.ds(pl.multiple_of(c * chunk, chunk), chunk)
        xx = jnp.concatenate([tail, xbc_ref[rows, :]], axis=0)
        _ssd_chunk(xx, z_ref[rows, :], small_ref[rows, :], cw_ref, cb_ref, dtb_ref, alog_ref, dvec_ref, nw_ref,
                   y_ref.at[rows, :], h_ref)
        return xx[chunk:chunk + SUBLANES]

    tail = lax.fori_loop(0, xbc_ref.shape[0] // chunk, one_chunk, prev_ref[...])
    convnew_ref[...] = _last_rows(tail, CONV_W - 1)


def _ssd_chunk(xx, z, small, cw_ref, cb_ref, dtb_ref, alog_ref, dvec_ref, nw_ref, y_ref, h_ref):
    n = xx.shape[0] - SUBLANES
    act = _silu(_causal_conv(xx, cw_ref[...], cb_ref[...], n))
    xs = act[:, :SSD_WIDTH]
    bm = act[:, SSD_WIDTH:SSD_WIDTH + SSD_GROUPS * D_STATE]
    cm = act[:, SSD_WIDTH + SSD_GROUPS * D_STATE:]

    dt = _softplus(small + dtb_ref[...])
    dta = dt * (-jnp.exp(alog_ref[...]))
    cum = _cumsum_rows(dta, _tri(n, True)) * LOG2E
    cum_t = cum.T
    rr = lax.broadcasted_iota(jnp.int32, (n, n), 0)
    cc = lax.broadcasted_iota(jnp.int32, (n, n), 1)
    causal = rr >= cc
    lo_lane = lax.broadcasted_iota(jnp.int32, (1, LANES), 1) < SSD_HEAD_DIM
    lo_row = lax.broadcasted_iota(jnp.int32, (LANES, 1), 0) < SSD_HEAD_DIM
    dvec = dvec_ref[...]

    ys = []
    for g in range(SSD_GROUPS):
        sl = slice(g * LANES, (g + 1) * LANES)
        xg, bg, cg = xs[:, sl], bm[:, sl].astype(BF16), cm[:, sl].astype(BF16)
        heads = (2 * g, 2 * g + 1)
        col = lambda a, h: a[:, DT_LANE0 + h:DT_LANE0 + h + 1]
        pick = lambda f: jnp.where(lo_lane, f(heads[0]), f(heads[1]))
        dx = xg * pick(lambda h: col(dt, h))
        dxb = dx.astype(BF16)
        cb_mat = _dot_nt(cg, bg)
        yd = []
        for h in heads:
            seg = col(cum, h) - cum_t[DT_LANE0 + h:DT_LANE0 + h + 1, :]
            lmat = jnp.exp2(jnp.where(causal, seg, NEG_BIG))
            yd.append(_dot((cb_mat * lmat).astype(BF16), dxb))
        y_diag = jnp.where(lo_lane, yd[0], yd[1])
        last = lambda h: col(cum, h)[n - 1:n, :]
        decay_end = pick(lambda h: jnp.exp2(last(h) - col(cum, h)))
        states = _dot_tn((dx * decay_end).astype(BF16), bg)
        h_prev = h_ref[sl, :]
        y_off = _dot_nt(cg, h_prev.astype(BF16)) * pick(lambda h: jnp.exp2(col(cum, h)))
        chunk_decay = jnp.where(lo_row, jnp.exp2(last(heads[0])), jnp.exp2(last(heads[1])))
        h_ref[sl, :] = chunk_decay * h_prev + states
        ys.append(y_diag + y_off + dvec[:, sl] * xg)
    y = jnp.concatenate(ys, axis=1)
    y_ref[...] = (_rms(y * _silu(z)) * nw_ref[...]).astype(BF16)


def _ssd_call(xbc, z, small, prev8, h0, cw, cb, dtb, alog, dvec, nw, layer, state_layer, chunk):
    bsz, n, _ = xbc.shape
    seq = lambda w: pl.BlockSpec((None, n, w), lambda b: (b, 0, 0))
    if state_layer is None:
        prev_spec = pl.BlockSpec((None, SUBLANES, SSD_CONV_DIM), lambda b: (0, 0, 0))
        h0_spec = pl.BlockSpec((None, SSD_WIDTH, D_STATE), lambda b: (0, 0, 0))
    else:
        prev_spec = pl.BlockSpec((None, None, SUBLANES, SSD_CONV_DIM), lambda b: (state_layer, b, 0, 0))
        h0_spec = pl.BlockSpec((None, None, SSD_WIDTH, D_STATE), lambda b: (state_layer, b, 0, 0))
    par = lambda r, w: pl.BlockSpec((None, r, w), lambda b: (layer, 0, 0))
    return pl.pallas_call(
        functools.partial(_ssd_kernel, chunk=chunk),
        grid=(bsz,),
        in_specs=[seq(SSD_CONV_DIM), seq(SSD_WIDTH), seq(LANES), prev_spec, h0_spec,
                  par(CONV_W, SSD_CONV_DIM), par(1, SSD_CONV_DIM), par(1, LANES), par(1, LANES),
                  par(1, SSD_WIDTH), par(1, SSD_WIDTH)],
        out_specs=[seq(SSD_WIDTH),
                   pl.BlockSpec((None, CONV_W - 1, SSD_CONV_DIM), lambda b: (b, 0, 0)),
                   pl.BlockSpec((None, SSD_WIDTH, D_STATE), lambda b: (b, 0, 0))],
        out_shape=[jax.ShapeDtypeStruct((bsz, n, SSD_WIDTH), BF16),
                   jax.ShapeDtypeStruct((bsz, CONV_W - 1, SSD_CONV_DIM), F32),
                   jax.ShapeDtypeStruct((bsz, SSD_WIDTH, D_STATE), F32)],
        compiler_params=_params(40, ("arbitrary",)),
        name="ssd",
    )(xbc, z, small, prev8, h0, cw, cb, dtb, alog, dvec, nw)


def _fox_prompt_kernel(q_ref, kt_ref, vt_ref, logft_ref, o_ref, k_scr, v_scr, f_scr, ft_scr):
    s = kt_ref.shape[1]
    tq = k_scr.shape[3]
    p = pl.program_id(1)
    heads = (HEAD_PAIR * p, HEAD_PAIR * p + 1)
    spare = (FOX_HEAD_DIM, 0)

    @pl.when(p == 0)
    def _():
        triu = _tri(SCAN_BLOCK, False)
        carry = jnp.zeros((LANES, 1), F32)
        per = tq // SCAN_BLOCK
        for c in range(s // SCAN_BLOCK):
            cols = slice(c * SCAN_BLOCK, (c + 1) * SCAN_BLOCK)
            fc = _cumsum_lanes(logft_ref[:, cols], triu) + carry
            ft_scr[c // per, :, (c % per) * SCAN_BLOCK:(c % per + 1) * SCAN_BLOCK] = fc
            f_scr[cols, :] = fc.T
            carry = fc[:, SCAN_BLOCK - 1:SCAN_BLOCK]

    row = lax.broadcasted_iota(jnp.int32, (LANES, 1), 0)
    for j in range(s // tq):
        kt = kt_ref[:, j * tq:(j + 1) * tq]
        vt = vt_ref[:, j * tq:(j + 1) * tq]
        for hl in range(HEAD_PAIR):
            own = (row < FOX_HEAD_DIM) if hl == 0 else (row >= FOX_HEAD_DIM)
            f_k = ft_scr[j, pl.ds(F_LANE0 + heads[hl], 1), :] * LOG2E
            kc = jnp.where(own, kt, 0.0).astype(BF16)
            for i, part in enumerate(_split3(-f_k)):
                kc = jnp.where(row == spare[hl] + i, part, kc)
            k_scr[hl, j] = kc
            v_scr[hl, j] = jnp.where(own, vt, jnp.where(row == spare[hl], 1.0, 0.0)).astype(BF16)

    lane = lax.broadcasted_iota(jnp.int32, (1, LANES), 1)
    lo_lane = lane < FOX_HEAD_DIM
    rb = min(ATTN_ROWS, tq)
    nrb = tq // rb

    def query_block(qi, _):
        rows = [pl.ds(pl.multiple_of(qi * tq + r * rb, rb), rb) for r in range(nrb)]
        q_blocks, fq_blocks = [], []
        for r in range(nrb):
            q = q_ref[rows[r], :].astype(F32)
            ones_at = lambda l0: jnp.where((lane >= l0) & (lane < l0 + 3), 1.0, 0.0)
            q_blocks.append((jnp.where(lo_lane, q, ones_at(spare[0])).astype(BF16),
                             jnp.where(lo_lane, ones_at(spare[1]), q).astype(BF16)))
            f_rows = f_scr[rows[r], :]
            fq_blocks.append([jnp.sum(jnp.where(lane == F_LANE0 + h, f_rows, 0.0), axis=1, keepdims=True) * LOG2E
                              for h in heads])
        for g0 in range(0, nrb, ATTN_GROUP):
            rs = tuple(range(g0, min(g0 + ATTN_GROUP, nrb)))
            carry = lax.fori_loop(0, qi, lambda j, cr: step(j, cr, rs, q_blocks, fq_blocks, False),
                                  (init_rows,) * len(rs))
            carry = step(qi, carry, rs, q_blocks, fq_blocks, True)
            for i, r in enumerate(rs):
                acc = [carry[i][2 * hl + 1] for hl in range(HEAD_PAIR)]
                out = [a / a[:, spare[hl]:spare[hl] + 1] for hl, a in enumerate(acc)]
                o_ref[rows[r], :] = jnp.where(lo_lane, out[0], out[1]).astype(BF16)
        return 0

    def softmax_unit(t, f_q, m_old, row0):
        if row0 is not None:
            rr = lax.broadcasted_iota(jnp.int32, t.shape, 0) + row0
            cc = lax.broadcasted_iota(jnp.int32, t.shape, 1)
            t = jnp.where(cc <= rr, t, NEG_BIG)
        m_new = jnp.maximum(m_old, jnp.max(t, axis=1, keepdims=True) + f_q)
        return m_new, jnp.exp2(m_old - m_new), jnp.exp2(t - (m_new - f_q)).astype(BF16)

    def step(j, carry, rs, q_blocks, fq_blocks, diagonal):
        nks = {r: (r + 1) * rb if diagonal else tq for r in rs}
        units = [(i, r, hl) for i, r in enumerate(rs) for hl in range(HEAD_PAIR)]
        dots = {(r, hl): _dot(q_blocks[r][hl], k_scr[hl, j, :, 0:nks[r]]) for _, r, hl in units}
        soft = {(r, hl): softmax_unit(dots[(r, hl)], fq_blocks[r][hl], carry[i][2 * hl], r * rb if diagonal else None)
                for i, r, hl in units}
        pvs = {(r, hl): _dot_nt(soft[(r, hl)][2], v_scr[hl, j, :, 0:nks[r]]) for _, r, hl in units}
        new = []
        for i, r in enumerate(rs):
            row_state = ()
            for hl in range(HEAD_PAIR):
                m_new, alpha, _ = soft[(r, hl)]
                row_state += (m_new, alpha * carry[i][2 * hl + 1] + pvs[(r, hl)])
            new.append(row_state)
        return tuple(new)

    init_rows = (jnp.full((rb, 1), NEG_BIG, F32), jnp.zeros((rb, LANES), F32)) * HEAD_PAIR
    lax.fori_loop(0, s // tq, query_block, 0)


def _fox_prompt_call(q, kt, vt, small_t, layer=None):
    bsz, s, _ = q.shape
    tq = min(ATTN_BLOCK, s)
    blk = pl.BlockSpec((None, s, LANES), lambda b, p: (b, 0, p))
    if layer is None:
        full = pl.BlockSpec((None, LANES, s), lambda b, p: (b, p, 0))
    else:
        full = pl.BlockSpec((None, None, LANES, s), lambda b, p: (layer, b, p, 0))
    return pl.pallas_call(
        _fox_prompt_kernel,
        grid=(bsz, FOX_WIDTH // LANES),
        in_specs=[blk, full, full, pl.BlockSpec((None, LANES, s), lambda b, p: (b, 0, 0))],
        out_specs=blk,
        out_shape=jax.ShapeDtypeStruct((bsz, s, FOX_WIDTH), BF16),
        scratch_shapes=[pltpu.VMEM((HEAD_PAIR, s // tq, LANES, tq), BF16),
                        pltpu.VMEM((HEAD_PAIR, s // tq, LANES, tq), BF16),
                        pltpu.VMEM((s, LANES), F32), pltpu.VMEM((s // tq, LANES, tq), F32)],
        compiler_params=_params(40, ("arbitrary", "arbitrary")),
        name="fox_prompt",
    )(q, kt, vt, small_t)


def _fox_sample_kernel(q_ref, k_ref, v_ref, logf_ref, ck_ref, cv_ref, clogft_ref, o_ref,
                       m_scr, l_scr, acc_scr, fk_scr, fq_scr, fnew_scr):
    t = q_ref.shape[0]
    c = pl.program_id(1)
    nchunk, _, chunk = fk_scr.shape

    @pl.when(c == 0)
    def _():
        blk = min(SCAN_BLOCK, chunk)
        per = chunk // blk
        nblk = nchunk * per
        stacked = jnp.concatenate([clogft_ref[:, i * blk:(i + 1) * blk] for i in range(nblk)], axis=0)
        local = _cumsum_lanes(stacked, _tri(blk, False))
        nrow = nblk * FOX_HEADS
        rr = lax.broadcasted_iota(jnp.int32, (nrow, nrow), 0)
        cc = lax.broadcasted_iota(jnp.int32, (nrow, nrow), 1)
        earlier = ((rr % FOX_HEADS == cc % FOX_HEADS) & (cc < rr - rr % FOX_HEADS)).astype(F32).astype(BF16)
        totals = jnp.broadcast_to(local[:, blk - 1:blk], (nrow, LANES))
        f_all = local + _cumsum_rows(totals, earlier)[:, 0:1]
        for i in range(nblk):
            fk_scr[i // per, :, (i % per) * blk:(i % per + 1) * blk] = f_all[i * FOX_HEADS:(i + 1) * FOX_HEADS, :] * LOG2E
        carry = f_all[nrow - FOX_HEADS:, blk - 1:blk]
        sub = lax.broadcasted_iota(jnp.int32, (FOX_HEADS, LANES), 0)
        ln = lax.broadcasted_iota(jnp.int32, (FOX_HEADS, LANES), 1)
        total_row = jnp.sum(jnp.where(sub == ln - F_LANE0, carry, 0.0), axis=0, keepdims=True)
        f_new = _cumsum_rows(logf_ref[...], _tri(t, True)) + total_row
        f_new = f_new * LOG2E
        fnew_scr[...] = f_new.T[F_LANE0:F_LANE0 + FOX_HEADS, :]
        fq_scr[...] = jnp.concatenate([f_new[:, F_LANE0 + h:F_LANE0 + h + 1] for h in range(FOX_HEADS)], axis=0)
        m_scr[...] = jnp.full(m_scr.shape, NEG_BIG, F32)
        l_scr[...] = jnp.zeros(l_scr.shape, F32)
        acc_scr[...] = jnp.zeros(acc_scr.shape, F32)

    q = q_ref[...]
    q_heads = [q[:, h * FOX_HEAD_DIM:(h + 1) * FOX_HEAD_DIM] for h in range(FOX_HEADS)]
    f_q = fq_scr[...]

    def update(keys, values, f_k, mask, channels_first):
        qk = _dot if channels_first else _dot_nt
        pv_dot = _dot_nt if channels_first else _dot
        tt = jnp.concatenate([qk(q_heads[h], keys[h]) - f_k[h:h + 1, :] for h in range(FOX_HEADS)], axis=0)
        if mask is not None:
            tt = jnp.where(mask, tt, NEG_BIG)
        m_old = m_scr[...]
        m_new = jnp.maximum(m_old, jnp.max(tt, axis=1, keepdims=True) + f_q)
        alpha = jnp.exp2(m_old - m_new)
        pr = jnp.exp2(tt - (m_new - f_q))
        l_scr[...] = alpha * l_scr[...] + jnp.sum(pr, axis=1, keepdims=True)
        prb = pr.astype(BF16)
        pv = jnp.concatenate([pv_dot(prb[h * t:(h + 1) * t, :], values[h]) for h in range(FOX_HEADS)], axis=0)
        acc_scr[...] = alpha * acc_scr[...] + pv
        m_scr[...] = m_new

    head_rows = lambda ref: [ref[h * FOX_HEAD_DIM:(h + 1) * FOX_HEAD_DIM, :].astype(BF16) for h in range(FOX_HEADS)]
    update(head_rows(ck_ref), head_rows(cv_ref), fk_scr[c], None, True)

    @pl.when(c == nchunk - 1)
    def _():
        k_new, v_new = k_ref[...].astype(BF16), v_ref[...].astype(BF16)
        cols = lambda a: [a[:, h * FOX_HEAD_DIM:(h + 1) * FOX_HEAD_DIM] for h in range(FOX_HEADS)]
        rr = lax.broadcasted_iota(jnp.int32, (FOX_HEADS * t, t), 0) % t
        cc = lax.broadcasted_iota(jnp.int32, (FOX_HEADS * t, t), 1)
        update(cols(k_new), cols(v_new), fnew_scr[...], cc <= rr, False)
        out = acc_scr[...] / l_scr[...]
        o_ref[...] = jnp.concatenate([out[h * t:(h + 1) * t, :] for h in range(FOX_HEADS)], axis=1).astype(BF16)


def _fox_sample_call(q, k, v, small, cache_k, cache_v, cache_logf_t, layer):
    bsz, t, _ = q.shape
    past = cache_logf_t.shape[3]
    chunk = min(SAMPLE_CHUNK, past)
    nchunk = past // chunk
    new = lambda w: pl.BlockSpec((None, t, w), lambda b, c: (b, 0, 0))
    cache = pl.BlockSpec((None, None, FOX_WIDTH, chunk), lambda b, c: (layer, b, 0, c))
    rows = FOX_HEADS * t
    return pl.pallas_call(
        _fox_sample_kernel,
        grid=(bsz, nchunk),
        in_specs=[new(FOX_WIDTH), new(FOX_WIDTH), new(FOX_WIDTH), new(LANES), cache, cache,
                  pl.BlockSpec((None, None, FOX_HEADS, past), lambda b, c: (layer, b, 0, 0))],
        out_specs=new(FOX_WIDTH),
        out_shape=jax.ShapeDtypeStruct((bsz, t, FOX_WIDTH), BF16),
        scratch_shapes=[pltpu.VMEM((rows, 1), F32), pltpu.VMEM((rows, 1), F32), pltpu.VMEM((rows, FOX_HEAD_DIM), F32),
                        pltpu.VMEM((nchunk, FOX_HEADS, chunk), F32), pltpu.VMEM((rows, 1), F32),
                        pltpu.VMEM((FOX_HEADS, t), F32)],
        compiler_params=_params(40, ("arbitrary", "arbitrary")),
        name="fox_sample",
    )(q, k, v, small, cache_k, cache_v, cache_logf_t)


def _small_rows(w_in_t):
    offs = np.concatenate([[0], np.cumsum(IN_SIZES)])
    seg = lambda i: w_in_t[:, int(offs[i]):int(offs[i + 1]), :]
    pad = jnp.zeros((w_in_t.shape[0], LANES - FOX_HEADS - SSD_HEADS, w_in_t.shape[2]), w_in_t.dtype)
    return jnp.concatenate([seg(5), seg(8), pad], axis=1)


def _block_diag(w):
    d, h, b, _ = w.shape
    eye = jnp.eye(h, dtype=w.dtype)
    return jnp.einsum("dhij,hg->dhigj", w, eye).reshape(d, h * b, h * b)


def _lane_slab(v, lane0):
    d, k = v.shape
    return jnp.zeros((d, 1, LANES), v.dtype).at[:, 0, lane0:lane0 + k].set(v)


def _pad_history(state):
    return jnp.pad(state, ((0, 0), (0, 0), (SUBLANES - (CONV_W - 1), 0), (0, 0)))


def _trunk(x, mod_group, caches, prm, ssd_chunk):
    bsz, n, _ = x.shape
    x = x.reshape(bsz * n, D_MODEL)
    states = {name: [] for name in ("fox_k", "fox_v", "fox_logf", "lru_conv", "lru_h", "ssd_conv", "ssd_h")}
    prev_kv = None
    for l in range(DEPTH):
        mod4 = mod_group[l].reshape(bsz, N_SUB, 3, D_MODEL)
        x = _ffn_call(x, mod4, prm["npre"], prm["npost"], prm["wg"], prm["wu"], prm["wd"], l, 0, 0, n)
        prompt = caches is None
        stacking = prompt and l == DEPTH - 1
        proj = _inproj_call(x, mod4, prm["npre"], prm["w_in_t"], prm["w_small"], prm["fbias"], l, n,
                            transposed=prompt, prev_kv=prev_kv if stacking else None)
        lrux, lrug, q, k, v, z, xbc, small = proj[:8]
        per_seq = lambda a: a.reshape(bsz, n, a.shape[-1])
        state_layer = None if prompt else l
        src = prm["zero_state"] if prompt else caches
        ya, lru_conv, lru_h = _lru_call(per_seq(lrux), per_seq(lrug), src["lru_conv"], src["lru_h"],
                                        prm["lru_cw"], prm["lru_cb"], prm["lru_wgate"], prm["lru_bgate"],
                                        prm["lru_lam"], l, state_layer)
        if prompt:
            small_t = proj[8]
            yb = _fox_prompt_call(per_seq(q), k, v, small_t, l if stacking else None)
            prev_kv = (k, v)
            logf_out = jnp.swapaxes(small_t[:, F_LANE0:F_LANE0 + FOX_HEADS, :], 1, 2)
        else:
            yb = _fox_sample_call(per_seq(q), per_seq(k), per_seq(v), per_seq(small),
                                  caches["fox_k"], caches["fox_v"], caches["fox_logf_t"], l)
            k_out = k.reshape(bsz, n, FOX_HEADS, FOX_HEAD_DIM)
            v_out = v.reshape(bsz, n, FOX_HEADS, FOX_HEAD_DIM)
            logf_out = small.reshape(bsz, n, LANES)[:, :, F_LANE0:F_LANE0 + FOX_HEADS]
        yc, ssd_conv, ssd_h = _ssd_call(per_seq(xbc), per_seq(z), per_seq(small), src["ssd_conv"], src["ssd_h"],
                                        prm["ssd_cw"], prm["ssd_cb"], prm["ssd_dtb"], prm["ssd_alog"],
                                        prm["ssd_dvec"], prm["ssd_nw"], l, state_layer, ssd_chunk)
        x = _outffn_call(x, ya.reshape(bsz * n, -1), yb.reshape(bsz * n, -1), yc.reshape(bsz * n, -1),
                         mod4, prm["npre"], prm["npost"], prm["w_out"], prm["wg"], prm["wu"], prm["wd"], l, n)
        if not prompt:
            states["fox_k"].append(k_out)
            states["fox_v"].append(v_out)
        states["fox_logf"].append(logf_out)
        states["lru_conv"].append(lru_conv)
        states["lru_h"].append(lru_h.reshape(bsz, LRU_WIDTH))
        states["ssd_conv"].append(ssd_conv)
        states["ssd_h"].append(ssd_h.reshape(bsz, SSD_HEADS, SSD_HEAD_DIM, D_STATE))
    out = {name: jnp.stack(vals, axis=0) for name, vals in states.items() if vals}
    if prev_kv is not None:
        heads_last = lambda a: jnp.transpose(a.reshape(DEPTH, bsz, FOX_HEADS, FOX_HEAD_DIM, n), (0, 1, 4, 2, 3))
        out["fox_k"], out["fox_v"] = heads_last(prev_kv[0]), heads_last(prev_kv[1])
    return x.reshape(bsz, n, D_MODEL), out


def kernel(x_prompt, x_sample, c_prompt, c_sample, cache_fox_k, cache_fox_v, cache_fox_logf, state_lru_conv, state_lru_h, state_ssd_conv, state_ssd_h, w_mod, b_mod, norm_pre, norm_post, ffn_w_gate, ffn_w_up, ffn_w_down, w_in, w_out, lru_conv_w, lru_conv_b, lru_wa, lru_ba, lru_wx, lru_bx, lru_lambda, fox_f_bias, ssd_conv_w, ssd_conv_b, ssd_dt_bias, ssd_a_log, ssd_d, ssd_norm_w):
    n_prompt, n_sample = x_prompt.shape[0], x_sample.shape[0]
    w_in_t = jnp.swapaxes(w_in, 1, 2)
    prm = {
        "npre": norm_pre.reshape(DEPTH, N_SUB, 1, D_MODEL),
        "npost": norm_post.reshape(DEPTH, N_SUB, 1, D_MODEL),
        "wg": ffn_w_gate.astype(BF16), "wu": ffn_w_up.astype(BF16), "wd": ffn_w_down.astype(BF16),
        "w_in_t": w_in_t, "w_small": _small_rows(w_in_t), "w_out": w_out.astype(BF16),
        "fbias": _lane_slab(fox_f_bias, F_LANE0),
        "lru_cw": lru_conv_w, "lru_cb": lru_conv_b.reshape(DEPTH, 1, LRU_WIDTH),
        "lru_wgate": jnp.concatenate([_block_diag(lru_wa), _block_diag(lru_wx)], axis=-1).astype(BF16),
        "lru_bgate": jnp.concatenate([lru_ba, lru_bx], axis=-1).reshape(DEPTH, 1, 2 * LRU_WIDTH),
        "lru_lam": lru_lambda.reshape(DEPTH, 1, LRU_WIDTH),
        "ssd_cw": ssd_conv_w, "ssd_cb": ssd_conv_b.reshape(DEPTH, 1, SSD_CONV_DIM),
        "ssd_dtb": _lane_slab(ssd_dt_bias, DT_LANE0), "ssd_alog": _lane_slab(ssd_a_log, DT_LANE0),
        "ssd_dvec": jnp.repeat(ssd_d, SSD_HEAD_DIM, axis=-1).reshape(DEPTH, 1, SSD_WIDTH),
        "ssd_nw": ssd_norm_w.reshape(DEPTH, 1, SSD_WIDTH),
        "zero_state": {
            "lru_conv": jnp.zeros((1, SUBLANES, LRU_WIDTH), F32), "lru_h": jnp.zeros((1, 1, LRU_WIDTH), F32),
            "ssd_conv": jnp.zeros((1, SUBLANES, SSD_CONV_DIM), F32), "ssd_h": jnp.zeros((1, SSD_WIDTH, D_STATE), F32),
        },
    }
    caches = {
        "fox_k": jnp.transpose(cache_fox_k, (0, 1, 3, 4, 2)).reshape(DEPTH, n_sample, FOX_WIDTH, -1),
        "fox_v": jnp.transpose(cache_fox_v, (0, 1, 3, 4, 2)).reshape(DEPTH, n_sample, FOX_WIDTH, -1),
        "fox_logf_t": jnp.swapaxes(cache_fox_logf, 2, 3),
        "lru_conv": _pad_history(state_lru_conv),
        "lru_h": state_lru_h.reshape(DEPTH, n_sample, 1, LRU_WIDTH),
        "ssd_conv": _pad_history(state_ssd_conv),
        "ssd_h": state_ssd_h.reshape(DEPTH, n_sample, SSD_WIDTH, D_STATE),
    }
    mod = _mod_call(jnp.concatenate([c_prompt, c_sample], axis=0), w_mod, b_mod)
    y_prompt, sp = _trunk(x_prompt, mod[:, :n_prompt], None, prm, ssd_chunk=256)
    y_sample, ss = _trunk(x_sample, mod[:, n_prompt:], caches, prm, ssd_chunk=x_sample.shape[1])
    names = ("fox_k", "fox_v", "fox_logf", "lru_conv", "lru_h", "ssd_conv", "ssd_h")
    return (y_prompt, y_sample) + tuple(sp[n] for n in names) + tuple(ss[n] for n in names)
```

```python
import functools

import numpy as np
import jax
import jax.numpy as jnp
from jax import lax
from jax.experimental import pallas as pl
from jax.experimental.pallas import tpu as pltpu

F32 = jnp.float32
BF16 = jnp.bfloat16

D_MODEL = 1024
DEPTH = 2
CONV_W = 4
EPS = 1e-6
LRU_WIDTH = 256
LRU_HEADS = 4
LRU_BLOCK = LRU_WIDTH // LRU_HEADS
LRU_C = 8.0
FOX_HEADS = 8
FOX_HEAD_DIM = 64
FOX_WIDTH = FOX_HEADS * FOX_HEAD_DIM
SSD_HEADS = 4
SSD_HEAD_DIM = 64
SSD_WIDTH = SSD_HEADS * SSD_HEAD_DIM
SSD_GROUPS = 2
D_STATE = 128
SSD_CONV_DIM = SSD_WIDTH + 2 * SSD_GROUPS * D_STATE
IN_SIZES = (LRU_WIDTH, LRU_WIDTH, FOX_WIDTH, FOX_WIDTH, FOX_WIDTH, FOX_HEADS, SSD_WIDTH, SSD_CONV_DIM, SSD_HEADS)
D_FF = 2816
N_SUB = 3

LANES = 128
SUBLANES = 8
HEAD_PAIR = LANES // FOX_HEAD_DIM

F_LANE0 = 0
DT_LANE0 = FOX_HEADS
COL_LRU_X = 0
COL_LRU_G = COL_LRU_X + LRU_WIDTH
COL_Q = COL_LRU_G + LRU_WIDTH
COL_K = COL_Q + FOX_WIDTH
COL_V = COL_K + FOX_WIDTH
COL_Z = COL_V + FOX_WIDTH
COL_XBC = COL_Z + SSD_WIDTH
COL_SMALL = COL_XBC + SSD_CONV_DIM
D_IN_PAD = COL_SMALL + LANES
D_IN = sum(IN_SIZES)

ROW_TILE = 512
FFN_ROW_TILE = 1024
FF_CHUNK = 512
ATTN_BLOCK = 2048
ATTN_ROWS = 128
ATTN_GROUP = 16
SCAN_BLOCK = 256
SAMPLE_CHUNK = 4096
NEG_BIG = -1e30
LOG2E = 1.4426950408889634
Q_SCALE = LOG2E * FOX_HEAD_DIM ** -0.5


def _dot(a, b):
    return jnp.dot(a, b, preferred_element_type=F32)


def _dot_nt(a, b):
    return lax.dot_general(a, b, (((1,), (1,)), ((), ())), preferred_element_type=F32)


def _dot_tn(a, b):
    return lax.dot_general(a, b, (((0,), (0,)), ((), ())), preferred_element_type=F32)


def _silu(x):
    return x * jax.nn.sigmoid(x)


def _softplus(x):
    return jnp.maximum(x, 0.0) + jnp.log1p(jnp.exp(-jnp.abs(x)))


def _rms(x):
    return x * lax.rsqrt(jnp.mean(x * x, axis=-1, keepdims=True) + EPS)


def _per_seq(rows, per_seq, fn):
    g = per_seq[0].shape[0]
    if g == 1:
        return fn(rows, *per_seq)
    tm, d = rows.shape
    out = fn(rows.reshape(g, tm // g, d), *[p[:, None, :] for p in per_seq])
    return out.reshape(tm, d)


def _pre_norm(x, npre, mod_ref, seqs=slice(None)):
    h = _rms(x) * npre
    return _per_seq(h, (mod_ref[seqs, 1, :], mod_ref[seqs, 0, :]), lambda r, sc, sh: r * (1.0 + sc) + sh)


def _post_norm(x, y, npost, mod_ref, w, seqs=slice(None)):
    yn = _rms(y) * npost
    return x + _per_seq(yn, (mod_ref[seqs, 2, :],), lambda r, gt: (w * gt) * r)


def _seq_grouping(rows_per_seq, tm):
    if rows_per_seq % tm == 0:
        return 1, rows_per_seq // tm
    assert tm % rows_per_seq == 0
    return tm // rows_per_seq, 1


def _mod_spec(g, tiles_per_seq, sub):
    if g == 1:
        return pl.BlockSpec((1, None, 3, D_MODEL), lambda i: (i // tiles_per_seq, sub, 0, 0))
    return pl.BlockSpec((g, None, 3, D_MODEL), lambda i: (i, sub, 0, 0))


def _params(vmem_mb, sem):
    return pltpu.CompilerParams(dimension_semantics=sem, vmem_limit_bytes=vmem_mb << 20)


def _mod_kernel(c_ref, w_ref, b_ref, o_ref):
    a = _silu(c_ref[...]).astype(BF16)
    o_ref[...] = _dot(a, w_ref[...].astype(BF16)) + b_ref[...]


def _mod_call(c_all, w_mod, b_mod):
    nseq = c_all.shape[0]
    width = N_SUB * 3 * D_MODEL
    tn = 1024
    return pl.pallas_call(
        _mod_kernel,
        grid=(DEPTH, width // tn),
        in_specs=[
            pl.BlockSpec((nseq, D_MODEL), lambda l, n: (0, 0)),
            pl.BlockSpec((None, D_MODEL, tn), lambda l, n: (l, 0, n)),
            pl.BlockSpec((None, 1, tn), lambda l, n: (l, 0, n)),
        ],
        out_specs=pl.BlockSpec((None, nseq, tn), lambda l, n: (l, 0, n)),
        out_shape=jax.ShapeDtypeStruct((DEPTH, nseq, width), F32),
        compiler_params=_params(24, ("arbitrary", "arbitrary")),
        name="adaln_mod",
    )(c_all, w_mod, b_mod.reshape(DEPTH, 1, width))


def _halves(tm, nseq):
    half = tm // 2
    rows = [slice(i * half, (i + 1) * half) for i in range(2)]
    seqs = [slice(0, 1)] * 2 if nseq == 1 else [slice(i * nseq // 2, (i + 1) * nseq // 2) for i in range(2)]
    return rows, seqs


def _swiglu(hs, wg_ref, wu_ref, wd_ref):
    accs = [None] * len(hs)
    for off in range(0, D_FF, FF_CHUNK):
        fc = min(FF_CHUNK, D_FF - off)
        for i, h in enumerate(hs):
            g = _dot(h, wg_ref[:, off:off + fc])
            u = _dot(h, wu_ref[:, off:off + fc])
            a = (_silu(g) * u).astype(BF16)
            d = _dot(a, wd_ref[off:off + fc, :])
            accs[i] = d if accs[i] is None else accs[i] + d
    return accs


def _ffn_kernel(x_ref, mod_ref, npre_ref, npost_ref, wg_ref, wu_ref, wd_ref, o_ref):
    rows, seqs = _halves(x_ref.shape[0], mod_ref.shape[0])
    xs = [x_ref[r, :] for r in rows]
    hs = [_pre_norm(x, npre_ref[...], mod_ref, sq).astype(BF16) for x, sq in zip(xs, seqs)]
    accs = _swiglu(hs, wg_ref, wu_ref, wd_ref)
    for i, r in enumerate(rows):
        o_ref[r, :] = _post_norm(xs[i], accs[i], npost_ref[...], mod_ref, 0.5, seqs[i])


def _outffn_kernel(x_ref, ya_ref, yb_ref, yc_ref, mod1_ref, mod2_ref, npost1_ref, npre2_ref, npost2_ref,
                   wo_ref, wg_ref, wu_ref, wd_ref, o_ref):
    rows, seqs = _halves(x_ref.shape[0], mod1_ref.shape[0])
    xs = []
    for r, sq in zip(rows, seqs):
        y = (_dot(ya_ref[r, :], wo_ref[0:LRU_WIDTH, :])
             + _dot(yb_ref[r, :], wo_ref[LRU_WIDTH:LRU_WIDTH + FOX_WIDTH, :])
             + _dot(yc_ref[r, :], wo_ref[LRU_WIDTH + FOX_WIDTH:, :]))
        xs.append(_post_norm(x_ref[r, :], y, npost1_ref[...], mod1_ref, 1.0, sq))
    hs = [_pre_norm(x, npre2_ref[...], mod2_ref, sq).astype(BF16) for x, sq in zip(xs, seqs)]
    accs = _swiglu(hs, wg_ref, wu_ref, wd_ref)
    for i, r in enumerate(rows):
        o_ref[r, :] = _post_norm(xs[i], accs[i], npost2_ref[...], mod2_ref, 0.5, seqs[i])


def _outffn_call(x, ya, yb, yc, mod4, npre, npost, w_out, wg, wu, wd, layer, rows_per_seq):
    m = x.shape[0]
    tm = min(FFN_ROW_TILE, m)
    g, tps = _seq_grouping(rows_per_seq, tm)
    row = lambda w: pl.BlockSpec((tm, w), lambda i: (i, 0))
    once = dict(pipeline_mode=pl.Buffered(1))
    wspec = lambda shape: pl.BlockSpec((None, None) + shape, lambda i: (layer, 1, 0, 0), **once)
    norm = lambda sub: pl.BlockSpec((None, None, 1, D_MODEL), lambda i: (layer, sub, 0, 0))
    return pl.pallas_call(
        _outffn_kernel,
        grid=(m // tm,),
        in_specs=[
            row(D_MODEL), row(LRU_WIDTH), row(FOX_WIDTH), row(SSD_WIDTH),
            _mod_spec(g, tps, 1), _mod_spec(g, tps, 2),
            norm(1), norm(2), norm(2),
            pl.BlockSpec((None, D_MODEL, D_MODEL), lambda i: (layer, 0, 0), **once),
            wspec((D_MODEL, D_FF)), wspec((D_MODEL, D_FF)), wspec((D_FF, D_MODEL)),
        ],
        out_specs=row(D_MODEL),
        out_shape=jax.ShapeDtypeStruct((m, D_MODEL), F32),
        compiler_params=_params(57, ("arbitrary",)),
        name="outproj_ffn",
    )(x, ya, yb, yc, mod4, mod4, npost, npre, npost, w_out, wg, wu, wd)


def _ffn_call(x, mod4, npre, npost, wg, wu, wd, layer, sub, ffn_idx, rows_per_seq):
    m = x.shape[0]
    tm = min(FFN_ROW_TILE, m)
    g, tps = _seq_grouping(rows_per_seq, tm)
    wspec = lambda shape: pl.BlockSpec((None, None) + shape, lambda i: (layer, ffn_idx, 0, 0),
                                       pipeline_mode=pl.Buffered(1))
    nspec = pl.BlockSpec((None, None, 1, D_MODEL), lambda i: (layer, sub, 0, 0))
    return pl.pallas_call(
        _ffn_kernel,
        grid=(m // tm,),
        in_specs=[
            pl.BlockSpec((tm, D_MODEL), lambda i: (i, 0)),
            _mod_spec(g, tps, sub),
            nspec, nspec,
            wspec((D_MODEL, D_FF)), wspec((D_MODEL, D_FF)), wspec((D_FF, D_MODEL)),
        ],
        out_specs=pl.BlockSpec((tm, D_MODEL), lambda i: (i, 0)),
        out_shape=jax.ShapeDtypeStruct((m, D_MODEL), F32),
        compiler_params=_params(56, ("arbitrary",)),
        name="ffn",
    )(x, mod4, npre, npost, wg, wu, wd)


def _inproj_kernel(*refs, transposed, stacked):
    n_in = 8 if stacked else 6
    x_ref, mod_ref, npre_ref, w_ref, wsmall_ref, fbias_ref = refs[:6]
    lrux_ref, lrug_ref, q_ref, k_ref, v_ref, z_ref, xbc_ref, small_ref, *rest = refs[n_in:]
    wt_ref = rest[-1]

    @pl.when(pl.program_id(0) == 0)
    def _():
        src = int(np.cumsum(IN_SIZES)[4])
        assert src == COL_Z
        moves = [(0, 0, src), (src + FOX_HEADS, COL_Z, SSD_WIDTH), (src + FOX_HEADS + SSD_WIDTH, COL_XBC, SSD_CONV_DIM)]
        for s0, d0, n in moves:
            for off in range(0, n, LRU_WIDTH):
                wt_ref[d0 + off:d0 + off + LRU_WIDTH, :] = w_ref[s0 + off:s0 + off + LRU_WIDTH, :].astype(BF16)
        wt_ref[COL_SMALL:COL_SMALL + LANES, :] = wsmall_ref[...].astype(BF16)

    h = _pre_norm(x_ref[...], npre_ref[...], mod_ref).astype(BF16)
    col = lambda start, width: _dot_nt(h, wt_ref[start:start + width, :])
    col_t = lambda start, width: _dot_nt(wt_ref[start:start + width, :], h)
    lrux_ref[...] = col(COL_LRU_X, LRU_WIDTH)
    lrug_ref[...] = col(COL_LRU_G, LRU_WIDTH)
    q_ref[...] = (col(COL_Q, FOX_WIDTH) * Q_SCALE).astype(BF16)
    if transposed:
        kv_t = col_t(COL_K, 2 * FOX_WIDTH)
        if stacked:
            k_ref[0], v_ref[0] = refs[6][...], refs[7][...]
            k_ref[1], v_ref[1] = kv_t[:FOX_WIDTH], kv_t[FOX_WIDTH:]
        else:
            k_ref[...] = kv_t[:FOX_WIDTH]
            v_ref[...] = kv_t[FOX_WIDTH:]
    else:
        k_ref[...] = col(COL_K, FOX_WIDTH)
        v_ref[...] = col(COL_V, FOX_WIDTH)
    z_ref[...] = col(COL_Z, SSD_WIDTH)
    xbc_ref[...] = col(COL_XBC, SSD_CONV_DIM)
    small = col(COL_SMALL, LANES)
    t = small + fbias_ref[...]
    logf = jnp.minimum(t, 0.0) - jnp.log1p(jnp.exp(-jnp.abs(t)))
    lane = lax.broadcasted_iota(jnp.int32, small.shape, 1)
    small = jnp.where(lane < DT_LANE0, logf, small)
    small_ref[...] = small
    if transposed:
        rest[0][...] = small.T


def _inproj_call(x, mod4, npre, w_in_t, w_small, fbias, layer, rows_per_seq, transposed, prev_kv=None):
    m = x.shape[0]
    tm = min(ROW_TILE, m)
    g, tps = _seq_grouping(rows_per_seq, tm)
    widths = (LRU_WIDTH, LRU_WIDTH, FOX_WIDTH, FOX_WIDTH, FOX_WIDTH, SSD_WIDTH, SSD_CONV_DIM, LANES)
    dtypes = (F32, F32, BF16, F32, F32, F32, F32, F32)
    out_specs = [pl.BlockSpec((tm, w), lambda i: (i, 0)) for w in widths]
    out_shape = [jax.ShapeDtypeStruct((m, w), dt) for w, dt in zip(widths, dtypes)]
    in_specs = [
        pl.BlockSpec((tm, D_MODEL), lambda i: (i, 0)),
        _mod_spec(g, tps, 1),
        pl.BlockSpec((None, None, 1, D_MODEL), lambda i: (layer, 1, 0, 0)),
        pl.BlockSpec((None, D_IN, D_MODEL), lambda i: (layer, 0, 0), pipeline_mode=pl.Buffered(1)),
        pl.BlockSpec((None, LANES, D_MODEL), lambda i: (layer, 0, 0)),
        pl.BlockSpec((None, 1, LANES), lambda i: (layer, 0, 0)),
    ]
    if transposed:
        assert g == 1
        nseq = m // rows_per_seq
        for idx, w in ((3, FOX_WIDTH), (4, FOX_WIDTH), (len(widths), LANES)):
            spec = pl.BlockSpec((None, w, tm), lambda i: (i // tps, 0, i % tps))
            shape = jax.ShapeDtypeStruct((nseq, w, rows_per_seq), F32)
            out_specs[idx:idx + 1], out_shape[idx:idx + 1] = [spec], [shape]
    extra = ()
    if prev_kv is not None:
        assert transposed and layer == 1 and DEPTH == 2
        extra = tuple(prev_kv)
        in_specs += [pl.BlockSpec((None, FOX_WIDTH, tm), lambda i: (i // tps, 0, i % tps))] * 2
        for idx in (3, 4):
            out_specs[idx] = pl.BlockSpec((DEPTH, None, FOX_WIDTH, tm), lambda i: (0, i // tps, 0, i % tps))
            out_shape[idx] = jax.ShapeDtypeStruct((DEPTH, nseq, FOX_WIDTH, rows_per_seq), F32)
    return pl.pallas_call(
        functools.partial(_inproj_kernel, transposed=transposed, stacked=prev_kv is not None),
        grid=(m // tm,),
        in_specs=in_specs,
        out_specs=out_specs,
        out_shape=out_shape,
        scratch_shapes=[pltpu.VMEM((D_IN_PAD, D_MODEL), BF16)],
        compiler_params=_params(56, ("arbitrary",)),
        name="inproj",
    )(x, mod4, npre, w_in_t, w_small, fbias, *extra)


def _causal_conv(xx, cw, cb, n):
    u = cb + cw[0:1] * pltpu.roll(xx, 3, 0)[SUBLANES:SUBLANES + n]
    u = u + cw[1:2] * pltpu.roll(xx, 2, 0)[SUBLANES:SUBLANES + n]
    u = u + cw[2:3] * pltpu.roll(xx, 1, 0)[SUBLANES:SUBLANES + n]
    return u + cw[3:4] * xx[SUBLANES:SUBLANES + n]


def _last_rows(xx, k):
    return pltpu.roll(xx, k, 0)[0:SUBLANES][0:k]


def _tri(n, lower):
    r = lax.broadcasted_iota(jnp.int32, (n, n), 0)
    c = lax.broadcasted_iota(jnp.int32, (n, n), 1)
    return ((r >= c) if lower else (r <= c)).astype(F32).astype(BF16)


def _split3(x):
    hi = x.astype(BF16)
    r1 = x - hi.astype(F32)
    mid = r1.astype(BF16)
    lo = (r1 - mid.astype(F32)).astype(BF16)
    return hi, mid, lo


def _cumsum_lanes(x, triu):
    hi, mid, lo = _split3(x)
    return _dot(hi, triu) + _dot(mid, triu) + _dot(lo, triu)


def _cumsum_rows(x, tril):
    hi, mid, lo = _split3(x)
    return _dot(tril, hi) + _dot(tril, mid) + _dot(tril, lo)


def _lru_kernel(x_ref, g_ref, prev_ref, h0_ref, cw_ref, cb_ref, wgate_ref, bgate_ref, lam_ref,
                y_ref, convnew_ref, hnew_ref, a_scr, b_scr):
    n = x_ref.shape[0]
    xx = jnp.concatenate([prev_ref[...], x_ref[...]], axis=0)
    convnew_ref[...] = _last_rows(xx, CONV_W - 1)
    u = _causal_conv(xx, cw_ref[...], cb_ref[...], n)
    gates = _dot(u.astype(BF16), wgate_ref[...]) + bgate_ref[...]
    r = jax.nn.sigmoid(gates[:, :LRU_WIDTH])
    i = jax.nn.sigmoid(gates[:, LRU_WIDTH:])
    log_a = (-LRU_C * r) * _softplus(-lam_ref[...])
    a = jnp.exp(log_a)
    b = jnp.sqrt(-jnp.tanh(log_a) * (a * a + 1.0)) * (i * u)
    a = a.reshape(n // SUBLANES, SUBLANES, LRU_WIDTH)
    b = b.reshape(n // SUBLANES, SUBLANES, LRU_WIDTH)
    row = lax.broadcasted_iota(jnp.int32, (1, SUBLANES, 1), 1)
    for d in (1, 2, 4):
        keep = row >= d
        b = jnp.where(keep, a * pltpu.roll(b, d, 1) + b, b)
        a = jnp.where(keep, a * pltpu.roll(a, d, 1), a)
    a_scr[...] = a.reshape(n, LRU_WIDTH)
    b_scr[...] = b.reshape(n, LRU_WIDTH)

    def group(j, h):
        off = pl.multiple_of(j * SUBLANES, SUBLANES)
        hb = a_scr[pl.ds(off, SUBLANES), :] * h + b_scr[pl.ds(off, SUBLANES), :]
        b_scr[pl.ds(off, SUBLANES), :] = hb
        return jnp.broadcast_to(hb[SUBLANES - 1:SUBLANES, :], hb.shape)

    h_last = lax.fori_loop(0, n // SUBLANES, group,
                           jnp.broadcast_to(h0_ref[...], (SUBLANES, LRU_WIDTH)), unroll=4)
    hnew_ref[...] = h_last[0:1]
    y_ref[...] = (b_scr[...] * jax.nn.gelu(g_ref[...])).astype(BF16)


def _lru_call(lru_x, lru_g, prev8, h0, cw, cb, wgate, bgate, lam, layer, state_layer):
    bsz, n, _ = lru_x.shape
    seq = pl.BlockSpec((None, n, LRU_WIDTH), lambda b: (b, 0, 0))
    if state_layer is None:
        prev_spec = pl.BlockSpec((None, SUBLANES, LRU_WIDTH), lambda b: (0, 0, 0))
        h0_spec = pl.BlockSpec((None, 1, LRU_WIDTH), lambda b: (0, 0, 0))
    else:
        prev_spec = pl.BlockSpec((None, None, SUBLANES, LRU_WIDTH), lambda b: (state_layer, b, 0, 0))
        h0_spec = pl.BlockSpec((None, None, 1, LRU_WIDTH), lambda b: (state_layer, b, 0, 0))
    par = lambda r, w: pl.BlockSpec((None, r, w), lambda b: (layer, 0, 0))
    return pl.pallas_call(
        _lru_kernel,
        grid=(bsz,),
        in_specs=[seq, seq, prev_spec, h0_spec, par(CONV_W, LRU_WIDTH), par(1, LRU_WIDTH),
                  par(LRU_WIDTH, 2 * LRU_WIDTH), par(1, 2 * LRU_WIDTH), par(1, LRU_WIDTH)],
        out_specs=[seq,
                   pl.BlockSpec((None, CONV_W - 1, LRU_WIDTH), lambda b: (b, 0, 0)),
                   pl.BlockSpec((None, 1, LRU_WIDTH), lambda b: (b, 0, 0))],
        out_shape=[jax.ShapeDtypeStruct((bsz, n, LRU_WIDTH), BF16),
                   jax.ShapeDtypeStruct((bsz, CONV_W - 1, LRU_WIDTH), F32),
                   jax.ShapeDtypeStruct((bsz, 1, LRU_WIDTH), F32)],
        scratch_shapes=[pltpu.VMEM((n, LRU_WIDTH), F32), pltpu.VMEM((n, LRU_WIDTH), F32)],
        compiler_params=_params(40, ("arbitrary",)),
        name="rg_lru",
    )(lru_x, lru_g, prev8, h0, cw, cb, wgate, bgate, lam)


def _ssd_kernel(xbc_ref, z_ref, small_ref, prev_ref, h0_ref, cw_ref, cb_ref, dtb_ref, alog_ref, dvec_ref, nw_ref,
                y_ref, convnew_ref, h_ref, *, chunk):
    h_ref[...] = h0_ref[...]

    def one_chunk(c, tail):
        rows = pl.ds(pl.multiple_of(c * chunk, chunk), chunk)
        xx = jnp.concatenate([tail, xbc_ref[rows, :]], axis=0)
        _ssd_chunk(xx, z_ref[rows, :], small_ref[rows, :], cw_ref, cb_ref, dtb_ref, alog_ref, dvec_ref, nw_ref,
                   y_ref.at[rows, :], h_ref)
        return xx[chunk:chunk + SUBLANES]

    tail = lax.fori_loop(0, xbc_ref.shape[0] // chunk, one_chunk, prev_ref[...])
    convnew_ref[...] = _last_rows(tail, CONV_W - 1)


def _ssd_chunk(xx, z, small, cw_ref, cb_ref, dtb_ref, alog_ref, dvec_ref, nw_ref, y_ref, h_ref):
    n = xx.shape[0] - SUBLANES
    act = _silu(_causal_conv(xx, cw_ref[...], cb_ref[...], n))
    xs = act[:, :SSD_WIDTH]
    bm = act[:, SSD_WIDTH:SSD_WIDTH + SSD_GROUPS * D_STATE]
    cm = act[:, SSD_WIDTH + SSD_GROUPS * D_STATE:]

    dt = _softplus(small + dtb_ref[...])
    dta = dt * (-jnp.exp(alog_ref[...]))
    cum = _cumsum_rows(dta, _tri(n, True)) * LOG2E
    cum_t = cum.T
    rr = lax.broadcasted_iota(jnp.int32, (n, n), 0)
    cc = lax.broadcasted_iota(jnp.int32, (n, n), 1)
    causal = rr >= cc
    lo_lane = lax.broadcasted_iota(jnp.int32, (1, LANES), 1) < SSD_HEAD_DIM
    lo_row = lax.broadcasted_iota(jnp.int32, (LANES, 1), 0) < SSD_HEAD_DIM
    dvec = dvec_ref[...]

    ys = []
    for g in range(SSD_GROUPS):
        sl = slice(g * LANES, (g + 1) * LANES)
        xg, bg, cg = xs[:, sl], bm[:, sl].astype(BF16), cm[:, sl].astype(BF16)
        heads = (2 * g, 2 * g + 1)
        col = lambda a, h: a[:, DT_LANE0 + h:DT_LANE0 + h + 1]
        pick = lambda f: jnp.where(lo_lane, f(heads[0]), f(heads[1]))
        dx = xg * pick(lambda h: col(dt, h))
        dxb = dx.astype(BF16)
        cb_mat = _dot_nt(cg, bg)
        yd = []
        for h in heads:
            seg = col(cum, h) - cum_t[DT_LANE0 + h:DT_LANE0 + h + 1, :]
            lmat = jnp.exp2(jnp.where(causal, seg, NEG_BIG))
            yd.append(_dot((cb_mat * lmat).astype(BF16), dxb))
        y_diag = jnp.where(lo_lane, yd[0], yd[1])
        last = lambda h: col(cum, h)[n - 1:n, :]
        decay_end = pick(lambda h: jnp.exp2(last(h) - col(cum, h)))
        states = _dot_tn((dx * decay_end).astype(BF16), bg)
        h_prev = h_ref[sl, :]
        y_off = _dot_nt(cg, h_prev.astype(BF16)) * pick(lambda h: jnp.exp2(col(cum, h)))
        chunk_decay = jnp.where(lo_row, jnp.exp2(last(heads[0])), jnp.exp2(last(heads[1])))
        h_ref[sl, :] = chunk_decay * h_prev + states
        ys.append(y_diag + y_off + dvec[:, sl] * xg)
    y = jnp.concatenate(ys, axis=1)
    y_ref[...] = (_rms(y * _silu(z)) * nw_ref[...]).astype(BF16)


def _ssd_call(xbc, z, small, prev8, h0, cw, cb, dtb, alog, dvec, nw, layer, state_layer, chunk):
    bsz, n, _ = xbc.shape
    seq = lambda w: pl.BlockSpec((None, n, w), lambda b: (b, 0, 0))
    if state_layer is None:
        prev_spec = pl.BlockSpec((None, SUBLANES, SSD_CONV_DIM), lambda b: (0, 0, 0))
        h0_spec = pl.BlockSpec((None, SSD_WIDTH, D_STATE), lambda b: (0, 0, 0))
    else:
        prev_spec = pl.BlockSpec((None, None, SUBLANES, SSD_CONV_DIM), lambda b: (state_layer, b, 0, 0))
        h0_spec = pl.BlockSpec((None, None, SSD_WIDTH, D_STATE), lambda b: (state_layer, b, 0, 0))
    par = lambda r, w: pl.BlockSpec((None, r, w), lambda b: (layer, 0, 0))
    return pl.pallas_call(
        functools.partial(_ssd_kernel, chunk=chunk),
        grid=(bsz,),
        in_specs=[seq(SSD_CONV_DIM), seq(SSD_WIDTH), seq(LANES), prev_spec, h0_spec,
                  par(CONV_W, SSD_CONV_DIM), par(1, SSD_CONV_DIM), par(1, LANES), par(1, LANES),
                  par(1, SSD_WIDTH), par(1, SSD_WIDTH)],
        out_specs=[seq(SSD_WIDTH),
                   pl.BlockSpec((None, CONV_W - 1, SSD_CONV_DIM), lambda b: (b, 0, 0)),
                   pl.BlockSpec((None, SSD_WIDTH, D_STATE), lambda b: (b, 0, 0))],
        out_shape=[jax.ShapeDtypeStruct((bsz, n, SSD_WIDTH), BF16),
                   jax.ShapeDtypeStruct((bsz, CONV_W - 1, SSD_CONV_DIM), F32),
                   jax.ShapeDtypeStruct((bsz, SSD_WIDTH, D_STATE), F32)],
        compiler_params=_params(40, ("arbitrary",)),
        name="ssd",
    )(xbc, z, small, prev8, h0, cw, cb, dtb, alog, dvec, nw)


def _fox_prompt_kernel(q_ref, kt_ref, vt_ref, logft_ref, o_ref, k_scr, v_scr, f_scr, ft_scr):
    s = kt_ref.shape[1]
    tq = k_scr.shape[3]
    p = pl.program_id(1)
    heads = (HEAD_PAIR * p, HEAD_PAIR * p + 1)
    spare = (FOX_HEAD_DIM, 0)

    @pl.when(p == 0)
    def _():
        triu = _tri(SCAN_BLOCK, False)
        carry = jnp.zeros((LANES, 1), F32)
        per = tq // SCAN_BLOCK
        for c in range(s // SCAN_BLOCK):
            cols = slice(c * SCAN_BLOCK, (c + 1) * SCAN_BLOCK)
            fc = _cumsum_lanes(logft_ref[:, cols], triu) + carry
            ft_scr[c // per, :, (c % per) * SCAN_BLOCK:(c % per + 1) * SCAN_BLOCK] = fc
            f_scr[cols, :] = fc.T
            carry = fc[:, SCAN_BLOCK - 1:SCAN_BLOCK]

    row = lax.broadcasted_iota(jnp.int32, (LANES, 1), 0)
    for j in range(s // tq):
        kt = kt_ref[:, j * tq:(j + 1) * tq]
        vt = vt_ref[:, j * tq:(j + 1) * tq]
        for hl in range(HEAD_PAIR):
            own = (row < FOX_HEAD_DIM) if hl == 0 else (row >= FOX_HEAD_DIM)
            f_k = ft_scr[j, pl.ds(F_LANE0 + heads[hl], 1), :] * LOG2E
            kc = jnp.where(own, kt, 0.0).astype(BF16)
            for i, part in enumerate(_split3(-f_k)):
                kc = jnp.where(row == spare[hl] + i, part, kc)
            k_scr[hl, j] = kc
            v_scr[hl, j] = jnp.where(own, vt, jnp.where(row == spare[hl], 1.0, 0.0)).astype(BF16)

    lane = lax.broadcasted_iota(jnp.int32, (1, LANES), 1)
    lo_lane = lane < FOX_HEAD_DIM
    rb = min(ATTN_ROWS, tq)
    nrb = tq // rb

    def query_block(qi, _):
        rows = [pl.ds(pl.multiple_of(qi * tq + r * rb, rb), rb) for r in range(nrb)]
        q_blocks, fq_blocks = [], []
        for r in range(nrb):
            q = q_ref[rows[r], :].astype(F32)
            ones_at = lambda l0: jnp.where((lane >= l0) & (lane < l0 + 3), 1.0, 0.0)
            q_blocks.append((jnp.where(lo_lane, q, ones_at(spare[0])).astype(BF16),
                             jnp.where(lo_lane, ones_at(spare[1]), q).astype(BF16)))
            f_rows = f_scr[rows[r], :]
            fq_blocks.append([jnp.sum(jnp.where(lane == F_LANE0 + h, f_rows, 0.0), axis=1, keepdims=True) * LOG2E
                              for h in heads])
        for g0 in range(0, nrb, ATTN_GROUP):
            rs = tuple(range(g0, min(g0 + ATTN_GROUP, nrb)))
            carry = lax.fori_loop(0, qi, lambda j, cr: step(j, cr, rs, q_blocks, fq_blocks, False),
                                  (init_rows,) * len(rs))
            carry = step(qi, carry, rs, q_blocks, fq_blocks, True)
            for i, r in enumerate(rs):
                acc = [carry[i][2 * hl + 1] for hl in range(HEAD_PAIR)]
                out = [a / a[:, spare[hl]:spare[hl] + 1] for hl, a in enumerate(acc)]
                o_ref[rows[r], :] = jnp.where(lo_lane, out[0], out[1]).astype(BF16)
        return 0

    def softmax_unit(t, f_q, m_old, row0):
        if row0 is not None:
            rr = lax.broadcasted_iota(jnp.int32, t.shape, 0) + row0
            cc = lax.broadcasted_iota(jnp.int32, t.shape, 1)
            t = jnp.where(cc <= rr, t, NEG_BIG)
        m_new = jnp.maximum(m_old, jnp.max(t, axis=1, keepdims=True) + f_q)
        return m_new, jnp.exp2(m_old - m_new), jnp.exp2(t - (m_new - f_q)).astype(BF16)

    def step(j, carry, rs, q_blocks, fq_blocks, diagonal):
        nks = {r: (r + 1) * rb if diagonal else tq for r in rs}
        units = [(i, r, hl) for i, r in enumerate(rs) for hl in range(HEAD_PAIR)]
        dots = {(r, hl): _dot(q_blocks[r][hl], k_scr[hl, j, :, 0:nks[r]]) for _, r, hl in units}
        soft = {(r, hl): softmax_unit(dots[(r, hl)], fq_blocks[r][hl], carry[i][2 * hl], r * rb if diagonal else None)
                for i, r, hl in units}
        pvs = {(r, hl): _dot_nt(soft[(r, hl)][2], v_scr[hl, j, :, 0:nks[r]]) for _, r, hl in units}
        new = []
        for i, r in enumerate(rs):
            row_state = ()
            for hl in range(HEAD_PAIR):
                m_new, alpha, _ = soft[(r, hl)]
                row_state += (m_new, alpha * carry[i][2 * hl + 1] + pvs[(r, hl)])
            new.append(row_state)
        return tuple(new)

    init_rows = (jnp.full((rb, 1), NEG_BIG, F32), jnp.zeros((rb, LANES), F32)) * HEAD_PAIR
    lax.fori_loop(0, s // tq, query_block, 0)


def _fox_prompt_call(q, kt, vt, small_t, layer=None):
    bsz, s, _ = q.shape
    tq = min(ATTN_BLOCK, s)
    blk = pl.BlockSpec((None, s, LANES), lambda b, p: (b, 0, p))
    if layer is None:
        full = pl.BlockSpec((None, LANES, s), lambda b, p: (b, p, 0))
    else:
        full = pl.BlockSpec((None, None, LANES, s), lambda b, p: (layer, b, p, 0))
    return pl.pallas_call(
        _fox_prompt_kernel,
        grid=(bsz, FOX_WIDTH // LANES),
        in_specs=[blk, full, full, pl.BlockSpec((None, LANES, s), lambda b, p: (b, 0, 0))],
        out_specs=blk,
        out_shape=jax.ShapeDtypeStruct((bsz, s, FOX_WIDTH), BF16),
        scratch_shapes=[pltpu.VMEM((HEAD_PAIR, s // tq, LANES, tq), BF16),
                        pltpu.VMEM((HEAD_PAIR, s // tq, LANES, tq), BF16),
                        pltpu.VMEM((s, LANES), F32), pltpu.VMEM((s // tq, LANES, tq), F32)],
        compiler_params=_params(40, ("arbitrary", "arbitrary")),
        name="fox_prompt",
    )(q, kt, vt, small_t)


def _fox_sample_kernel(q_ref, k_ref, v_ref, logf_ref, ck_ref, cv_ref, clogft_ref, o_ref,
                       m_scr, l_scr, acc_scr, fk_scr, fq_scr, fnew_scr):
    t = q_ref.shape[0]
    c = pl.program_id(1)
    nchunk, _, chunk = fk_scr.shape

    @pl.when(c == 0)
    def _():
        blk = min(SCAN_BLOCK, chunk)
        per = chunk // blk
        nblk = nchunk * per
        stacked = jnp.concatenate([clogft_ref[:, i * blk:(i + 1) * blk] for i in range(nblk)], axis=0)
        local = _cumsum_lanes(stacked, _tri(blk, False))
        nrow = nblk * FOX_HEADS
        rr = lax.broadcasted_iota(jnp.int32, (nrow, nrow), 0)
        cc = lax.broadcasted_iota(jnp.int32, (nrow, nrow), 1)
        earlier = ((rr % FOX_HEADS == cc % FOX_HEADS) & (cc < rr - rr % FOX_HEADS)).astype(F32).astype(BF16)
        totals = jnp.broadcast_to(local[:, blk - 1:blk], (nrow, LANES))
        f_all = local + _cumsum_rows(totals, earlier)[:, 0:1]
        for i in range(nblk):
            fk_scr[i // per, :, (i % per) * blk:(i % per + 1) * blk] = f_all[i * FOX_HEADS:(i + 1) * FOX_HEADS, :] * LOG2E
        carry = f_all[nrow - FOX_HEADS:, blk - 1:blk]
        sub = lax.broadcasted_iota(jnp.int32, (FOX_HEADS, LANES), 0)
        ln = lax.broadcasted_iota(jnp.int32, (FOX_HEADS, LANES), 1)
        total_row = jnp.sum(jnp.where(sub == ln - F_LANE0, carry, 0.0), axis=0, keepdims=True)
        f_new = _cumsum_rows(logf_ref[...], _tri(t, True)) + total_row
        f_new = f_new * LOG2E
        fnew_scr[...] = f_new.T[F_LANE0:F_LANE0 + FOX_HEADS, :]
        fq_scr[...] = jnp.concatenate([f_new[:, F_LANE0 + h:F_LANE0 + h + 1] for h in range(FOX_HEADS)], axis=0)
        m_scr[...] = jnp.full(m_scr.shape, NEG_BIG, F32)
        l_scr[...] = jnp.zeros(l_scr.shape, F32)
        acc_scr[...] = jnp.zeros(acc_scr.shape, F32)

    q = q_ref[...]
    q_heads = [q[:, h * FOX_HEAD_DIM:(h + 1) * FOX_HEAD_DIM] for h in range(FOX_HEADS)]
    f_q = fq_scr[...]

    def update(keys, values, f_k, mask, channels_first):
        qk = _dot if channels_first else _dot_nt
        pv_dot = _dot_nt if channels_first else _dot
        tt = jnp.concatenate([qk(q_heads[h], keys[h]) - f_k[h:h + 1, :] for h in range(FOX_HEADS)], axis=0)
        if mask is not None:
            tt = jnp.where(mask, tt, NEG_BIG)
        m_old = m_scr[...]
        m_new = jnp.maximum(m_old, jnp.max(tt, axis=1, keepdims=True) + f_q)
        alpha = jnp.exp2(m_old - m_new)
        pr = jnp.exp2(tt - (m_new - f_q))
        l_scr[...] = alpha * l_scr[...] + jnp.sum(pr, axis=1, keepdims=True)
        prb = pr.astype(BF16)
        pv = jnp.concatenate([pv_dot(prb[h * t:(h + 1) * t, :], values[h]) for h in range(FOX_HEADS)], axis=0)
        acc_scr[...] = alpha * acc_scr[...] + pv
        m_scr[...] = m_new

    head_rows = lambda ref: [ref[h * FOX_HEAD_DIM:(h + 1) * FOX_HEAD_DIM, :].astype(BF16) for h in range(FOX_HEADS)]
    update(head_rows(ck_ref), head_rows(cv_ref), fk_scr[c], None, True)

    @pl.when(c == nchunk - 1)
    def _():
        k_new, v_new = k_ref[...].astype(BF16), v_ref[...].astype(BF16)
        cols = lambda a: [a[:, h * FOX_HEAD_DIM:(h + 1) * FOX_HEAD_DIM] for h in range(FOX_HEADS)]
        rr = lax.broadcasted_iota(jnp.int32, (FOX_HEADS * t, t), 0) % t
        cc = lax.broadcasted_iota(jnp.int32, (FOX_HEADS * t, t), 1)
        update(cols(k_new), cols(v_new), fnew_scr[...], cc <= rr, False)
        out = acc_scr[...] / l_scr[...]
        o_ref[...] = jnp.concatenate([out[h * t:(h + 1) * t, :] for h in range(FOX_HEADS)], axis=1).astype(BF16)


def _fox_sample_call(q, k, v, small, cache_k, cache_v, cache_logf_t, layer):
    bsz, t, _ = q.shape
    past = cache_logf_t.shape[3]
    chunk = min(SAMPLE_CHUNK, past)
    nchunk = past // chunk
    new = lambda w: pl.BlockSpec((None, t, w), lambda b, c: (b, 0, 0))
    cache = pl.BlockSpec((None, None, FOX_WIDTH, chunk), lambda b, c: (layer, b, 0, c))
    rows = FOX_HEADS * t
    return pl.pallas_call(
        _fox_sample_kernel,
        grid=(bsz, nchunk),
        in_specs=[new(FOX_WIDTH), new(FOX_WIDTH), new(FOX_WIDTH), new(LANES), cache, cache,
                  pl.BlockSpec((None, None, FOX_HEADS, past), lambda b, c: (layer, b, 0, 0))],
        out_specs=new(FOX_WIDTH),
        out_shape=jax.ShapeDtypeStruct((bsz, t, FOX_WIDTH), BF16),
        scratch_shapes=[pltpu.VMEM((rows, 1), F32), pltpu.VMEM((rows, 1), F32), pltpu.VMEM((rows, FOX_HEAD_DIM), F32),
                        pltpu.VMEM((nchunk, FOX_HEADS, chunk), F32), pltpu.VMEM((rows, 1), F32),
                        pltpu.VMEM((FOX_HEADS, t), F32)],
        compiler_params=_params(56, ("arbitrary", "arbitrary")),
        name="fox_sample",
    )(q, k, v, small, cache_k, cache_v, cache_logf_t)


def _small_rows(w_in_t):
    offs = np.concatenate([[0], np.cumsum(IN_SIZES)])
    seg = lambda i: w_in_t[:, int(offs[i]):int(offs[i + 1]), :]
    pad = jnp.zeros((w_in_t.shape[0], LANES - FOX_HEADS - SSD_HEADS, w_in_t.shape[2]), w_in_t.dtype)
    return jnp.concatenate([seg(5), seg(8), pad], axis=1)


def _block_diag(w):
    d, h, b, _ = w.shape
    eye = jnp.eye(h, dtype=w.dtype)
    return jnp.einsum("dhij,hg->dhigj", w, eye).reshape(d, h * b, h * b)


def _lane_slab(v, lane0):
    d, k = v.shape
    return jnp.zeros((d, 1, LANES), v.dtype).at[:, 0, lane0:lane0 + k].set(v)


def _pad_history(state):
    return jnp.pad(state, ((0, 0), (0, 0), (SUBLANES - (CONV_W - 1), 0), (0, 0)))


def _trunk(x, mod_group, caches, prm, ssd_chunk):
    bsz, n, _ = x.shape
    x = x.reshape(bsz * n, D_MODEL)
    states = {name: [] for name in ("fox_k", "fox_v", "fox_logf", "lru_conv", "lru_h", "ssd_conv", "ssd_h")}
    prev_kv = None
    for l in range(DEPTH):
        mod4 = mod_group[l].reshape(bsz, N_SUB, 3, D_MODEL)
        x = _ffn_call(x, mod4, prm["npre"], prm["npost"], prm["wg"], prm["wu"], prm["wd"], l, 0, 0, n)
        prompt = caches is None
        stacking = prompt and l == DEPTH - 1
        proj = _inproj_call(x, mod4, prm["npre"], prm["w_in_t"], prm["w_small"], prm["fbias"], l, n,
                            transposed=prompt, prev_kv=prev_kv if stacking else None)
        lrux, lrug, q, k, v, z, xbc, small = proj[:8]
        per_seq = lambda a: a.reshape(bsz, n, a.shape[-1])
        state_layer = None if prompt else l
        src = prm["zero_state"] if prompt else caches
        ya, lru_conv, lru_h = _lru_call(per_seq(lrux), per_seq(lrug), src["lru_conv"], src["lru_h"],
                                        prm["lru_cw"], prm["lru_cb"], prm["lru_wgate"], prm["lru_bgate"],
                                        prm["lru_lam"], l, state_layer)
        if prompt:
            small_t = proj[8]
            yb = _fox_prompt_call(per_seq(q), k, v, small_t, l if stacking else None)
            prev_kv = (k, v)
            logf_out = jnp.swapaxes(small_t[:, F_LANE0:F_LANE0 + FOX_HEADS, :], 1, 2)
        else:
            yb = _fox_sample_call(per_seq(q), per_seq(k), per_seq(v), per_seq(small),
                                  caches["fox_k"], caches["fox_v"], caches["fox_logf_t"], l)
            k_out = k.reshape(bsz, n, FOX_HEADS, FOX_HEAD_DIM)
            v_out = v.reshape(bsz, n, FOX_HEADS, FOX_HEAD_DIM)
            logf_out = small.reshape(bsz, n, LANES)[:, :, F_LANE0:F_LANE0 + FOX_HEADS]
        yc, ssd_conv, ssd_h = _ssd_call(per_seq(xbc), per_seq(z), per_seq(small), src["ssd_conv"], src["ssd_h"],
                                        prm["ssd_cw"], prm["ssd_cb"], prm["ssd_dtb"], prm["ssd_alog"],
                                        prm["ssd_dvec"], prm["ssd_nw"], l, state_layer, ssd_chunk)
        x = _outffn_call(x, ya.reshape(bsz * n, -1), yb.reshape(bsz * n, -1), yc.reshape(bsz * n, -1),
                         mod4, prm["npre"], prm["npost"], prm["w_out"], prm["wg"], prm["wu"], prm["wd"], l, n)
        if not prompt:
            states["fox_k"].append(k_out)
            states["fox_v"].append(v_out)
        states["fox_logf"].append(logf_out)
        states["lru_conv"].append(lru_conv)
        states["lru_h"].append(lru_h.reshape(bsz, LRU_WIDTH))
        states["ssd_conv"].append(ssd_conv)
        states["ssd_h"].append(ssd_h.reshape(bsz, SSD_HEADS, SSD_HEAD_DIM, D_STATE))
    out = {name: jnp.stack(vals, axis=0) for name, vals in states.items() if vals}
    if prev_kv is not None:
        heads_last = lambda a: jnp.transpose(a.reshape(DEPTH, bsz, FOX_HEADS, FOX_HEAD_DIM, n), (0, 1, 4, 2, 3))
        out["fox_k"], out["fox_v"] = heads_last(prev_kv[0]), heads_last(prev_kv[1])
    return x.reshape(bsz, n, D_MODEL), out


def kernel(x_prompt, x_sample, c_prompt, c_sample, cache_fox_k, cache_fox_v, cache_fox_logf, state_lru_conv, state_lru_h, state_ssd_conv, state_ssd_h, w_mod, b_mod, norm_pre, norm_post, ffn_w_gate, ffn_w_up, ffn_w_down, w_in, w_out, lru_conv_w, lru_conv_b, lru_wa, lru_ba, lru_wx, lru_bx, lru_lambda, fox_f_bias, ssd_conv_w, ssd_conv_b, ssd_dt_bias, ssd_a_log, ssd_d, ssd_norm_w):
    n_prompt, n_sample = x_prompt.shape[0], x_sample.shape[0]
    w_in_t = jnp.swapaxes(w_in, 1, 2)
    prm = {
        "npre": norm_pre.reshape(DEPTH, N_SUB, 1, D_MODEL),
        "npost": norm_post.reshape(DEPTH, N_SUB, 1, D_MODEL),
        "wg": ffn_w_gate.astype(BF16), "wu": ffn_w_up.astype(BF16), "wd": ffn_w_down.astype(BF16),
        "w_in_t": w_in_t, "w_small": _small_rows(w_in_t), "w_out": w_out.astype(BF16),
        "fbias": _lane_slab(fox_f_bias, F_LANE0),
        "lru_cw": lru_conv_w, "lru_cb": lru_conv_b.reshape(DEPTH, 1, LRU_WIDTH),
        "lru_wgate": jnp.concatenate([_block_diag(lru_wa), _block_diag(lru_wx)], axis=-1).astype(BF16),
        "lru_bgate": jnp.concatenate([lru_ba, lru_bx], axis=-1).reshape(DEPTH, 1, 2 * LRU_WIDTH),
        "lru_lam": lru_lambda.reshape(DEPTH, 1, LRU_WIDTH),
        "ssd_cw": ssd_conv_w, "ssd_cb": ssd_conv_b.reshape(DEPTH, 1, SSD_CONV_DIM),
        "ssd_dtb": _lane_slab(ssd_dt_bias, DT_LANE0), "ssd_alog": _lane_slab(ssd_a_log, DT_LANE0),
        "ssd_dvec": jnp.repeat(ssd_d, SSD_HEAD_DIM, axis=-1).reshape(DEPTH, 1, SSD_WIDTH),
        "ssd_nw": ssd_norm_w.reshape(DEPTH, 1, SSD_WIDTH),
        "zero_state": {
            "lru_conv": jnp.zeros((1, SUBLANES, LRU_WIDTH), F32), "lru_h": jnp.zeros((1, 1, LRU_WIDTH), F32),
            "ssd_conv": jnp.zeros((1, SUBLANES, SSD_CONV_DIM), F32), "ssd_h": jnp.zeros((1, SSD_WIDTH, D_STATE), F32),
        },
    }
    caches = {
        "fox_k": jnp.transpose(cache_fox_k, (0, 1, 3, 4, 2)).reshape(DEPTH, n_sample, FOX_WIDTH, -1),
        "fox_v": jnp.transpose(cache_fox_v, (0, 1, 3, 4, 2)).reshape(DEPTH, n_sample, FOX_WIDTH, -1),
        "fox_logf_t": jnp.swapaxes(cache_fox_logf, 2, 3),
        "lru_conv": _pad_history(state_lru_conv),
        "lru_h": state_lru_h.reshape(DEPTH, n_sample, 1, LRU_WIDTH),
        "ssd_conv": _pad_history(state_ssd_conv),
        "ssd_h": state_ssd_h.reshape(DEPTH, n_sample, SSD_WIDTH, D_STATE),
    }
    mod = _mod_call(jnp.concatenate([c_prompt, c_sample], axis=0), w_mod, b_mod)
    y_prompt, sp = _trunk(x_prompt, mod[:, :n_prompt], None, prm, ssd_chunk=256)
    y_sample, ss = _trunk(x_sample, mod[:, n_prompt:], caches, prm, ssd_chunk=x_sample.shape[1])
    names = ("fox_k", "fox_v", "fox_logf", "lru_conv", "lru_h", "ssd_conv", "ssd_h")
    return (y_prompt, y_sample) + tuple(sp[n] for n in names) + tuple(ss[n] for n in names)
```

```python
import functools

import numpy as np
import jax
import jax.numpy as jnp
from jax import lax
from jax.experimental import pallas as pl
from jax.experimental.pallas import tpu as pltpu

F32 = jnp.float32
BF16 = jnp.bfloat16

D_MODEL = 1024
DEPTH = 2
CONV_W = 4
EPS = 1e-6
LRU_WIDTH = 256
LRU_C = 8.0
FOX_HEADS = 8
FOX_HEAD_DIM = 64
FOX_WIDTH = FOX_HEADS * FOX_HEAD_DIM
SSD_HEADS = 4
SSD_HEAD_DIM = 64
SSD_WIDTH = SSD_HEADS * SSD_HEAD_DIM
SSD_GROUPS = 2
D_STATE = 128
SSD_CONV_DIM = SSD_WIDTH + 2 * SSD_GROUPS * D_STATE
IN_SIZES = (LRU_WIDTH, LRU_WIDTH, FOX_WIDTH, FOX_WIDTH, FOX_WIDTH, FOX_HEADS, SSD_WIDTH, SSD_CONV_DIM, SSD_HEADS)
D_FF = 2816
N_SUB = 3

LANES = 128
SUBLANES = 8
HEAD_PAIR = LANES // FOX_HEAD_DIM

F_LANE0 = 0
DT_LANE0 = FOX_HEADS
COL_LRU_X = 0
COL_LRU_G = COL_LRU_X + LRU_WIDTH
COL_Q = COL_LRU_G + LRU_WIDTH
COL_K = COL_Q + FOX_WIDTH
COL_V = COL_K + FOX_WIDTH
COL_Z = COL_V + FOX_WIDTH
COL_XBC = COL_Z + SSD_WIDTH
COL_SMALL = COL_XBC + SSD_CONV_DIM
D_IN_PAD = COL_SMALL + LANES
D_IN = sum(IN_SIZES)

ROW_TILE = 512
FFN_ROW_TILE = 1024
FF_CHUNK = 512
ATTN_BLOCK = 2048
ATTN_ROWS = 128
ATTN_GROUP = 16
SCAN_BLOCK = 256
SAMPLE_CHUNK = 4096
NEG_BIG = -1e30
LOG2E = 1.4426950408889634
Q_SCALE = LOG2E * FOX_HEAD_DIM ** -0.5


def _dot(a, b):
    return jnp.dot(a, b, preferred_element_type=F32)


def _dot_nt(a, b):
    return lax.dot_general(a, b, (((1,), (1,)), ((), ())), preferred_element_type=F32)


def _dot_tn(a, b):
    return lax.dot_general(a, b, (((0,), (0,)), ((), ())), preferred_element_type=F32)


def _silu(x):
    return x * jax.nn.sigmoid(x)


def _softplus(x):
    return jnp.maximum(x, 0.0) + jnp.log1p(jnp.exp(-jnp.abs(x)))


def _rms(x):
    return x * lax.rsqrt(jnp.mean(x * x, axis=-1, keepdims=True) + EPS)


def _per_seq(rows, per_seq, fn):
    g = per_seq[0].shape[0]
    if g == 1:
        return fn(rows, *per_seq)
    tm, d = rows.shape
    out = fn(rows.reshape(g, tm // g, d), *[p[:, None, :] for p in per_seq])
    return out.reshape(tm, d)


def _pre_norm(x, npre, mod_ref, seqs=slice(None)):
    h = _rms(x) * npre
    return _per_seq(h, (mod_ref[seqs, 1, :], mod_ref[seqs, 0, :]), lambda r, sc, sh: r * (1.0 + sc) + sh)


def _post_norm(x, y, npost, mod_ref, w, seqs=slice(None)):
    yn = _rms(y) * npost
    return x + _per_seq(yn, (mod_ref[seqs, 2, :],), lambda r, gt: (w * gt) * r)


def _seq_grouping(rows_per_seq, tm):
    if rows_per_seq % tm == 0:
        return 1, rows_per_seq // tm
    assert tm % rows_per_seq == 0
    return tm // rows_per_seq, 1


def _mod_spec(g, tiles_per_seq, sub):
    if g == 1:
        return pl.BlockSpec((1, None, 3, D_MODEL), lambda i: (i // tiles_per_seq, sub, 0, 0))
    return pl.BlockSpec((g, None, 3, D_MODEL), lambda i: (i, sub, 0, 0))


VMEM_LIMIT_MB = {
    "adaln_mod": 24,
    "ffn": 56,
    "outproj_ffn": 57,
    "inproj": 56,
    "rg_lru": 40,
    "ssd": 40,
    "fox_prompt": 40,
    "fox_sample": 56,
}


def _params(name, sem):
    return pltpu.CompilerParams(dimension_semantics=sem, vmem_limit_bytes=VMEM_LIMIT_MB[name] << 20)


def _mod_kernel(c_ref, w_ref, b_ref, o_ref):
    a = _silu(c_ref[...]).astype(BF16)
    o_ref[...] = _dot(a, w_ref[...].astype(BF16)) + b_ref[...]


def _mod_call(c_all, w_mod, b_mod):
    nseq = c_all.shape[0]
    width = N_SUB * 3 * D_MODEL
    tn = 1024
    return pl.pallas_call(
        _mod_kernel,
        grid=(DEPTH, width // tn),
        in_specs=[
            pl.BlockSpec((nseq, D_MODEL), lambda l, n: (0, 0)),
            pl.BlockSpec((None, D_MODEL, tn), lambda l, n: (l, 0, n)),
            pl.BlockSpec((None, 1, tn), lambda l, n: (l, 0, n)),
        ],
        out_specs=pl.BlockSpec((None, nseq, tn), lambda l, n: (l, 0, n)),
        out_shape=jax.ShapeDtypeStruct((DEPTH, nseq, width), F32),
        compiler_params=_params("adaln_mod", ("arbitrary", "arbitrary")),
        name="adaln_mod",
    )(c_all, w_mod, b_mod.reshape(DEPTH, 1, width))


def _halves(tm, nseq):
    half = tm // 2
    rows = [slice(i * half, (i + 1) * half) for i in range(2)]
    seqs = [slice(0, 1)] * 2 if nseq == 1 else [slice(i * nseq // 2, (i + 1) * nseq // 2) for i in range(2)]
    return rows, seqs


def _swiglu(hs, wg_ref, wu_ref, wd_ref):
    accs = [None] * len(hs)
    for off in range(0, D_FF, FF_CHUNK):
        fc = min(FF_CHUNK, D_FF - off)
        for i, h in enumerate(hs):
            g = _dot(h, wg_ref[:, off:off + fc])
            u = _dot(h, wu_ref[:, off:off + fc])
            a = (_silu(g) * u).astype(BF16)
            d = _dot(a, wd_ref[off:off + fc, :])
            accs[i] = d if accs[i] is None else accs[i] + d
    return accs


def _ffn_kernel(x_ref, mod_ref, npre_ref, npost_ref, wg_ref, wu_ref, wd_ref, o_ref):
    rows, seqs = _halves(x_ref.shape[0], mod_ref.shape[0])
    xs = [x_ref[r, :] for r in rows]
    hs = [_pre_norm(x, npre_ref[...], mod_ref, sq).astype(BF16) for x, sq in zip(xs, seqs)]
    accs = _swiglu(hs, wg_ref, wu_ref, wd_ref)
    for i, r in enumerate(rows):
        o_ref[r, :] = _post_norm(xs[i], accs[i], npost_ref[...], mod_ref, 0.5, seqs[i])


def _outffn_kernel(x_ref, ya_ref, yb_ref, yc_ref, mod1_ref, mod2_ref, npost1_ref, npre2_ref, npost2_ref,
                   wo_ref, wg_ref, wu_ref, wd_ref, o_ref):
    rows, seqs = _halves(x_ref.shape[0], mod1_ref.shape[0])
    xs = []
    for r, sq in zip(rows, seqs):
        y = (_dot(ya_ref[r, :], wo_ref[0:LRU_WIDTH, :])
             + _dot(yb_ref[r, :], wo_ref[LRU_WIDTH:LRU_WIDTH + FOX_WIDTH, :])
             + _dot(yc_ref[r, :], wo_ref[LRU_WIDTH + FOX_WIDTH:, :]))
        xs.append(_post_norm(x_ref[r, :], y, npost1_ref[...], mod1_ref, 1.0, sq))
    hs = [_pre_norm(x, npre2_ref[...], mod2_ref, sq).astype(BF16) for x, sq in zip(xs, seqs)]
    accs = _swiglu(hs, wg_ref, wu_ref, wd_ref)
    for i, r in enumerate(rows):
        o_ref[r, :] = _post_norm(xs[i], accs[i], npost2_ref[...], mod2_ref, 0.5, seqs[i])


def _outffn_call(x, ya, yb, yc, mod4, npre, npost, w_out, wg, wu, wd, layer, rows_per_seq):
    m = x.shape[0]
    tm = min(FFN_ROW_TILE, m)
    g, tps = _seq_grouping(rows_per_seq, tm)
    row = lambda w: pl.BlockSpec((tm, w), lambda i: (i, 0))
    once = dict(pipeline_mode=pl.Buffered(1))
    wspec = lambda shape: pl.BlockSpec((None, None) + shape, lambda i: (layer, 1, 0, 0), **once)
    norm = lambda sub: pl.BlockSpec((None, None, 1, D_MODEL), lambda i: (layer, sub, 0, 0))
    return pl.pallas_call(
        _outffn_kernel,
        grid=(m // tm,),
        in_specs=[
            row(D_MODEL), row(LRU_WIDTH), row(FOX_WIDTH), row(SSD_WIDTH),
            _mod_spec(g, tps, 1), _mod_spec(g, tps, 2),
            norm(1), norm(2), norm(2),
            pl.BlockSpec((None, D_MODEL, D_MODEL), lambda i: (layer, 0, 0), **once),
            wspec((D_MODEL, D_FF)), wspec((D_MODEL, D_FF)), wspec((D_FF, D_MODEL)),
        ],
        out_specs=row(D_MODEL),
        out_shape=jax.ShapeDtypeStruct((m, D_MODEL), F32),
        compiler_params=_params("outproj_ffn", ("arbitrary",)),
        name="outproj_ffn",
    )(x, ya, yb, yc, mod4, mod4, npost, npre, npost, w_out, wg, wu, wd)


def _ffn_call(x, mod4, npre, npost, wg, wu, wd, layer, sub, ffn_idx, rows_per_seq):
    m = x.shape[0]
    tm = min(FFN_ROW_TILE, m)
    g, tps = _seq_grouping(rows_per_seq, tm)
    wspec = lambda shape: pl.BlockSpec((None, None) + shape, lambda i: (layer, ffn_idx, 0, 0),
                                       pipeline_mode=pl.Buffered(1))
    nspec = pl.BlockSpec((None, None, 1, D_MODEL), lambda i: (layer, sub, 0, 0))
    return pl.pallas_call(
        _ffn_kernel,
        grid=(m // tm,),
        in_specs=[
            pl.BlockSpec((tm, D_MODEL), lambda i: (i, 0)),
            _mod_spec(g, tps, sub),
            nspec, nspec,
            wspec((D_MODEL, D_FF)), wspec((D_MODEL, D_FF)), wspec((D_FF, D_MODEL)),
        ],
        out_specs=pl.BlockSpec((tm, D_MODEL), lambda i: (i, 0)),
        out_shape=jax.ShapeDtypeStruct((m, D_MODEL), F32),
        compiler_params=_params("ffn", ("arbitrary",)),
        name="ffn",
    )(x, mod4, npre, npost, wg, wu, wd)


def _inproj_kernel(*refs, transposed, stacked):
    n_in = 8 if stacked else 6
    x_ref, mod_ref, npre_ref, w_ref, wsmall_ref, fbias_ref = refs[:6]
    lrux_ref, lrug_ref, q_ref, k_ref, v_ref, z_ref, xbc_ref, small_ref, *rest = refs[n_in:]
    wt_ref = rest[-1]

    @pl.when(pl.program_id(0) == 0)
    def _():
        src = int(np.cumsum(IN_SIZES)[4])
        assert src == COL_Z
        moves = [(0, 0, src), (src + FOX_HEADS, COL_Z, SSD_WIDTH), (src + FOX_HEADS + SSD_WIDTH, COL_XBC, SSD_CONV_DIM)]
        for s0, d0, n in moves:
            for off in range(0, n, LRU_WIDTH):
                wt_ref[d0 + off:d0 + off + LRU_WIDTH, :] = w_ref[s0 + off:s0 + off + LRU_WIDTH, :].astype(BF16)
        wt_ref[COL_SMALL:COL_SMALL + LANES, :] = wsmall_ref[...].astype(BF16)

    h = _pre_norm(x_ref[...], npre_ref[...], mod_ref).astype(BF16)
    col = lambda start, width: _dot_nt(h, wt_ref[start:start + width, :])
    col_t = lambda start, width: _dot_nt(wt_ref[start:start + width, :], h)
    lrux_ref[...] = col(COL_LRU_X, LRU_WIDTH)
    lrug_ref[...] = col(COL_LRU_G, LRU_WIDTH)
    q_ref[...] = (col(COL_Q, FOX_WIDTH) * Q_SCALE).astype(BF16)
    if transposed:
        kv_t = col_t(COL_K, 2 * FOX_WIDTH)
        if stacked:
            k_ref[0], v_ref[0] = refs[6][...], refs[7][...]
            k_ref[1], v_ref[1] = kv_t[:FOX_WIDTH], kv_t[FOX_WIDTH:]
        else:
            k_ref[...] = kv_t[:FOX_WIDTH]
            v_ref[...] = kv_t[FOX_WIDTH:]
    else:
        k_ref[...] = col(COL_K, FOX_WIDTH)
        v_ref[...] = col(COL_V, FOX_WIDTH)
    z_ref[...] = col(COL_Z, SSD_WIDTH)
    xbc_ref[...] = col(COL_XBC, SSD_CONV_DIM)
    small = col(COL_SMALL, LANES)
    t = small + fbias_ref[...]
    logf = jnp.minimum(t, 0.0) - jnp.log1p(jnp.exp(-jnp.abs(t)))
    lane = lax.broadcasted_iota(jnp.int32, small.shape, 1)
    small = jnp.where(lane < DT_LANE0, logf, small)
    small_ref[...] = small
    if transposed:
        rest[0][...] = small.T


def _inproj_call(x, mod4, npre, w_in_t, w_small, fbias, layer, rows_per_seq, transposed, prev_kv=None):
    m = x.shape[0]
    tm = min(ROW_TILE, m)
    g, tps = _seq_grouping(rows_per_seq, tm)
    widths = (LRU_WIDTH, LRU_WIDTH, FOX_WIDTH, FOX_WIDTH, FOX_WIDTH, SSD_WIDTH, SSD_CONV_DIM, LANES)
    dtypes = (F32, F32, BF16, F32, F32, F32, F32, F32)
    out_specs = [pl.BlockSpec((tm, w), lambda i: (i, 0)) for w in widths]
    out_shape = [jax.ShapeDtypeStruct((m, w), dt) for w, dt in zip(widths, dtypes)]
    in_specs = [
        pl.BlockSpec((tm, D_MODEL), lambda i: (i, 0)),
        _mod_spec(g, tps, 1),
        pl.BlockSpec((None, None, 1, D_MODEL), lambda i: (layer, 1, 0, 0)),
        pl.BlockSpec((None, D_IN, D_MODEL), lambda i: (layer, 0, 0), pipeline_mode=pl.Buffered(1)),
        pl.BlockSpec((None, LANES, D_MODEL), lambda i: (layer, 0, 0)),
        pl.BlockSpec((None, 1, LANES), lambda i: (layer, 0, 0)),
    ]
    if transposed:
        assert g == 1
        nseq = m // rows_per_seq
        for idx, w in ((3, FOX_WIDTH), (4, FOX_WIDTH), (len(widths), LANES)):
            spec = pl.BlockSpec((None, w, tm), lambda i: (i // tps, 0, i % tps))
            shape = jax.ShapeDtypeStruct((nseq, w, rows_per_seq), F32)
            out_specs[idx:idx + 1], out_shape[idx:idx + 1] = [spec], [shape]
    extra = ()
    if prev_kv is not None:
        assert transposed and layer == 1 and DEPTH == 2
        extra = tuple(prev_kv)
        in_specs += [pl.BlockSpec((None, FOX_WIDTH, tm), lambda i: (i // tps, 0, i % tps))] * 2
        for idx in (3, 4):
            out_specs[idx] = pl.BlockSpec((DEPTH, None, FOX_WIDTH, tm), lambda i: (0, i // tps, 0, i % tps))
            out_shape[idx] = jax.ShapeDtypeStruct((DEPTH, nseq, FOX_WIDTH, rows_per_seq), F32)
    return pl.pallas_call(
        functools.partial(_inproj_kernel, transposed=transposed, stacked=prev_kv is not None),
        grid=(m // tm,),
        in_specs=in_specs,
        out_specs=out_specs,
        out_shape=out_shape,
        scratch_shapes=[pltpu.VMEM((D_IN_PAD, D_MODEL), BF16)],
        compiler_params=_params("inproj", ("arbitrary",)),
        name="inproj",
    )(x, mod4, npre, w_in_t, w_small, fbias, *extra)


def _causal_conv(xx, cw, cb, n):
    u = cb + cw[0:1] * pltpu.roll(xx, 3, 0)[SUBLANES:SUBLANES + n]
    u = u + cw[1:2] * pltpu.roll(xx, 2, 0)[SUBLANES:SUBLANES + n]
    u = u + cw[2:3] * pltpu.roll(xx, 1, 0)[SUBLANES:SUBLANES + n]
    return u + cw[3:4] * xx[SUBLANES:SUBLANES + n]


def _last_rows(xx, k):
    return pltpu.roll(xx, k, 0)[0:SUBLANES][0:k]


def _tri(n, lower):
    r = lax.broadcasted_iota(jnp.int32, (n, n), 0)
    c = lax.broadcasted_iota(jnp.int32, (n, n), 1)
    return ((r >= c) if lower else (r <= c)).astype(F32).astype(BF16)


def _split3(x):
    hi = x.astype(BF16)
    r1 = x - hi.astype(F32)
    mid = r1.astype(BF16)
    lo = (r1 - mid.astype(F32)).astype(BF16)
    return hi, mid, lo


def _cumsum_lanes(x, triu):
    hi, mid, lo = _split3(x)
    return _dot(hi, triu) + _dot(mid, triu) + _dot(lo, triu)


def _cumsum_rows(x, tril):
    hi, mid, lo = _split3(x)
    return _dot(tril, hi) + _dot(tril, mid) + _dot(tril, lo)


def _lru_kernel(x_ref, g_ref, prev_ref, h0_ref, cw_ref, cb_ref, wgate_ref, bgate_ref, lam_ref,
                y_ref, convnew_ref, hnew_ref, a_scr, b_scr):
    n = x_ref.shape[0]
    xx = jnp.concatenate([prev_ref[...], x_ref[...]], axis=0)
    convnew_ref[...] = _last_rows(xx, CONV_W - 1)
    u = _causal_conv(xx, cw_ref[...], cb_ref[...], n)
    gates = _dot(u.astype(BF16), wgate_ref[...]) + bgate_ref[...]
    r = jax.nn.sigmoid(gates[:, :LRU_WIDTH])
    i = jax.nn.sigmoid(gates[:, LRU_WIDTH:])
    log_a = (-LRU_C * r) * _softplus(-lam_ref[...])
    a = jnp.exp(log_a)
    b = jnp.sqrt(-jnp.tanh(log_a) * (a * a + 1.0)) * (i * u)
    a = a.reshape(n // SUBLANES, SUBLANES, LRU_WIDTH)
    b = b.reshape(n // SUBLANES, SUBLANES, LRU_WIDTH)
    row = lax.broadcasted_iota(jnp.int32, (1, SUBLANES, 1), 1)
    for d in (1, 2, 4):
        keep = row >= d
        b = jnp.where(keep, a * pltpu.roll(b, d, 1) + b, b)
        a = jnp.where(keep, a * pltpu.roll(a, d, 1), a)
    a_scr[...] = a.reshape(n, LRU_WIDTH)
    b_scr[...] = b.reshape(n, LRU_WIDTH)

    def group(j, h):
        off = pl.multiple_of(j * SUBLANES, SUBLANES)
        hb = a_scr[pl.ds(off, SUBLANES), :] * h + b_scr[pl.ds(off, SUBLANES), :]
        b_scr[pl.ds(off, SUBLANES), :] = hb
        return jnp.broadcast_to(hb[SUBLANES - 1:SUBLANES, :], hb.shape)

    h_last = lax.fori_loop(0, n // SUBLANES, group,
                           jnp.broadcast_to(h0_ref[...], (SUBLANES, LRU_WIDTH)), unroll=4)
    hnew_ref[...] = h_last[0:1]
    y_ref[...] = (b_scr[...] * jax.nn.gelu(g_ref[...])).astype(BF16)


def _lru_call(lru_x, lru_g, prev8, h0, cw, cb, wgate, bgate, lam, layer, state_layer):
    bsz, n, _ = lru_x.shape
    seq = pl.BlockSpec((None, n, LRU_WIDTH), lambda b: (b, 0, 0))
    if state_layer is None:
        prev_spec = pl.BlockSpec((None, SUBLANES, LRU_WIDTH), lambda b: (0, 0, 0))
        h0_spec = pl.BlockSpec((None, 1, LRU_WIDTH), lambda b: (0, 0, 0))
    else:
        prev_spec = pl.BlockSpec((None, None, SUBLANES, LRU_WIDTH), lambda b: (state_layer, b, 0, 0))
        h0_spec = pl.BlockSpec((None, None, 1, LRU_WIDTH), lambda b: (state_layer, b, 0, 0))
    par = lambda r, w: pl.BlockSpec((None, r, w), lambda b: (layer, 0, 0))
    return pl.pallas_call(
        _lru_kernel,
        grid=(bsz,),
        in_specs=[seq, seq, prev_spec, h0_spec, par(CONV_W, LRU_WIDTH), par(1, LRU_WIDTH),
                  par(LRU_WIDTH, 2 * LRU_WIDTH), par(1, 2 * LRU_WIDTH), par(1, LRU_WIDTH)],
        out_specs=[seq,
                   pl.BlockSpec((None, CONV_W - 1, LRU_WIDTH), lambda b: (b, 0, 0)),
                   pl.BlockSpec((None, 1, LRU_WIDTH), lambda b: (b, 0, 0))],
        out_shape=[jax.ShapeDtypeStruct((bsz, n, LRU_WIDTH), BF16),
                   jax.ShapeDtypeStruct((bsz, CONV_W - 1, LRU_WIDTH), F32),
                   jax.ShapeDtypeStruct((bsz, 1, LRU_WIDTH), F32)],
        scratch_shapes=[pltpu.VMEM((n, LRU_WIDTH), F32), pltpu.VMEM((n, LRU_WIDTH), F32)],
        compiler_params=_params("rg_lru", ("arbitrary",)),
        name="rg_lru",
    )(lru_x, lru_g, prev8, h0, cw, cb, wgate, bgate, lam)


def _ssd_kernel(xbc_ref, z_ref, small_ref, prev_ref, h0_ref, cw_ref, cb_ref, dtb_ref, alog_ref, dvec_ref, nw_ref,
                y_ref, convnew_ref, h_ref, *, chunk):
    h_ref[...] = h0_ref[...]

    def one_chunk(c, tail):
        rows = pl.ds(pl.multiple_of(c * chunk, chunk), chunk)
        xx = jnp.concatenate([tail, xbc_ref[rows, :]], axis=0)
        _ssd_chunk(xx, z_ref[rows, :], small_ref[rows, :], cw_ref, cb_ref, dtb_ref, alog_ref, dvec_ref, nw_ref,
                   y_ref.at[rows, :], h_ref)
        return xx[chunk:chunk + SUBLANES]

    tail = lax.fori_loop(0, xbc_ref.shape[0] // chunk, one_chunk, prev_ref[...])
    convnew_ref[...] = _last_rows(tail, CONV_W - 1)


def _ssd_chunk(xx, z, small, cw_ref, cb_ref, dtb_ref, alog_ref, dvec_ref, nw_ref, y_ref, h_ref):
    n = xx.shape[0] - SUBLANES
    act = _silu(_causal_conv(xx, cw_ref[...], cb_ref[...], n))
    xs = act[:, :SSD_WIDTH]
    bm = act[:, SSD_WIDTH:SSD_WIDTH + SSD_GROUPS * D_STATE]
    cm = act[:, SSD_WIDTH + SSD_GROUPS * D_STATE:]

    dt = _softplus(small + dtb_ref[...])
    dta = dt * (-jnp.exp(alog_ref[...]))
    cum = _cumsum_rows(dta, _tri(n, True)) * LOG2E
    cum_t = cum.T
    rr = lax.broadcasted_iota(jnp.int32, (n, n), 0)
    cc = lax.broadcasted_iota(jnp.int32, (n, n), 1)
    causal = rr >= cc
    lo_lane = lax.broadcasted_iota(jnp.int32, (1, LANES), 1) < SSD_HEAD_DIM
    lo_row = lax.broadcasted_iota(jnp.int32, (LANES, 1), 0) < SSD_HEAD_DIM
    dvec = dvec_ref[...]

    ys = []
    for g in range(SSD_GROUPS):
        sl = slice(g * LANES, (g + 1) * LANES)
        xg, bg, cg = xs[:, sl], bm[:, sl].astype(BF16), cm[:, sl].astype(BF16)
        heads = (2 * g, 2 * g + 1)
        col = lambda a, h: a[:, DT_LANE0 + h:DT_LANE0 + h + 1]
        pick = lambda f: jnp.where(lo_lane, f(heads[0]), f(heads[1]))
        dx = xg * pick(lambda h: col(dt, h))
        dxb = dx.astype(BF16)
        cb_mat = _dot_nt(cg, bg)
        yd = []
        for h in heads:
            seg = col(cum, h) - cum_t[DT_LANE0 + h:DT_LANE0 + h + 1, :]
            lmat = jnp.exp2(jnp.where(causal, seg, NEG_BIG))
            yd.append(_dot((cb_mat * lmat).astype(BF16), dxb))
        y_diag = jnp.where(lo_lane, yd[0], yd[1])
        last = lambda h: col(cum, h)[n - 1:n, :]
        decay_end = pick(lambda h: jnp.exp2(last(h) - col(cum, h)))
        states = _dot_tn((dx * decay_end).astype(BF16), bg)
        h_prev = h_ref[sl, :]
        y_off = _dot_nt(cg, h_prev.astype(BF16)) * pick(lambda h: jnp.exp2(col(cum, h)))
        chunk_decay = jnp.where(lo_row, jnp.exp2(last(heads[0])), jnp.exp2(last(heads[1])))
        h_ref[sl, :] = chunk_decay * h_prev + states
        ys.append(y_diag + y_off + dvec[:, sl] * xg)
    y = jnp.concatenate(ys, axis=1)
    y_ref[...] = (_rms(y * _silu(z)) * nw_ref[...]).astype(BF16)


def _ssd_call(xbc, z, small, prev8, h0, cw, cb, dtb, alog, dvec, nw, layer, state_layer, chunk):
    bsz, n, _ = xbc.shape
    seq = lambda w: pl.BlockSpec((None, n, w), lambda b: (b, 0, 0))
    if state_layer is None:
        prev_spec = pl.BlockSpec((None, SUBLANES, SSD_CONV_DIM), lambda b: (0, 0, 0))
        h0_spec = pl.BlockSpec((None, SSD_WIDTH, D_STATE), lambda b: (0, 0, 0))
    else:
        prev_spec = pl.BlockSpec((None, None, SUBLANES, SSD_CONV_DIM), lambda b: (state_layer, b, 0, 0))
        h0_spec = pl.BlockSpec((None, None, SSD_WIDTH, D_STATE), lambda b: (state_layer, b, 0, 0))
    par = lambda r, w: pl.BlockSpec((None, r, w), lambda b: (layer, 0, 0))
    return pl.pallas_call(
        functools.partial(_ssd_kernel, chunk=chunk),
        grid=(bsz,),
        in_specs=[seq(SSD_CONV_DIM), seq(SSD_WIDTH), seq(LANES), prev_spec, h0_spec,
                  par(CONV_W, SSD_CONV_DIM), par(1, SSD_CONV_DIM), par(1, LANES), par(1, LANES),
                  par(1, SSD_WIDTH), par(1, SSD_WIDTH)],
        out_specs=[seq(SSD_WIDTH),
                   pl.BlockSpec((None, CONV_W - 1, SSD_CONV_DIM), lambda b: (b, 0, 0)),
                   pl.BlockSpec((None, SSD_WIDTH, D_STATE), lambda b: (b, 0, 0))],
        out_shape=[jax.ShapeDtypeStruct((bsz, n, SSD_WIDTH), BF16),
                   jax.ShapeDtypeStruct((bsz, CONV_W - 1, SSD_CONV_DIM), F32),
                   jax.ShapeDtypeStruct((bsz, SSD_WIDTH, D_STATE), F32)],
        compiler_params=_params("ssd", ("arbitrary",)),
        name="ssd",
    )(xbc, z, small, prev8, h0, cw, cb, dtb, alog, dvec, nw)


def _fox_prompt_kernel(q_ref, kt_ref, vt_ref, logft_ref, o_ref, k_scr, v_scr, f_scr, ft_scr):
    s = kt_ref.shape[1]
    tq = k_scr.shape[3]
    p = pl.program_id(1)
    heads = (HEAD_PAIR * p, HEAD_PAIR * p + 1)
    spare = (FOX_HEAD_DIM, 0)

    @pl.when(p == 0)
    def _():
        triu = _tri(SCAN_BLOCK, False)
        carry = jnp.zeros((LANES, 1), F32)
        per = tq // SCAN_BLOCK
        for c in range(s // SCAN_BLOCK):
            cols = slice(c * SCAN_BLOCK, (c + 1) * SCAN_BLOCK)
            fc = _cumsum_lanes(logft_ref[:, cols], triu) + carry
            ft_scr[c // per, :, (c % per) * SCAN_BLOCK:(c % per + 1) * SCAN_BLOCK] = fc
            f_scr[cols, :] = fc.T
            carry = fc[:, SCAN_BLOCK - 1:SCAN_BLOCK]

    row = lax.broadcasted_iota(jnp.int32, (LANES, 1), 0)
    for j in range(s // tq):
        kt = kt_ref[:, j * tq:(j + 1) * tq]
        vt = vt_ref[:, j * tq:(j + 1) * tq]
        for hl in range(HEAD_PAIR):
            own = (row < FOX_HEAD_DIM) if hl == 0 else (row >= FOX_HEAD_DIM)
            f_k = ft_scr[j, pl.ds(F_LANE0 + heads[hl], 1), :] * LOG2E
            kc = jnp.where(own, kt, 0.0).astype(BF16)
            for i, part in enumerate(_split3(-f_k)):
                kc = jnp.where(row == spare[hl] + i, part, kc)
            k_scr[hl, j] = kc
            v_scr[hl, j] = jnp.where(own, vt, jnp.where(row == spare[hl], 1.0, 0.0)).astype(BF16)

    lane = lax.broadcasted_iota(jnp.int32, (1, LANES), 1)
    lo_lane = lane < FOX_HEAD_DIM
    rb = min(ATTN_ROWS, tq)
    nrb = tq // rb

    def query_block(qi, _):
        rows = [pl.ds(pl.multiple_of(qi * tq + r * rb, rb), rb) for r in range(nrb)]
        q_blocks, fq_blocks = [], []
        for r in range(nrb):
            q = q_ref[rows[r], :].astype(F32)
            ones_at = lambda l0: jnp.where((lane >= l0) & (lane < l0 + 3), 1.0, 0.0)
            q_blocks.append((jnp.where(lo_lane, q, ones_at(spare[0])).astype(BF16),
                             jnp.where(lo_lane, ones_at(spare[1]), q).astype(BF16)))
            f_rows = f_scr[rows[r], :]
            fq_blocks.append([jnp.sum(jnp.where(lane == F_LANE0 + h, f_rows, 0.0), axis=1, keepdims=True) * LOG2E
                              for h in heads])
        for g0 in range(0, nrb, ATTN_GROUP):
            rs = tuple(range(g0, min(g0 + ATTN_GROUP, nrb)))
            carry = lax.fori_loop(0, qi, lambda j, cr: step(j, cr, rs, q_blocks, fq_blocks, False),
                                  (init_rows,) * len(rs))
            carry = step(qi, carry, rs, q_blocks, fq_blocks, True)
            for i, r in enumerate(rs):
                acc = [carry[i][2 * hl + 1] for hl in range(HEAD_PAIR)]
                out = [a / a[:, spare[hl]:spare[hl] + 1] for hl, a in enumerate(acc)]
                o_ref[rows[r], :] = jnp.where(lo_lane, out[0], out[1]).astype(BF16)
        return 0

    def softmax_unit(t, f_q, m_old, row0):
        if row0 is not None:
            rr = lax.broadcasted_iota(jnp.int32, t.shape, 0) + row0
            cc = lax.broadcasted_iota(jnp.int32, t.shape, 1)
            t = jnp.where(cc <= rr, t, NEG_BIG)
        m_new = jnp.maximum(m_old, jnp.max(t, axis=1, keepdims=True) + f_q)
        return m_new, jnp.exp2(m_old - m_new), jnp.exp2(t - (m_new - f_q)).astype(BF16)

    def step(j, carry, rs, q_blocks, fq_blocks, diagonal):
        nks = {r: (r + 1) * rb if diagonal else tq for r in rs}
        units = [(i, r, hl) for i, r in enumerate(rs) for hl in range(HEAD_PAIR)]
        dots = {(r, hl): _dot(q_blocks[r][hl], k_scr[hl, j, :, 0:nks[r]]) for _, r, hl in units}
        soft = {(r, hl): softmax_unit(dots[(r, hl)], fq_blocks[r][hl], carry[i][2 * hl], r * rb if diagonal else None)
                for i, r, hl in units}
        pvs = {(r, hl): _dot_nt(soft[(r, hl)][2], v_scr[hl, j, :, 0:nks[r]]) for _, r, hl in units}
        new = []
        for i, r in enumerate(rs):
            row_state = ()
            for hl in range(HEAD_PAIR):
                m_new, alpha, _ = soft[(r, hl)]
                row_state += (m_new, alpha * carry[i][2 * hl + 1] + pvs[(r, hl)])
            new.append(row_state)
        return tuple(new)

    init_rows = (jnp.full((rb, 1), NEG_BIG, F32), jnp.zeros((rb, LANES), F32)) * HEAD_PAIR
    lax.fori_loop(0, s // tq, query_block, 0)


def _fox_prompt_call(q, kt, vt, small_t, layer=None):
    bsz, s, _ = q.shape
    tq = min(ATTN_BLOCK, s)
    blk = pl.BlockSpec((None, s, LANES), lambda b, p: (b, 0, p))
    if layer is None:
        full = pl.BlockSpec((None, LANES, s), lambda b, p: (b, p, 0))
    else:
        full = pl.BlockSpec((None, None, LANES, s), lambda b, p: (layer, b, p, 0))
    return pl.pallas_call(
        _fox_prompt_kernel,
        grid=(bsz, FOX_WIDTH // LANES),
        in_specs=[blk, full, full, pl.BlockSpec((None, LANES, s), lambda b, p: (b, 0, 0))],
        out_specs=blk,
        out_shape=jax.ShapeDtypeStruct((bsz, s, FOX_WIDTH), BF16),
        scratch_shapes=[pltpu.VMEM((HEAD_PAIR, s // tq, LANES, tq), BF16),
                        pltpu.VMEM((HEAD_PAIR, s // tq, LANES, tq), BF16),
                        pltpu.VMEM((s, LANES), F32), pltpu.VMEM((s // tq, LANES, tq), F32)],
        compiler_params=_params("fox_prompt", ("arbitrary", "arbitrary")),
        name="fox_prompt",
    )(q, kt, vt, small_t)


def _fox_sample_kernel(q_ref, k_ref, v_ref, logf_ref, ck_ref, cv_ref, clogft_ref, o_ref,
                       m_scr, l_scr, acc_scr, fk_scr, fq_scr, fnew_scr):
    t = q_ref.shape[0]
    c = pl.program_id(1)
    nchunk, _, chunk = fk_scr.shape

    @pl.when(c == 0)
    def _():
        blk = min(SCAN_BLOCK, chunk)
        per = chunk // blk
        nblk = nchunk * per
        stacked = jnp.concatenate([clogft_ref[:, i * blk:(i + 1) * blk] for i in range(nblk)], axis=0)
        local = _cumsum_lanes(stacked, _tri(blk, False))
        nrow = nblk * FOX_HEADS
        rr = lax.broadcasted_iota(jnp.int32, (nrow, nrow), 0)
        cc = lax.broadcasted_iota(jnp.int32, (nrow, nrow), 1)
        earlier = ((rr % FOX_HEADS == cc % FOX_HEADS) & (cc < rr - rr % FOX_HEADS)).astype(F32).astype(BF16)
        totals = jnp.broadcast_to(local[:, blk - 1:blk], (nrow, LANES))
        f_all = local + _cumsum_rows(totals, earlier)[:, 0:1]
        for i in range(nblk):
            fk_scr[i // per, :, (i % per) * blk:(i % per + 1) * blk] = f_all[i * FOX_HEADS:(i + 1) * FOX_HEADS, :] * LOG2E
        carry = f_all[nrow - FOX_HEADS:, blk - 1:blk]
        sub = lax.broadcasted_iota(jnp.int32, (FOX_HEADS, LANES), 0)
        ln = lax.broadcasted_iota(jnp.int32, (FOX_HEADS, LANES), 1)
        total_row = jnp.sum(jnp.where(sub == ln - F_LANE0, carry, 0.0), axis=0, keepdims=True)
        f_new = _cumsum_rows(logf_ref[...], _tri(t, True)) + total_row
        f_new = f_new * LOG2E
        fnew_scr[...] = f_new.T[F_LANE0:F_LANE0 + FOX_HEADS, :]
        fq_scr[...] = jnp.concatenate([f_new[:, F_LANE0 + h:F_LANE0 + h + 1] for h in range(FOX_HEADS)], axis=0)
        m_scr[...] = jnp.full(m_scr.shape, NEG_BIG, F32)
        l_scr[...] = jnp.zeros(l_scr.shape, F32)
        acc_scr[...] = jnp.zeros(acc_scr.shape, F32)

    q = q_ref[...]
    q_heads = [q[:, h * FOX_HEAD_DIM:(h + 1) * FOX_HEAD_DIM] for h in range(FOX_HEADS)]
    f_q = fq_scr[...]

    def update(keys, values, f_k, mask, channels_first):
        qk = _dot if channels_first else _dot_nt
        pv_dot = _dot_nt if channels_first else _dot
        tt = jnp.concatenate([qk(q_heads[h], keys[h]) - f_k[h:h + 1, :] for h in range(FOX_HEADS)], axis=0)
        if mask is not None:
            tt = jnp.where(mask, tt, NEG_BIG)
        m_old = m_scr[...]
        m_new = jnp.maximum(m_old, jnp.max(tt, axis=1, keepdims=True) + f_q)
        alpha = jnp.exp2(m_old - m_new)
        pr = jnp.exp2(tt - (m_new - f_q))
        l_scr[...] = alpha * l_scr[...] + jnp.sum(pr, axis=1, keepdims=True)
        prb = pr.astype(BF16)
        pv = jnp.concatenate([pv_dot(prb[h * t:(h + 1) * t, :], values[h]) for h in range(FOX_HEADS)], axis=0)
        acc_scr[...] = alpha * acc_scr[...] + pv
        m_scr[...] = m_new

    head_rows = lambda ref: [ref[h * FOX_HEAD_DIM:(h + 1) * FOX_HEAD_DIM, :].astype(BF16) for h in range(FOX_HEADS)]
    update(head_rows(ck_ref), head_rows(cv_ref), fk_scr[c], None, True)

    @pl.when(c == nchunk - 1)
    def _():
        k_new, v_new = k_ref[...].astype(BF16), v_ref[...].astype(BF16)
        cols = lambda a: [a[:, h * FOX_HEAD_DIM:(h + 1) * FOX_HEAD_DIM] for h in range(FOX_HEADS)]
        rr = lax.broadcasted_iota(jnp.int32, (FOX_HEADS * t, t), 0) % t
        cc = lax.broadcasted_iota(jnp.int32, (FOX_HEADS * t, t), 1)
        update(cols(k_new), cols(v_new), fnew_scr[...], cc <= rr, False)
        out = acc_scr[...] / l_scr[...]
        o_ref[...] = jnp.concatenate([out[h * t:(h + 1) * t, :] for h in range(FOX_HEADS)], axis=1).astype(BF16)


def _fox_sample_call(q, k, v, small, cache_k, cache_v, cache_logf_t, layer):
    bsz, t, _ = q.shape
    past = cache_logf_t.shape[3]
    chunk = min(SAMPLE_CHUNK, past)
    nchunk = past // chunk
    new = lambda w: pl.BlockSpec((None, t, w), lambda b, c: (b, 0, 0))
    cache = pl.BlockSpec((None, None, FOX_WIDTH, chunk), lambda b, c: (layer, b, 0, c))
    rows = FOX_HEADS * t
    return pl.pallas_call(
        _fox_sample_kernel,
        grid=(bsz, nchunk),
        in_specs=[new(FOX_WIDTH), new(FOX_WIDTH), new(FOX_WIDTH), new(LANES), cache, cache,
                  pl.BlockSpec((None, None, FOX_HEADS, past), lambda b, c: (layer, b, 0, 0))],
        out_specs=new(FOX_WIDTH),
        out_shape=jax.ShapeDtypeStruct((bsz, t, FOX_WIDTH), BF16),
        scratch_shapes=[pltpu.VMEM((rows, 1), F32), pltpu.VMEM((rows, 1), F32), pltpu.VMEM((rows, FOX_HEAD_DIM), F32),
                        pltpu.VMEM((nchunk, FOX_HEADS, chunk), F32), pltpu.VMEM((rows, 1), F32),
                        pltpu.VMEM((FOX_HEADS, t), F32)],
        compiler_params=_params("fox_sample", ("arbitrary", "arbitrary")),
        name="fox_sample",
    )(q, k, v, small, cache_k, cache_v, cache_logf_t)


def _small_rows(w_in_t):
    offs = np.concatenate([[0], np.cumsum(IN_SIZES)])
    seg = lambda i: w_in_t[:, int(offs[i]):int(offs[i + 1]), :]
    pad = jnp.zeros((w_in_t.shape[0], LANES - FOX_HEADS - SSD_HEADS, w_in_t.shape[2]), w_in_t.dtype)
    return jnp.concatenate([seg(5), seg(8), pad], axis=1)


def _block_diag(w):
    d, h, b, _ = w.shape
    eye = jnp.eye(h, dtype=w.dtype)
    return jnp.einsum("dhij,hg->dhigj", w, eye).reshape(d, h * b, h * b)


def _lane_slab(v, lane0):
    d, k = v.shape
    return jnp.zeros((d, 1, LANES), v.dtype).at[:, 0, lane0:lane0 + k].set(v)


def _pad_history(state):
    return jnp.pad(state, ((0, 0), (0, 0), (SUBLANES - (CONV_W - 1), 0), (0, 0)))


def _trunk(x, mod_group, caches, prm, ssd_chunk):
    bsz, n, _ = x.shape
    x = x.reshape(bsz * n, D_MODEL)
    states = {name: [] for name in ("fox_k", "fox_v", "fox_logf", "lru_conv", "lru_h", "ssd_conv", "ssd_h")}
    prev_kv = None
    for l in range(DEPTH):
        mod4 = mod_group[l].reshape(bsz, N_SUB, 3, D_MODEL)
        x = _ffn_call(x, mod4, prm["npre"], prm["npost"], prm["wg"], prm["wu"], prm["wd"], l, 0, 0, n)
        prompt = caches is None
        stacking = prompt and l == DEPTH - 1
        proj = _inproj_call(x, mod4, prm["npre"], prm["w_in_t"], prm["w_small"], prm["fbias"], l, n,
                            transposed=prompt, prev_kv=prev_kv if stacking else None)
        lrux, lrug, q, k, v, z, xbc, small = proj[:8]
        per_seq = lambda a: a.reshape(bsz, n, a.shape[-1])
        state_layer = None if prompt else l
        src = prm["zero_state"] if prompt else caches
        ya, lru_conv, lru_h = _lru_call(per_seq(lrux), per_seq(lrug), src["lru_conv"], src["lru_h"],
                                        prm["lru_cw"], prm["lru_cb"], prm["lru_wgate"], prm["lru_bgate"],
                                        prm["lru_lam"], l, state_layer)
        if prompt:
            small_t = proj[8]
            yb = _fox_prompt_call(per_seq(q), k, v, small_t, l if stacking else None)
            prev_kv = (k, v)
            logf_out = jnp.swapaxes(small_t[:, F_LANE0:F_LANE0 + FOX_HEADS, :], 1, 2)
        else:
            yb = _fox_sample_call(per_seq(q), per_seq(k), per_seq(v), per_seq(small),
                                  caches["fox_k"], caches["fox_v"], caches["fox_logf_t"], l)
            k_out = k.reshape(bsz, n, FOX_HEADS, FOX_HEAD_DIM)
            v_out = v.reshape(bsz, n, FOX_HEADS, FOX_HEAD_DIM)
            logf_out = small.reshape(bsz, n, LANES)[:, :, F_LANE0:F_LANE0 + FOX_HEADS]
        yc, ssd_conv, ssd_h = _ssd_call(per_seq(xbc), per_seq(z), per_seq(small), src["ssd_conv"], src["ssd_h"],
                                        prm["ssd_cw"], prm["ssd_cb"], prm["ssd_dtb"], prm["ssd_alog"],
                                        prm["ssd_dvec"], prm["ssd_nw"], l, state_layer, ssd_chunk)
        x = _outffn_call(x, ya.reshape(bsz * n, -1), yb.reshape(bsz * n, -1), yc.reshape(bsz * n, -1),
                         mod4, prm["npre"], prm["npost"], prm["w_out"], prm["wg"], prm["wu"], prm["wd"], l, n)
        if not prompt:
            states["fox_k"].append(k_out)
            states["fox_v"].append(v_out)
        states["fox_logf"].append(logf_out)
        states["lru_conv"].append(lru_conv)
        states["lru_h"].append(lru_h.reshape(bsz, LRU_WIDTH))
        states["ssd_conv"].append(ssd_conv)
        states["ssd_h"].append(ssd_h.reshape(bsz, SSD_HEADS, SSD_HEAD_DIM, D_STATE))
    out = {name: jnp.stack(vals, axis=0) for name, vals in states.items() if vals}
    if prev_kv is not None:
        heads_last = lambda a: jnp.transpose(a.reshape(DEPTH, bsz, FOX_HEADS, FOX_HEAD_DIM, n), (0, 1, 4, 2, 3))
        out["fox_k"], out["fox_v"] = heads_last(prev_kv[0]), heads_last(prev_kv[1])
    return x.reshape(bsz, n, D_MODEL), out


def kernel(x_prompt, x_sample, c_prompt, c_sample, cache_fox_k, cache_fox_v, cache_fox_logf, state_lru_conv, state_lru_h, state_ssd_conv, state_ssd_h, w_mod, b_mod, norm_pre, norm_post, ffn_w_gate, ffn_w_up, ffn_w_down, w_in, w_out, lru_conv_w, lru_conv_b, lru_wa, lru_ba, lru_wx, lru_bx, lru_lambda, fox_f_bias, ssd_conv_w, ssd_conv_b, ssd_dt_bias, ssd_a_log, ssd_d, ssd_norm_w):
    n_prompt, n_sample = x_prompt.shape[0], x_sample.shape[0]
    w_in_t = jnp.swapaxes(w_in, 1, 2)
    prm = {
        "npre": norm_pre.reshape(DEPTH, N_SUB, 1, D_MODEL),
        "npost": norm_post.reshape(DEPTH, N_SUB, 1, D_MODEL),
        "wg": ffn_w_gate.astype(BF16), "wu": ffn_w_up.astype(BF16), "wd": ffn_w_down.astype(BF16),
        "w_in_t": w_in_t, "w_small": _small_rows(w_in_t), "w_out": w_out.astype(BF16),
        "fbias": _lane_slab(fox_f_bias, F_LANE0),
        "lru_cw": lru_conv_w, "lru_cb": lru_conv_b.reshape(DEPTH, 1, LRU_WIDTH),
        "lru_wgate": jnp.concatenate([_block_diag(lru_wa), _block_diag(lru_wx)], axis=-1).astype(BF16),
        "lru_bgate": jnp.concatenate([lru_ba, lru_bx], axis=-1).reshape(DEPTH, 1, 2 * LRU_WIDTH),
        "lru_lam": lru_lambda.reshape(DEPTH, 1, LRU_WIDTH),
        "ssd_cw": ssd_conv_w, "ssd_cb": ssd_conv_b.reshape(DEPTH, 1, SSD_CONV_DIM),
        "ssd_dtb": _lane_slab(ssd_dt_bias, DT_LANE0), "ssd_alog": _lane_slab(ssd_a_log, DT_LANE0),
        "ssd_dvec": jnp.repeat(ssd_d, SSD_HEAD_DIM, axis=-1).reshape(DEPTH, 1, SSD_WIDTH),
        "ssd_nw": ssd_norm_w.reshape(DEPTH, 1, SSD_WIDTH),
        "zero_state": {
            "lru_conv": jnp.zeros((1, SUBLANES, LRU_WIDTH), F32), "lru_h": jnp.zeros((1, 1, LRU_WIDTH), F32),
            "ssd_conv": jnp.zeros((1, SUBLANES, SSD_CONV_DIM), F32), "ssd_h": jnp.zeros((1, SSD_WIDTH, D_STATE), F32),
        },
    }
    caches = {
        "fox_k": jnp.transpose(cache_fox_k, (0, 1, 3, 4, 2)).reshape(DEPTH, n_sample, FOX_WIDTH, -1),
        "fox_v": jnp.transpose(cache_fox_v, (0, 1, 3, 4, 2)).reshape(DEPTH, n_sample, FOX_WIDTH, -1),
        "fox_logf_t": jnp.swapaxes(cache_fox_logf, 2, 3),
        "lru_conv": _pad_history(state_lru_conv),
        "lru_h": state_lru_h.reshape(DEPTH, n_sample, 1, LRU_WIDTH),
        "ssd_conv": _pad_history(state_ssd_conv),
        "ssd_h": state_ssd_h.reshape(DEPTH, n_sample, SSD_WIDTH, D_STATE),
    }
    mod = _mod_call(jnp.concatenate([c_prompt, c_sample], axis=0), w_mod, b_mod)
    y_prompt, sp = _trunk(x_prompt, mod[:, :n_prompt], None, prm, ssd_chunk=256)
    y_sample, ss = _trunk(x_sample, mod[:, n_prompt:], caches, prm, ssd_chunk=x_sample.shape[1])
    names = ("fox_k", "fox_v", "fox_logf", "lru_conv", "lru_h", "ssd_conv", "ssd_h")
    return (y_prompt, y_sample) + tuple(sp[n] for n in names) + tuple(ss[n] for n in names)
```

```python
import functools

import numpy as np
import jax
import jax.numpy as jnp
from jax import lax
from jax.experimental import pallas as pl
from jax.experimental.pallas import tpu as pltpu

F32 = jnp.float32
BF16 = jnp.bfloat16

D_MODEL = 1024
DEPTH = 2
CONV_W = 4
EPS = 1e-6
LRU_WIDTH = 256
LRU_C = 8.0
FOX_HEADS = 8
FOX_HEAD_DIM = 64
FOX_WIDTH = FOX_HEADS * FOX_HEAD_DIM
SSD_HEADS = 4
SSD_HEAD_DIM = 64
SSD_WIDTH = SSD_HEADS * SSD_HEAD_DIM
SSD_GROUPS = 2
D_STATE = 128
SSD_CONV_DIM = SSD_WIDTH + 2 * SSD_GROUPS * D_STATE
IN_SIZES = (LRU_WIDTH, LRU_WIDTH, FOX_WIDTH, FOX_WIDTH, FOX_WIDTH, FOX_HEADS, SSD_WIDTH, SSD_CONV_DIM, SSD_HEADS)
D_FF = 2816
N_SUB = 3

LANES = 128
SUBLANES = 8
HEAD_PAIR = LANES // FOX_HEAD_DIM

F_LANE0 = 0
DT_LANE0 = FOX_HEADS
COL_LRU_X = 0
COL_LRU_G = COL_LRU_X + LRU_WIDTH
COL_Q = COL_LRU_G + LRU_WIDTH
COL_K = COL_Q + FOX_WIDTH
COL_V = COL_K + FOX_WIDTH
COL_Z = COL_V + FOX_WIDTH
COL_XBC = COL_Z + SSD_WIDTH
COL_SMALL = COL_XBC + SSD_CONV_DIM
D_IN_PAD = COL_SMALL + LANES
D_IN = sum(IN_SIZES)

ROW_TILE = 512
FFN_ROW_TILE = 1024
FF_CHUNK = 512
ATTN_BLOCK = 2048
ATTN_ROWS = 128
ATTN_GROUP = 16
ATTN_PAIRS = 2
SCAN_BLOCK = 256
SAMPLE_CHUNK = 4096
NEG_BIG = -1e30
LOG2E = 1.4426950408889634
Q_SCALE = LOG2E * FOX_HEAD_DIM ** -0.5


def _dot(a, b):
    return jnp.dot(a, b, preferred_element_type=F32)


def _dot_nt(a, b):
    return lax.dot_general(a, b, (((1,), (1,)), ((), ())), preferred_element_type=F32)


def _dot_tn(a, b):
    return lax.dot_general(a, b, (((0,), (0,)), ((), ())), preferred_element_type=F32)


def _silu(x):
    return x * jax.nn.sigmoid(x)


def _softplus(x):
    return jnp.maximum(x, 0.0) + jnp.log1p(jnp.exp(-jnp.abs(x)))


def _rms(x):
    return x * lax.rsqrt(jnp.mean(x * x, axis=-1, keepdims=True) + EPS)


def _per_seq(rows, per_seq, fn):
    g = per_seq[0].shape[0]
    if g == 1:
        return fn(rows, *per_seq)
    tm, d = rows.shape
    out = fn(rows.reshape(g, tm // g, d), *[p[:, None, :] for p in per_seq])
    return out.reshape(tm, d)


def _pre_norm(x, npre, mod_ref, seqs=slice(None)):
    h = _rms(x) * npre
    return _per_seq(h, (mod_ref[seqs, 1, :], mod_ref[seqs, 0, :]), lambda r, sc, sh: r * (1.0 + sc) + sh)


def _post_norm(x, y, npost, mod_ref, w, seqs=slice(None)):
    yn = _rms(y) * npost
    return x + _per_seq(yn, (mod_ref[seqs, 2, :],), lambda r, gt: (w * gt) * r)


def _seq_grouping(rows_per_seq, tm):
    if rows_per_seq % tm == 0:
        return 1, rows_per_seq // tm
    assert tm % rows_per_seq == 0
    return tm // rows_per_seq, 1


def _mod_spec(g, tiles_per_seq, sub):
    if g == 1:
        return pl.BlockSpec((1, None, 3, D_MODEL), lambda i: (i // tiles_per_seq, sub, 0, 0))
    return pl.BlockSpec((g, None, 3, D_MODEL), lambda i: (i, sub, 0, 0))


VMEM_LIMIT_MB = {
    "adaln_mod": 24,
    "ffn": 56,
    "outproj_ffn": 57,
    "inproj": 56,
    "rg_lru": 40,
    "ssd": 40,
    "fox_prompt": 40,
    "fox_sample": 56,
}


def _params(name, sem):
    return pltpu.CompilerParams(dimension_semantics=sem, vmem_limit_bytes=VMEM_LIMIT_MB[name] << 20)


def _mod_kernel(c_ref, w_ref, b_ref, o_ref):
    a = _silu(c_ref[...]).astype(BF16)
    o_ref[...] = _dot(a, w_ref[...].astype(BF16)) + b_ref[...]


def _mod_call(c_all, w_mod, b_mod):
    nseq = c_all.shape[0]
    width = N_SUB * 3 * D_MODEL
    tn = 1024
    return pl.pallas_call(
        _mod_kernel,
        grid=(DEPTH, width // tn),
        in_specs=[
            pl.BlockSpec((nseq, D_MODEL), lambda l, n: (0, 0)),
            pl.BlockSpec((None, D_MODEL, tn), lambda l, n: (l, 0, n)),
            pl.BlockSpec((None, 1, tn), lambda l, n: (l, 0, n)),
        ],
        out_specs=pl.BlockSpec((None, nseq, tn), lambda l, n: (l, 0, n)),
        out_shape=jax.ShapeDtypeStruct((DEPTH, nseq, width), F32),
        compiler_params=_params("adaln_mod", ("arbitrary", "arbitrary")),
        name="adaln_mod",
    )(c_all, w_mod, b_mod.reshape(DEPTH, 1, width))


def _halves(tm, nseq):
    half = tm // 2
    rows = [slice(i * half, (i + 1) * half) for i in range(2)]
    seqs = [slice(0, 1)] * 2 if nseq == 1 else [slice(i * nseq // 2, (i + 1) * nseq // 2) for i in range(2)]
    return rows, seqs


def _swiglu(hs, wg_ref, wu_ref, wd_ref):
    accs = [None] * len(hs)
    for off in range(0, D_FF, FF_CHUNK):
        fc = min(FF_CHUNK, D_FF - off)
        for i, h in enumerate(hs):
            g = _dot(h, wg_ref[:, off:off + fc])
            u = _dot(h, wu_ref[:, off:off + fc])
            a = (_silu(g) * u).astype(BF16)
            d = _dot(a, wd_ref[off:off + fc, :])
            accs[i] = d if accs[i] is None else accs[i] + d
    return accs


def _ffn_kernel(x_ref, mod_ref, npre_ref, npost_ref, wg_ref, wu_ref, wd_ref, o_ref):
    rows, seqs = _halves(x_ref.shape[0], mod_ref.shape[0])
    xs = [x_ref[r, :] for r in rows]
    hs = [_pre_norm(x, npre_ref[...], mod_ref, sq).astype(BF16) for x, sq in zip(xs, seqs)]
    accs = _swiglu(hs, wg_ref, wu_ref, wd_ref)
    for i, r in enumerate(rows):
        o_ref[r, :] = _post_norm(xs[i], accs[i], npost_ref[...], mod_ref, 0.5, seqs[i])


def _outffn_kernel(x_ref, ya_ref, yb_ref, yc_ref, mod1_ref, mod2_ref, npost1_ref, npre2_ref, npost2_ref,
                   wo_ref, wg_ref, wu_ref, wd_ref, o_ref):
    rows, seqs = _halves(x_ref.shape[0], mod1_ref.shape[0])
    xs = []
    for r, sq in zip(rows, seqs):
        y = (_dot(ya_ref[r, :], wo_ref[0:LRU_WIDTH, :])
             + _dot(yb_ref[r, :], wo_ref[LRU_WIDTH:LRU_WIDTH + FOX_WIDTH, :])
             + _dot(yc_ref[r, :], wo_ref[LRU_WIDTH + FOX_WIDTH:, :]))
        xs.append(_post_norm(x_ref[r, :], y, npost1_ref[...], mod1_ref, 1.0, sq))
    hs = [_pre_norm(x, npre2_ref[...], mod2_ref, sq).astype(BF16) for x, sq in zip(xs, seqs)]
    accs = _swiglu(hs, wg_ref, wu_ref, wd_ref)
    for i, r in enumerate(rows):
        o_ref[r, :] = _post_norm(xs[i], accs[i], npost2_ref[...], mod2_ref, 0.5, seqs[i])


def _outffn_call(x, ya, yb, yc, mod4, npre, npost, w_out, wg, wu, wd, layer, rows_per_seq):
    m = x.shape[0]
    tm = min(FFN_ROW_TILE, m)
    g, tps = _seq_grouping(rows_per_seq, tm)
    row = lambda w: pl.BlockSpec((tm, w), lambda i: (i, 0))
    once = dict(pipeline_mode=pl.Buffered(1))
    wspec = lambda shape: pl.BlockSpec((None, None) + shape, lambda i: (layer, 1, 0, 0), **once)
    norm = lambda sub: pl.BlockSpec((None, None, 1, D_MODEL), lambda i: (layer, sub, 0, 0))
    return pl.pallas_call(
        _outffn_kernel,
        grid=(m // tm,),
        in_specs=[
            row(D_MODEL), row(LRU_WIDTH), row(FOX_WIDTH), row(SSD_WIDTH),
            _mod_spec(g, tps, 1), _mod_spec(g, tps, 2),
            norm(1), norm(2), norm(2),
            pl.BlockSpec((None, D_MODEL, D_MODEL), lambda i: (layer, 0, 0), **once),
            wspec((D_MODEL, D_FF)), wspec((D_MODEL, D_FF)), wspec((D_FF, D_MODEL)),
        ],
        out_specs=row(D_MODEL),
        out_shape=jax.ShapeDtypeStruct((m, D_MODEL), F32),
        compiler_params=_params("outproj_ffn", ("arbitrary",)),
        name="outproj_ffn",
    )(x, ya, yb, yc, mod4, mod4, npost, npre, npost, w_out, wg, wu, wd)


def _ffn_call(x, mod4, npre, npost, wg, wu, wd, layer, sub, ffn_idx, rows_per_seq):
    m = x.shape[0]
    tm = min(FFN_ROW_TILE, m)
    g, tps = _seq_grouping(rows_per_seq, tm)
    wspec = lambda shape: pl.BlockSpec((None, None) + shape, lambda i: (layer, ffn_idx, 0, 0),
                                       pipeline_mode=pl.Buffered(1))
    nspec = pl.BlockSpec((None, None, 1, D_MODEL), lambda i: (layer, sub, 0, 0))
    return pl.pallas_call(
        _ffn_kernel,
        grid=(m // tm,),
        in_specs=[
            pl.BlockSpec((tm, D_MODEL), lambda i: (i, 0)),
            _mod_spec(g, tps, sub),
            nspec, nspec,
            wspec((D_MODEL, D_FF)), wspec((D_MODEL, D_FF)), wspec((D_FF, D_MODEL)),
        ],
        out_specs=pl.BlockSpec((tm, D_MODEL), lambda i: (i, 0)),
        out_shape=jax.ShapeDtypeStruct((m, D_MODEL), F32),
        compiler_params=_params("ffn", ("arbitrary",)),
        name="ffn",
    )(x, mod4, npre, npost, wg, wu, wd)


def _inproj_kernel(*refs, transposed, stacked):
    n_in = 8 if stacked else 6
    x_ref, mod_ref, npre_ref, w_ref, wsmall_ref, fbias_ref = refs[:6]
    lrux_ref, lrug_ref, q_ref, k_ref, v_ref, z_ref, xbc_ref, small_ref, *rest = refs[n_in:]
    wt_ref = rest[-1]

    @pl.when(pl.program_id(0) == 0)
    def _():
        src = int(np.cumsum(IN_SIZES)[4])
        assert src == COL_Z
        moves = [(0, 0, src), (src + FOX_HEADS, COL_Z, SSD_WIDTH), (src + FOX_HEADS + SSD_WIDTH, COL_XBC, SSD_CONV_DIM)]
        for s0, d0, n in moves:
            for off in range(0, n, LRU_WIDTH):
                wt_ref[d0 + off:d0 + off + LRU_WIDTH, :] = w_ref[s0 + off:s0 + off + LRU_WIDTH, :].astype(BF16)
        wt_ref[COL_SMALL:COL_SMALL + LANES, :] = wsmall_ref[...].astype(BF16)

    h = _pre_norm(x_ref[...], npre_ref[...], mod_ref).astype(BF16)
    col = lambda start, width: _dot_nt(h, wt_ref[start:start + width, :])
    col_t = lambda start, width: _dot_nt(wt_ref[start:start + width, :], h)
    lrux_ref[...] = col(COL_LRU_X, LRU_WIDTH)
    lrug_ref[...] = col(COL_LRU_G, LRU_WIDTH)
    q_ref[...] = (col(COL_Q, FOX_WIDTH) * Q_SCALE).astype(BF16)
    if transposed:
        kv_t = col_t(COL_K, 2 * FOX_WIDTH)
        if stacked:
            k_ref[0], v_ref[0] = refs[6][...], refs[7][...]
            k_ref[1], v_ref[1] = kv_t[:FOX_WIDTH], kv_t[FOX_WIDTH:]
        else:
            k_ref[...] = kv_t[:FOX_WIDTH]
            v_ref[...] = kv_t[FOX_WIDTH:]
    else:
        k_ref[...] = col(COL_K, FOX_WIDTH)
        v_ref[...] = col(COL_V, FOX_WIDTH)
    z_ref[...] = col(COL_Z, SSD_WIDTH)
    xbc_ref[...] = col(COL_XBC, SSD_CONV_DIM)
    small = col(COL_SMALL, LANES)
    t = small + fbias_ref[...]
    logf = jnp.minimum(t, 0.0) - jnp.log1p(jnp.exp(-jnp.abs(t)))
    lane = lax.broadcasted_iota(jnp.int32, small.shape, 1)
    small = jnp.where(lane < DT_LANE0, logf, small)
    small_ref[...] = small
    if transposed:
        rest[0][...] = small.T


def _inproj_call(x, mod4, npre, w_in_t, w_small, fbias, layer, rows_per_seq, transposed, prev_kv=None):
    m = x.shape[0]
    tm = min(ROW_TILE, m)
    g, tps = _seq_grouping(rows_per_seq, tm)
    widths = (LRU_WIDTH, LRU_WIDTH, FOX_WIDTH, FOX_WIDTH, FOX_WIDTH, SSD_WIDTH, SSD_CONV_DIM, LANES)
    dtypes = (F32, F32, BF16, F32, F32, F32, F32, F32)
    out_specs = [pl.BlockSpec((tm, w), lambda i: (i, 0)) for w in widths]
    out_shape = [jax.ShapeDtypeStruct((m, w), dt) for w, dt in zip(widths, dtypes)]
    in_specs = [
        pl.BlockSpec((tm, D_MODEL), lambda i: (i, 0)),
        _mod_spec(g, tps, 1),
        pl.BlockSpec((None, None, 1, D_MODEL), lambda i: (layer, 1, 0, 0)),
        pl.BlockSpec((None, D_IN, D_MODEL), lambda i: (layer, 0, 0), pipeline_mode=pl.Buffered(1)),
        pl.BlockSpec((None, LANES, D_MODEL), lambda i: (layer, 0, 0)),
        pl.BlockSpec((None, 1, LANES), lambda i: (layer, 0, 0)),
    ]
    if transposed:
        assert g == 1
        nseq = m // rows_per_seq
        for idx, w in ((3, FOX_WIDTH), (4, FOX_WIDTH), (len(widths), LANES)):
            spec = pl.BlockSpec((None, w, tm), lambda i: (i // tps, 0, i % tps))
            shape = jax.ShapeDtypeStruct((nseq, w, rows_per_seq), F32)
            out_specs[idx:idx + 1], out_shape[idx:idx + 1] = [spec], [shape]
    extra = ()
    if prev_kv is not None:
        assert transposed and layer == 1 and DEPTH == 2
        extra = tuple(prev_kv)
        in_specs += [pl.BlockSpec((None, FOX_WIDTH, tm), lambda i: (i // tps, 0, i % tps))] * 2
        for idx in (3, 4):
            out_specs[idx] = pl.BlockSpec((DEPTH, None, FOX_WIDTH, tm), lambda i: (0, i // tps, 0, i % tps))
            out_shape[idx] = jax.ShapeDtypeStruct((DEPTH, nseq, FOX_WIDTH, rows_per_seq), F32)
    return pl.pallas_call(
        functools.partial(_inproj_kernel, transposed=transposed, stacked=prev_kv is not None),
        grid=(m // tm,),
        in_specs=in_specs,
        out_specs=out_specs,
        out_shape=out_shape,
        scratch_shapes=[pltpu.VMEM((D_IN_PAD, D_MODEL), BF16)],
        compiler_params=_params("inproj", ("arbitrary",)),
        name="inproj",
    )(x, mod4, npre, w_in_t, w_small, fbias, *extra)


def _causal_conv(xx, cw, cb, n):
    u = cb + cw[0:1] * pltpu.roll(xx, 3, 0)[SUBLANES:SUBLANES + n]
    u = u + cw[1:2] * pltpu.roll(xx, 2, 0)[SUBLANES:SUBLANES + n]
    u = u + cw[2:3] * pltpu.roll(xx, 1, 0)[SUBLANES:SUBLANES + n]
    return u + cw[3:4] * xx[SUBLANES:SUBLANES + n]


def _last_rows(xx, k):
    return pltpu.roll(xx, k, 0)[0:SUBLANES][0:k]


def _tri(n, lower):
    r = lax.broadcasted_iota(jnp.int32, (n, n), 0)
    c = lax.broadcasted_iota(jnp.int32, (n, n), 1)
    return ((r >= c) if lower else (r <= c)).astype(F32).astype(BF16)


def _split3(x):
    hi = x.astype(BF16)
    r1 = x - hi.astype(F32)
    mid = r1.astype(BF16)
    lo = (r1 - mid.astype(F32)).astype(BF16)
    return hi, mid, lo


def _cumsum_lanes(x, triu):
    hi, mid, lo = _split3(x)
    return _dot(hi, triu) + _dot(mid, triu) + _dot(lo, triu)


def _cumsum_rows(x, tril):
    hi, mid, lo = _split3(x)
    return _dot(tril, hi) + _dot(tril, mid) + _dot(tril, lo)


def _lru_kernel(x_ref, g_ref, prev_ref, h0_ref, cw_ref, cb_ref, wgate_ref, bgate_ref, lam_ref,
                y_ref, convnew_ref, hnew_ref, a_scr, b_scr):
    n = x_ref.shape[0]
    xx = jnp.concatenate([prev_ref[...], x_ref[...]], axis=0)
    convnew_ref[...] = _last_rows(xx, CONV_W - 1)
    u = _causal_conv(xx, cw_ref[...], cb_ref[...], n)
    gates = _dot(u.astype(BF16), wgate_ref[...]) + bgate_ref[...]
    r = jax.nn.sigmoid(gates[:, :LRU_WIDTH])
    i = jax.nn.sigmoid(gates[:, LRU_WIDTH:])
    log_a = (-LRU_C * r) * _softplus(-lam_ref[...])
    a = jnp.exp(log_a)
    b = jnp.sqrt(-jnp.tanh(log_a) * (a * a + 1.0)) * (i * u)
    a = a.reshape(n // SUBLANES, SUBLANES, LRU_WIDTH)
    b = b.reshape(n // SUBLANES, SUBLANES, LRU_WIDTH)
    row = lax.broadcasted_iota(jnp.int32, (1, SUBLANES, 1), 1)
    for d in (1, 2, 4):
        keep = row >= d
        b = jnp.where(keep, a * pltpu.roll(b, d, 1) + b, b)
        a = jnp.where(keep, a * pltpu.roll(a, d, 1), a)
    a_scr[...] = a.reshape(n, LRU_WIDTH)
    b_scr[...] = b.reshape(n, LRU_WIDTH)

    def group(j, h):
        off = pl.multiple_of(j * SUBLANES, SUBLANES)
        hb = a_scr[pl.ds(off, SUBLANES), :] * h + b_scr[pl.ds(off, SUBLANES), :]
        b_scr[pl.ds(off, SUBLANES), :] = hb
        return jnp.broadcast_to(hb[SUBLANES - 1:SUBLANES, :], hb.shape)

    h_last = lax.fori_loop(0, n // SUBLANES, group,
                           jnp.broadcast_to(h0_ref[...], (SUBLANES, LRU_WIDTH)), unroll=4)
    hnew_ref[...] = h_last[0:1]
    y_ref[...] = (b_scr[...] * jax.nn.gelu(g_ref[...])).astype(BF16)


def _lru_call(lru_x, lru_g, prev8, h0, cw, cb, wgate, bgate, lam, layer, state_layer):
    bsz, n, _ = lru_x.shape
    seq = pl.BlockSpec((None, n, LRU_WIDTH), lambda b: (b, 0, 0))
    if state_layer is None:
        prev_spec = pl.BlockSpec((None, SUBLANES, LRU_WIDTH), lambda b: (0, 0, 0))
        h0_spec = pl.BlockSpec((None, 1, LRU_WIDTH), lambda b: (0, 0, 0))
    else:
        prev_spec = pl.BlockSpec((None, None, SUBLANES, LRU_WIDTH), lambda b: (state_layer, b, 0, 0))
        h0_spec = pl.BlockSpec((None, None, 1, LRU_WIDTH), lambda b: (state_layer, b, 0, 0))
    par = lambda r, w: pl.BlockSpec((None, r, w), lambda b: (layer, 0, 0))
    return pl.pallas_call(
        _lru_kernel,
        grid=(bsz,),
        in_specs=[seq, seq, prev_spec, h0_spec, par(CONV_W, LRU_WIDTH), par(1, LRU_WIDTH),
                  par(LRU_WIDTH, 2 * LRU_WIDTH), par(1, 2 * LRU_WIDTH), par(1, LRU_WIDTH)],
        out_specs=[seq,
                   pl.BlockSpec((None, CONV_W - 1, LRU_WIDTH), lambda b: (b, 0, 0)),
                   pl.BlockSpec((None, 1, LRU_WIDTH), lambda b: (b, 0, 0))],
        out_shape=[jax.ShapeDtypeStruct((bsz, n, LRU_WIDTH), BF16),
                   jax.ShapeDtypeStruct((bsz, CONV_W - 1, LRU_WIDTH), F32),
                   jax.ShapeDtypeStruct((bsz, 1, LRU_WIDTH), F32)],
        scratch_shapes=[pltpu.VMEM((n, LRU_WIDTH), F32), pltpu.VMEM((n, LRU_WIDTH), F32)],
        compiler_params=_params("rg_lru", ("arbitrary",)),
        name="rg_lru",
    )(lru_x, lru_g, prev8, h0, cw, cb, wgate, bgate, lam)


def _ssd_kernel(xbc_ref, z_ref, small_ref, prev_ref, h0_ref, cw_ref, cb_ref, dtb_ref, alog_ref, dvec_ref, nw_ref,
                y_ref, convnew_ref, h_ref, *, chunk):
    h_ref[...] = h0_ref[...]

    def one_chunk(c, tail):
        rows = pl.ds(pl.multiple_of(c * chunk, chunk), chunk)
        xx = jnp.concatenate([tail, xbc_ref[rows, :]], axis=0)
        _ssd_chunk(xx, z_ref[rows, :], small_ref[rows, :], cw_ref, cb_ref, dtb_ref, alog_ref, dvec_ref, nw_ref,
                   y_ref.at[rows, :], h_ref)
        return xx[chunk:chunk + SUBLANES]

    tail = lax.fori_loop(0, xbc_ref.shape[0] // chunk, one_chunk, prev_ref[...])
    convnew_ref[...] = _last_rows(tail, CONV_W - 1)


def _ssd_chunk(xx, z, small, cw_ref, cb_ref, dtb_ref, alog_ref, dvec_ref, nw_ref, y_ref, h_ref):
    n = xx.shape[0] - SUBLANES
    act = _silu(_causal_conv(xx, cw_ref[...], cb_ref[...], n))
    xs = act[:, :SSD_WIDTH]
    bm = act[:, SSD_WIDTH:SSD_WIDTH + SSD_GROUPS * D_STATE]
    cm = act[:, SSD_WIDTH + SSD_GROUPS * D_STATE:]

    dt = _softplus(small + dtb_ref[...])
    dta = dt * (-jnp.exp(alog_ref[...]))
    cum = _cumsum_rows(dta, _tri(n, True)) * LOG2E
    cum_t = cum.T
    rr = lax.broadcasted_iota(jnp.int32, (n, n), 0)
    cc = lax.broadcasted_iota(jnp.int32, (n, n), 1)
    causal = rr >= cc
    lo_lane = lax.broadcasted_iota(jnp.int32, (1, LANES), 1) < SSD_HEAD_DIM
    lo_row = lax.broadcasted_iota(jnp.int32, (LANES, 1), 0) < SSD_HEAD_DIM
    dvec = dvec_ref[...]

    ys = []
    for g in range(SSD_GROUPS):
        sl = slice(g * LANES, (g + 1) * LANES)
        xg, bg, cg = xs[:, sl], bm[:, sl].astype(BF16), cm[:, sl].astype(BF16)
        heads = (2 * g, 2 * g + 1)
        col = lambda a, h: a[:, DT_LANE0 + h:DT_LANE0 + h + 1]
        pick = lambda f: jnp.where(lo_lane, f(heads[0]), f(heads[1]))
        dx = xg * pick(lambda h: col(dt, h))
        dxb = dx.astype(BF16)
        cb_mat = _dot_nt(cg, bg)
        yd = []
        for h in heads:
            seg = col(cum, h) - cum_t[DT_LANE0 + h:DT_LANE0 + h + 1, :]
            lmat = jnp.exp2(jnp.where(causal, seg, NEG_BIG))
            yd.append(_dot((cb_mat * lmat).astype(BF16), dxb))
        y_diag = jnp.where(lo_lane, yd[0], yd[1])
        last = lambda h: col(cum, h)[n - 1:n, :]
        decay_end = pick(lambda h: jnp.exp2(last(h) - col(cum, h)))
        states = _dot_tn((dx * decay_end).astype(BF16), bg)
        h_prev = h_ref[sl, :]
        y_off = _dot_nt(cg, h_prev.astype(BF16)) * pick(lambda h: jnp.exp2(col(cum, h)))
        chunk_decay = jnp.where(lo_row, jnp.exp2(last(heads[0])), jnp.exp2(last(heads[1])))
        h_ref[sl, :] = chunk_decay * h_prev + states
        ys.append(y_diag + y_off + dvec[:, sl] * xg)
    y = jnp.concatenate(ys, axis=1)
    y_ref[...] = (_rms(y * _silu(z)) * nw_ref[...]).astype(BF16)


def _ssd_call(xbc, z, small, prev8, h0, cw, cb, dtb, alog, dvec, nw, layer, state_layer, chunk):
    bsz, n, _ = xbc.shape
    seq = lambda w: pl.BlockSpec((None, n, w), lambda b: (b, 0, 0))
    if state_layer is None:
        prev_spec = pl.BlockSpec((None, SUBLANES, SSD_CONV_DIM), lambda b: (0, 0, 0))
        h0_spec = pl.BlockSpec((None, SSD_WIDTH, D_STATE), lambda b: (0, 0, 0))
    else:
        prev_spec = pl.BlockSpec((None, None, SUBLANES, SSD_CONV_DIM), lambda b: (state_layer, b, 0, 0))
        h0_spec = pl.BlockSpec((None, None, SSD_WIDTH, D_STATE), lambda b: (state_layer, b, 0, 0))
    par = lambda r, w: pl.BlockSpec((None, r, w), lambda b: (layer, 0, 0))
    return pl.pallas_call(
        functools.partial(_ssd_kernel, chunk=chunk),
        grid=(bsz,),
        in_specs=[seq(SSD_CONV_DIM), seq(SSD_WIDTH), seq(LANES), prev_spec, h0_spec,
                  par(CONV_W, SSD_CONV_DIM), par(1, SSD_CONV_DIM), par(1, LANES), par(1, LANES),
                  par(1, SSD_WIDTH), par(1, SSD_WIDTH)],
        out_specs=[seq(SSD_WIDTH),
                   pl.BlockSpec((None, CONV_W - 1, SSD_CONV_DIM), lambda b: (b, 0, 0)),
                   pl.BlockSpec((None, SSD_WIDTH, D_STATE), lambda b: (b, 0, 0))],
        out_shape=[jax.ShapeDtypeStruct((bsz, n, SSD_WIDTH), BF16),
                   jax.ShapeDtypeStruct((bsz, CONV_W - 1, SSD_CONV_DIM), F32),
                   jax.ShapeDtypeStruct((bsz, SSD_WIDTH, D_STATE), F32)],
        compiler_params=_params("ssd", ("arbitrary",)),
        name="ssd",
    )(xbc, z, small, prev8, h0, cw, cb, dtb, alog, dvec, nw)


def _fox_prompt_kernel(q_ref, kt_ref, vt_ref, logft_ref, o_ref, k_scr, v_scr, f_scr, ft_scr):
    s = kt_ref.shape[1]
    tq = k_scr.shape[4]
    npair = k_scr.shape[0]
    spare = (FOX_HEAD_DIM, 0)

    @pl.when(pl.program_id(1) == 0)
    def _():
        triu = _tri(SCAN_BLOCK, False)
        carry = jnp.zeros((LANES, 1), F32)
        per = tq // SCAN_BLOCK
        for c in range(s // SCAN_BLOCK):
            cols = slice(c * SCAN_BLOCK, (c + 1) * SCAN_BLOCK)
            fc = _cumsum_lanes(logft_ref[:, cols], triu) + carry
            ft_scr[c // per, :, (c % per) * SCAN_BLOCK:(c % per + 1) * SCAN_BLOCK] = fc
            f_scr[cols, :] = fc.T
            carry = fc[:, SCAN_BLOCK - 1:SCAN_BLOCK]

    row = lax.broadcasted_iota(jnp.int32, (LANES, 1), 0)
    lane = lax.broadcasted_iota(jnp.int32, (1, LANES), 1)
    lo_lane = lane < FOX_HEAD_DIM
    rb = min(ATTN_ROWS, tq)
    nrb = tq // rb
    for pp in range(npair):
        _fox_pair(pp, q_ref, kt_ref, vt_ref, o_ref, k_scr.at[pp], v_scr.at[pp], f_scr, ft_scr,
                  HEAD_PAIR * (npair * pl.program_id(1) + pp), spare, row, lane, lo_lane, s, tq, rb, nrb)


def _fox_pair(pp, q_ref, kt_ref, vt_ref, o_ref, k_scr, v_scr, f_scr, ft_scr, head0, spare, row, lane, lo_lane,
              s, tq, rb, nrb):
    heads = (head0, head0 + 1)
    chan = slice(pp * LANES, (pp + 1) * LANES)
    for j in range(s // tq):
        kt = kt_ref[chan, j * tq:(j + 1) * tq]
        vt = vt_ref[chan, j * tq:(j + 1) * tq]
        for hl in range(HEAD_PAIR):
            own = (row < FOX_HEAD_DIM) if hl == 0 else (row >= FOX_HEAD_DIM)
            f_k = ft_scr[j, pl.ds(F_LANE0 + heads[hl], 1), :] * LOG2E
            kc = jnp.where(own, kt, 0.0).astype(BF16)
            for i, part in enumerate(_split3(-f_k)):
                kc = jnp.where(row == spare[hl] + i, part, kc)
            k_scr[hl, j] = kc
            v_scr[hl, j] = jnp.where(own, vt, jnp.where(row == spare[hl], 1.0, 0.0)).astype(BF16)

    def query_block(qi, _):
        rows = [pl.ds(pl.multiple_of(qi * tq + r * rb, rb), rb) for r in range(nrb)]
        q_blocks, fq_blocks = [], []
        for r in range(nrb):
            q = q_ref[rows[r], chan].astype(F32)
            ones_at = lambda l0: jnp.where((lane >= l0) & (lane < l0 + 3), 1.0, 0.0)
            q_blocks.append((jnp.where(lo_lane, q, ones_at(spare[0])).astype(BF16),
                             jnp.where(lo_lane, ones_at(spare[1]), q).astype(BF16)))
            f_rows = f_scr[rows[r], :]
            fq_blocks.append([jnp.sum(jnp.where(lane == F_LANE0 + h, f_rows, 0.0), axis=1, keepdims=True) * LOG2E
                              for h in heads])
        for g0 in range(0, nrb, ATTN_GROUP):
            rs = tuple(range(g0, min(g0 + ATTN_GROUP, nrb)))
            carry = lax.fori_loop(0, qi, lambda j, cr: step(j, cr, rs, q_blocks, fq_blocks, False),
                                  (init_rows,) * len(rs))
            carry = step(qi, carry, rs, q_blocks, fq_blocks, True)
            for i, r in enumerate(rs):
                acc = [carry[i][2 * hl + 1] for hl in range(HEAD_PAIR)]
                out = [a / a[:, spare[hl]:spare[hl] + 1] for hl, a in enumerate(acc)]
                o_ref[rows[r], chan] = jnp.where(lo_lane, out[0], out[1]).astype(BF16)
        return 0

    def softmax_unit(t, f_q, m_old, row0):
        if row0 is not None:
            rr = lax.broadcasted_iota(jnp.int32, t.shape, 0) + row0
            cc = lax.broadcasted_iota(jnp.int32, t.shape, 1)
            t = jnp.where(cc <= rr, t, NEG_BIG)
        m_new = jnp.maximum(m_old, jnp.max(t, axis=1, keepdims=True) + f_q)
        return m_new, jnp.exp2(m_old - m_new), jnp.exp2(t - (m_new - f_q)).astype(BF16)

    def step(j, carry, rs, q_blocks, fq_blocks, diagonal):
        nks = {r: (r + 1) * rb if diagonal else tq for r in rs}
        units = [(i, r, hl) for i, r in enumerate(rs) for hl in range(HEAD_PAIR)]
        dots = {(r, hl): _dot(q_blocks[r][hl], k_scr[hl, j, :, 0:nks[r]]) for _, r, hl in units}
        soft = {(r, hl): softmax_unit(dots[(r, hl)], fq_blocks[r][hl], carry[i][2 * hl], r * rb if diagonal else None)
                for i, r, hl in units}
        pvs = {(r, hl): _dot_nt(soft[(r, hl)][2], v_scr[hl, j, :, 0:nks[r]]) for _, r, hl in units}
        new = []
        for i, r in enumerate(rs):
            row_state = ()
            for hl in range(HEAD_PAIR):
                m_new, alpha, _ = soft[(r, hl)]
                row_state += (m_new, alpha * carry[i][2 * hl + 1] + pvs[(r, hl)])
            new.append(row_state)
        return tuple(new)

    init_rows = (jnp.full((rb, 1), NEG_BIG, F32), jnp.zeros((rb, LANES), F32)) * HEAD_PAIR
    lax.fori_loop(0, s // tq, query_block, 0)


def _fox_prompt_call(q, kt, vt, small_t, layer=None):
    bsz, s, _ = q.shape
    tq = min(ATTN_BLOCK, s)
    width = ATTN_PAIRS * LANES
    blk = pl.BlockSpec((None, s, width), lambda b, p: (b, 0, p))
    if layer is None:
        full = pl.BlockSpec((None, width, s), lambda b, p: (b, p, 0))
    else:
        full = pl.BlockSpec((None, None, width, s), lambda b, p: (layer, b, p, 0))
    return pl.pallas_call(
        _fox_prompt_kernel,
        grid=(bsz, FOX_WIDTH // width),
        in_specs=[blk, full, full, pl.BlockSpec((None, LANES, s), lambda b, p: (b, 0, 0))],
        out_specs=blk,
        out_shape=jax.ShapeDtypeStruct((bsz, s, FOX_WIDTH), BF16),
        scratch_shapes=[pltpu.VMEM((ATTN_PAIRS, HEAD_PAIR, s // tq, LANES, tq), BF16),
                        pltpu.VMEM((ATTN_PAIRS, HEAD_PAIR, s // tq, LANES, tq), BF16),
                        pltpu.VMEM((s, LANES), F32), pltpu.VMEM((s // tq, LANES, tq), F32)],
        compiler_params=_params("fox_prompt", ("arbitrary", "arbitrary")),
        name="fox_prompt",
    )(q, kt, vt, small_t)


def _fox_sample_kernel(q_ref, k_ref, v_ref, logf_ref, ck_ref, cv_ref, clogft_ref, o_ref,
                       m_scr, l_scr, acc_scr, fk_scr, fq_scr, fnew_scr):
    t = q_ref.shape[0]
    c = pl.program_id(1)
    nchunk, _, chunk = fk_scr.shape

    @pl.when(c == 0)
    def _():
        blk = min(SCAN_BLOCK, chunk)
        per = chunk // blk
        nblk = nchunk * per
        stacked = jnp.concatenate([clogft_ref[:, i * blk:(i + 1) * blk] for i in range(nblk)], axis=0)
        local = _cumsum_lanes(stacked, _tri(blk, False))
        nrow = nblk * FOX_HEADS
        rr = lax.broadcasted_iota(jnp.int32, (nrow, nrow), 0)
        cc = lax.broadcasted_iota(jnp.int32, (nrow, nrow), 1)
        earlier = ((rr % FOX_HEADS == cc % FOX_HEADS) & (cc < rr - rr % FOX_HEADS)).astype(F32).astype(BF16)
        totals = jnp.broadcast_to(local[:, blk - 1:blk], (nrow, LANES))
        f_all = local + _cumsum_rows(totals, earlier)[:, 0:1]
        for i in range(nblk):
            fk_scr[i // per, :, (i % per) * blk:(i % per + 1) * blk] = f_all[i * FOX_HEADS:(i + 1) * FOX_HEADS, :] * LOG2E
        carry = f_all[nrow - FOX_HEADS:, blk - 1:blk]
        sub = lax.broadcasted_iota(jnp.int32, (FOX_HEADS, LANES), 0)
        ln = lax.broadcasted_iota(jnp.int32, (FOX_HEADS, LANES), 1)
        total_row = jnp.sum(jnp.where(sub == ln - F_LANE0, carry, 0.0), axis=0, keepdims=True)
        f_new = _cumsum_rows(logf_ref[...], _tri(t, True)) + total_row
        f_new = f_new * LOG2E
        fnew_scr[...] = f_new.T[F_LANE0:F_LANE0 + FOX_HEADS, :]
        fq_scr[...] = jnp.concatenate([f_new[:, F_LANE0 + h:F_LANE0 + h + 1] for h in range(FOX_HEADS)], axis=0)
        m_scr[...] = jnp.full(m_scr.shape, NEG_BIG, F32)
        l_scr[...] = jnp.zeros(l_scr.shape, F32)
        acc_scr[...] = jnp.zeros(acc_scr.shape, F32)

    q = q_ref[...]
    q_heads = [q[:, h * FOX_HEAD_DIM:(h + 1) * FOX_HEAD_DIM] for h in range(FOX_HEADS)]
    f_q = fq_scr[...]

    def update(keys, values, f_k, mask, channels_first):
        qk = _dot if channels_first else _dot_nt
        pv_dot = _dot_nt if channels_first else _dot
        tt = jnp.concatenate([qk(q_heads[h], keys[h]) - f_k[h:h + 1, :] for h in range(FOX_HEADS)], axis=0)
        if mask is not None:
            tt = jnp.where(mask, tt, NEG_BIG)
        m_old = m_scr[...]
        m_new = jnp.maximum(m_old, jnp.max(tt, axis=1, keepdims=True) + f_q)
        alpha = jnp.exp2(m_old - m_new)
        pr = jnp.exp2(tt - (m_new - f_q))
        l_scr[...] = alpha * l_scr[...] + jnp.sum(pr, axis=1, keepdims=True)
        prb = pr.astype(BF16)
        pv = jnp.concatenate([pv_dot(prb[h * t:(h + 1) * t, :], values[h]) for h in range(FOX_HEADS)], axis=0)
        acc_scr[...] = alpha * acc_scr[...] + pv
        m_scr[...] = m_new

    head_rows = lambda ref: [ref[h * FOX_HEAD_DIM:(h + 1) * FOX_HEAD_DIM, :].astype(BF16) for h in range(FOX_HEADS)]
    update(head_rows(ck_ref), head_rows(cv_ref), fk_scr[c], None, True)

    @pl.when(c == nchunk - 1)
    def _():
        k_new, v_new = k_ref[...].astype(BF16), v_ref[...].astype(BF16)
        cols = lambda a: [a[:, h * FOX_HEAD_DIM:(h + 1) * FOX_HEAD_DIM] for h in range(FOX_HEADS)]
        rr = lax.broadcasted_iota(jnp.int32, (FOX_HEADS * t, t), 0) % t
        cc = lax.broadcasted_iota(jnp.int32, (FOX_HEADS * t, t), 1)
        update(cols(k_new), cols(v_new), fnew_scr[...], cc <= rr, False)
        out = acc_scr[...] / l_scr[...]
        o_ref[...] = jnp.concatenate([out[h * t:(h + 1) * t, :] for h in range(FOX_HEADS)], axis=1).astype(BF16)


def _fox_sample_call(q, k, v, small, cache_k, cache_v, cache_logf_t, layer):
    bsz, t, _ = q.shape
    past = cache_logf_t.shape[3]
    chunk = min(SAMPLE_CHUNK, past)
    nchunk = past // chunk
    new = lambda w: pl.BlockSpec((None, t, w), lambda b, c: (b, 0, 0))
    cache = pl.BlockSpec((None, None, FOX_WIDTH, chunk), lambda b, c: (layer, b, 0, c))
    rows = FOX_HEADS * t
    return pl.pallas_call(
        _fox_sample_kernel,
        grid=(bsz, nchunk),
        in_specs=[new(FOX_WIDTH), new(FOX_WIDTH), new(FOX_WIDTH), new(LANES), cache, cache,
                  pl.BlockSpec((None, None, FOX_HEADS, past), lambda b, c: (layer, b, 0, 0))],
        out_specs=new(FOX_WIDTH),
        out_shape=jax.ShapeDtypeStruct((bsz, t, FOX_WIDTH), BF16),
        scratch_shapes=[pltpu.VMEM((rows, 1), F32), pltpu.VMEM((rows, 1), F32), pltpu.VMEM((rows, FOX_HEAD_DIM), F32),
                        pltpu.VMEM((nchunk, FOX_HEADS, chunk), F32), pltpu.VMEM((rows, 1), F32),
                        pltpu.VMEM((FOX_HEADS, t), F32)],
        compiler_params=_params("fox_sample", ("arbitrary", "arbitrary")),
        name="fox_sample",
    )(q, k, v, small, cache_k, cache_v, cache_logf_t)


def _small_rows(w_in_t):
    offs = np.concatenate([[0], np.cumsum(IN_SIZES)])
    seg = lambda i: w_in_t[:, int(offs[i]):int(offs[i + 1]), :]
    pad = jnp.zeros((w_in_t.shape[0], LANES - FOX_HEADS - SSD_HEADS, w_in_t.shape[2]), w_in_t.dtype)
    return jnp.concatenate([seg(5), seg(8), pad], axis=1)


def _block_diag(w):
    d, h, b, _ = w.shape
    eye = jnp.eye(h, dtype=w.dtype)
    return jnp.einsum("dhij,hg->dhigj", w, eye).reshape(d, h * b, h * b)


def _lane_slab(v, lane0):
    d, k = v.shape
    return jnp.zeros((d, 1, LANES), v.dtype).at[:, 0, lane0:lane0 + k].set(v)


def _pad_history(state):
    return jnp.pad(state, ((0, 0), (0, 0), (SUBLANES - (CONV_W - 1), 0), (0, 0)))


def _trunk(x, mod_group, caches, prm, ssd_chunk):
    bsz, n, _ = x.shape
    x = x.reshape(bsz * n, D_MODEL)
    states = {name: [] for name in ("fox_k", "fox_v", "fox_logf", "lru_conv", "lru_h", "ssd_conv", "ssd_h")}
    prev_kv = None
    for l in range(DEPTH):
        mod4 = mod_group[l].reshape(bsz, N_SUB, 3, D_MODEL)
        x = _ffn_call(x, mod4, prm["npre"], prm["npost"], prm["wg"], prm["wu"], prm["wd"], l, 0, 0, n)
        prompt = caches is None
        stacking = prompt and l == DEPTH - 1
        proj = _inproj_call(x, mod4, prm["npre"], prm["w_in_t"], prm["w_small"], prm["fbias"], l, n,
                            transposed=prompt, prev_kv=prev_kv if stacking else None)
        lrux, lrug, q, k, v, z, xbc, small = proj[:8]
        per_seq = lambda a: a.reshape(bsz, n, a.shape[-1])
        state_layer = None if prompt else l
        src = prm["zero_state"] if prompt else caches
        ya, lru_conv, lru_h = _lru_call(per_seq(lrux), per_seq(lrug), src["lru_conv"], src["lru_h"],
                                        prm["lru_cw"], prm["lru_cb"], prm["lru_wgate"], prm["lru_bgate"],
                                        prm["lru_lam"], l, state_layer)
        if prompt:
            small_t = proj[8]
            yb = _fox_prompt_call(per_seq(q), k, v, small_t, l if stacking else None)
            prev_kv = (k, v)
            logf_out = jnp.swapaxes(small_t[:, F_LANE0:F_LANE0 + FOX_HEADS, :], 1, 2)
        else:
            yb = _fox_sample_call(per_seq(q), per_seq(k), per_seq(v), per_seq(small),
                                  caches["fox_k"], caches["fox_v"], caches["fox_logf_t"], l)
            k_out = k.reshape(bsz, n, FOX_HEADS, FOX_HEAD_DIM)
            v_out = v.reshape(bsz, n, FOX_HEADS, FOX_HEAD_DIM)
            logf_out = small.reshape(bsz, n, LANES)[:, :, F_LANE0:F_LANE0 + FOX_HEADS]
        yc, ssd_conv, ssd_h = _ssd_call(per_seq(xbc), per_seq(z), per_seq(small), src["ssd_conv"], src["ssd_h"],
                                        prm["ssd_cw"], prm["ssd_cb"], prm["ssd_dtb"], prm["ssd_alog"],
                                        prm["ssd_dvec"], prm["ssd_nw"], l, state_layer, ssd_chunk)
        x = _outffn_call(x, ya.reshape(bsz * n, -1), yb.reshape(bsz * n, -1), yc.reshape(bsz * n, -1),
                         mod4, prm["npre"], prm["npost"], prm["w_out"], prm["wg"], prm["wu"], prm["wd"], l, n)
        if not prompt:
            states["fox_k"].append(k_out)
            states["fox_v"].append(v_out)
        states["fox_logf"].append(logf_out)
        states["lru_conv"].append(lru_conv)
        states["lru_h"].append(lru_h.reshape(bsz, LRU_WIDTH))
        states["ssd_conv"].append(ssd_conv)
        states["ssd_h"].append(ssd_h.reshape(bsz, SSD_HEADS, SSD_HEAD_DIM, D_STATE))
    out = {name: jnp.stack(vals, axis=0) for name, vals in states.items() if vals}
    if prev_kv is not None:
        heads_last = lambda a: jnp.transpose(a.reshape(DEPTH, bsz, FOX_HEADS, FOX_HEAD_DIM, n), (0, 1, 4, 2, 3))
        out["fox_k"], out["fox_v"] = heads_last(prev_kv[0]), heads_last(prev_kv[1])
    return x.reshape(bsz, n, D_MODEL), out


def kernel(x_prompt, x_sample, c_prompt, c_sample, cache_fox_k, cache_fox_v, cache_fox_logf, state_lru_conv, state_lru_h, state_ssd_conv, state_ssd_h, w_mod, b_mod, norm_pre, norm_post, ffn_w_gate, ffn_w_up, ffn_w_down, w_in, w_out, lru_conv_w, lru_conv_b, lru_wa, lru_ba, lru_wx, lru_bx, lru_lambda, fox_f_bias, ssd_conv_w, ssd_conv_b, ssd_dt_bias, ssd_a_log, ssd_d, ssd_norm_w):
    n_prompt, n_sample = x_prompt.shape[0], x_sample.shape[0]
    w_in_t = jnp.swapaxes(w_in, 1, 2)
    prm = {
        "npre": norm_pre.reshape(DEPTH, N_SUB, 1, D_MODEL),
        "npost": norm_post.reshape(DEPTH, N_SUB, 1, D_MODEL),
        "wg": ffn_w_gate.astype(BF16), "wu": ffn_w_up.astype(BF16), "wd": ffn_w_down.astype(BF16),
        "w_in_t": w_in_t, "w_small": _small_rows(w_in_t), "w_out": w_out.astype(BF16),
        "fbias": _lane_slab(fox_f_bias, F_LANE0),
        "lru_cw": lru_conv_w, "lru_cb": lru_conv_b.reshape(DEPTH, 1, LRU_WIDTH),
        "lru_wgate": jnp.concatenate([_block_diag(lru_wa), _block_diag(lru_wx)], axis=-1).astype(BF16),
        "lru_bgate": jnp.concatenate([lru_ba, lru_bx], axis=-1).reshape(DEPTH, 1, 2 * LRU_WIDTH),
        "lru_lam": lru_lambda.reshape(DEPTH, 1, LRU_WIDTH),
        "ssd_cw": ssd_conv_w, "ssd_cb": ssd_conv_b.reshape(DEPTH, 1, SSD_CONV_DIM),
        "ssd_dtb": _lane_slab(ssd_dt_bias, DT_LANE0), "ssd_alog": _lane_slab(ssd_a_log, DT_LANE0),
        "ssd_dvec": jnp.repeat(ssd_d, SSD_HEAD_DIM, axis=-1).reshape(DEPTH, 1, SSD_WIDTH),
        "ssd_nw": ssd_norm_w.reshape(DEPTH, 1, SSD_WIDTH),
        "zero_state": {
            "lru_conv": jnp.zeros((1, SUBLANES, LRU_WIDTH), F32), "lru_h": jnp.zeros((1, 1, LRU_WIDTH), F32),
            "ssd_conv": jnp.zeros((1, SUBLANES, SSD_CONV_DIM), F32), "ssd_h": jnp.zeros((1, SSD_WIDTH, D_STATE), F32),
        },
    }
    caches = {
        "fox_k": jnp.transpose(cache_fox_k, (0, 1, 3, 4, 2)).reshape(DEPTH, n_sample, FOX_WIDTH, -1),
        "fox_v": jnp.transpose(cache_fox_v, (0, 1, 3, 4, 2)).reshape(DEPTH, n_sample, FOX_WIDTH, -1),
        "fox_logf_t": jnp.swapaxes(cache_fox_logf, 2, 3),
        "lru_conv": _pad_history(state_lru_conv),
        "lru_h": state_lru_h.reshape(DEPTH, n_sample, 1, LRU_WIDTH),
        "ssd_conv": _pad_history(state_ssd_conv),
        "ssd_h": state_ssd_h.reshape(DEPTH, n_sample, SSD_WIDTH, D_STATE),
    }
    mod = _mod_call(jnp.concatenate([c_prompt, c_sample], axis=0), w_mod, b_mod)
    y_prompt, sp = _trunk(x_prompt, mod[:, :n_prompt], None, prm, ssd_chunk=256)
    y_sample, ss = _trunk(x_sample, mod[:, n_prompt:], caches, prm, ssd_chunk=x_sample.shape[1])
    names = ("fox_k", "fox_v", "fox_logf", "lru_conv", "lru_h", "ssd_conv", "ssd_h")
    return (y_prompt, y_sample) + tuple(sp[n] for n in names) + tuple(ss[n] for n in names)
```

```python
import functools

import numpy as np
import jax
import jax.numpy as jnp
from jax import lax
from jax.experimental import pallas as pl
from jax.experimental.pallas import tpu as pltpu

F32 = jnp.float32
BF16 = jnp.bfloat16

D_MODEL = 1024
DEPTH = 2
CONV_W = 4
EPS = 1e-6
LRU_WIDTH = 256
LRU_C = 8.0
FOX_HEADS = 8
FOX_HEAD_DIM = 64
FOX_WIDTH = FOX_HEADS * FOX_HEAD_DIM
SSD_HEADS = 4
SSD_HEAD_DIM = 64
SSD_WIDTH = SSD_HEADS * SSD_HEAD_DIM
SSD_GROUPS = 2
D_STATE = 128
SSD_CONV_DIM = SSD_WIDTH + 2 * SSD_GROUPS * D_STATE
IN_SIZES = (LRU_WIDTH, LRU_WIDTH, FOX_WIDTH, FOX_WIDTH, FOX_WIDTH, FOX_HEADS, SSD_WIDTH, SSD_CONV_DIM, SSD_HEADS)
D_FF = 2816
N_SUB = 3

LANES = 128
SUBLANES = 8
HEAD_PAIR = LANES // FOX_HEAD_DIM

F_LANE0 = 0
DT_LANE0 = FOX_HEADS
COL_LRU_X = 0
COL_LRU_G = COL_LRU_X + LRU_WIDTH
COL_Q = COL_LRU_G + LRU_WIDTH
COL_K = COL_Q + FOX_WIDTH
COL_V = COL_K + FOX_WIDTH
COL_Z = COL_V + FOX_WIDTH
COL_XBC = COL_Z + SSD_WIDTH
COL_SMALL = COL_XBC + SSD_CONV_DIM
D_IN_PAD = COL_SMALL + LANES
D_IN = sum(IN_SIZES)

ROW_TILE = 512
FFN_ROW_TILE = 1024
FF_CHUNK = 256
ATTN_BLOCK = 2048
ATTN_ROWS = 128
ATTN_GROUP = 16
ATTN_PAIRS = 2
SCAN_BLOCK = 256
SAMPLE_CHUNK = 4096
NEG_BIG = -1e30
LOG2E = 1.4426950408889634
Q_SCALE = LOG2E * FOX_HEAD_DIM ** -0.5


def _dot(a, b):
    return jnp.dot(a, b, preferred_element_type=F32)


def _dot_nt(a, b):
    return lax.dot_general(a, b, (((1,), (1,)), ((), ())), preferred_element_type=F32)


def _dot_tn(a, b):
    return lax.dot_general(a, b, (((0,), (0,)), ((), ())), preferred_element_type=F32)


def _silu(x):
    return x * jax.nn.sigmoid(x)


def _softplus(x):
    return jnp.maximum(x, 0.0) + jnp.log1p(jnp.exp(-jnp.abs(x)))


def _rms(x):
    return x * lax.rsqrt(jnp.mean(x * x, axis=-1, keepdims=True) + EPS)


def _per_seq(rows, per_seq, fn):
    g = per_seq[0].shape[0]
    if g == 1:
        return fn(rows, *per_seq)
    tm, d = rows.shape
    out = fn(rows.reshape(g, tm // g, d), *[p[:, None, :] for p in per_seq])
    return out.reshape(tm, d)


def _pre_norm(x, npre, mod_ref, seqs=slice(None)):
    h = _rms(x) * npre
    return _per_seq(h, (mod_ref[seqs, 1, :], mod_ref[seqs, 0, :]), lambda r, sc, sh: r * (1.0 + sc) + sh)


def _post_norm(x, y, npost, mod_ref, w, seqs=slice(None)):
    yn = _rms(y) * npost
    return x + _per_seq(yn, (mod_ref[seqs, 2, :],), lambda r, gt: (w * gt) * r)


def _seq_grouping(rows_per_seq, tm):
    if rows_per_seq % tm == 0:
        return 1, rows_per_seq // tm
    assert tm % rows_per_seq == 0
    return tm // rows_per_seq, 1


def _mod_spec(g, tiles_per_seq, sub):
    if g == 1:
        return pl.BlockSpec((1, None, 3, D_MODEL), lambda i: (i // tiles_per_seq, sub, 0, 0))
    return pl.BlockSpec((g, None, 3, D_MODEL), lambda i: (i, sub, 0, 0))


VMEM_LIMIT_MB = {
    "adaln_mod": 24,
    "ffn": 56,
    "outproj_ffn": 57,
    "inproj": 56,
    "rg_lru": 40,
    "ssd": 40,
    "fox_prompt": 40,
    "fox_sample": 56,
}


def _params(name, sem):
    return pltpu.CompilerParams(dimension_semantics=sem, vmem_limit_bytes=VMEM_LIMIT_MB[name] << 20)


def _mod_kernel(c_ref, w_ref, b_ref, o_ref):
    a = _silu(c_ref[...]).astype(BF16)
    o_ref[...] = _dot(a, w_ref[...].astype(BF16)) + b_ref[...]


def _mod_call(c_all, w_mod, b_mod):
    nseq = c_all.shape[0]
    width = N_SUB * 3 * D_MODEL
    tn = 1024
    return pl.pallas_call(
        _mod_kernel,
        grid=(DEPTH, width // tn),
        in_specs=[
            pl.BlockSpec((nseq, D_MODEL), lambda l, n: (0, 0)),
            pl.BlockSpec((None, D_MODEL, tn), lambda l, n: (l, 0, n)),
            pl.BlockSpec((None, 1, tn), lambda l, n: (l, 0, n)),
        ],
        out_specs=pl.BlockSpec((None, nseq, tn), lambda l, n: (l, 0, n)),
        out_shape=jax.ShapeDtypeStruct((DEPTH, nseq, width), F32),
        compiler_params=_params("adaln_mod", ("arbitrary", "arbitrary")),
        name="adaln_mod",
    )(c_all, w_mod, b_mod.reshape(DEPTH, 1, width))


def _halves(tm, nseq):
    half = tm // 2
    rows = [slice(i * half, (i + 1) * half) for i in range(2)]
    seqs = [slice(0, 1)] * 2 if nseq == 1 else [slice(i * nseq // 2, (i + 1) * nseq // 2) for i in range(2)]
    return rows, seqs


def _swiglu(hs, wg_ref, wu_ref, wd_ref):
    accs = [None] * len(hs)
    for off in range(0, D_FF, FF_CHUNK):
        fc = min(FF_CHUNK, D_FF - off)
        for i, h in enumerate(hs):
            g = _dot(h, wg_ref[:, off:off + fc])
            u = _dot(h, wu_ref[:, off:off + fc])
            a = (_silu(g) * u).astype(BF16)
            d = _dot(a, wd_ref[off:off + fc, :])
            accs[i] = d if accs[i] is None else accs[i] + d
    return accs


def _ffn_kernel(x_ref, mod_ref, npre_ref, npost_ref, wg_ref, wu_ref, wd_ref, o_ref):
    rows, seqs = _halves(x_ref.shape[0], mod_ref.shape[0])
    xs = [x_ref[r, :] for r in rows]
    hs = [_pre_norm(x, npre_ref[...], mod_ref, sq).astype(BF16) for x, sq in zip(xs, seqs)]
    accs = _swiglu(hs, wg_ref, wu_ref, wd_ref)
    for i, r in enumerate(rows):
        o_ref[r, :] = _post_norm(xs[i], accs[i], npost_ref[...], mod_ref, 0.5, seqs[i])


def _outffn_kernel(x_ref, ya_ref, yb_ref, yc_ref, mod1_ref, mod2_ref, npost1_ref, npre2_ref, npost2_ref,
                   wo_ref, wg_ref, wu_ref, wd_ref, o_ref):
    rows, seqs = _halves(x_ref.shape[0], mod1_ref.shape[0])
    xs = []
    for r, sq in zip(rows, seqs):
        y = (_dot(ya_ref[r, :], wo_ref[0:LRU_WIDTH, :])
             + _dot(yb_ref[r, :], wo_ref[LRU_WIDTH:LRU_WIDTH + FOX_WIDTH, :])
             + _dot(yc_ref[r, :], wo_ref[LRU_WIDTH + FOX_WIDTH:, :]))
        xs.append(_post_norm(x_ref[r, :], y, npost1_ref[...], mod1_ref, 1.0, sq))
    hs = [_pre_norm(x, npre2_ref[...], mod2_ref, sq).astype(BF16) for x, sq in zip(xs, seqs)]
    accs = _swiglu(hs, wg_ref, wu_ref, wd_ref)
    for i, r in enumerate(rows):
        o_ref[r, :] = _post_norm(xs[i], accs[i], npost2_ref[...], mod2_ref, 0.5, seqs[i])


def _outffn_call(x, ya, yb, yc, mod4, npre, npost, w_out, wg, wu, wd, layer, rows_per_seq):
    m = x.shape[0]
    tm = min(FFN_ROW_TILE, m)
    g, tps = _seq_grouping(rows_per_seq, tm)
    row = lambda w: pl.BlockSpec((tm, w), lambda i: (i, 0))
    once = dict(pipeline_mode=pl.Buffered(1))
    wspec = lambda shape: pl.BlockSpec((None, None) + shape, lambda i: (layer, 1, 0, 0), **once)
    norm = lambda sub: pl.BlockSpec((None, None, 1, D_MODEL), lambda i: (layer, sub, 0, 0))
    return pl.pallas_call(
        _outffn_kernel,
        grid=(m // tm,),
        in_specs=[
            row(D_MODEL), row(LRU_WIDTH), row(FOX_WIDTH), row(SSD_WIDTH),
            _mod_spec(g, tps, 1), _mod_spec(g, tps, 2),
            norm(1), norm(2), norm(2),
            pl.BlockSpec((None, D_MODEL, D_MODEL), lambda i: (layer, 0, 0), **once),
            wspec((D_MODEL, D_FF)), wspec((D_MODEL, D_FF)), wspec((D_FF, D_MODEL)),
        ],
        out_specs=row(D_MODEL),
        out_shape=jax.ShapeDtypeStruct((m, D_MODEL), F32),
        compiler_params=_params("outproj_ffn", ("arbitrary",)),
        name="outproj_ffn",
    )(x, ya, yb, yc, mod4, mod4, npost, npre, npost, w_out, wg, wu, wd)


def _ffn_call(x, mod4, npre, npost, wg, wu, wd, layer, sub, ffn_idx, rows_per_seq):
    m = x.shape[0]
    tm = min(FFN_ROW_TILE, m)
    g, tps = _seq_grouping(rows_per_seq, tm)
    wspec = lambda shape: pl.BlockSpec((None, None) + shape, lambda i: (layer, ffn_idx, 0, 0),
                                       pipeline_mode=pl.Buffered(1))
    nspec = pl.BlockSpec((None, None, 1, D_MODEL), lambda i: (layer, sub, 0, 0))
    return pl.pallas_call(
        _ffn_kernel,
        grid=(m // tm,),
        in_specs=[
            pl.BlockSpec((tm, D_MODEL), lambda i: (i, 0)),
            _mod_spec(g, tps, sub),
            nspec, nspec,
            wspec((D_MODEL, D_FF)), wspec((D_MODEL, D_FF)), wspec((D_FF, D_MODEL)),
        ],
        out_specs=pl.BlockSpec((tm, D_MODEL), lambda i: (i, 0)),
        out_shape=jax.ShapeDtypeStruct((m, D_MODEL), F32),
        compiler_params=_params("ffn", ("arbitrary",)),
        name="ffn",
    )(x, mod4, npre, npost, wg, wu, wd)


def _inproj_kernel(*refs, transposed, stacked):
    n_in = 8 if stacked else 6
    x_ref, mod_ref, npre_ref, w_ref, wsmall_ref, fbias_ref = refs[:6]
    lrux_ref, lrug_ref, q_ref, k_ref, v_ref, z_ref, xbc_ref, small_ref, *rest = refs[n_in:]
    wt_ref = rest[-1]

    @pl.when(pl.program_id(0) == 0)
    def _():
        src = int(np.cumsum(IN_SIZES)[4])
        assert src == COL_Z
        moves = [(0, 0, src), (src + FOX_HEADS, COL_Z, SSD_WIDTH), (src + FOX_HEADS + SSD_WIDTH, COL_XBC, SSD_CONV_DIM)]
        for s0, d0, n in moves:
            for off in range(0, n, LRU_WIDTH):
                wt_ref[d0 + off:d0 + off + LRU_WIDTH, :] = w_ref[s0 + off:s0 + off + LRU_WIDTH, :].astype(BF16)
        wt_ref[COL_SMALL:COL_SMALL + LANES, :] = wsmall_ref[...].astype(BF16)

    h = _pre_norm(x_ref[...], npre_ref[...], mod_ref).astype(BF16)
    col = lambda start, width: _dot_nt(h, wt_ref[start:start + width, :])
    col_t = lambda start, width: _dot_nt(wt_ref[start:start + width, :], h)
    lrux_ref[...] = col(COL_LRU_X, LRU_WIDTH)
    lrug_ref[...] = col(COL_LRU_G, LRU_WIDTH)
    q_ref[...] = (col(COL_Q, FOX_WIDTH) * Q_SCALE).astype(BF16)
    if transposed:
        kv_t = col_t(COL_K, 2 * FOX_WIDTH)
        if stacked:
            k_ref[0], v_ref[0] = refs[6][...], refs[7][...]
            k_ref[1], v_ref[1] = kv_t[:FOX_WIDTH], kv_t[FOX_WIDTH:]
        else:
            k_ref[...] = kv_t[:FOX_WIDTH]
            v_ref[...] = kv_t[FOX_WIDTH:]
    else:
        k_ref[...] = col(COL_K, FOX_WIDTH)
        v_ref[...] = col(COL_V, FOX_WIDTH)
    z_ref[...] = col(COL_Z, SSD_WIDTH)
    xbc_ref[...] = col(COL_XBC, SSD_CONV_DIM)
    small = col(COL_SMALL, LANES)
    t = small + fbias_ref[...]
    logf = jnp.minimum(t, 0.0) - jnp.log1p(jnp.exp(-jnp.abs(t)))
    lane = lax.broadcasted_iota(jnp.int32, small.shape, 1)
    small = jnp.where(lane < DT_LANE0, logf, small)
    small_ref[...] = small
    if transposed:
        rest[0][...] = small.T


def _inproj_call(x, mod4, npre, w_in_t, w_small, fbias, layer, rows_per_seq, transposed, prev_kv=None):
    m = x.shape[0]
    tm = min(ROW_TILE, m)
    g, tps = _seq_grouping(rows_per_seq, tm)
    widths = (LRU_WIDTH, LRU_WIDTH, FOX_WIDTH, FOX_WIDTH, FOX_WIDTH, SSD_WIDTH, SSD_CONV_DIM, LANES)
    dtypes = (F32, F32, BF16, F32, F32, F32, F32, F32)
    out_specs = [pl.BlockSpec((tm, w), lambda i: (i, 0)) for w in widths]
    out_shape = [jax.ShapeDtypeStruct((m, w), dt) for w, dt in zip(widths, dtypes)]
    in_specs = [
        pl.BlockSpec((tm, D_MODEL), lambda i: (i, 0)),
        _mod_spec(g, tps, 1),
        pl.BlockSpec((None, None, 1, D_MODEL), lambda i: (layer, 1, 0, 0)),
        pl.BlockSpec((None, D_IN, D_MODEL), lambda i: (layer, 0, 0), pipeline_mode=pl.Buffered(1)),
        pl.BlockSpec((None, LANES, D_MODEL), lambda i: (layer, 0, 0)),
        pl.BlockSpec((None, 1, LANES), lambda i: (layer, 0, 0)),
    ]
    if transposed:
        assert g == 1
        nseq = m // rows_per_seq
        for idx, w in ((3, FOX_WIDTH), (4, FOX_WIDTH), (len(widths), LANES)):
            spec = pl.BlockSpec((None, w, tm), lambda i: (i // tps, 0, i % tps))
            shape = jax.ShapeDtypeStruct((nseq, w, rows_per_seq), F32)
            out_specs[idx:idx + 1], out_shape[idx:idx + 1] = [spec], [shape]
    extra = ()
    if prev_kv is not None:
        assert transposed and layer == 1 and DEPTH == 2
        extra = tuple(prev_kv)
        in_specs += [pl.BlockSpec((None, FOX_WIDTH, tm), lambda i: (i // tps, 0, i % tps))] * 2
        for idx in (3, 4):
            out_specs[idx] = pl.BlockSpec((DEPTH, None, FOX_WIDTH, tm), lambda i: (0, i // tps, 0, i % tps))
            out_shape[idx] = jax.ShapeDtypeStruct((DEPTH, nseq, FOX_WIDTH, rows_per_seq), F32)
    return pl.pallas_call(
        functools.partial(_inproj_kernel, transposed=transposed, stacked=prev_kv is not None),
        grid=(m // tm,),
        in_specs=in_specs,
        out_specs=out_specs,
        out_shape=out_shape,
        scratch_shapes=[pltpu.VMEM((D_IN_PAD, D_MODEL), BF16)],
        compiler_params=_params("inproj", ("arbitrary",)),
        name="inproj",
    )(x, mod4, npre, w_in_t, w_small, fbias, *extra)


def _causal_conv(xx, cw, cb, n):
    u = cb + cw[0:1] * pltpu.roll(xx, 3, 0)[SUBLANES:SUBLANES + n]
    u = u + cw[1:2] * pltpu.roll(xx, 2, 0)[SUBLANES:SUBLANES + n]
    u = u + cw[2:3] * pltpu.roll(xx, 1, 0)[SUBLANES:SUBLANES + n]
    return u + cw[3:4] * xx[SUBLANES:SUBLANES + n]


def _last_rows(xx, k):
    return pltpu.roll(xx, k, 0)[0:SUBLANES][0:k]


def _tri(n, lower):
    r = lax.broadcasted_iota(jnp.int32, (n, n), 0)
    c = lax.broadcasted_iota(jnp.int32, (n, n), 1)
    return ((r >= c) if lower else (r <= c)).astype(F32).astype(BF16)


def _split3(x):
    hi = x.astype(BF16)
    r1 = x - hi.astype(F32)
    mid = r1.astype(BF16)
    lo = (r1 - mid.astype(F32)).astype(BF16)
    return hi, mid, lo


def _cumsum_lanes(x, triu):
    hi, mid, lo = _split3(x)
    return _dot(hi, triu) + _dot(mid, triu) + _dot(lo, triu)


def _cumsum_rows(x, tril):
    hi, mid, lo = _split3(x)
    return _dot(tril, hi) + _dot(tril, mid) + _dot(tril, lo)


def _lru_kernel(x_ref, g_ref, prev_ref, h0_ref, cw_ref, cb_ref, wgate_ref, bgate_ref, lam_ref,
                y_ref, convnew_ref, hnew_ref, a_scr, b_scr):
    n = x_ref.shape[0]
    xx = jnp.concatenate([prev_ref[...], x_ref[...]], axis=0)
    convnew_ref[...] = _last_rows(xx, CONV_W - 1)
    u = _causal_conv(xx, cw_ref[...], cb_ref[...], n)
    gates = _dot(u.astype(BF16), wgate_ref[...]) + bgate_ref[...]
    r = jax.nn.sigmoid(gates[:, :LRU_WIDTH])
    i = jax.nn.sigmoid(gates[:, LRU_WIDTH:])
    log_a = (-LRU_C * r) * _softplus(-lam_ref[...])
    a = jnp.exp(log_a)
    b = jnp.sqrt(-jnp.tanh(log_a) * (a * a + 1.0)) * (i * u)
    a = a.reshape(n // SUBLANES, SUBLANES, LRU_WIDTH)
    b = b.reshape(n // SUBLANES, SUBLANES, LRU_WIDTH)
    row = lax.broadcasted_iota(jnp.int32, (1, SUBLANES, 1), 1)
    for d in (1, 2, 4):
        keep = row >= d
        b = jnp.where(keep, a * pltpu.roll(b, d, 1) + b, b)
        a = jnp.where(keep, a * pltpu.roll(a, d, 1), a)
    a_scr[...] = a.reshape(n, LRU_WIDTH)
    b_scr[...] = b.reshape(n, LRU_WIDTH)

    def group(j, h):
        off = pl.multiple_of(j * SUBLANES, SUBLANES)
        hb = a_scr[pl.ds(off, SUBLANES), :] * h + b_scr[pl.ds(off, SUBLANES), :]
        b_scr[pl.ds(off, SUBLANES), :] = hb
        return jnp.broadcast_to(hb[SUBLANES - 1:SUBLANES, :], hb.shape)

    h_last = lax.fori_loop(0, n // SUBLANES, group,
                           jnp.broadcast_to(h0_ref[...], (SUBLANES, LRU_WIDTH)), unroll=4)
    hnew_ref[...] = h_last[0:1]
    y_ref[...] = (b_scr[...] * jax.nn.gelu(g_ref[...])).astype(BF16)


def _lru_call(lru_x, lru_g, prev8, h0, cw, cb, wgate, bgate, lam, layer, state_layer):
    bsz, n, _ = lru_x.shape
    seq = pl.BlockSpec((None, n, LRU_WIDTH), lambda b: (b, 0, 0))
    if state_layer is None:
        prev_spec = pl.BlockSpec((None, SUBLANES, LRU_WIDTH), lambda b: (0, 0, 0))
        h0_spec = pl.BlockSpec((None, 1, LRU_WIDTH), lambda b: (0, 0, 0))
    else:
        prev_spec = pl.BlockSpec((None, None, SUBLANES, LRU_WIDTH), lambda b: (state_layer, b, 0, 0))
        h0_spec = pl.BlockSpec((None, None, 1, LRU_WIDTH), lambda b: (state_layer, b, 0, 0))
    par = lambda r, w: pl.BlockSpec((None, r, w), lambda b: (layer, 0, 0))
    return pl.pallas_call(
        _lru_kernel,
        grid=(bsz,),
        in_specs=[seq, seq, prev_spec, h0_spec, par(CONV_W, LRU_WIDTH), par(1, LRU_WIDTH),
                  par(LRU_WIDTH, 2 * LRU_WIDTH), par(1, 2 * LRU_WIDTH), par(1, LRU_WIDTH)],
        out_specs=[seq,
                   pl.BlockSpec((None, CONV_W - 1, LRU_WIDTH), lambda b: (b, 0, 0)),
                   pl.BlockSpec((None, 1, LRU_WIDTH), lambda b: (b, 0, 0))],
        out_shape=[jax.ShapeDtypeStruct((bsz, n, LRU_WIDTH), BF16),
                   jax.ShapeDtypeStruct((bsz, CONV_W - 1, LRU_WIDTH), F32),
                   jax.ShapeDtypeStruct((bsz, 1, LRU_WIDTH), F32)],
        scratch_shapes=[pltpu.VMEM((n, LRU_WIDTH), F32), pltpu.VMEM((n, LRU_WIDTH), F32)],
        compiler_params=_params("rg_lru", ("arbitrary",)),
        name="rg_lru",
    )(lru_x, lru_g, prev8, h0, cw, cb, wgate, bgate, lam)


def _ssd_kernel(xbc_ref, z_ref, small_ref, prev_ref, h0_ref, cw_ref, cb_ref, dtb_ref, alog_ref, dvec_ref, nw_ref,
                y_ref, convnew_ref, h_ref, *, chunk):
    h_ref[...] = h0_ref[...]

    def one_chunk(c, tail):
        rows = pl.ds(pl.multiple_of(c * chunk, chunk), chunk)
        xx = jnp.concatenate([tail, xbc_ref[rows, :]], axis=0)
        _ssd_chunk(xx, z_ref[rows, :], small_ref[rows, :], cw_ref, cb_ref, dtb_ref, alog_ref, dvec_ref, nw_ref,
                   y_ref.at[rows, :], h_ref)
        return xx[chunk:chunk + SUBLANES]

    tail = lax.fori_loop(0, xbc_ref.shape[0] // chunk, one_chunk, prev_ref[...])
    convnew_ref[...] = _last_rows(tail, CONV_W - 1)


def _ssd_chunk(xx, z, small, cw_ref, cb_ref, dtb_ref, alog_ref, dvec_ref, nw_ref, y_ref, h_ref):
    n = xx.shape[0] - SUBLANES
    act = _silu(_causal_conv(xx, cw_ref[...], cb_ref[...], n))
    xs = act[:, :SSD_WIDTH]
    bm = act[:, SSD_WIDTH:SSD_WIDTH + SSD_GROUPS * D_STATE]
    cm = act[:, SSD_WIDTH + SSD_GROUPS * D_STATE:]

    dt = _softplus(small + dtb_ref[...])
    dta = dt * (-jnp.exp(alog_ref[...]))
    cum = _cumsum_rows(dta, _tri(n, True)) * LOG2E
    cum_t = cum.T
    rr = lax.broadcasted_iota(jnp.int32, (n, n), 0)
    cc = lax.broadcasted_iota(jnp.int32, (n, n), 1)
    causal = rr >= cc
    lo_lane = lax.broadcasted_iota(jnp.int32, (1, LANES), 1) < SSD_HEAD_DIM
    lo_row = lax.broadcasted_iota(jnp.int32, (LANES, 1), 0) < SSD_HEAD_DIM
    dvec = dvec_ref[...]

    ys = []
    for g in range(SSD_GROUPS):
        sl = slice(g * LANES, (g + 1) * LANES)
        xg, bg, cg = xs[:, sl], bm[:, sl].astype(BF16), cm[:, sl].astype(BF16)
        heads = (2 * g, 2 * g + 1)
        col = lambda a, h: a[:, DT_LANE0 + h:DT_LANE0 + h + 1]
        pick = lambda f: jnp.where(lo_lane, f(heads[0]), f(heads[1]))
        dx = xg * pick(lambda h: col(dt, h))
        dxb = dx.astype(BF16)
        cb_mat = _dot_nt(cg, bg)
        yd = []
        for h in heads:
            seg = col(cum, h) - cum_t[DT_LANE0 + h:DT_LANE0 + h + 1, :]
            lmat = jnp.exp2(jnp.where(causal, seg, NEG_BIG))
            yd.append(_dot((cb_mat * lmat).astype(BF16), dxb))
        y_diag = jnp.where(lo_lane, yd[0], yd[1])
        last = lambda h: col(cum, h)[n - 1:n, :]
        decay_end = pick(lambda h: jnp.exp2(last(h) - col(cum, h)))
        states = _dot_tn((dx * decay_end).astype(BF16), bg)
        h_prev = h_ref[sl, :]
        y_off = _dot_nt(cg, h_prev.astype(BF16)) * pick(lambda h: jnp.exp2(col(cum, h)))
        chunk_decay = jnp.where(lo_row, jnp.exp2(last(heads[0])), jnp.exp2(last(heads[1])))
        h_ref[sl, :] = chunk_decay * h_prev + states
        ys.append(y_diag + y_off + dvec[:, sl] * xg)
    y = jnp.concatenate(ys, axis=1)
    y_ref[...] = (_rms(y * _silu(z)) * nw_ref[...]).astype(BF16)


def _ssd_call(xbc, z, small, prev8, h0, cw, cb, dtb, alog, dvec, nw, layer, state_layer, chunk):
    bsz, n, _ = xbc.shape
    seq = lambda w: pl.BlockSpec((None, n, w), lambda b: (b, 0, 0))
    if state_layer is None:
        prev_spec = pl.BlockSpec((None, SUBLANES, SSD_CONV_DIM), lambda b: (0, 0, 0))
        h0_spec = pl.BlockSpec((None, SSD_WIDTH, D_STATE), lambda b: (0, 0, 0))
    else:
        prev_spec = pl.BlockSpec((None, None, SUBLANES, SSD_CONV_DIM), lambda b: (state_layer, b, 0, 0))
        h0_spec = pl.BlockSpec((None, None, SSD_WIDTH, D_STATE), lambda b: (state_layer, b, 0, 0))
    par = lambda r, w: pl.BlockSpec((None, r, w), lambda b: (layer, 0, 0))
    return pl.pallas_call(
        functools.partial(_ssd_kernel, chunk=chunk),
        grid=(bsz,),
        in_specs=[seq(SSD_CONV_DIM), seq(SSD_WIDTH), seq(LANES), prev_spec, h0_spec,
                  par(CONV_W, SSD_CONV_DIM), par(1, SSD_CONV_DIM), par(1, LANES), par(1, LANES),
                  par(1, SSD_WIDTH), par(1, SSD_WIDTH)],
        out_specs=[seq(SSD_WIDTH),
                   pl.BlockSpec((None, CONV_W - 1, SSD_CONV_DIM), lambda b: (b, 0, 0)),
                   pl.BlockSpec((None, SSD_WIDTH, D_STATE), lambda b: (b, 0, 0))],
        out_shape=[jax.ShapeDtypeStruct((bsz, n, SSD_WIDTH), BF16),
                   jax.ShapeDtypeStruct((bsz, CONV_W - 1, SSD_CONV_DIM), F32),
                   jax.ShapeDtypeStruct((bsz, SSD_WIDTH, D_STATE), F32)],
        compiler_params=_params("ssd", ("arbitrary",)),
        name="ssd",
    )(xbc, z, small, prev8, h0, cw, cb, dtb, alog, dvec, nw)


def _fox_prompt_kernel(q_ref, kt_ref, vt_ref, logft_ref, o_ref, k_scr, v_scr, f_scr, ft_scr):
    s = kt_ref.shape[1]
    tq = k_scr.shape[4]
    npair = k_scr.shape[0]
    spare = (FOX_HEAD_DIM, 0)

    @pl.when(pl.program_id(1) == 0)
    def _():
        triu = _tri(SCAN_BLOCK, False)
        carry = jnp.zeros((LANES, 1), F32)
        per = tq // SCAN_BLOCK
        for c in range(s // SCAN_BLOCK):
            cols = slice(c * SCAN_BLOCK, (c + 1) * SCAN_BLOCK)
            fc = _cumsum_lanes(logft_ref[:, cols], triu) + carry
            ft_scr[c // per, :, (c % per) * SCAN_BLOCK:(c % per + 1) * SCAN_BLOCK] = fc
            f_scr[cols, :] = fc.T
            carry = fc[:, SCAN_BLOCK - 1:SCAN_BLOCK]

    row = lax.broadcasted_iota(jnp.int32, (LANES, 1), 0)
    lane = lax.broadcasted_iota(jnp.int32, (1, LANES), 1)
    lo_lane = lane < FOX_HEAD_DIM
    rb = min(ATTN_ROWS, tq)
    nrb = tq // rb
    for pp in range(npair):
        _fox_pair(pp, q_ref, kt_ref, vt_ref, o_ref, k_scr.at[pp], v_scr.at[pp], f_scr, ft_scr,
                  HEAD_PAIR * (npair * pl.program_id(1) + pp), spare, row, lane, lo_lane, s, tq, rb, nrb)


def _fox_pair(pp, q_ref, kt_ref, vt_ref, o_ref, k_scr, v_scr, f_scr, ft_scr, head0, spare, row, lane, lo_lane,
              s, tq, rb, nrb):
    heads = (head0, head0 + 1)
    chan = slice(pp * LANES, (pp + 1) * LANES)
    for j in range(s // tq):
        kt = kt_ref[chan, j * tq:(j + 1) * tq]
        vt = vt_ref[chan, j * tq:(j + 1) * tq]
        for hl in range(HEAD_PAIR):
            own = (row < FOX_HEAD_DIM) if hl == 0 else (row >= FOX_HEAD_DIM)
            f_k = ft_scr[j, pl.ds(F_LANE0 + heads[hl], 1), :] * LOG2E
            kc = jnp.where(own, kt, 0.0).astype(BF16)
            for i, part in enumerate(_split3(-f_k)):
                kc = jnp.where(row == spare[hl] + i, part, kc)
            k_scr[hl, j] = kc
            v_scr[hl, j] = jnp.where(own, vt, jnp.where(row == spare[hl], 1.0, 0.0)).astype(BF16)

    def query_block(qi, _):
        rows = [pl.ds(pl.multiple_of(qi * tq + r * rb, rb), rb) for r in range(nrb)]
        q_blocks, fq_blocks = [], []
        for r in range(nrb):
            q = q_ref[rows[r], chan].astype(F32)
            ones_at = lambda l0: jnp.where((lane >= l0) & (lane < l0 + 3), 1.0, 0.0)
            q_blocks.append((jnp.where(lo_lane, q, ones_at(spare[0])).astype(BF16),
                             jnp.where(lo_lane, ones_at(spare[1]), q).astype(BF16)))
            f_rows = f_scr[rows[r], :]
            fq_blocks.append([jnp.sum(jnp.where(lane == F_LANE0 + h, f_rows, 0.0), axis=1, keepdims=True) * LOG2E
                              for h in heads])
        for g0 in range(0, nrb, ATTN_GROUP):
            rs = tuple(range(g0, min(g0 + ATTN_GROUP, nrb)))
            carry = lax.fori_loop(0, qi, lambda j, cr: step(j, cr, rs, q_blocks, fq_blocks, False),
                                  (init_rows,) * len(rs))
            carry = step(qi, carry, rs, q_blocks, fq_blocks, True)
            for i, r in enumerate(rs):
                acc = [carry[i][2 * hl + 1] for hl in range(HEAD_PAIR)]
                out = [a / a[:, spare[hl]:spare[hl] + 1] for hl, a in enumerate(acc)]
                o_ref[rows[r], chan] = jnp.where(lo_lane, out[0], out[1]).astype(BF16)
        return 0

    def softmax_unit(t, f_q, m_old, row0):
        if row0 is not None:
            rr = lax.broadcasted_iota(jnp.int32, t.shape, 0) + row0
            cc = lax.broadcasted_iota(jnp.int32, t.shape, 1)
            t = jnp.where(cc <= rr, t, NEG_BIG)
        m_new = jnp.maximum(m_old, jnp.max(t, axis=1, keepdims=True) + f_q)
        return m_new, jnp.exp2(m_old - m_new), jnp.exp2(t - (m_new - f_q)).astype(BF16)

    def step(j, carry, rs, q_blocks, fq_blocks, diagonal):
        nks = {r: (r + 1) * rb if diagonal else tq for r in rs}
        units = [(i, r, hl) for i, r in enumerate(rs) for hl in range(HEAD_PAIR)]
        dots = {(r, hl): _dot(q_blocks[r][hl], k_scr[hl, j, :, 0:nks[r]]) for _, r, hl in units}
        soft = {(r, hl): softmax_unit(dots[(r, hl)], fq_blocks[r][hl], carry[i][2 * hl], r * rb if diagonal else None)
                for i, r, hl in units}
        pvs = {(r, hl): _dot_nt(soft[(r, hl)][2], v_scr[hl, j, :, 0:nks[r]]) for _, r, hl in units}
        new = []
        for i, r in enumerate(rs):
            row_state = ()
            for hl in range(HEAD_PAIR):
                m_new, alpha, _ = soft[(r, hl)]
                row_state += (m_new, alpha * carry[i][2 * hl + 1] + pvs[(r, hl)])
            new.append(row_state)
        return tuple(new)

    init_rows = (jnp.full((rb, 1), NEG_BIG, F32), jnp.zeros((rb, LANES), F32)) * HEAD_PAIR
    lax.fori_loop(0, s // tq, query_block, 0)


def _fox_prompt_call(q, kt, vt, small_t, layer=None):
    bsz, s, _ = q.shape
    tq = min(ATTN_BLOCK, s)
    width = ATTN_PAIRS * LANES
    blk = pl.BlockSpec((None, s, width), lambda b, p: (b, 0, p))
    if layer is None:
        full = pl.BlockSpec((None, width, s), lambda b, p: (b, p, 0))
    else:
        full = pl.BlockSpec((None, None, width, s), lambda b, p: (layer, b, p, 0))
    return pl.pallas_call(
        _fox_prompt_kernel,
        grid=(bsz, FOX_WIDTH // width),
        in_specs=[blk, full, full, pl.BlockSpec((None, LANES, s), lambda b, p: (b, 0, 0))],
        out_specs=blk,
        out_shape=jax.ShapeDtypeStruct((bsz, s, FOX_WIDTH), BF16),
        scratch_shapes=[pltpu.VMEM((ATTN_PAIRS, HEAD_PAIR, s // tq, LANES, tq), BF16),
                        pltpu.VMEM((ATTN_PAIRS, HEAD_PAIR, s // tq, LANES, tq), BF16),
                        pltpu.VMEM((s, LANES), F32), pltpu.VMEM((s // tq, LANES, tq), F32)],
        compiler_params=_params("fox_prompt", ("arbitrary", "arbitrary")),
        name="fox_prompt",
    )(q, kt, vt, small_t)


def _fox_sample_kernel(q_ref, k_ref, v_ref, logf_ref, ck_ref, cv_ref, clogft_ref, o_ref,
                       m_scr, l_scr, acc_scr, fk_scr, fq_scr, fnew_scr):
    t = q_ref.shape[0]
    c = pl.program_id(1)
    nchunk, _, chunk = fk_scr.shape

    @pl.when(c == 0)
    def _():
        blk = min(SCAN_BLOCK, chunk)
        per = chunk // blk
        nblk = nchunk * per
        stacked = jnp.concatenate([clogft_ref[:, i * blk:(i + 1) * blk] for i in range(nblk)], axis=0)
        local = _cumsum_lanes(stacked, _tri(blk, False))
        nrow = nblk * FOX_HEADS
        rr = lax.broadcasted_iota(jnp.int32, (nrow, nrow), 0)
        cc = lax.broadcasted_iota(jnp.int32, (nrow, nrow), 1)
        earlier = ((rr % FOX_HEADS == cc % FOX_HEADS) & (cc < rr - rr % FOX_HEADS)).astype(F32).astype(BF16)
        totals = jnp.broadcast_to(local[:, blk - 1:blk], (nrow, LANES))
        f_all = local + _cumsum_rows(totals, earlier)[:, 0:1]
        for i in range(nblk):
            fk_scr[i // per, :, (i % per) * blk:(i % per + 1) * blk] = f_all[i * FOX_HEADS:(i + 1) * FOX_HEADS, :] * LOG2E
        carry = f_all[nrow - FOX_HEADS:, blk - 1:blk]
        sub = lax.broadcasted_iota(jnp.int32, (FOX_HEADS, LANES), 0)
        ln = lax.broadcasted_iota(jnp.int32, (FOX_HEADS, LANES), 1)
        total_row = jnp.sum(jnp.where(sub == ln - F_LANE0, carry, 0.0), axis=0, keepdims=True)
        f_new = _cumsum_rows(logf_ref[...], _tri(t, True)) + total_row
        f_new = f_new * LOG2E
        fnew_scr[...] = f_new.T[F_LANE0:F_LANE0 + FOX_HEADS, :]
        fq_scr[...] = jnp.concatenate([f_new[:, F_LANE0 + h:F_LANE0 + h + 1] for h in range(FOX_HEADS)], axis=0)
        m_scr[...] = jnp.full(m_scr.shape, NEG_BIG, F32)
        l_scr[...] = jnp.zeros(l_scr.shape, F32)
        acc_scr[...] = jnp.zeros(acc_scr.shape, F32)

    q = q_ref[...]
    q_heads = [q[:, h * FOX_HEAD_DIM:(h + 1) * FOX_HEAD_DIM] for h in range(FOX_HEADS)]
    f_q = fq_scr[...]

    def update(keys, values, f_k, mask, channels_first):
        qk = _dot if channels_first else _dot_nt
        pv_dot = _dot_nt if channels_first else _dot
        tt = jnp.concatenate([qk(q_heads[h], keys[h]) - f_k[h:h + 1, :] for h in range(FOX_HEADS)], axis=0)
        if mask is not None:
            tt = jnp.where(mask, tt, NEG_BIG)
        m_old = m_scr[...]
        m_new = jnp.maximum(m_old, jnp.max(tt, axis=1, keepdims=True) + f_q)
        alpha = jnp.exp2(m_old - m_new)
        pr = jnp.exp2(tt - (m_new - f_q))
        l_scr[...] = alpha * l_scr[...] + jnp.sum(pr, axis=1, keepdims=True)
        prb = pr.astype(BF16)
        pv = jnp.concatenate([pv_dot(prb[h * t:(h + 1) * t, :], values[h]) for h in range(FOX_HEADS)], axis=0)
        acc_scr[...] = alpha * acc_scr[...] + pv
        m_scr[...] = m_new

    head_rows = lambda ref: [ref[h * FOX_HEAD_DIM:(h + 1) * FOX_HEAD_DIM, :].astype(BF16) for h in range(FOX_HEADS)]
    update(head_rows(ck_ref), head_rows(cv_ref), fk_scr[c], None, True)

    @pl.when(c == nchunk - 1)
    def _():
        k_new, v_new = k_ref[...].astype(BF16), v_ref[...].astype(BF16)
        cols = lambda a: [a[:, h * FOX_HEAD_DIM:(h + 1) * FOX_HEAD_DIM] for h in range(FOX_HEADS)]
        rr = lax.broadcasted_iota(jnp.int32, (FOX_HEADS * t, t), 0) % t
        cc = lax.broadcasted_iota(jnp.int32, (FOX_HEADS * t, t), 1)
        update(cols(k_new), cols(v_new), fnew_scr[...], cc <= rr, False)
        out = acc_scr[...] / l_scr[...]
        o_ref[...] = jnp.concatenate([out[h * t:(h + 1) * t, :] for h in range(FOX_HEADS)], axis=1).astype(BF16)


def _fox_sample_call(q, k, v, small, cache_k, cache_v, cache_logf_t, layer):
    bsz, t, _ = q.shape
    past = cache_logf_t.shape[3]
    chunk = min(SAMPLE_CHUNK, past)
    nchunk = past // chunk
    new = lambda w: pl.BlockSpec((None, t, w), lambda b, c: (b, 0, 0))
    cache = pl.BlockSpec((None, None, FOX_WIDTH, chunk), lambda b, c: (layer, b, 0, c))
    rows = FOX_HEADS * t
    return pl.pallas_call(
        _fox_sample_kernel,
        grid=(bsz, nchunk),
        in_specs=[new(FOX_WIDTH), new(FOX_WIDTH), new(FOX_WIDTH), new(LANES), cache, cache,
                  pl.BlockSpec((None, None, FOX_HEADS, past), lambda b, c: (layer, b, 0, 0))],
        out_specs=new(FOX_WIDTH),
        out_shape=jax.ShapeDtypeStruct((bsz, t, FOX_WIDTH), BF16),
        scratch_shapes=[pltpu.VMEM((rows, 1), F32), pltpu.VMEM((rows, 1), F32), pltpu.VMEM((rows, FOX_HEAD_DIM), F32),
                        pltpu.VMEM((nchunk, FOX_HEADS, chunk), F32), pltpu.VMEM((rows, 1), F32),
                        pltpu.VMEM((FOX_HEADS, t), F32)],
        compiler_params=_params("fox_sample", ("arbitrary", "arbitrary")),
        name="fox_sample",
    )(q, k, v, small, cache_k, cache_v, cache_logf_t)


def _small_rows(w_in_t):
    offs = np.concatenate([[0], np.cumsum(IN_SIZES)])
    seg = lambda i: w_in_t[:, int(offs[i]):int(offs[i + 1]), :]
    pad = jnp.zeros((w_in_t.shape[0], LANES - FOX_HEADS - SSD_HEADS, w_in_t.shape[2]), w_in_t.dtype)
    return jnp.concatenate([seg(5), seg(8), pad], axis=1)


def _block_diag(w):
    d, h, b, _ = w.shape
    eye = jnp.eye(h, dtype=w.dtype)
    return jnp.einsum("dhij,hg->dhigj", w, eye).reshape(d, h * b, h * b)


def _lane_slab(v, lane0):
    d, k = v.shape
    return jnp.zeros((d, 1, LANES), v.dtype).at[:, 0, lane0:lane0 + k].set(v)


def _pad_history(state):
    return jnp.pad(state, ((0, 0), (0, 0), (SUBLANES - (CONV_W - 1), 0), (0, 0)))


def _trunk(x, mod_group, caches, prm, ssd_chunk):
    bsz, n, _ = x.shape
    x = x.reshape(bsz * n, D_MODEL)
    states = {name: [] for name in ("fox_k", "fox_v", "fox_logf", "lru_conv", "lru_h", "ssd_conv", "ssd_h")}
    prev_kv = None
    for l in range(DEPTH):
        mod4 = mod_group[l].reshape(bsz, N_SUB, 3, D_MODEL)
        x = _ffn_call(x, mod4, prm["npre"], prm["npost"], prm["wg"], prm["wu"], prm["wd"], l, 0, 0, n)
        prompt = caches is None
        stacking = prompt and l == DEPTH - 1
        proj = _inproj_call(x, mod4, prm["npre"], prm["w_in_t"], prm["w_small"], prm["fbias"], l, n,
                            transposed=prompt, prev_kv=prev_kv if stacking else None)
        lrux, lrug, q, k, v, z, xbc, small = proj[:8]
        per_seq = lambda a: a.reshape(bsz, n, a.shape[-1])
        state_layer = None if prompt else l
        src = prm["zero_state"] if prompt else caches
        ya, lru_conv, lru_h = _lru_call(per_seq(lrux), per_seq(lrug), src["lru_conv"], src["lru_h"],
                                        prm["lru_cw"], prm["lru_cb"], prm["lru_wgate"], prm["lru_bgate"],
                                        prm["lru_lam"], l, state_layer)
        if prompt:
            small_t = proj[8]
            yb = _fox_prompt_call(per_seq(q), k, v, small_t, l if stacking else None)
            prev_kv = (k, v)
            logf_out = jnp.swapaxes(small_t[:, F_LANE0:F_LANE0 + FOX_HEADS, :], 1, 2)
        else:
            yb = _fox_sample_call(per_seq(q), per_seq(k), per_seq(v), per_seq(small),
                                  caches["fox_k"], caches["fox_v"], caches["fox_logf_t"], l)
            k_out = k.reshape(bsz, n, FOX_HEADS, FOX_HEAD_DIM)
            v_out = v.reshape(bsz, n, FOX_HEADS, FOX_HEAD_DIM)
            logf_out = small.reshape(bsz, n, LANES)[:, :, F_LANE0:F_LANE0 + FOX_HEADS]
        yc, ssd_conv, ssd_h = _ssd_call(per_seq(xbc), per_seq(z), per_seq(small), src["ssd_conv"], src["ssd_h"],
                                        prm["ssd_cw"], prm["ssd_cb"], prm["ssd_dtb"], prm["ssd_alog"],
                                        prm["ssd_dvec"], prm["ssd_nw"], l, state_layer, ssd_chunk)
        x = _outffn_call(x, ya.reshape(bsz * n, -1), yb.reshape(bsz * n, -1), yc.reshape(bsz * n, -1),
                         mod4, prm["npre"], prm["npost"], prm["w_out"], prm["wg"], prm["wu"], prm["wd"], l, n)
        if not prompt:
            states["fox_k"].append(k_out)
            states["fox_v"].append(v_out)
        states["fox_logf"].append(logf_out)
        states["lru_conv"].append(lru_conv)
        states["lru_h"].append(lru_h.reshape(bsz, LRU_WIDTH))
        states["ssd_conv"].append(ssd_conv)
        states["ssd_h"].append(ssd_h.reshape(bsz, SSD_HEADS, SSD_HEAD_DIM, D_STATE))
    out = {name: jnp.stack(vals, axis=0) for name, vals in states.items() if vals}
    if prev_kv is not None:
        heads_last = lambda a: jnp.transpose(a.reshape(DEPTH, bsz, FOX_HEADS, FOX_HEAD_DIM, n), (0, 1, 4, 2, 3))
        out["fox_k"], out["fox_v"] = heads_last(prev_kv[0]), heads_last(prev_kv[1])
    return x.reshape(bsz, n, D_MODEL), out


def kernel(x_prompt, x_sample, c_prompt, c_sample, cache_fox_k, cache_fox_v, cache_fox_logf, state_lru_conv, state_lru_h, state_ssd_conv, state_ssd_h, w_mod, b_mod, norm_pre, norm_post, ffn_w_gate, ffn_w_up, ffn_w_down, w_in, w_out, lru_conv_w, lru_conv_b, lru_wa, lru_ba, lru_wx, lru_bx, lru_lambda, fox_f_bias, ssd_conv_w, ssd_conv_b, ssd_dt_bias, ssd_a_log, ssd_d, ssd_norm_w):
    n_prompt, n_sample = x_prompt.shape[0], x_sample.shape[0]
    w_in_t = jnp.swapaxes(w_in, 1, 2)
    prm = {
        "npre": norm_pre.reshape(DEPTH, N_SUB, 1, D_MODEL),
        "npost": norm_post.reshape(DEPTH, N_SUB, 1, D_MODEL),
        "wg": ffn_w_gate.astype(BF16), "wu": ffn_w_up.astype(BF16), "wd": ffn_w_down.astype(BF16),
        "w_in_t": w_in_t, "w_small": _small_rows(w_in_t), "w_out": w_out.astype(BF16),
        "fbias": _lane_slab(fox_f_bias, F_LANE0),
        "lru_cw": lru_conv_w, "lru_cb": lru_conv_b.reshape(DEPTH, 1, LRU_WIDTH),
        "lru_wgate": jnp.concatenate([_block_diag(lru_wa), _block_diag(lru_wx)], axis=-1).astype(BF16),
        "lru_bgate": jnp.concatenate([lru_ba, lru_bx], axis=-1).reshape(DEPTH, 1, 2 * LRU_WIDTH),
        "lru_lam": lru_lambda.reshape(DEPTH, 1, LRU_WIDTH),
        "ssd_cw": ssd_conv_w, "ssd_cb": ssd_conv_b.reshape(DEPTH, 1, SSD_CONV_DIM),
        "ssd_dtb": _lane_slab(ssd_dt_bias, DT_LANE0), "ssd_alog": _lane_slab(ssd_a_log, DT_LANE0),
        "ssd_dvec": jnp.repeat(ssd_d, SSD_HEAD_DIM, axis=-1).reshape(DEPTH, 1, SSD_WIDTH),
        "ssd_nw": ssd_norm_w.reshape(DEPTH, 1, SSD_WIDTH),
        "zero_state": {
            "lru_conv": jnp.zeros((1, SUBLANES, LRU_WIDTH), F32), "lru_h": jnp.zeros((1, 1, LRU_WIDTH), F32),
            "ssd_conv": jnp.zeros((1, SUBLANES, SSD_CONV_DIM), F32), "ssd_h": jnp.zeros((1, SSD_WIDTH, D_STATE), F32),
        },
    }
    caches = {
        "fox_k": jnp.transpose(cache_fox_k, (0, 1, 3, 4, 2)).reshape(DEPTH, n_sample, FOX_WIDTH, -1),
        "fox_v": jnp.transpose(cache_fox_v, (0, 1, 3, 4, 2)).reshape(DEPTH, n_sample, FOX_WIDTH, -1),
        "fox_logf_t": jnp.swapaxes(cache_fox_logf, 2, 3),
        "lru_conv": _pad_history(state_lru_conv),
        "lru_h": state_lru_h.reshape(DEPTH, n_sample, 1, LRU_WIDTH),
        "ssd_conv": _pad_history(state_ssd_conv),
        "ssd_h": state_ssd_h.reshape(DEPTH, n_sample, SSD_WIDTH, D_STATE),
    }
    mod = _mod_call(jnp.concatenate([c_prompt, c_sample], axis=0), w_mod, b_mod)
    y_prompt, sp = _trunk(x_prompt, mod[:, :n_prompt], None, prm, ssd_chunk=256)
    y_sample, ss = _trunk(x_sample, mod[:, n_prompt:], caches, prm, ssd_chunk=x_sample.shape[1])
    names = ("fox_k", "fox_v", "fox_logf", "lru_conv", "lru_h", "ssd_conv", "ssd_h")
    return (y_prompt, y_sample) + tuple(sp[n] for n in names) + tuple(ss[n] for n in names)
```

```python
import functools

import numpy as np
import jax
import jax.numpy as jnp
from jax import lax
from jax.experimental import pallas as pl
from jax.experimental.pallas import tpu as pltpu

F32 = jnp.float32
BF16 = jnp.bfloat16

D_MODEL = 1024
DEPTH = 2
CONV_W = 4
EPS = 1e-6
LRU_WIDTH = 256
LRU_C = 8.0
FOX_HEADS = 8
FOX_HEAD_DIM = 64
FOX_WIDTH = FOX_HEADS * FOX_HEAD_DIM
SSD_HEADS = 4
SSD_HEAD_DIM = 64
SSD_WIDTH = SSD_HEADS * SSD_HEAD_DIM
SSD_GROUPS = 2
D_STATE = 128
SSD_CONV_DIM = SSD_WIDTH + 2 * SSD_GROUPS * D_STATE
IN_SIZES = (LRU_WIDTH, LRU_WIDTH, FOX_WIDTH, FOX_WIDTH, FOX_WIDTH, FOX_HEADS, SSD_WIDTH, SSD_CONV_DIM, SSD_HEADS)
D_FF = 2816
N_SUB = 3

LANES = 128
SUBLANES = 8
HEAD_PAIR = LANES // FOX_HEAD_DIM

F_LANE0 = 0
DT_LANE0 = FOX_HEADS
COL_LRU_X = 0
COL_LRU_G = COL_LRU_X + LRU_WIDTH
COL_Q = COL_LRU_G + LRU_WIDTH
COL_K = COL_Q + FOX_WIDTH
COL_V = COL_K + FOX_WIDTH
COL_Z = COL_V + FOX_WIDTH
COL_XBC = COL_Z + SSD_WIDTH
COL_SMALL = COL_XBC + SSD_CONV_DIM
D_IN_PAD = COL_SMALL + LANES
D_IN = sum(IN_SIZES)

ROW_TILE = 512
FFN_ROW_TILE = 1024
FF_CHUNK = 256
ATTN_BLOCK = 2048
ATTN_ROWS = 128
ATTN_GROUP = 16
ATTN_PAIRS = 2
SCAN_BLOCK = 256
LRU_SEQS = 2
SAMPLE_CHUNK = 4096
NEG_BIG = -1e30
LOG2E = 1.4426950408889634
Q_SCALE = LOG2E * FOX_HEAD_DIM ** -0.5


def _dot(a, b):
    return jnp.dot(a, b, preferred_element_type=F32)


def _dot_nt(a, b):
    return lax.dot_general(a, b, (((1,), (1,)), ((), ())), preferred_element_type=F32)


def _dot_tn(a, b):
    return lax.dot_general(a, b, (((0,), (0,)), ((), ())), preferred_element_type=F32)


def _silu(x):
    return x * jax.nn.sigmoid(x)


def _softplus(x):
    return jnp.maximum(x, 0.0) + jnp.log1p(jnp.exp(-jnp.abs(x)))


def _rms(x):
    return x * lax.rsqrt(jnp.mean(x * x, axis=-1, keepdims=True) + EPS)


def _per_seq(rows, per_seq, fn):
    g = per_seq[0].shape[0]
    if g == 1:
        return fn(rows, *per_seq)
    tm, d = rows.shape
    out = fn(rows.reshape(g, tm // g, d), *[p[:, None, :] for p in per_seq])
    return out.reshape(tm, d)


def _pre_norm(x, npre, mod_ref, seqs=slice(None)):
    h = _rms(x) * npre
    return _per_seq(h, (mod_ref[seqs, 1, :], mod_ref[seqs, 0, :]), lambda r, sc, sh: r * (1.0 + sc) + sh)


def _post_norm(x, y, npost, mod_ref, w, seqs=slice(None)):
    yn = _rms(y) * npost
    return x + _per_seq(yn, (mod_ref[seqs, 2, :],), lambda r, gt: (w * gt) * r)


def _seq_grouping(rows_per_seq, tm):
    if rows_per_seq % tm == 0:
        return 1, rows_per_seq // tm
    assert tm % rows_per_seq == 0
    return tm // rows_per_seq, 1


def _mod_spec(g, tiles_per_seq, sub):
    if g == 1:
        return pl.BlockSpec((1, None, 3, D_MODEL), lambda i: (i // tiles_per_seq, sub, 0, 0))
    return pl.BlockSpec((g, None, 3, D_MODEL), lambda i: (i, sub, 0, 0))


VMEM_LIMIT_MB = {
    "adaln_mod": 24,
    "ffn": 56,
    "outproj_ffn": 57,
    "inproj": 56,
    "rg_lru": 56,
    "ssd": 40,
    "fox_prompt": 40,
    "fox_sample": 56,
}


def _params(name, sem):
    return pltpu.CompilerParams(dimension_semantics=sem, vmem_limit_bytes=VMEM_LIMIT_MB[name] << 20)


def _mod_kernel(c_ref, w_ref, b_ref, o_ref):
    a = _silu(c_ref[...]).astype(BF16)
    o_ref[...] = _dot(a, w_ref[...].astype(BF16)) + b_ref[...]


def _mod_call(c_all, w_mod, b_mod):
    nseq = c_all.shape[0]
    width = N_SUB * 3 * D_MODEL
    tn = 1024
    return pl.pallas_call(
        _mod_kernel,
        grid=(DEPTH, width // tn),
        in_specs=[
            pl.BlockSpec((nseq, D_MODEL), lambda l, n: (0, 0)),
            pl.BlockSpec((None, D_MODEL, tn), lambda l, n: (l, 0, n)),
            pl.BlockSpec((None, 1, tn), lambda l, n: (l, 0, n)),
        ],
        out_specs=pl.BlockSpec((None, nseq, tn), lambda l, n: (l, 0, n)),
        out_shape=jax.ShapeDtypeStruct((DEPTH, nseq, width), F32),
        compiler_params=_params("adaln_mod", ("arbitrary", "arbitrary")),
        name="adaln_mod",
    )(c_all, w_mod, b_mod.reshape(DEPTH, 1, width))


def _halves(tm, nseq):
    half = tm // 2
    rows = [slice(i * half, (i + 1) * half) for i in range(2)]
    seqs = [slice(0, 1)] * 2 if nseq == 1 else [slice(i * nseq // 2, (i + 1) * nseq // 2) for i in range(2)]
    return rows, seqs


def _swiglu(hs, wg_ref, wu_ref, wd_ref):
    accs = [None] * len(hs)
    for off in range(0, D_FF, FF_CHUNK):
        fc = min(FF_CHUNK, D_FF - off)
        for i, h in enumerate(hs):
            g = _dot(h, wg_ref[:, off:off + fc])
            u = _dot(h, wu_ref[:, off:off + fc])
            a = (_silu(g) * u).astype(BF16)
            d = _dot(a, wd_ref[off:off + fc, :])
            accs[i] = d if accs[i] is None else accs[i] + d
    return accs


def _ffn_kernel(x_ref, mod_ref, npre_ref, npost_ref, wg_ref, wu_ref, wd_ref, o_ref):
    rows, seqs = _halves(x_ref.shape[0], mod_ref.shape[0])
    xs = [x_ref[r, :] for r in rows]
    hs = [_pre_norm(x, npre_ref[...], mod_ref, sq).astype(BF16) for x, sq in zip(xs, seqs)]
    accs = _swiglu(hs, wg_ref, wu_ref, wd_ref)
    for i, r in enumerate(rows):
        o_ref[r, :] = _post_norm(xs[i], accs[i], npost_ref[...], mod_ref, 0.5, seqs[i])


def _outffn_kernel(x_ref, ya_ref, yb_ref, yc_ref, mod1_ref, mod2_ref, npost1_ref, npre2_ref, npost2_ref,
                   wo_ref, wg_ref, wu_ref, wd_ref, o_ref):
    rows, seqs = _halves(x_ref.shape[0], mod1_ref.shape[0])
    xs = []
    for r, sq in zip(rows, seqs):
        y = (_dot(ya_ref[r, :], wo_ref[0:LRU_WIDTH, :])
             + _dot(yb_ref[r, :], wo_ref[LRU_WIDTH:LRU_WIDTH + FOX_WIDTH, :])
             + _dot(yc_ref[r, :], wo_ref[LRU_WIDTH + FOX_WIDTH:, :]))
        xs.append(_post_norm(x_ref[r, :], y, npost1_ref[...], mod1_ref, 1.0, sq))
    hs = [_pre_norm(x, npre2_ref[...], mod2_ref, sq).astype(BF16) for x, sq in zip(xs, seqs)]
    accs = _swiglu(hs, wg_ref, wu_ref, wd_ref)
    for i, r in enumerate(rows):
        o_ref[r, :] = _post_norm(xs[i], accs[i], npost2_ref[...], mod2_ref, 0.5, seqs[i])


def _outffn_call(x, ya, yb, yc, mod4, npre, npost, w_out, wg, wu, wd, layer, rows_per_seq):
    m = x.shape[0]
    tm = min(FFN_ROW_TILE, m)
    g, tps = _seq_grouping(rows_per_seq, tm)
    row = lambda w: pl.BlockSpec((tm, w), lambda i: (i, 0))
    once = dict(pipeline_mode=pl.Buffered(1))
    wspec = lambda shape: pl.BlockSpec((None, None) + shape, lambda i: (layer, 1, 0, 0), **once)
    norm = lambda sub: pl.BlockSpec((None, None, 1, D_MODEL), lambda i: (layer, sub, 0, 0))
    return pl.pallas_call(
        _outffn_kernel,
        grid=(m // tm,),
        in_specs=[
            row(D_MODEL), row(LRU_WIDTH), row(FOX_WIDTH), row(SSD_WIDTH),
            _mod_spec(g, tps, 1), _mod_spec(g, tps, 2),
            norm(1), norm(2), norm(2),
            pl.BlockSpec((None, D_MODEL, D_MODEL), lambda i: (layer, 0, 0), **once),
            wspec((D_MODEL, D_FF)), wspec((D_MODEL, D_FF)), wspec((D_FF, D_MODEL)),
        ],
        out_specs=row(D_MODEL),
        out_shape=jax.ShapeDtypeStruct((m, D_MODEL), F32),
        compiler_params=_params("outproj_ffn", ("arbitrary",)),
        name="outproj_ffn",
    )(x, ya, yb, yc, mod4, mod4, npost, npre, npost, w_out, wg, wu, wd)


def _ffn_call(x, mod4, npre, npost, wg, wu, wd, layer, sub, ffn_idx, rows_per_seq):
    m = x.shape[0]
    tm = min(FFN_ROW_TILE, m)
    g, tps = _seq_grouping(rows_per_seq, tm)
    wspec = lambda shape: pl.BlockSpec((None, None) + shape, lambda i: (layer, ffn_idx, 0, 0),
                                       pipeline_mode=pl.Buffered(1))
    nspec = pl.BlockSpec((None, None, 1, D_MODEL), lambda i: (layer, sub, 0, 0))
    return pl.pallas_call(
        _ffn_kernel,
        grid=(m // tm,),
        in_specs=[
            pl.BlockSpec((tm, D_MODEL), lambda i: (i, 0)),
            _mod_spec(g, tps, sub),
            nspec, nspec,
            wspec((D_MODEL, D_FF)), wspec((D_MODEL, D_FF)), wspec((D_FF, D_MODEL)),
        ],
        out_specs=pl.BlockSpec((tm, D_MODEL), lambda i: (i, 0)),
        out_shape=jax.ShapeDtypeStruct((m, D_MODEL), F32),
        compiler_params=_params("ffn", ("arbitrary",)),
        name="ffn",
    )(x, mod4, npre, npost, wg, wu, wd)


def _inproj_kernel(*refs, transposed, stacked):
    n_in = 8 if stacked else 6
    x_ref, mod_ref, npre_ref, w_ref, wsmall_ref, fbias_ref = refs[:6]
    lrux_ref, lrug_ref, q_ref, k_ref, v_ref, z_ref, xbc_ref, small_ref, *rest = refs[n_in:]
    wt_ref = rest[-1]

    @pl.when(pl.program_id(0) == 0)
    def _():
        src = int(np.cumsum(IN_SIZES)[4])
        assert src == COL_Z
        moves = [(0, 0, src), (src + FOX_HEADS, COL_Z, SSD_WIDTH), (src + FOX_HEADS + SSD_WIDTH, COL_XBC, SSD_CONV_DIM)]
        for s0, d0, n in moves:
            for off in range(0, n, LRU_WIDTH):
                wt_ref[d0 + off:d0 + off + LRU_WIDTH, :] = w_ref[s0 + off:s0 + off + LRU_WIDTH, :].astype(BF16)
        wt_ref[COL_SMALL:COL_SMALL + LANES, :] = wsmall_ref[...].astype(BF16)

    h = _pre_norm(x_ref[...], npre_ref[...], mod_ref).astype(BF16)
    col = lambda start, width: _dot_nt(h, wt_ref[start:start + width, :])
    col_t = lambda start, width: _dot_nt(wt_ref[start:start + width, :], h)
    lrux_ref[...] = col(COL_LRU_X, LRU_WIDTH)
    lrug_ref[...] = col(COL_LRU_G, LRU_WIDTH)
    q_ref[...] = (col(COL_Q, FOX_WIDTH) * Q_SCALE).astype(BF16)
    if transposed:
        kv_t = col_t(COL_K, 2 * FOX_WIDTH)
        if stacked:
            k_ref[0], v_ref[0] = refs[6][...], refs[7][...]
            k_ref[1], v_ref[1] = kv_t[:FOX_WIDTH], kv_t[FOX_WIDTH:]
        else:
            k_ref[...] = kv_t[:FOX_WIDTH]
            v_ref[...] = kv_t[FOX_WIDTH:]
    else:
        k_ref[...] = col(COL_K, FOX_WIDTH)
        v_ref[...] = col(COL_V, FOX_WIDTH)
    z_ref[...] = col(COL_Z, SSD_WIDTH)
    xbc_ref[...] = col(COL_XBC, SSD_CONV_DIM)
    small = col(COL_SMALL, LANES)
    t = small + fbias_ref[...]
    logf = jnp.minimum(t, 0.0) - jnp.log1p(jnp.exp(-jnp.abs(t)))
    lane = lax.broadcasted_iota(jnp.int32, small.shape, 1)
    small = jnp.where(lane < DT_LANE0, logf, small)
    small_ref[...] = small
    if transposed:
        rest[0][...] = small.T


def _inproj_call(x, mod4, npre, w_in_t, w_small, fbias, layer, rows_per_seq, transposed, prev_kv=None):
    m = x.shape[0]
    tm = min(ROW_TILE, m)
    g, tps = _seq_grouping(rows_per_seq, tm)
    widths = (LRU_WIDTH, LRU_WIDTH, FOX_WIDTH, FOX_WIDTH, FOX_WIDTH, SSD_WIDTH, SSD_CONV_DIM, LANES)
    dtypes = (F32, F32, BF16, F32, F32, F32, F32, F32)
    out_specs = [pl.BlockSpec((tm, w), lambda i: (i, 0)) for w in widths]
    out_shape = [jax.ShapeDtypeStruct((m, w), dt) for w, dt in zip(widths, dtypes)]
    in_specs = [
        pl.BlockSpec((tm, D_MODEL), lambda i: (i, 0)),
        _mod_spec(g, tps, 1),
        pl.BlockSpec((None, None, 1, D_MODEL), lambda i: (layer, 1, 0, 0)),
        pl.BlockSpec((None, D_IN, D_MODEL), lambda i: (layer, 0, 0), pipeline_mode=pl.Buffered(1)),
        pl.BlockSpec((None, LANES, D_MODEL), lambda i: (layer, 0, 0)),
        pl.BlockSpec((None, 1, LANES), lambda i: (layer, 0, 0)),
    ]
    if transposed:
        assert g == 1
        nseq = m // rows_per_seq
        for idx, w in ((3, FOX_WIDTH), (4, FOX_WIDTH), (len(widths), LANES)):
            spec = pl.BlockSpec((None, w, tm), lambda i: (i // tps, 0, i % tps))
            shape = jax.ShapeDtypeStruct((nseq, w, rows_per_seq), F32)
            out_specs[idx:idx + 1], out_shape[idx:idx + 1] = [spec], [shape]
    extra = ()
    if prev_kv is not None:
        assert transposed and layer == 1 and DEPTH == 2
        extra = tuple(prev_kv)
        in_specs += [pl.BlockSpec((None, FOX_WIDTH, tm), lambda i: (i // tps, 0, i % tps))] * 2
        for idx in (3, 4):
            out_specs[idx] = pl.BlockSpec((DEPTH, None, FOX_WIDTH, tm), lambda i: (0, i // tps, 0, i % tps))
            out_shape[idx] = jax.ShapeDtypeStruct((DEPTH, nseq, FOX_WIDTH, rows_per_seq), F32)
    return pl.pallas_call(
        functools.partial(_inproj_kernel, transposed=transposed, stacked=prev_kv is not None),
        grid=(m // tm,),
        in_specs=in_specs,
        out_specs=out_specs,
        out_shape=out_shape,
        scratch_shapes=[pltpu.VMEM((D_IN_PAD, D_MODEL), BF16)],
        compiler_params=_params("inproj", ("arbitrary",)),
        name="inproj",
    )(x, mod4, npre, w_in_t, w_small, fbias, *extra)


def _causal_conv(xx, cw, cb, n):
    u = cb + cw[0:1] * pltpu.roll(xx, 3, 0)[SUBLANES:SUBLANES + n]
    u = u + cw[1:2] * pltpu.roll(xx, 2, 0)[SUBLANES:SUBLANES + n]
    u = u + cw[2:3] * pltpu.roll(xx, 1, 0)[SUBLANES:SUBLANES + n]
    return u + cw[3:4] * xx[SUBLANES:SUBLANES + n]


def _last_rows(xx, k):
    return pltpu.roll(xx, k, 0)[0:SUBLANES][0:k]


def _tri(n, lower):
    r = lax.broadcasted_iota(jnp.int32, (n, n), 0)
    c = lax.broadcasted_iota(jnp.int32, (n, n), 1)
    return ((r >= c) if lower else (r <= c)).astype(F32).astype(BF16)


def _split3(x):
    hi = x.astype(BF16)
    r1 = x - hi.astype(F32)
    mid = r1.astype(BF16)
    lo = (r1 - mid.astype(F32)).astype(BF16)
    return hi, mid, lo


def _cumsum_lanes(x, triu):
    hi, mid, lo = _split3(x)
    return _dot(hi, triu) + _dot(mid, triu) + _dot(lo, triu)


def _cumsum_rows(x, tril):
    hi, mid, lo = _split3(x)
    return _dot(tril, hi) + _dot(tril, mid) + _dot(tril, lo)


def _lru_kernel(x_ref, g_ref, prev_ref, h0_ref, cw_ref, cb_ref, wgate_ref, bgate_ref, lam_ref,
                y_ref, convnew_ref, hnew_ref, a_scr, b_scr):
    ns, n = x_ref.shape[0], x_ref.shape[1]
    shared = len(prev_ref.shape) == 2
    row = lax.broadcasted_iota(jnp.int32, (1, SUBLANES, 1), 1)
    for s in range(ns):
        xx = jnp.concatenate([prev_ref[...] if shared else prev_ref[s], x_ref[s]], axis=0)
        convnew_ref[s] = _last_rows(xx, CONV_W - 1)
        u = _causal_conv(xx, cw_ref[...], cb_ref[...], n)
        gates = _dot(u.astype(BF16), wgate_ref[...]) + bgate_ref[...]
        r = jax.nn.sigmoid(gates[:, :LRU_WIDTH])
        i = jax.nn.sigmoid(gates[:, LRU_WIDTH:])
        log_a = (-LRU_C * r) * _softplus(-lam_ref[...])
        a = jnp.exp(log_a)
        b = jnp.sqrt(-jnp.tanh(log_a) * (a * a + 1.0)) * (i * u)
        a = a.reshape(n // SUBLANES, SUBLANES, LRU_WIDTH)
        b = b.reshape(n // SUBLANES, SUBLANES, LRU_WIDTH)
        for d in (1, 2, 4):
            keep = row >= d
            b = jnp.where(keep, a * pltpu.roll(b, d, 1) + b, b)
            a = jnp.where(keep, a * pltpu.roll(a, d, 1), a)
        a_scr[s] = a.reshape(n, LRU_WIDTH)
        b_scr[s] = b.reshape(n, LRU_WIDTH)

    def group(j, hs):
        off = pl.multiple_of(j * SUBLANES, SUBLANES)
        out = []
        for s in range(ns):
            hb = a_scr[s, pl.ds(off, SUBLANES), :] * hs[s] + b_scr[s, pl.ds(off, SUBLANES), :]
            b_scr[s, pl.ds(off, SUBLANES), :] = hb
            out.append(jnp.broadcast_to(hb[SUBLANES - 1:SUBLANES, :], hb.shape))
        return tuple(out)

    h_first = tuple(jnp.broadcast_to(h0_ref[...] if shared else h0_ref[s], (SUBLANES, LRU_WIDTH))
                    for s in range(ns))
    h_last = lax.fori_loop(0, n // SUBLANES, group, h_first, unroll=4)
    for s in range(ns):
        hnew_ref[s] = h_last[s][0:1]
        y_ref[s] = (b_scr[s] * jax.nn.gelu(g_ref[s])).astype(BF16)


def _lru_call(lru_x, lru_g, prev8, h0, cw, cb, wgate, bgate, lam, layer, state_layer):
    bsz, n, _ = lru_x.shape
    ns = LRU_SEQS
    seq = pl.BlockSpec((ns, n, LRU_WIDTH), lambda b: (b, 0, 0))
    if state_layer is None:
        prev_spec = pl.BlockSpec((None, SUBLANES, LRU_WIDTH), lambda b: (0, 0, 0))
        h0_spec = pl.BlockSpec((None, 1, LRU_WIDTH), lambda b: (0, 0, 0))
    else:
        prev_spec = pl.BlockSpec((None, ns, SUBLANES, LRU_WIDTH), lambda b: (state_layer, b, 0, 0))
        h0_spec = pl.BlockSpec((None, ns, 1, LRU_WIDTH), lambda b: (state_layer, b, 0, 0))
    par = lambda r, w: pl.BlockSpec((None, r, w), lambda b: (layer, 0, 0))
    return pl.pallas_call(
        _lru_kernel,
        grid=(bsz // ns,),
        in_specs=[seq, seq, prev_spec, h0_spec, par(CONV_W, LRU_WIDTH), par(1, LRU_WIDTH),
                  par(LRU_WIDTH, 2 * LRU_WIDTH), par(1, 2 * LRU_WIDTH), par(1, LRU_WIDTH)],
        out_specs=[seq,
                   pl.BlockSpec((ns, CONV_W - 1, LRU_WIDTH), lambda b: (b, 0, 0)),
                   pl.BlockSpec((ns, 1, LRU_WIDTH), lambda b: (b, 0, 0))],
        out_shape=[jax.ShapeDtypeStruct((bsz, n, LRU_WIDTH), BF16),
                   jax.ShapeDtypeStruct((bsz, CONV_W - 1, LRU_WIDTH), F32),
                   jax.ShapeDtypeStruct((bsz, 1, LRU_WIDTH), F32)],
        scratch_shapes=[pltpu.VMEM((ns, n, LRU_WIDTH), F32), pltpu.VMEM((ns, n, LRU_WIDTH), F32)],
        compiler_params=_params("rg_lru", ("arbitrary",)),
        name="rg_lru",
    )(lru_x, lru_g, prev8, h0, cw, cb, wgate, bgate, lam)


def _ssd_kernel(xbc_ref, z_ref, small_ref, prev_ref, h0_ref, cw_ref, cb_ref, dtb_ref, alog_ref, dvec_ref, nw_ref,
                y_ref, convnew_ref, h_ref, *, chunk):
    h_ref[...] = h0_ref[...]

    def one_chunk(c, tail):
        rows = pl.ds(pl.multiple_of(c * chunk, chunk), chunk)
        xx = jnp.concatenate([tail, xbc_ref[rows, :]], axis=0)
        _ssd_chunk(xx, z_ref[rows, :], small_ref[rows, :], cw_ref, cb_ref, dtb_ref, alog_ref, dvec_ref, nw_ref,
                   y_ref.at[rows, :], h_ref)
        return xx[chunk:chunk + SUBLANES]

    tail = lax.fori_loop(0, xbc_ref.shape[0] // chunk, one_chunk, prev_ref[...])
    convnew_ref[...] = _last_rows(tail, CONV_W - 1)


def _ssd_chunk(xx, z, small, cw_ref, cb_ref, dtb_ref, alog_ref, dvec_ref, nw_ref, y_ref, h_ref):
    n = xx.shape[0] - SUBLANES
    act = _silu(_causal_conv(xx, cw_ref[...], cb_ref[...], n))
    xs = act[:, :SSD_WIDTH]
    bm = act[:, SSD_WIDTH:SSD_WIDTH + SSD_GROUPS * D_STATE]
    cm = act[:, SSD_WIDTH + SSD_GROUPS * D_STATE:]

    dt = _softplus(small + dtb_ref[...])
    dta = dt * (-jnp.exp(alog_ref[...]))
    cum = _cumsum_rows(dta, _tri(n, True)) * LOG2E
    cum_t = cum.T
    rr = lax.broadcasted_iota(jnp.int32, (n, n), 0)
    cc = lax.broadcasted_iota(jnp.int32, (n, n), 1)
    causal = rr >= cc
    lo_lane = lax.broadcasted_iota(jnp.int32, (1, LANES), 1) < SSD_HEAD_DIM
    lo_row = lax.broadcasted_iota(jnp.int32, (LANES, 1), 0) < SSD_HEAD_DIM
    dvec = dvec_ref[...]

    ys = []
    for g in range(SSD_GROUPS):
        sl = slice(g * LANES, (g + 1) * LANES)
        xg, bg, cg = xs[:, sl], bm[:, sl].astype(BF16), cm[:, sl].astype(BF16)
        heads = (2 * g, 2 * g + 1)
        col = lambda a, h: a[:, DT_LANE0 + h:DT_LANE0 + h + 1]
        pick = lambda f: jnp.where(lo_lane, f(heads[0]), f(heads[1]))
        dx = xg * pick(lambda h: col(dt, h))
        dxb = dx.astype(BF16)
        cb_mat = _dot_nt(cg, bg)
        yd = []
        for h in heads:
            seg = col(cum, h) - cum_t[DT_LANE0 + h:DT_LANE0 + h + 1, :]
            lmat = jnp.exp2(jnp.where(causal, seg, NEG_BIG))
            yd.append(_dot((cb_mat * lmat).astype(BF16), dxb))
        y_diag = jnp.where(lo_lane, yd[0], yd[1])
        last = lambda h: col(cum, h)[n - 1:n, :]
        decay_end = pick(lambda h: jnp.exp2(last(h) - col(cum, h)))
        states = _dot_tn((dx * decay_end).astype(BF16), bg)
        h_prev = h_ref[sl, :]
        y_off = _dot_nt(cg, h_prev.astype(BF16)) * pick(lambda h: jnp.exp2(col(cum, h)))
        chunk_decay = jnp.where(lo_row, jnp.exp2(last(heads[0])), jnp.exp2(last(heads[1])))
        h_ref[sl, :] = chunk_decay * h_prev + states
        ys.append(y_diag + y_off + dvec[:, sl] * xg)
    y = jnp.concatenate(ys, axis=1)
    y_ref[...] = (_rms(y * _silu(z)) * nw_ref[...]).astype(BF16)


def _ssd_call(xbc, z, small, prev8, h0, cw, cb, dtb, alog, dvec, nw, layer, state_layer, chunk):
    bsz, n, _ = xbc.shape
    seq = lambda w: pl.BlockSpec((None, n, w), lambda b: (b, 0, 0))
    if state_layer is None:
        prev_spec = pl.BlockSpec((None, SUBLANES, SSD_CONV_DIM), lambda b: (0, 0, 0))
        h0_spec = pl.BlockSpec((None, SSD_WIDTH, D_STATE), lambda b: (0, 0, 0))
    else:
        prev_spec = pl.BlockSpec((None, None, SUBLANES, SSD_CONV_DIM), lambda b: (state_layer, b, 0, 0))
        h0_spec = pl.BlockSpec((None, None, SSD_WIDTH, D_STATE), lambda b: (state_layer, b, 0, 0))
    par = lambda r, w: pl.BlockSpec((None, r, w), lambda b: (layer, 0, 0))
    return pl.pallas_call(
        functools.partial(_ssd_kernel, chunk=chunk),
        grid=(bsz,),
        in_specs=[seq(SSD_CONV_DIM), seq(SSD_WIDTH), seq(LANES), prev_spec, h0_spec,
                  par(CONV_W, SSD_CONV_DIM), par(1, SSD_CONV_DIM), par(1, LANES), par(1, LANES),
                  par(1, SSD_WIDTH), par(1, SSD_WIDTH)],
        out_specs=[seq(SSD_WIDTH),
                   pl.BlockSpec((None, CONV_W - 1, SSD_CONV_DIM), lambda b: (b, 0, 0)),
                   pl.BlockSpec((None, SSD_WIDTH, D_STATE), lambda b: (b, 0, 0))],
        out_shape=[jax.ShapeDtypeStruct((bsz, n, SSD_WIDTH), BF16),
                   jax.ShapeDtypeStruct((bsz, CONV_W - 1, SSD_CONV_DIM), F32),
                   jax.ShapeDtypeStruct((bsz, SSD_WIDTH, D_STATE), F32)],
        compiler_params=_params("ssd", ("arbitrary",)),
        name="ssd",
    )(xbc, z, small, prev8, h0, cw, cb, dtb, alog, dvec, nw)


def _fox_prompt_kernel(q_ref, kt_ref, vt_ref, logft_ref, o_ref, k_scr, v_scr, f_scr, ft_scr):
    s = kt_ref.shape[1]
    tq = k_scr.shape[4]
    npair = k_scr.shape[0]
    spare = (FOX_HEAD_DIM, 0)

    @pl.when(pl.program_id(1) == 0)
    def _():
        triu = _tri(SCAN_BLOCK, False)
        carry = jnp.zeros((LANES, 1), F32)
        per = tq // SCAN_BLOCK
        for c in range(s // SCAN_BLOCK):
            cols = slice(c * SCAN_BLOCK, (c + 1) * SCAN_BLOCK)
            fc = _cumsum_lanes(logft_ref[:, cols], triu) + carry
            ft_scr[c // per, :, (c % per) * SCAN_BLOCK:(c % per + 1) * SCAN_BLOCK] = fc
            f_scr[cols, :] = fc.T
            carry = fc[:, SCAN_BLOCK - 1:SCAN_BLOCK]

    row = lax.broadcasted_iota(jnp.int32, (LANES, 1), 0)
    lane = lax.broadcasted_iota(jnp.int32, (1, LANES), 1)
    lo_lane = lane < FOX_HEAD_DIM
    rb = min(ATTN_ROWS, tq)
    nrb = tq // rb
    for pp in range(npair):
        _fox_pair(pp, q_ref, kt_ref, vt_ref, o_ref, k_scr.at[pp], v_scr.at[pp], f_scr, ft_scr,
                  HEAD_PAIR * (npair * pl.program_id(1) + pp), spare, row, lane, lo_lane, s, tq, rb, nrb)


def _fox_pair(pp, q_ref, kt_ref, vt_ref, o_ref, k_scr, v_scr, f_scr, ft_scr, head0, spare, row, lane, lo_lane,
              s, tq, rb, nrb):
    heads = (head0, head0 + 1)
    chan = slice(pp * LANES, (pp + 1) * LANES)
    for j in range(s // tq):
        kt = kt_ref[chan, j * tq:(j + 1) * tq]
        vt = vt_ref[chan, j * tq:(j + 1) * tq]
        for hl in range(HEAD_PAIR):
            own = (row < FOX_HEAD_DIM) if hl == 0 else (row >= FOX_HEAD_DIM)
            f_k = ft_scr[j, pl.ds(F_LANE0 + heads[hl], 1), :] * LOG2E
            kc = jnp.where(own, kt, 0.0).astype(BF16)
            for i, part in enumerate(_split3(-f_k)):
                kc = jnp.where(row == spare[hl] + i, part, kc)
            k_scr[hl, j] = kc
            v_scr[hl, j] = jnp.where(own, vt, jnp.where(row == spare[hl], 1.0, 0.0)).astype(BF16)

    def query_block(qi, _):
        rows = [pl.ds(pl.multiple_of(qi * tq + r * rb, rb), rb) for r in range(nrb)]
        q_blocks, fq_blocks = [], []
        for r in range(nrb):
            q = q_ref[rows[r], chan].astype(F32)
            ones_at = lambda l0: jnp.where((lane >= l0) & (lane < l0 + 3), 1.0, 0.0)
            q_blocks.append((jnp.where(lo_lane, q, ones_at(spare[0])).astype(BF16),
                             jnp.where(lo_lane, ones_at(spare[1]), q).astype(BF16)))
            f_rows = f_scr[rows[r], :]
            fq_blocks.append([jnp.sum(jnp.where(lane == F_LANE0 + h, f_rows, 0.0), axis=1, keepdims=True) * LOG2E
                              for h in heads])
        for g0 in range(0, nrb, ATTN_GROUP):
            rs = tuple(range(g0, min(g0 + ATTN_GROUP, nrb)))
            carry = lax.fori_loop(0, qi, lambda j, cr: step(j, cr, rs, q_blocks, fq_blocks, False),
                                  (init_rows,) * len(rs))
            carry = step(qi, carry, rs, q_blocks, fq_blocks, True)
            for i, r in enumerate(rs):
                acc = [carry[i][2 * hl + 1] for hl in range(HEAD_PAIR)]
                out = [a / a[:, spare[hl]:spare[hl] + 1] for hl, a in enumerate(acc)]
                o_ref[rows[r], chan] = jnp.where(lo_lane, out[0], out[1]).astype(BF16)
        return 0

    def softmax_unit(t, f_q, m_old, row0):
        if row0 is not None:
            rr = lax.broadcasted_iota(jnp.int32, t.shape, 0) + row0
            cc = lax.broadcasted_iota(jnp.int32, t.shape, 1)
            t = jnp.where(cc <= rr, t, NEG_BIG)
        m_new = jnp.maximum(m_old, jnp.max(t, axis=1, keepdims=True) + f_q)
        return m_new, jnp.exp2(m_old - m_new), jnp.exp2(t - (m_new - f_q)).astype(BF16)

    def step(j, carry, rs, q_blocks, fq_blocks, diagonal):
        nks = {r: (r + 1) * rb if diagonal else tq for r in rs}
        units = [(i, r, hl) for i, r in enumerate(rs) for hl in range(HEAD_PAIR)]
        dots = {(r, hl): _dot(q_blocks[r][hl], k_scr[hl, j, :, 0:nks[r]]) for _, r, hl in units}
        soft = {(r, hl): softmax_unit(dots[(r, hl)], fq_blocks[r][hl], carry[i][2 * hl], r * rb if diagonal else None)
                for i, r, hl in units}
        pvs = {(r, hl): _dot_nt(soft[(r, hl)][2], v_scr[hl, j, :, 0:nks[r]]) for _, r, hl in units}
        new = []
        for i, r in enumerate(rs):
            row_state = ()
            for hl in range(HEAD_PAIR):
                m_new, alpha, _ = soft[(r, hl)]
                row_state += (m_new, alpha * carry[i][2 * hl + 1] + pvs[(r, hl)])
            new.append(row_state)
        return tuple(new)

    init_rows = (jnp.full((rb, 1), NEG_BIG, F32), jnp.zeros((rb, LANES), F32)) * HEAD_PAIR
    lax.fori_loop(0, s // tq, query_block, 0)


def _fox_prompt_call(q, kt, vt, small_t, layer=None):
    bsz, s, _ = q.shape
    tq = min(ATTN_BLOCK, s)
    width = ATTN_PAIRS * LANES
    blk = pl.BlockSpec((None, s, width), lambda b, p: (b, 0, p))
    if layer is None:
        full = pl.BlockSpec((None, width, s), lambda b, p: (b, p, 0))
    else:
        full = pl.BlockSpec((None, None, width, s), lambda b, p: (layer, b, p, 0))
    return pl.pallas_call(
        _fox_prompt_kernel,
        grid=(bsz, FOX_WIDTH // width),
        in_specs=[blk, full, full, pl.BlockSpec((None, LANES, s), lambda b, p: (b, 0, 0))],
        out_specs=blk,
        out_shape=jax.ShapeDtypeStruct((bsz, s, FOX_WIDTH), BF16),
        scratch_shapes=[pltpu.VMEM((ATTN_PAIRS, HEAD_PAIR, s // tq, LANES, tq), BF16),
                        pltpu.VMEM((ATTN_PAIRS, HEAD_PAIR, s // tq, LANES, tq), BF16),
                        pltpu.VMEM((s, LANES), F32), pltpu.VMEM((s // tq, LANES, tq), F32)],
        compiler_params=_params("fox_prompt", ("arbitrary", "arbitrary")),
        name="fox_prompt",
    )(q, kt, vt, small_t)


def _fox_sample_kernel(q_ref, k_ref, v_ref, logf_ref, ck_ref, cv_ref, clogft_ref, o_ref,
                       m_scr, l_scr, acc_scr, fk_scr, fq_scr, fnew_scr):
    t = q_ref.shape[0]
    c = pl.program_id(1)
    nchunk, _, chunk = fk_scr.shape

    @pl.when(c == 0)
    def _():
        blk = min(SCAN_BLOCK, chunk)
        per = chunk // blk
        nblk = nchunk * per
        stacked = jnp.concatenate([clogft_ref[:, i * blk:(i + 1) * blk] for i in range(nblk)], axis=0)
        local = _cumsum_lanes(stacked, _tri(blk, False))
        nrow = nblk * FOX_HEADS
        rr = lax.broadcasted_iota(jnp.int32, (nrow, nrow), 0)
        cc = lax.broadcasted_iota(jnp.int32, (nrow, nrow), 1)
        earlier = ((rr % FOX_HEADS == cc % FOX_HEADS) & (cc < rr - rr % FOX_HEADS)).astype(F32).astype(BF16)
        totals = jnp.broadcast_to(local[:, blk - 1:blk], (nrow, LANES))
        f_all = local + _cumsum_rows(totals, earlier)[:, 0:1]
        for i in range(nblk):
            fk_scr[i // per, :, (i % per) * blk:(i % per + 1) * blk] = f_all[i * FOX_HEADS:(i + 1) * FOX_HEADS, :] * LOG2E
        carry = f_all[nrow - FOX_HEADS:, blk - 1:blk]
        sub = lax.broadcasted_iota(jnp.int32, (FOX_HEADS, LANES), 0)
        ln = lax.broadcasted_iota(jnp.int32, (FOX_HEADS, LANES), 1)
        total_row = jnp.sum(jnp.where(sub == ln - F_LANE0, carry, 0.0), axis=0, keepdims=True)
        f_new = _cumsum_rows(logf_ref[...], _tri(t, True)) + total_row
        f_new = f_new * LOG2E
        fnew_scr[...] = f_new.T[F_LANE0:F_LANE0 + FOX_HEADS, :]
        fq_scr[...] = jnp.concatenate([f_new[:, F_LANE0 + h:F_LANE0 + h + 1] for h in range(FOX_HEADS)], axis=0)
        m_scr[...] = jnp.full(m_scr.shape, NEG_BIG, F32)
        l_scr[...] = jnp.zeros(l_scr.shape, F32)
        acc_scr[...] = jnp.zeros(acc_scr.shape, F32)

    q = q_ref[...]
    q_heads = [q[:, h * FOX_HEAD_DIM:(h + 1) * FOX_HEAD_DIM] for h in range(FOX_HEADS)]
    f_q = fq_scr[...]

    def update(keys, values, f_k, mask, channels_first):
        qk = _dot if channels_first else _dot_nt
        pv_dot = _dot_nt if channels_first else _dot
        tt = jnp.concatenate([qk(q_heads[h], keys[h]) - f_k[h:h + 1, :] for h in range(FOX_HEADS)], axis=0)
        if mask is not None:
            tt = jnp.where(mask, tt, NEG_BIG)
        m_old = m_scr[...]
        m_new = jnp.maximum(m_old, jnp.max(tt, axis=1, keepdims=True) + f_q)
        alpha = jnp.exp2(m_old - m_new)
        pr = jnp.exp2(tt - (m_new - f_q))
        l_scr[...] = alpha * l_scr[...] + jnp.sum(pr, axis=1, keepdims=True)
        prb = pr.astype(BF16)
        pv = jnp.concatenate([pv_dot(prb[h * t:(h + 1) * t, :], values[h]) for h in range(FOX_HEADS)], axis=0)
        acc_scr[...] = alpha * acc_scr[...] + pv
        m_scr[...] = m_new

    head_rows = lambda ref: [ref[h * FOX_HEAD_DIM:(h + 1) * FOX_HEAD_DIM, :].astype(BF16) for h in range(FOX_HEADS)]
    update(head_rows(ck_ref), head_rows(cv_ref), fk_scr[c], None, True)

    @pl.when(c == nchunk - 1)
    def _():
        k_new, v_new = k_ref[...].astype(BF16), v_ref[...].astype(BF16)
        cols = lambda a: [a[:, h * FOX_HEAD_DIM:(h + 1) * FOX_HEAD_DIM] for h in range(FOX_HEADS)]
        rr = lax.broadcasted_iota(jnp.int32, (FOX_HEADS * t, t), 0) % t
        cc = lax.broadcasted_iota(jnp.int32, (FOX_HEADS * t, t), 1)
        update(cols(k_new), cols(v_new), fnew_scr[...], cc <= rr, False)
        out = acc_scr[...] / l_scr[...]
        o_ref[...] = jnp.concatenate([out[h * t:(h + 1) * t, :] for h in range(FOX_HEADS)], axis=1).astype(BF16)


def _fox_sample_call(q, k, v, small, cache_k, cache_v, cache_logf_t, layer):
    bsz, t, _ = q.shape
    past = cache_logf_t.shape[3]
    chunk = min(SAMPLE_CHUNK, past)
    nchunk = past // chunk
    new = lambda w: pl.BlockSpec((None, t, w), lambda b, c: (b, 0, 0))
    cache = pl.BlockSpec((None, None, FOX_WIDTH, chunk), lambda b, c: (layer, b, 0, c))
    rows = FOX_HEADS * t
    return pl.pallas_call(
        _fox_sample_kernel,
        grid=(bsz, nchunk),
        in_specs=[new(FOX_WIDTH), new(FOX_WIDTH), new(FOX_WIDTH), new(LANES), cache, cache,
                  pl.BlockSpec((None, None, FOX_HEADS, past), lambda b, c: (layer, b, 0, 0))],
        out_specs=new(FOX_WIDTH),
        out_shape=jax.ShapeDtypeStruct((bsz, t, FOX_WIDTH), BF16),
        scratch_shapes=[pltpu.VMEM((rows, 1), F32), pltpu.VMEM((rows, 1), F32), pltpu.VMEM((rows, FOX_HEAD_DIM), F32),
                        pltpu.VMEM((nchunk, FOX_HEADS, chunk), F32), pltpu.VMEM((rows, 1), F32),
                        pltpu.VMEM((FOX_HEADS, t), F32)],
        compiler_params=_params("fox_sample", ("arbitrary", "arbitrary")),
        name="fox_sample",
    )(q, k, v, small, cache_k, cache_v, cache_logf_t)


def _small_rows(w_in_t):
    offs = np.concatenate([[0], np.cumsum(IN_SIZES)])
    seg = lambda i: w_in_t[:, int(offs[i]):int(offs[i + 1]), :]
    pad = jnp.zeros((w_in_t.shape[0], LANES - FOX_HEADS - SSD_HEADS, w_in_t.shape[2]), w_in_t.dtype)
    return jnp.concatenate([seg(5), seg(8), pad], axis=1)


def _block_diag(w):
    d, h, b, _ = w.shape
    eye = jnp.eye(h, dtype=w.dtype)
    return jnp.einsum("dhij,hg->dhigj", w, eye).reshape(d, h * b, h * b)


def _lane_slab(v, lane0):
    d, k = v.shape
    return jnp.zeros((d, 1, LANES), v.dtype).at[:, 0, lane0:lane0 + k].set(v)


def _pad_history(state):
    return jnp.pad(state, ((0, 0), (0, 0), (SUBLANES - (CONV_W - 1), 0), (0, 0)))


def _trunk(x, mod_group, caches, prm, ssd_chunk):
    bsz, n, _ = x.shape
    x = x.reshape(bsz * n, D_MODEL)
    states = {name: [] for name in ("fox_k", "fox_v", "fox_logf", "lru_conv", "lru_h", "ssd_conv", "ssd_h")}
    prev_kv = None
    for l in range(DEPTH):
        mod4 = mod_group[l].reshape(bsz, N_SUB, 3, D_MODEL)
        x = _ffn_call(x, mod4, prm["npre"], prm["npost"], prm["wg"], prm["wu"], prm["wd"], l, 0, 0, n)
        prompt = caches is None
        stacking = prompt and l == DEPTH - 1
        proj = _inproj_call(x, mod4, prm["npre"], prm["w_in_t"], prm["w_small"], prm["fbias"], l, n,
                            transposed=prompt, prev_kv=prev_kv if stacking else None)
        lrux, lrug, q, k, v, z, xbc, small = proj[:8]
        per_seq = lambda a: a.reshape(bsz, n, a.shape[-1])
        state_layer = None if prompt else l
        src = prm["zero_state"] if prompt else caches
        ya, lru_conv, lru_h = _lru_call(per_seq(lrux), per_seq(lrug), src["lru_conv"], src["lru_h"],
                                        prm["lru_cw"], prm["lru_cb"], prm["lru_wgate"], prm["lru_bgate"],
                                        prm["lru_lam"], l, state_layer)
        if prompt:
            small_t = proj[8]
            yb = _fox_prompt_call(per_seq(q), k, v, small_t, l if stacking else None)
            prev_kv = (k, v)
            logf_out = jnp.swapaxes(small_t[:, F_LANE0:F_LANE0 + FOX_HEADS, :], 1, 2)
        else:
            yb = _fox_sample_call(per_seq(q), per_seq(k), per_seq(v), per_seq(small),
                                  caches["fox_k"], caches["fox_v"], caches["fox_logf_t"], l)
            k_out = k.reshape(bsz, n, FOX_HEADS, FOX_HEAD_DIM)
            v_out = v.reshape(bsz, n, FOX_HEADS, FOX_HEAD_DIM)
            logf_out = small.reshape(bsz, n, LANES)[:, :, F_LANE0:F_LANE0 + FOX_HEADS]
        yc, ssd_conv, ssd_h = _ssd_call(per_seq(xbc), per_seq(z), per_seq(small), src["ssd_conv"], src["ssd_h"],
                                        prm["ssd_cw"], prm["ssd_cb"], prm["ssd_dtb"], prm["ssd_alog"],
                                        prm["ssd_dvec"], prm["ssd_nw"], l, state_layer, ssd_chunk)
        x = _outffn_call(x, ya.reshape(bsz * n, -1), yb.reshape(bsz * n, -1), yc.reshape(bsz * n, -1),
                         mod4, prm["npre"], prm["npost"], prm["w_out"], prm["wg"], prm["wu"], prm["wd"], l, n)
        if not prompt:
            states["fox_k"].append(k_out)
            states["fox_v"].append(v_out)
        states["fox_logf"].append(logf_out)
        states["lru_conv"].append(lru_conv)
        states["lru_h"].append(lru_h.reshape(bsz, LRU_WIDTH))
        states["ssd_conv"].append(ssd_conv)
        states["ssd_h"].append(ssd_h.reshape(bsz, SSD_HEADS, SSD_HEAD_DIM, D_STATE))
    out = {name: jnp.stack(vals, axis=0) for name, vals in states.items() if vals}
    if prev_kv is not None:
        heads_last = lambda a: jnp.transpose(a.reshape(DEPTH, bsz, FOX_HEADS, FOX_HEAD_DIM, n), (0, 1, 4, 2, 3))
        out["fox_k"], out["fox_v"] = heads_last(prev_kv[0]), heads_last(prev_kv[1])
    return x.reshape(bsz, n, D_MODEL), out


def kernel(x_prompt, x_sample, c_prompt, c_sample, cache_fox_k, cache_fox_v, cache_fox_logf, state_lru_conv, state_lru_h, state_ssd_conv, state_ssd_h, w_mod, b_mod, norm_pre, norm_post, ffn_w_gate, ffn_w_up, ffn_w_down, w_in, w_out, lru_conv_w, lru_conv_b, lru_wa, lru_ba, lru_wx, lru_bx, lru_lambda, fox_f_bias, ssd_conv_w, ssd_conv_b, ssd_dt_bias, ssd_a_log, ssd_d, ssd_norm_w):
    n_prompt, n_sample = x_prompt.shape[0], x_sample.shape[0]
    w_in_t = jnp.swapaxes(w_in, 1, 2)
    prm = {
        "npre": norm_pre.reshape(DEPTH, N_SUB, 1, D_MODEL),
        "npost": norm_post.reshape(DEPTH, N_SUB, 1, D_MODEL),
        "wg": ffn_w_gate.astype(BF16), "wu": ffn_w_up.astype(BF16), "wd": ffn_w_down.astype(BF16),
        "w_in_t": w_in_t, "w_small": _small_rows(w_in_t), "w_out": w_out.astype(BF16),
        "fbias": _lane_slab(fox_f_bias, F_LANE0),
        "lru_cw": lru_conv_w, "lru_cb": lru_conv_b.reshape(DEPTH, 1, LRU_WIDTH),
        "lru_wgate": jnp.concatenate([_block_diag(lru_wa), _block_diag(lru_wx)], axis=-1).astype(BF16),
        "lru_bgate": jnp.concatenate([lru_ba, lru_bx], axis=-1).reshape(DEPTH, 1, 2 * LRU_WIDTH),
        "lru_lam": lru_lambda.reshape(DEPTH, 1, LRU_WIDTH),
        "ssd_cw": ssd_conv_w, "ssd_cb": ssd_conv_b.reshape(DEPTH, 1, SSD_CONV_DIM),
        "ssd_dtb": _lane_slab(ssd_dt_bias, DT_LANE0), "ssd_alog": _lane_slab(ssd_a_log, DT_LANE0),
        "ssd_dvec": jnp.repeat(ssd_d, SSD_HEAD_DIM, axis=-1).reshape(DEPTH, 1, SSD_WIDTH),
        "ssd_nw": ssd_norm_w.reshape(DEPTH, 1, SSD_WIDTH),
        "zero_state": {
            "lru_conv": jnp.zeros((1, SUBLANES, LRU_WIDTH), F32), "lru_h": jnp.zeros((1, 1, LRU_WIDTH), F32),
            "ssd_conv": jnp.zeros((1, SUBLANES, SSD_CONV_DIM), F32), "ssd_h": jnp.zeros((1, SSD_WIDTH, D_STATE), F32),
        },
    }
    caches = {
        "fox_k": jnp.transpose(cache_fox_k, (0, 1, 3, 4, 2)).reshape(DEPTH, n_sample, FOX_WIDTH, -1),
        "fox_v": jnp.transpose(cache_fox_v, (0, 1, 3, 4, 2)).reshape(DEPTH, n_sample, FOX_WIDTH, -1),
        "fox_logf_t": jnp.swapaxes(cache_fox_logf, 2, 3),
        "lru_conv": _pad_history(state_lru_conv),
        "lru_h": state_lru_h.reshape(DEPTH, n_sample, 1, LRU_WIDTH),
        "ssd_conv": _pad_history(state_ssd_conv),
        "ssd_h": state_ssd_h.reshape(DEPTH, n_sample, SSD_WIDTH, D_STATE),
    }
    mod = _mod_call(jnp.concatenate([c_prompt, c_sample], axis=0), w_mod, b_mod)
    y_prompt, sp = _trunk(x_prompt, mod[:, :n_prompt], None, prm, ssd_chunk=256)
    y_sample, ss = _trunk(x_sample, mod[:, n_prompt:], caches, prm, ssd_chunk=x_sample.shape[1])
    names = ("fox_k", "fox_v", "fox_logf", "lru_conv", "lru_h", "ssd_conv", "ssd_h")
    return (y_prompt, y_sample) + tuple(sp[n] for n in names) + tuple(ss[n] for n in names)
```
